```python
import math
import jax, jax.numpy as jnp
from jax import lax
import numpy as np

D_MODEL = 1024
BATCH = 4
SEQ = 4096
DEPTH = 1

HEAD_DIM = 64
NSA_HEADS = 8
NSA_KV_GROUPS = 2
NSA_REP = NSA_HEADS // NSA_KV_GROUPS
CMP_BLOCK = 32
CMP_STRIDE = 16
CMP_HIDDEN = 128
SLC_BLOCK = 64
SLC_TOP = 16
NSA_WINDOW = 512
NSA_QBLOCK = 64
FORCE_SCORE = 1.0e4
DIL_CONFIGS = ((128, 1), (512, 4), (2048, 16))
DIL_GROUPS = 3
DIL_HEADS_PER_GROUP = 4
DIL_HEADS = DIL_GROUPS * DIL_HEADS_PER_GROUP
DIL_QBLOCK = 128
D_FF = 2816
CONV_WIDTH = 3
RMS_EPS = 1e-6
NEG_INF = -1e30
D_IN = (NSA_HEADS * HEAD_DIM + 6 * NSA_KV_GROUPS * HEAD_DIM + 3 * NSA_HEADS
        + 3 * DIL_HEADS * HEAD_DIM + 2 * D_MODEL)

kernel_name = "hybrid_nsa_dilated_gated_merge"


def rmsnorm(x, g):
    xf = x.astype(jnp.float32)
    y = xf * lax.rsqrt(jnp.mean(xf * xf, axis=-1, keepdims=True) + RMS_EPS)
    return (y * g.astype(jnp.float32)).astype(x.dtype)


def alibi_slopes(n):
    return jnp.asarray(2.0 ** (-8.0 * np.arange(1, n + 1) / n), dtype=jnp.float32)


def masked_softmax(s, mask):
    s = jnp.where(mask, s.astype(jnp.float32), NEG_INF)
    m = jnp.max(s, axis=-1, keepdims=True)
    p = jnp.where(mask, jnp.exp(s - m), 0.0)
    denom = jnp.sum(p, axis=-1, keepdims=True)
    probs = p / jnp.maximum(denom, 1e-30)
    return probs, m + jnp.log(denom)


def in_splits():
    sizes = (NSA_HEADS * HEAD_DIM, 6 * NSA_KV_GROUPS * HEAD_DIM, 3 * NSA_HEADS,
             3 * DIL_HEADS * HEAD_DIM)
    idx, acc = [], 0
    for s in sizes:
        acc += s
        idx.append(acc)
    return idx


def compress(kv, pe, w1, w2):
    B, S, G, dh = kv.shape
    b16 = kv.reshape(B, S // CMP_STRIDE, CMP_STRIDE, G, dh)
    blocks = jnp.concatenate([b16[:, :-1], b16[:, 1:]], axis=2)
    blocks = blocks + pe[None, None, :, None, :]
    nc = blocks.shape[1]
    flat = blocks.transpose(0, 1, 3, 2, 4).reshape(B, nc, G, CMP_BLOCK * dh)
    return jax.nn.gelu(flat @ w1) @ w2


def nsa_attention(q, kc, vc, ks, vs, kw, vw, gates):
    B, S, H, dh = q.shape
    G, R, QB, W, L = NSA_KV_GROUPS, NSA_REP, NSA_QBLOCK, NSA_WINDOW, SLC_BLOCK
    NC = kc.shape[1]
    NS = S // L
    NQ = S // QB
    top = min(SLC_TOP, NS)
    scale = dh ** -0.5
    slopes = alibi_slopes(H).reshape(G, R)
    cmp_start = jnp.arange(NC) * CMP_STRIDE
    cmp_end = cmp_start + CMP_BLOCK - 1
    slc_start = jnp.arange(NS) * L
    overlap = jnp.clip(jnp.minimum(cmp_start[:, None] + CMP_BLOCK, slc_start[None, :] + L)
                       - jnp.maximum(cmp_start[:, None], slc_start[None, :]), 0, None)
    overlap = overlap.astype(jnp.float32) / CMP_BLOCK
    kc_t = kc.transpose(0, 2, 1, 3)
    vc_t = vc.transpose(0, 2, 1, 3)
    ks_blk = ks.reshape(B, NS, L, G, dh).transpose(0, 3, 1, 2, 4)
    vs_blk = vs.reshape(B, NS, L, G, dh).transpose(0, 3, 1, 2, 4)
    kw_pad = jnp.pad(kw, ((0, 0), (W, 0), (0, 0), (0, 0)))
    vw_pad = jnp.pad(vw, ((0, 0), (W, 0), (0, 0), (0, 0)))
    bi = jnp.arange(B)[:, None, None, None]
    gi = jnp.arange(G)[None, :, None, None]
    blk = jnp.arange(NS)

    def block_fn(i):
        q0 = i * QB
        t = q0 + jnp.arange(QB)
        qb = lax.dynamic_slice_in_dim(q, q0, QB, axis=1)
        qb = qb.reshape(B, QB, G, R, dh).transpose(0, 2, 3, 1, 4) * scale
        gb = lax.dynamic_slice_in_dim(gates, q0, QB, axis=1)
        gb = gb.reshape(B, QB, G, R, 3).transpose(0, 2, 3, 1, 4)

        d_cmp = (t[:, None] - cmp_end[None, :]).astype(jnp.float32)
        s = jnp.einsum('bgrqd,bgcd->bgrqc', qb, kc_t).astype(jnp.float32)
        s = s - slopes[None, :, :, None, None] * d_cmp
        p_cmp, _ = masked_softmax(s, d_cmp >= 0)
        o_cmp = jnp.einsum('bgrqc,bgcd->bgrqd', p_cmp.astype(vc.dtype), vc_t)

        imp = jnp.einsum('bgrqc,cs->bgqs', p_cmp, overlap)
        cur = t // L
        imp = jnp.where((blk[None, :] == cur[:, None]) | (blk[None, :] == 0), FORCE_SCORE, imp)
        imp = jnp.where(blk[None, :] <= cur[:, None], imp, -1.0)
        top_val, top_idx = lax.top_k(imp, top)
        k_sel = ks_blk[bi, gi, top_idx]
        v_sel = vs_blk[bi, gi, top_idx]
        pos = top_idx[..., None] * L + jnp.arange(L)
        d_sel = t[None, None, :, None, None] - pos
        m_sel = ((top_val >= 0)[..., None] & (d_sel >= 0))[:, :, None]
        s = jnp.einsum('bgrqd,bgqtld->bgrqtl', qb, k_sel).astype(jnp.float32)
        s = s - slopes[None, :, :, None, None, None] * d_sel[:, :, None].astype(jnp.float32)
        p_sel, _ = masked_softmax(s.reshape(B, G, R, QB, top * L), m_sel.reshape(B, G, 1, QB, top * L))
        o_sel = jnp.einsum('bgrqtl,bgqtld->bgrqd',
                           p_sel.reshape(B, G, R, QB, top, L).astype(vs.dtype), v_sel)

        kwb = lax.dynamic_slice_in_dim(kw_pad, q0, QB + W, axis=1)
        vwb = lax.dynamic_slice_in_dim(vw_pad, q0, QB + W, axis=1)
        key_pos = q0 - W + jnp.arange(QB + W)
        d_win = t[:, None] - key_pos[None, :]
        m_win = (d_win >= 0) & (d_win < W) & (key_pos[None, :] >= 0)
        s = jnp.einsum('bgrqd,bkgd->bgrqk', qb, kwb).astype(jnp.float32)
        s = s - slopes[None, :, :, None, None] * d_win.astype(jnp.float32)
        p_win, _ = masked_softmax(s, m_win)
        o_win = jnp.einsum('bgrqk,bkgd->bgrqd', p_win.astype(vw.dtype), vwb)

        o = gb[..., 0:1] * o_cmp + gb[..., 1:2] * o_sel + gb[..., 2:3] * o_win
        return o.transpose(0, 3, 1, 2, 4).reshape(B, QB, H * dh)

    out = lax.map(block_fn, jnp.arange(NQ))
    return out.transpose(1, 0, 2, 3).reshape(B, S, H * dh)


def dilated_attention(q, k, v):
    B, S, NG, HG, dh = q.shape
    QB = DIL_QBLOCK
    NQ = S // QB
    scale = dh ** -0.5
    slopes = alibi_slopes(DIL_HEADS).reshape(NG, HG)

    def block_fn(i):
        q0 = i * QB
        t = q0 + jnp.arange(QB)
        qb = lax.dynamic_slice_in_dim(q, q0, QB, axis=1) * scale
        outs, lses = [], []
        for g, (w, r) in enumerate(DIL_CONFIGS):
            dist = r * jnp.arange(w // r + 1)
            key_pos = t[:, None] - dist[None, :]
            idx = jnp.maximum(key_pos, 0)
            kg = jnp.take(k[:, :, g], idx, axis=1)
            vg = jnp.take(v[:, :, g], idx, axis=1)
            s = jnp.einsum('bqhd,bqnhd->bhqn', qb[:, :, g], kg).astype(jnp.float32)
            s = s - slopes[g][:, None, None] * dist.astype(jnp.float32)
            p, lse = masked_softmax(s, (key_pos >= 0)[None, None])
            outs.append(jnp.einsum('bhqn,bqnhd->bqhd', p.astype(v.dtype), vg))
            lses.append(lse[..., 0])
        wts = jax.nn.softmax(jnp.stack(lses, axis=0), axis=0)
        o = outs[0] * wts[0].transpose(0, 2, 1)[..., None].astype(outs[0].dtype)
        for g in range(1, NG):
            o = o + outs[g] * wts[g].transpose(0, 2, 1)[..., None].astype(outs[g].dtype)
        return o.reshape(B, QB, HG * dh)

    out = lax.map(block_fn, jnp.arange(NQ))
    return out.transpose(1, 0, 2, 3).reshape(B, S, HG * dh)


def conv_ffn(h, w_up, conv_w, conv_b, w_down):
    S = h.shape[1]
    u, gate = jnp.split(h @ w_up, 2, axis=-1)
    up = jnp.pad(u, ((0, 0), (CONV_WIDTH - 1, 0), (0, 0)))
    uc = conv_b
    for j in range(CONV_WIDTH):
        uc = uc + conv_w[j] * up[:, j:j + S]
    return (jax.nn.gelu(uc) * gate) @ w_down


def setup_inputs(seed: int = 0) -> dict:
    key = jax.random.key(seed)
    ks = jax.random.split(key, 20)
    f = jnp.float32
    dh = HEAD_DIM

    def nrm(k, shape, scale):
        return jax.random.normal(k, shape, f) * scale

    return {
        "x": jax.random.normal(ks[0], (BATCH, SEQ, D_MODEL), f),
        "g_mix": 1.0 + nrm(ks[1], (DEPTH, D_MODEL), 0.01),
        "w_in": nrm(ks[2], (DEPTH, D_MODEL, D_IN), D_MODEL ** -0.5),
        "pe_cmp_k": nrm(ks[3], (DEPTH, CMP_BLOCK, dh), 0.1),
        "w_cmp_k1": nrm(ks[4], (DEPTH, CMP_BLOCK * dh, CMP_HIDDEN), (CMP_BLOCK * dh) ** -0.5),
        "w_cmp_k2": nrm(ks[5], (DEPTH, CMP_HIDDEN, dh), CMP_HIDDEN ** -0.5),
        "pe_cmp_v": nrm(ks[6], (DEPTH, CMP_BLOCK, dh), 0.1),
        "w_cmp_v1": nrm(ks[7], (DEPTH, CMP_BLOCK * dh, CMP_HIDDEN), (CMP_BLOCK * dh) ** -0.5),
        "w_cmp_v2": nrm(ks[8], (DEPTH, CMP_HIDDEN, dh), CMP_HIDDEN ** -0.5),
        "w_proj_nsa": nrm(ks[9], (DEPTH, NSA_HEADS * dh, D_MODEL), (NSA_HEADS * dh) ** -0.5),
        "w_proj_dil": nrm(ks[10], (DEPTH, DIL_HEADS_PER_GROUP * dh, D_MODEL), (DIL_HEADS_PER_GROUP * dh) ** -0.5),
        "w_out": nrm(ks[11], (DEPTH, D_MODEL, D_MODEL), D_MODEL ** -0.5),
        "g_ffn": 1.0 + nrm(ks[12], (DEPTH, D_MODEL), 0.01),
        "w_up": nrm(ks[13], (DEPTH, D_MODEL, 2 * D_FF), D_MODEL ** -0.5),
        "conv_w": nrm(ks[14], (DEPTH, CONV_WIDTH, D_FF), CONV_WIDTH ** -0.5),
        "conv_b": nrm(ks[15], (DEPTH, D_FF), 0.01),
        "w_down": nrm(ks[16], (DEPTH, D_FF, D_MODEL), D_FF ** -0.5),
        "g_final": 1.0 + nrm(ks[17], (D_MODEL,), 0.01),
    }


def reference(x, g_mix, w_in, pe_cmp_k, w_cmp_k1, w_cmp_k2, pe_cmp_v, w_cmp_v1, w_cmp_v2,
              w_proj_nsa, w_proj_dil, w_out, g_ffn, w_up, conv_w, conv_b, w_down, g_final):
    B, S, _ = x.shape
    G, dh = NSA_KV_GROUPS, HEAD_DIM
    for l in range(DEPTH):
        h = rmsnorm(x, g_mix[l])
        proj = h @ w_in[l]
        q_a, kv_a, gate_a, qkv_b, merge_logits = jnp.split(proj, in_splits(), axis=-1)
        q_a = q_a.reshape(B, S, NSA_HEADS, dh)
        kv_a = kv_a.reshape(B, S, 6, G, dh)
        kc = compress(kv_a[:, :, 0], pe_cmp_k[l], w_cmp_k1[l], w_cmp_k2[l])
        vc = compress(kv_a[:, :, 1], pe_cmp_v[l], w_cmp_v1[l], w_cmp_v2[l])
        gate_a = jax.nn.sigmoid(gate_a).reshape(B, S, NSA_HEADS, 3)
        o_a = nsa_attention(q_a, kc, vc, kv_a[:, :, 2], kv_a[:, :, 3], kv_a[:, :, 4], kv_a[:, :, 5], gate_a)
        qkv_b = qkv_b.reshape(B, S, 3, DIL_GROUPS, DIL_HEADS_PER_GROUP, dh)
        o_b = dilated_attention(qkv_b[:, :, 0], qkv_b[:, :, 1], qkv_b[:, :, 2])
        gate_m = jax.nn.sigmoid(merge_logits)
        mixed = gate_m[..., :D_MODEL] * (o_a @ w_proj_nsa[l]) + gate_m[..., D_MODEL:] * (o_b @ w_proj_dil[l])
        x = x + mixed @ w_out[l]
        h = rmsnorm(x, g_ffn[l])
        x = x + conv_ffn(h, w_up[l], conv_w[l], conv_b[l], w_down[l])
    return rmsnorm(x, g_final)
```

```python
import functools
import math

import numpy as np
import jax
import jax.numpy as jnp
from jax import lax
from jax.experimental import pallas as pl
from jax.experimental.pallas import tpu as pltpu

HEAD_DIM = 64
NSA_HEADS = 8
NSA_GROUPS = 2
NSA_REP = NSA_HEADS // NSA_GROUPS
CMP_BLOCK = 32
CMP_STRIDE = 16
CMP_HIDDEN = 128
SLC_BLOCK = 64
SLC_TOP = 16
NSA_WINDOW = 512
FORCE_SCORE = 1.0e4
DIL_CONFIGS = ((128, 1), (512, 4), (2048, 16))
DIL_GROUPS = 3
DIL_HEADS_PER_GROUP = 4
DIL_HEADS = DIL_GROUPS * DIL_HEADS_PER_GROUP
D_FF = 2816
CONV_WIDTH = 3
RMS_EPS = 1e-6
NEG_INF = -1e30

LANES = 128
VMEM_LIMIT_BYTES = 56 * 1024 * 1024

F32 = jnp.float32
BF16 = jnp.bfloat16
NT_DIMS = (((1,), (1,)), ((), ()))


def _alibi_slopes(n):
    return [float(2.0 ** (-8.0 * i / n)) for i in range(1, n + 1)]


def _rms(xf, g):
    ms = jnp.mean(xf * xf, axis=-1, keepdims=True)
    return xf * lax.rsqrt(ms + RMS_EPS) * g


def _dot(a, b):
    return jnp.dot(a, b, preferred_element_type=F32)


def _dot_nt(a, b):
    return lax.dot_general(a, b, NT_DIMS, preferred_element_type=F32)


def _sigmoid(z):
    return 1.0 / (1.0 + jnp.exp(-z))


def _params(*sem):
    return pltpu.CompilerParams(dimension_semantics=sem, vmem_limit_bytes=VMEM_LIMIT_BYTES)


IN_TM = 256
N_KVC = 4 * HEAD_DIM
N_KSEL = NSA_GROUPS * HEAD_DIM
N_DIL = 3 * DIL_HEADS * HEAD_DIM
T_Q = NSA_HEADS * HEAD_DIM
T_V = NSA_GROUPS * HEAD_DIM
GATE_ROWS = 16


def _in_proj_kernel(x_ref, g_ref, wn_ref, wt_ref,
                    kvc_ref, ks_ref, kw_ref, dil_ref, qt_ref, vst_ref, vwt_ref, gate_ref):
    h = _rms(x_ref[...], g_ref[...]).astype(BF16)
    c0 = 0
    kvc_ref[...] = _dot(h, wn_ref[:, c0:c0 + N_KVC])
    c0 += N_KVC
    ks = _dot(h, wn_ref[:, c0:c0 + N_KSEL]).astype(BF16)
    c0 += N_KSEL
    kw = _dot(h, wn_ref[:, c0:c0 + N_KSEL]).astype(BF16)
    c0 += N_KSEL
    for g in range(NSA_GROUPS):
        ks_ref[g] = ks[:, g * HEAD_DIM:(g + 1) * HEAD_DIM]
        kw_ref[g] = kw[:, g * HEAD_DIM:(g + 1) * HEAD_DIM]
    seg = DIL_HEADS * HEAD_DIM
    for j in range(3):
        dil_ref[:, j * seg:(j + 1) * seg] = _dot(h, wn_ref[:, c0 + j * seg:c0 + (j + 1) * seg]).astype(BF16)
    r0 = 0
    qt_ref[...] = _dot_nt(wt_ref[r0:r0 + T_Q, :], h).astype(BF16)
    r0 += T_Q
    vst_ref[...] = _dot_nt(wt_ref[r0:r0 + T_V, :], h).astype(BF16)
    r0 += T_V
    vwt_ref[...] = _dot_nt(wt_ref[r0:r0 + T_V, :], h).astype(BF16)
    r0 += T_V
    gate_ref[...] = _sigmoid(_dot_nt(wt_ref[r0:r0 + NSA_GROUPS * GATE_ROWS, :], h))


def _in_proj(x, g_mix, w_in):
    B, S, D = x.shape
    scale = HEAD_DIM ** -0.5
    o_q, o_kv = 0, T_Q
    o_gate = o_kv + 6 * N_KSEL
    o_dil = o_gate + 3 * NSA_HEADS
    o_merge = o_dil + N_DIL
    kv = w_in[:, o_kv:o_gate]

    def kind(k):
        return kv[:, k * N_KSEL:(k + 1) * N_KSEL]

    dil = w_in[:, o_dil:o_merge]
    dil = jnp.concatenate([dil[:, :DIL_HEADS * HEAD_DIM] * scale, dil[:, DIL_HEADS * HEAD_DIM:]], axis=1)
    wn = jnp.concatenate([kind(0), kind(1), kind(2), kind(4), dil], axis=1).astype(BF16)
    wg = w_in[:, o_gate:o_dil].reshape(D, NSA_GROUPS, 3 * NSA_REP)
    wg = jnp.pad(wg, ((0, 0), (0, 0), (0, GATE_ROWS - 3 * NSA_REP))).reshape(D, NSA_GROUPS * GATE_ROWS)
    wt = jnp.concatenate([w_in[:, o_q:o_kv] * scale, kind(3), kind(5), wg], axis=1).T.astype(BF16)
    tm = IN_TM
    grid = (B, S // tm)
    full = lambda a: pl.BlockSpec(a.shape, lambda b, i: (0,) * a.ndim)
    out_shape = (
        jax.ShapeDtypeStruct((B, S, N_KVC), F32),
        jax.ShapeDtypeStruct((B, NSA_GROUPS, S, HEAD_DIM), BF16),
        jax.ShapeDtypeStruct((B, NSA_GROUPS, S, HEAD_DIM), BF16),
        jax.ShapeDtypeStruct((B, S, N_DIL), BF16),
        jax.ShapeDtypeStruct((B, T_Q, S), BF16),
        jax.ShapeDtypeStruct((B, T_V, S), BF16),
        jax.ShapeDtypeStruct((B, T_V, S), BF16),
        jax.ShapeDtypeStruct((B, NSA_GROUPS * GATE_ROWS, S), F32),
    )
    out_specs = (
        pl.BlockSpec((None, tm, N_KVC), lambda b, i: (b, i, 0)),
        pl.BlockSpec((None, NSA_GROUPS, tm, HEAD_DIM), lambda b, i: (b, 0, i, 0)),
        pl.BlockSpec((None, NSA_GROUPS, tm, HEAD_DIM), lambda b, i: (b, 0, i, 0)),
        pl.BlockSpec((None, tm, N_DIL), lambda b, i: (b, i, 0)),
        pl.BlockSpec((None, T_Q, tm), lambda b, i: (b, 0, i)),
        pl.BlockSpec((None, T_V, tm), lambda b, i: (b, 0, i)),
        pl.BlockSpec((None, T_V, tm), lambda b, i: (b, 0, i)),
        pl.BlockSpec((None, NSA_GROUPS * GATE_ROWS, tm), lambda b, i: (b, 0, i)),
    )
    return pl.pallas_call(
        _in_proj_kernel,
        grid=grid,
        in_specs=[pl.BlockSpec((None, tm, D), lambda b, i: (b, i, 0)), full(g_mix), full(wn), full(wt)],
        out_specs=out_specs,
        out_shape=out_shape,
        compiler_params=_params("parallel", "parallel"),
        name="in_proj",
    )(x, g_mix, wn, wt)


def _compress_kernel(a_ref, pe_ref, wlo_ref, whi_ref, w2_ref, w2t_ref, c_ref, ct_ref):
    a = a_ref[...]
    nch = a.shape[0]
    alo = (a + pe_ref[0:1, :]).astype(BF16)
    ahi = (a + pe_ref[1:2, :]).astype(BF16)
    ulo = _dot(alo, wlo_ref[...])
    uhi = _dot(ahi, whi_ref[...])
    pre = ulo + pltpu.roll(uhi, nch - 1, 0)
    hid = jax.nn.gelu(pre).astype(BF16)
    for g in range(NSA_GROUPS):
        hg = hid[:, g * CMP_HIDDEN:(g + 1) * CMP_HIDDEN]
        c_ref[g] = _dot(hg, w2_ref[...]).astype(BF16)
        ct_ref[g] = _dot_nt(w2t_ref[...], hg).astype(BF16)


def _compress(part, pe, w1, w2):
    B, nch, width = part.shape
    G, dh, hid = NSA_GROUPS, HEAD_DIM, CMP_HIDDEN
    pe_t = jnp.broadcast_to(pe.reshape(2, CMP_STRIDE, 1, dh), (2, CMP_STRIDE, G, dh)).reshape(2, width)
    w1r = w1.reshape(2, CMP_STRIDE, dh, hid)
    eye = jnp.eye(G, dtype=w1.dtype)
    wexp = jnp.einsum('hjdn,ge->hjgden', w1r, eye).reshape(2, width, G * hid).astype(BF16)
    full = lambda a: pl.BlockSpec(a.shape, lambda b: (0,) * a.ndim)
    w2b = w2.astype(BF16)
    w2t = w2.T.astype(BF16)
    return pl.pallas_call(
        _compress_kernel,
        grid=(B,),
        in_specs=[pl.BlockSpec((None, nch, width), lambda b: (b, 0, 0)), full(pe_t),
                  full(wexp[0]), full(wexp[1]), full(w2b), full(w2t)],
        out_specs=(pl.BlockSpec((None, G, nch, dh), lambda b: (b, 0, 0, 0)),
                   pl.BlockSpec((None, G, dh, nch), lambda b: (b, 0, 0, 0))),
        out_shape=(jax.ShapeDtypeStruct((B, G, nch, dh), BF16),
                   jax.ShapeDtypeStruct((B, G, dh, nch), BF16)),
        compiler_params=_params("parallel"),
        name="compress",
    )(part, pe_t, wexp[0], wexp[1], w2b, w2t)


NSA_TQ = 128
NSA_TK = 128
BLK_PER_TILE = NSA_TK // SLC_BLOCK


def _nsa_kernel(qt_ref, kc_ref, vct_ref, ks_ref, vst_ref, kw_ref, vwt_ref, gate_ref, ov_ref,
                o_ref, selb_ref, *, slopes):
    g = pl.program_id(1)
    i = pl.program_id(2)
    R, dh, tq, tk = NSA_REP, HEAD_DIM, NSA_TQ, NSA_TK
    t0 = i * tq
    nc = kc_ref.shape[0]
    ns = ov_ref.shape[0]

    qt = qt_ref[...]
    qs = jnp.concatenate([qt[r * dh:(r + 1) * dh, :] for r in range(R)], axis=1)
    slope = [jnp.where(g == 0, slopes[r], slopes[R + r]).astype(F32) for r in range(R)]
    t_row = t0 + lax.broadcasted_iota(jnp.int32, (1, tq), 1)

    sc = _dot(kc_ref[...], qs)
    cmp_end = lax.broadcasted_iota(jnp.int32, (nc, tq), 0) * CMP_STRIDE + (CMP_BLOCK - 1)
    d_cmp = t_row - cmp_end
    m_cmp = d_cmp >= 0
    d_cmpf = d_cmp.astype(F32)
    probs = []
    psum = jnp.zeros((nc, tq), F32)
    for r in range(R):
        s = sc[:, r * tq:(r + 1) * tq] - slope[r] * d_cmpf
        s = jnp.where(m_cmp, s, NEG_INF)
        mx = jnp.max(s, axis=0, keepdims=True)
        p = jnp.where(m_cmp, jnp.exp(s - mx), 0.0)
        den = jnp.sum(p, axis=0, keepdims=True)
        pr = p * (1.0 / jnp.maximum(den, 1e-30))
        probs.append(pr.astype(BF16))
        psum = psum + pr
    o_cmp = _dot(vct_ref[...], jnp.concatenate(probs, axis=1))

    p_hi = psum.astype(BF16)
    p_lo = (psum - p_hi.astype(F32)).astype(BF16)
    imp = _dot(ov_ref[...], p_hi) + _dot(ov_ref[...], p_lo)
    blk = lax.broadcasted_iota(jnp.int32, (ns, tq), 0)
    cur = lax.shift_right_logical(t_row, int(math.log2(SLC_BLOCK)))
    val = jnp.where((blk == cur) | (blk == 0), FORCE_SCORE, imp)
    val = jnp.where(blk <= cur, val, -1.0)
    rank = jnp.zeros((ns, tq), F32)
    for j in range(ns):
        vj = val[j:j + 1, :]
        ahead = (vj > val) | ((vj == val) & (blk > j))
        rank = rank + jnp.where(ahead, 1.0, 0.0)
    selb = jnp.where((rank < float(SLC_TOP)) & (val >= 0.0), 0.0, NEG_INF)
    for s_ in range(ns):
        selb_ref[s_] = jnp.broadcast_to(selb[s_:s_ + 1, :], (8, tq))

    krow = lax.broadcasted_iota(jnp.int32, (tk, tq), 0)
    qcol = t0 + lax.broadcasted_iota(jnp.int32, (tk, tq), 1)

    def tile_scores(k_ref, kt):
        k0 = pl.multiple_of(kt * tk, tk)
        s = _dot(k_ref[pl.ds(k0, tk), :], qs)
        kpos = krow + k0
        rel = (kpos - t0).astype(F32)
        return s, kpos, rel, k0

    def online(carry, s_list, v_tile):
        m, l, acc = carry
        s = jnp.concatenate(s_list, axis=1)
        m_new = jnp.maximum(m, jnp.max(s, axis=0, keepdims=True))
        alpha = jnp.exp(m - m_new)
        p = jnp.exp(s - m_new)
        l = alpha * l + jnp.sum(p, axis=0, keepdims=True)
        acc = alpha * acc + _dot(v_tile, p.astype(BF16))
        return m_new, l, acc

    init = (jnp.full((1, R * tq), NEG_INF, F32), jnp.zeros((1, R * tq), F32), jnp.zeros((dh, R * tq), F32))

    def sel_body(kt, carry):
        s, kpos, rel, k0 = tile_scores(ks_ref, kt)
        bias = jnp.concatenate(
            [selb_ref[kt * BLK_PER_TILE + b] for b in range(BLK_PER_TILE) for _ in range(SLC_BLOCK // 8)], axis=0)
        bias = jnp.where(kpos <= qcol, bias, NEG_INF)
        s_list = [s[:, r * tq:(r + 1) * tq] + (slope[r] * rel + bias) for r in range(R)]
        return online(carry, s_list, vst_ref[:, pl.ds(k0, tk)])

    _, l_sel, acc_sel = lax.fori_loop(0, i + 1, sel_body, init)

    n_win = NSA_WINDOW // tk + 1

    def win_body(j, carry):
        kt = i - j
        s, kpos, rel, k0 = tile_scores(kw_ref, kt)
        d = qcol - kpos
        bias = jnp.where((d >= 0) & (d < NSA_WINDOW), 0.0, NEG_INF)
        s_list = [s[:, r * tq:(r + 1) * tq] + (slope[r] * rel + bias) for r in range(R)]
        return online(carry, s_list, vwt_ref[:, pl.ds(k0, tk)])

    _, l_win, acc_win = lax.fori_loop(0, jnp.minimum(i + 1, n_win), win_body, init)

    o_sel = acc_sel * (1.0 / l_sel)
    o_win = acc_win * (1.0 / l_win)
    gates = gate_ref[...]
    tiles = []
    for r in range(R):
        sl = slice(r * tq, (r + 1) * tq)
        tiles.append(gates[3 * r:3 * r + 1, :] * o_cmp[:, sl]
                     + gates[3 * r + 1:3 * r + 2, :] * o_sel[:, sl]
                     + gates[3 * r + 2:3 * r + 3, :] * o_win[:, sl])
    o_ref[...] = jnp.concatenate(tiles, axis=0).T.astype(o_ref.dtype)


def _overlap_matrix(nc, ns):
    cs = np.arange(nc)[None, :] * CMP_STRIDE
    ss = np.arange(ns)[:, None] * SLC_BLOCK
    ov = np.clip(np.minimum(cs + CMP_BLOCK, ss + SLC_BLOCK) - np.maximum(cs, ss), 0, None)
    return jnp.asarray(ov.astype(np.float32) / CMP_BLOCK, dtype=BF16)


def _nsa(qt, kc, vct, ks, vst, kw, vwt, gates):
    B, _, S = qt.shape
    G, R, dh, tq = NSA_GROUPS, NSA_REP, HEAD_DIM, NSA_TQ
    nc = kc.shape[2]
    ns = S // SLC_BLOCK
    ov = _overlap_matrix(nc, ns)
    kern = functools.partial(_nsa_kernel, slopes=tuple(_alibi_slopes(NSA_HEADS)))
    per_bg = lambda shape: pl.BlockSpec((None, None) + shape, lambda b, g, i: (b, g, 0, 0))
    return pl.pallas_call(
        kern,
        grid=(B, G, S // tq),
        in_specs=[
            pl.BlockSpec((None, R * dh, tq), lambda b, g, i: (b, g, i)),
            per_bg((nc, dh)), per_bg((dh, nc)),
            per_bg((S, dh)),
            pl.BlockSpec((None, dh, S), lambda b, g, i: (b, g, 0)),
            per_bg((S, dh)),
            pl.BlockSpec((None, dh, S), lambda b, g, i: (b, g, 0)),
            pl.BlockSpec((None, GATE_ROWS, tq), lambda b, g, i: (b, g, i)),
            pl.BlockSpec(ov.shape, lambda b, g, i: (0, 0)),
        ],
        out_specs=pl.BlockSpec((None, tq, R * dh), lambda b, g, i: (b, i, g)),
        out_shape=jax.ShapeDtypeStruct((B, S, G * R * dh), BF16),
        scratch_shapes=[pltpu.VMEM((ns, 8, tq), F32)],
        compiler_params=_params("parallel", "parallel", "arbitrary"),
        name="nsa_attention",
    )(qt, kc, vct, ks, vst, kw, vwt, gates, ov)


BAND_TQ = 128
BAND = 128


def _banded_kernel(q_ref, k_ref, v_ref, o_ref, lse_ref, *, slopes):
    i = pl.program_id(1)
    tq, dh = BAND_TQ, HEAD_DIM
    tk = tq + BAND
    k0 = pl.multiple_of(jnp.maximum(i - 1, 0) * tq, tq)
    q = q_ref[...]
    k = k_ref[pl.ds(k0, tk), :]
    v = v_ref[pl.ds(k0, tk), :]
    qpos = i * tq + lax.broadcasted_iota(jnp.int32, (tq, tk), 0)
    kpos = k0 + lax.broadcasted_iota(jnp.int32, (tq, tk), 1)
    d = qpos - kpos
    mask = (d >= 0) & (d <= BAND)
    df = d.astype(F32)
    outs, lses = [], []
    for h in range(DIL_HEADS_PER_GROUP):
        sl = slice(h * dh, (h + 1) * dh)
        s = _dot_nt(q[:, sl], k[:, sl]) - slopes[h] * df
        s = jnp.where(mask, s, NEG_INF)
        mx = jnp.max(s, axis=-1, keepdims=True)
        p = jnp.where(mask, jnp.exp(s - mx), 0.0)
        den = jnp.sum(p, axis=-1, keepdims=True)
        pr = p * (1.0 / jnp.maximum(den, 1e-30))
        outs.append(_dot(pr.astype(BF16), v[:, sl]))
        lses.append(jnp.broadcast_to(mx + jnp.log(den), (tq, dh)))
    o_ref[...] = jnp.concatenate(outs, axis=1)
    lse_ref[...] = jnp.concatenate(lses, axis=1)


def _banded(q, k, v, slopes):
    nseq, n, width = q.shape
    tq = BAND_TQ
    kern = functools.partial(_banded_kernel, slopes=tuple(slopes))
    seq = pl.BlockSpec((None, n, width), lambda s, i: (s, 0, 0))
    tile = pl.BlockSpec((None, tq, width), lambda s, i: (s, i, 0))
    return pl.pallas_call(
        kern,
        grid=(nseq, n // tq),
        in_specs=[tile, seq, seq],
        out_specs=(tile, tile),
        out_shape=(jax.ShapeDtypeStruct((nseq, n, width), F32),) * 2,
        compiler_params=_params("parallel", "arbitrary"),
        name="banded_attention",
    )(q, k, v)


def _dilated(dil):
    B, S, _ = dil.shape
    width = DIL_HEADS_PER_GROUP * HEAD_DIM
    slopes = _alibi_slopes(DIL_HEADS)
    outs, lses = [], []
    for gi, (w, r) in enumerate(DIL_CONFIGS):
        assert w // r == BAND
        n = S // r

        def classes(which):
            c0 = which * DIL_HEADS * HEAD_DIM + gi * width
            a = dil[:, :, c0:c0 + width].reshape(B, n, r, width)
            return a.transpose(0, 2, 1, 3).reshape(B * r, n, width)

        sl = [s_ * r for s_ in slopes[gi * DIL_HEADS_PER_GROUP:(gi + 1) * DIL_HEADS_PER_GROUP]]
        o, lse = _banded(classes(0), classes(1), classes(2), sl)
        back = lambda a: a.reshape(B, r, n, width).transpose(0, 2, 1, 3).reshape(B, S, width)
        outs.append(back(o))
        lses.append(back(lse))
    return outs, lses


MERGE_TM = 256


def _merge_kernel(x_ref, oa_ref, o0_ref, o1_ref, o2_ref, l0_ref, l1_ref, l2_ref,
                  gmix_ref, wm_ref, wpn_ref, wpd_ref, wo_ref, gffn_ref, x1_ref, h2_ref):
    x = x_ref[...]
    D = x.shape[1]
    h = _rms(x, gmix_ref[...]).astype(BF16)
    gm = _sigmoid(_dot(h, wm_ref[...]))
    l0, l1, l2 = l0_ref[...], l1_ref[...], l2_ref[...]
    mx = jnp.maximum(jnp.maximum(l0, l1), l2)
    e0, e1, e2 = jnp.exp(l0 - mx), jnp.exp(l1 - mx), jnp.exp(l2 - mx)
    inv = 1.0 / (e0 + e1 + e2)
    ob = o0_ref[...] * (e0 * inv) + o1_ref[...] * (e1 * inv) + o2_ref[...] * (e2 * inv)
    a = _dot(oa_ref[...], wpn_ref[...])
    d = _dot(ob.astype(BF16), wpd_ref[...])
    mixed = gm[:, :D] * a + gm[:, D:] * d
    x1 = x + _dot(mixed.astype(BF16), wo_ref[...])
    x1_ref[...] = x1
    h2_ref[...] = _rms(x1, gffn_ref[...]).astype(BF16)


def _merge(x2, oa2, outs, lses, g_mix, w_merge, w_proj_nsa, w_proj_dil, w_out, g_ffn):
    T, D = x2.shape
    tm = MERGE_TM
    row = lambda a: pl.BlockSpec((tm, a.shape[1]), lambda i: (i, 0))
    full = lambda a: pl.BlockSpec(a.shape, lambda i: (0,) * a.ndim)
    ws = [w_merge.astype(BF16), w_proj_nsa.astype(BF16), w_proj_dil.astype(BF16), w_out.astype(BF16)]
    args = [x2, oa2, *outs, *lses, g_mix, *ws, g_ffn]
    in_specs = [row(a) for a in args[:8]] + [full(a) for a in args[8:]]
    return pl.pallas_call(
        _merge_kernel,
        grid=(T // tm,),
        in_specs=in_specs,
        out_specs=(pl.BlockSpec((tm, D), lambda i: (i, 0)),) * 2,
        out_shape=(jax.ShapeDtypeStruct((T, D), F32), jax.ShapeDtypeStruct((T, D), BF16)),
        compiler_params=_params("parallel"),
        name="merge_proj",
    )(*args)


FFN_TM = 256
FFN_TN = 256
HALO = 16


def _ffn_kernel(h_ref, halo_ref, x1_ref, wu_ref, wg_ref, cw_ref, cb_ref, wd_ref, gfin_ref, o_ref, acc_ref):
    i = pl.program_id(1)
    h = h_ref[...]
    halo = halo_ref[...]
    tm = h.shape[0]
    row = lax.broadcasted_iota(jnp.int32, (tm, FFN_TN), 0)
    live = (i > 0).astype(F32)
    acc_ref[...] = jnp.zeros_like(acc_ref)

    def body(j, _):
        wu = wu_ref[j]
        u = _dot(h, wu)
        uh = _dot(halo, wu) * live
        gate = _dot(h, wg_ref[j])
        p1 = jnp.broadcast_to(uh[HALO - 1:HALO, :], (tm, FFN_TN))
        p2 = jnp.broadcast_to(uh[HALO - 2:HALO - 1, :], (tm, FFN_TN))
        u1 = jnp.where(row == 0, p1, pltpu.roll(u, 1, 0))
        u2 = jnp.where(row == 0, p2, jnp.where(row == 1, p1, pltpu.roll(u, 2, 0)))
        cw = cw_ref[j]
        uc = cb_ref[j] + cw[0:1, :] * u2
        uc = uc + cw[1:2, :] * u1
        uc = uc + cw[2:3, :] * u
        act = (jax.nn.gelu(uc) * gate).astype(BF16)
        acc_ref[...] += _dot(act, wd_ref[j])
        return 0

    lax.fori_loop(0, wu_ref.shape[0], body, 0)
    o_ref[...] = _rms(x1_ref[...] + acc_ref[...], gfin_ref[...])


def _ffn(h2, x1, w_up, conv_w, conv_b, w_down, g_final):
    B, S, D = h2.shape
    tm, tn = FFN_TM, FFN_TN
    nj = D_FF // tn
    wu = w_up[:, :D_FF].reshape(D, nj, tn).transpose(1, 0, 2).astype(BF16)
    wg = w_up[:, D_FF:].reshape(D, nj, tn).transpose(1, 0, 2).astype(BF16)
    cw = jnp.pad(conv_w, ((0, 8 - CONV_WIDTH), (0, 0))).reshape(8, nj, tn).transpose(1, 0, 2)
    cb = conv_b.reshape(nj, 1, tn)
    wd = w_down.reshape(nj, tn, D).astype(BF16)
    gfin = g_final.reshape(1, D)
    full = lambda a: pl.BlockSpec(a.shape, lambda b, i: (0,) * a.ndim, pipeline_mode=pl.Buffered(1))
    tile = pl.BlockSpec((None, tm, D), lambda b, i: (b, i, 0))
    halo = pl.BlockSpec((None, HALO, D), lambda b, i: (b, jnp.maximum(i * (tm // HALO) - 1, 0), 0))
    return pl.pallas_call(
        _ffn_kernel,
        grid=(B, S // tm),
        in_specs=[tile, halo, tile, full(wu), full(wg), full(cw), full(cb), full(wd), full(gfin)],
        out_specs=tile,
        out_shape=jax.ShapeDtypeStruct((B, S, D), F32),
        scratch_shapes=[pltpu.VMEM((tm, D), F32)],
        compiler_params=_params("parallel", "parallel"),
        name="conv_ffn",
    )(h2, h2, x1, wu, wg, cw, cb, wd, gfin)


@jax.jit
def _layer(x, g_mix, w_in, pe_cmp_k, w_cmp_k1, w_cmp_k2, pe_cmp_v, w_cmp_v1, w_cmp_v2,
           w_proj_nsa, w_proj_dil, w_out, g_ffn, w_up, conv_w, conv_b, w_down, g_final):
    B, S, D = x.shape
    depth = g_mix.shape[0]
    for l in range(depth):
        gm = g_mix[l].reshape(1, D)
        kvc, ks, kw, dil, qt, vst, vwt, gates = _in_proj(x, gm, w_in[l])
        nch = S // CMP_STRIDE
        half = NSA_GROUPS * HEAD_DIM
        kpart = kvc[:, :, :half].reshape(B, nch, CMP_STRIDE * half)
        vpart = kvc[:, :, half:].reshape(B, nch, CMP_STRIDE * half)
        kc, _ = _compress(kpart, pe_cmp_k[l], w_cmp_k1[l], w_cmp_k2[l])
        _, vct = _compress(vpart, pe_cmp_v[l], w_cmp_v1[l], w_cmp_v2[l])
        o_a = _nsa(qt, kc, vct, ks, vst, kw, vwt, gates)
        outs, lses = _dilated(dil)
        flat = lambda a: a.reshape(B * S, a.shape[-1])
        merge_cols = w_in[l][:, w_in.shape[2] - 2 * D:]
        x1, h2 = _merge(flat(x), flat(o_a), [flat(a) for a in outs], [flat(a) for a in lses],
                        gm, merge_cols, w_proj_nsa[l], w_proj_dil[l], w_out[l], g_ffn[l].reshape(1, D))
        x = _ffn(h2.reshape(B, S, D), x1.reshape(B, S, D), w_up[l], conv_w[l], conv_b[l], w_down[l], g_final)
        assert depth == 1
    return x


def kernel(x, g_mix, w_in, pe_cmp_k, w_cmp_k1, w_cmp_k2, pe_cmp_v, w_cmp_v1, w_cmp_v2, w_proj_nsa, w_proj_dil, w_out, g_ffn, w_up, conv_w, conv_b, w_down, g_final):
    return _layer(x, g_mix, w_in, pe_cmp_k, w_cmp_k1, w_cmp_k2, pe_cmp_v, w_cmp_v1, w_cmp_v2,
                  w_proj_nsa, w_proj_dil, w_out, g_ffn, w_up, conv_w, conv_b, w_down, g_final)
```

```python
import functools
import math

import numpy as np
import jax
import jax.numpy as jnp
from jax import lax
from jax.experimental import pallas as pl
from jax.experimental.pallas import tpu as pltpu

HEAD_DIM = 64
NSA_HEADS = 8
NSA_GROUPS = 2
NSA_REP = NSA_HEADS // NSA_GROUPS
CMP_BLOCK = 32
CMP_STRIDE = 16
CMP_HIDDEN = 128
SLC_BLOCK = 64
SLC_TOP = 16
NSA_WINDOW = 512
FORCE_SCORE = 1.0e4
DIL_CONFIGS = ((128, 1), (512, 4), (2048, 16))
DIL_GROUPS = 3
DIL_HEADS_PER_GROUP = 4
DIL_HEADS = DIL_GROUPS * DIL_HEADS_PER_GROUP
D_FF = 2816
CONV_WIDTH = 3
RMS_EPS = 1e-6
NEG_INF = -1e30

LANES = 128
VMEM_LIMIT_BYTES = 56 * 1024 * 1024

F32 = jnp.float32
BF16 = jnp.bfloat16
NT_DIMS = (((1,), (1,)), ((), ()))


def _alibi_slopes(n):
    return [float(2.0 ** (-8.0 * i / n)) for i in range(1, n + 1)]


def _rms(xf, g):
    ms = jnp.mean(xf * xf, axis=-1, keepdims=True)
    return xf * lax.rsqrt(ms + RMS_EPS) * g


def _dot(a, b):
    return jnp.dot(a, b, preferred_element_type=F32)


def _dot_nt(a, b):
    return lax.dot_general(a, b, NT_DIMS, preferred_element_type=F32)


def _sigmoid(z):
    return 1.0 / (1.0 + jnp.exp(-z))


def _params(*sem):
    return pltpu.CompilerParams(dimension_semantics=sem, vmem_limit_bytes=VMEM_LIMIT_BYTES)


IN_TM = 256
N_KVC = 4 * HEAD_DIM
N_KSEL = NSA_GROUPS * HEAD_DIM
N_DIL = 3 * DIL_HEADS * HEAD_DIM
T_Q = NSA_HEADS * HEAD_DIM
T_V = NSA_GROUPS * HEAD_DIM
GATE_ROWS = 16
K_AUG = 2 * HEAD_DIM
SLC_SHIFT = int(math.log2(SLC_BLOCK))
AUG_NBLK = 8
AUG_HI, AUG_LO, AUG_ONE = AUG_NBLK, AUG_NBLK + 1, AUG_NBLK + 2
V_PAD = 16
V_ROWS = HEAD_DIM + V_PAD


def _key_position_columns(pos0, rows):
    pos = pos0 + lax.broadcasted_iota(jnp.int32, (rows, HEAD_DIM), 0)
    col = lax.broadcasted_iota(jnp.int32, (rows, HEAD_DIM), 1)
    blk = jnp.bitwise_and(lax.shift_right_logical(pos, SLC_SHIFT), AUG_NBLK - 1)
    hi = lax.shift_left(lax.shift_right_logical(pos, 7), 7).astype(F32)
    lo = jnp.bitwise_and(pos, 127).astype(F32)
    c = jnp.where(col == AUG_HI, hi, jnp.where(col == AUG_LO, lo, jnp.where(col == AUG_ONE, 1.0, 0.0)))
    return jnp.where((col < AUG_NBLK) & (blk == col), 1.0, c)


def _in_proj_kernel(x_ref, g_ref, wn_ref, wt_ref,
                    kvc_ref, ks_ref, kw_ref, dil_ref, qt_ref, vst_ref, vwt_ref, gate_ref):
    tm = x_ref.shape[0]
    h = _rms(x_ref[...], g_ref[...]).astype(BF16)
    c0 = 0
    kvc_ref[...] = _dot(h, wn_ref[:, c0:c0 + N_KVC])
    c0 += N_KVC
    ks = _dot(h, wn_ref[:, c0:c0 + N_KSEL]).astype(BF16)
    c0 += N_KSEL
    kw = _dot(h, wn_ref[:, c0:c0 + N_KSEL]).astype(BF16)
    c0 += N_KSEL
    aug = _key_position_columns(pl.program_id(1) * tm, tm).astype(BF16)
    for g in range(NSA_GROUPS):
        ks_ref[g] = jnp.concatenate([ks[:, g * HEAD_DIM:(g + 1) * HEAD_DIM], aug], axis=1)
        kw_ref[g] = jnp.concatenate([kw[:, g * HEAD_DIM:(g + 1) * HEAD_DIM], aug], axis=1)
    seg = DIL_HEADS * HEAD_DIM
    for j in range(3):
        dil_ref[:, j * seg:(j + 1) * seg] = _dot(h, wn_ref[:, c0 + j * seg:c0 + (j + 1) * seg]).astype(BF16)
    r0 = 0
    qt_ref[...] = _dot_nt(wt_ref[r0:r0 + T_Q, :], h).astype(BF16)
    r0 += T_Q
    ones = jnp.where(lax.broadcasted_iota(jnp.int32, (V_PAD, tm), 0) == 0, 1.0, 0.0).astype(BF16)
    for ref in (vst_ref, vwt_ref):
        vt = _dot_nt(wt_ref[r0:r0 + T_V, :], h).astype(BF16)
        r0 += T_V
        for g in range(NSA_GROUPS):
            ref[g] = jnp.concatenate([vt[g * HEAD_DIM:(g + 1) * HEAD_DIM, :], ones], axis=0)
    gate_ref[...] = _sigmoid(_dot_nt(wt_ref[r0:r0 + NSA_GROUPS * GATE_ROWS, :], h))


def _in_proj(x, g_mix, w_in):
    B, S, D = x.shape
    scale = HEAD_DIM ** -0.5
    o_q, o_kv = 0, T_Q
    o_gate = o_kv + 6 * N_KSEL
    o_dil = o_gate + 3 * NSA_HEADS
    o_merge = o_dil + N_DIL
    kv = w_in[:, o_kv:o_gate]

    def kind(k):
        return kv[:, k * N_KSEL:(k + 1) * N_KSEL]

    dil = w_in[:, o_dil:o_merge]
    dil = jnp.concatenate([dil[:, :DIL_HEADS * HEAD_DIM] * scale, dil[:, DIL_HEADS * HEAD_DIM:]], axis=1)
    wn = jnp.concatenate([kind(0), kind(1), kind(2), kind(4), dil], axis=1).astype(BF16)
    wg = w_in[:, o_gate:o_dil].reshape(D, NSA_GROUPS, 3 * NSA_REP)
    wg = jnp.pad(wg, ((0, 0), (0, 0), (0, GATE_ROWS - 3 * NSA_REP))).reshape(D, NSA_GROUPS * GATE_ROWS)
    wt = jnp.concatenate([w_in[:, o_q:o_kv] * scale, kind(3), kind(5), wg], axis=1).T.astype(BF16)
    tm = IN_TM
    grid = (B, S // tm)
    full = lambda a: pl.BlockSpec(a.shape, lambda b, i: (0,) * a.ndim)
    k_shape = jax.ShapeDtypeStruct((B, NSA_GROUPS, S, K_AUG), BF16)
    v_shape = jax.ShapeDtypeStruct((B, NSA_GROUPS, V_ROWS, S), BF16)
    k_spec = pl.BlockSpec((None, NSA_GROUPS, tm, K_AUG), lambda b, i: (b, 0, i, 0))
    v_spec = pl.BlockSpec((None, NSA_GROUPS, V_ROWS, tm), lambda b, i: (b, 0, 0, i))
    out_shape = (
        jax.ShapeDtypeStruct((B, S, N_KVC), F32),
        k_shape,
        k_shape,
        jax.ShapeDtypeStruct((B, S, N_DIL), BF16),
        jax.ShapeDtypeStruct((B, T_Q, S), BF16),
        v_shape,
        v_shape,
        jax.ShapeDtypeStruct((B, NSA_GROUPS * GATE_ROWS, S), F32),
    )
    out_specs = (
        pl.BlockSpec((None, tm, N_KVC), lambda b, i: (b, i, 0)),
        k_spec,
        k_spec,
        pl.BlockSpec((None, tm, N_DIL), lambda b, i: (b, i, 0)),
        pl.BlockSpec((None, T_Q, tm), lambda b, i: (b, 0, i)),
        v_spec,
        v_spec,
        pl.BlockSpec((None, NSA_GROUPS * GATE_ROWS, tm), lambda b, i: (b, 0, i)),
    )
    return pl.pallas_call(
        _in_proj_kernel,
        grid=grid,
        in_specs=[pl.BlockSpec((None, tm, D), lambda b, i: (b, i, 0)), full(g_mix), full(wn), full(wt)],
        out_specs=out_specs,
        out_shape=out_shape,
        compiler_params=_params("parallel", "parallel"),
        name="in_proj",
    )(x, g_mix, wn, wt)


def _compress_kernel(a_ref, pe_ref, wlo_ref, whi_ref, w2_ref, w2t_ref, c_ref, ct_ref):
    a = a_ref[...]
    nch = a.shape[0]
    alo = (a + pe_ref[0:1, :]).astype(BF16)
    ahi = (a + pe_ref[1:2, :]).astype(BF16)
    ulo = _dot(alo, wlo_ref[...])
    uhi = _dot(ahi, whi_ref[...])
    pre = ulo + pltpu.roll(uhi, nch - 1, 0)
    hid = jax.nn.gelu(pre).astype(BF16)
    for g in range(NSA_GROUPS):
        hg = hid[:, g * CMP_HIDDEN:(g + 1) * CMP_HIDDEN]
        c_ref[g] = _dot(hg, w2_ref[...]).astype(BF16)
        ct_ref[g] = _dot_nt(w2t_ref[...], hg).astype(BF16)


def _compress(part, pe, w1, w2):
    B, nch, width = part.shape
    G, dh, hid = NSA_GROUPS, HEAD_DIM, CMP_HIDDEN
    pe_t = jnp.broadcast_to(pe.reshape(2, CMP_STRIDE, 1, dh), (2, CMP_STRIDE, G, dh)).reshape(2, width)
    w1r = w1.reshape(2, CMP_STRIDE, dh, hid)
    eye = jnp.eye(G, dtype=w1.dtype)
    wexp = jnp.einsum('hjdn,ge->hjgden', w1r, eye).reshape(2, width, G * hid).astype(BF16)
    full = lambda a: pl.BlockSpec(a.shape, lambda b: (0,) * a.ndim)
    w2b = w2.astype(BF16)
    w2t = w2.T.astype(BF16)
    return pl.pallas_call(
        _compress_kernel,
        grid=(B,),
        in_specs=[pl.BlockSpec((None, nch, width), lambda b: (b, 0, 0)), full(pe_t),
                  full(wexp[0]), full(wexp[1]), full(w2b), full(w2t)],
        out_specs=(pl.BlockSpec((None, G, nch, dh), lambda b: (b, 0, 0, 0)),
                   pl.BlockSpec((None, G, dh, nch), lambda b: (b, 0, 0, 0))),
        out_shape=(jax.ShapeDtypeStruct((B, G, nch, dh), BF16),
                   jax.ShapeDtypeStruct((B, G, dh, nch), BF16)),
        compiler_params=_params("parallel"),
        name="compress",
    )(part, pe_t, wexp[0], wexp[1], w2b, w2t)


NSA_TQ = 128
SWEEP_TK = AUG_NBLK * SLC_BLOCK


def _rowmax8(s):
    return jnp.max(s.reshape(s.shape[0] // 8, 8, s.shape[1]), axis=0)


def _nsa_kernel(qt_ref, kc_ref, vct_ref, ks_ref, vst_ref, kw_ref, vwt_ref, gate_ref, ov_ref,
                o_ref, sbt_ref, osel_ref, *, slopes):
    g = pl.program_id(1)
    i = pl.program_id(2)
    R, dh, tq, tk = NSA_REP, HEAD_DIM, NSA_TQ, SWEEP_TK
    L = R * tq
    t0 = i * tq
    nc = kc_ref.shape[0]
    ns = ov_ref.shape[0]
    n_tiles = ns // AUG_NBLK
    tile4 = lambda a: jnp.concatenate([a] * R, axis=1)

    qt = qt_ref[...]
    qs = jnp.concatenate([qt[r * dh:(r + 1) * dh, :] for r in range(R)], axis=1)
    slope = [jnp.where(g == 0, slopes[r], slopes[R + r]).astype(F32) for r in range(R)]
    slope_row = jnp.concatenate([jnp.full((1, tq), 1.0, F32) * slope[r] for r in range(R)], axis=1)
    t_row = t0 + lax.broadcasted_iota(jnp.int32, (1, tq), 1)

    r8 = lax.broadcasted_iota(jnp.int32, (AUG_NBLK, L), 0)
    alibi8 = jnp.where(r8 < 2, slope_row, jnp.where(r8 == 2, -slope_row * t0.astype(F32), 0.0))
    q_pad = jnp.zeros((K_AUG - dh - 2 * AUG_NBLK, L), BF16)
    q_plain = jnp.concatenate([qs, jnp.concatenate([jnp.zeros((AUG_NBLK, L), F32), alibi8], axis=0).astype(BF16),
                               q_pad], axis=0)

    sc = _dot(kc_ref[...], qs)
    cmp_end = lax.broadcasted_iota(jnp.int32, (nc, tq), 0) * CMP_STRIDE + (CMP_BLOCK - 1)
    d_cmp = t_row - cmp_end
    m_cmp = d_cmp >= 0
    d_cmpf = d_cmp.astype(F32)
    probs = []
    psum = jnp.zeros((nc, tq), F32)
    for r in range(R):
        s = sc[:, r * tq:(r + 1) * tq] - slope[r] * d_cmpf
        s = jnp.where(m_cmp, s, NEG_INF)
        mx = jnp.max(s, axis=0, keepdims=True)
        p = jnp.where(m_cmp, jnp.exp(s - mx), 0.0)
        den = jnp.sum(p, axis=0, keepdims=True)
        pr = p * (1.0 / jnp.maximum(den, 1e-30))
        probs.append(pr.astype(BF16))
        psum = psum + pr
    o_cmp = _dot(vct_ref[...], jnp.concatenate(probs, axis=1))

    p_hi = psum.astype(BF16)
    p_lo = (psum - p_hi.astype(F32)).astype(BF16)
    imp = _dot(ov_ref[...], p_hi) + _dot(ov_ref[...], p_lo)
    blk = lax.broadcasted_iota(jnp.int32, (ns, tq), 0)
    cur = lax.shift_right_logical(t_row, SLC_SHIFT)
    val = jnp.where((blk == cur) | (blk == 0), FORCE_SCORE, imp)
    val = jnp.where(blk <= cur, val, -1.0)
    vals = [val[8 * v:8 * v + 8, :] for v in range(ns // 8)]
    ranks = [jnp.zeros((8, tq), F32) for _ in vals]
    row8 = lax.broadcasted_iota(jnp.int32, (8, tq), 0)
    for j in range(ns):
        vj = jnp.broadcast_to(val[j:j + 1, :], (8, tq))
        for v in range(len(vals)):
            if 8 * v > j:
                ahead = vj >= vals[v]
            elif 8 * v + 7 <= j:
                ahead = vj > vals[v]
            else:
                ahead = (vj > vals[v]) | ((vj == vals[v]) & (row8 > j - 8 * v))
            ranks[v] = ranks[v] + jnp.where(ahead, 1.0, 0.0)
    for T in range(n_tiles):
        selb = jnp.where((ranks[T] < float(SLC_TOP)) & (vals[T] >= 0.0), 0.0, NEG_INF)
        sbt_ref[T] = jnp.concatenate([tile4(selb), alibi8], axis=0).astype(BF16)

    a0 = pl.multiple_of(jnp.maximum(t0 - NSA_WINDOW, 0), tq)
    d0 = pl.multiple_of(t0, tq)
    s_a = _dot(kw_ref[pl.ds(a0, NSA_WINDOW), :], q_plain)
    s_d = _dot(kw_ref[pl.ds(d0, tq), :], q_plain)
    kpos_a = a0 + lax.broadcasted_iota(jnp.int32, (NSA_WINDOW, tq), 0)
    t_a = t0 + lax.broadcasted_iota(jnp.int32, (NSA_WINDOW, tq), 1)
    bias_a = jnp.where((kpos_a < t0) & (t_a - kpos_a < NSA_WINDOW), 0.0, NEG_INF)
    causal_d = jnp.where(lax.broadcasted_iota(jnp.int32, (tq, tq), 0)
                         <= lax.broadcasted_iota(jnp.int32, (tq, tq), 1), 0.0, NEG_INF)
    s_a = s_a + tile4(bias_a)
    s_d = s_d + tile4(causal_d)
    m_w = jnp.max(jnp.maximum(_rowmax8(s_a), _rowmax8(s_d)), axis=0, keepdims=True)
    acc_w = (_dot(vwt_ref[:, pl.ds(a0, NSA_WINDOW)], jnp.exp(s_a - m_w).astype(BF16))
             + _dot(vwt_ref[:, pl.ds(d0, tq)], jnp.exp(s_d - m_w).astype(BF16)))
    o_win = acc_w[:dh] * (1.0 / acc_w[dh:dh + 1])

    td = lax.div(t0, tk)
    for c in range(n_tiles):
        @pl.when(td == c)
        def _(c=c):
            def scores(T):
                q_aug = jnp.concatenate([qs, sbt_ref[T], q_pad], axis=0)
                return _dot(ks_ref[T * tk:(T + 1) * tk, :], q_aug)

            kpos = c * tk + lax.broadcasted_iota(jnp.int32, (tk, tq), 0)
            t_q = t0 + lax.broadcasted_iota(jnp.int32, (tk, tq), 1)
            s_all = [scores(T) for T in range(c)]
            s_all.append(scores(c) + tile4(jnp.where(kpos <= t_q, 0.0, NEG_INF)))
            m8 = _rowmax8(s_all[0])
            for s in s_all[1:]:
                m8 = jnp.maximum(m8, _rowmax8(s))
            m = jnp.max(m8, axis=0, keepdims=True)
            p = jnp.concatenate([jnp.exp(s - m).astype(BF16) for s in s_all], axis=0)
            acc = _dot(vst_ref[:, 0:(c + 1) * tk], p)
            osel_ref[...] = acc[:dh] * (1.0 / acc[dh:dh + 1])

    o_sel = osel_ref[...]
    gates = gate_ref[...]
    tiles = []
    for r in range(R):
        sl = slice(r * tq, (r + 1) * tq)
        tiles.append(gates[3 * r:3 * r + 1, :] * o_cmp[:, sl]
                     + gates[3 * r + 1:3 * r + 2, :] * o_sel[:, sl]
                     + gates[3 * r + 2:3 * r + 3, :] * o_win[:, sl])
    o_ref[...] = jnp.concatenate(tiles, axis=0).T.astype(o_ref.dtype)


def _overlap_matrix(nc, ns):
    cs = np.arange(nc)[None, :] * CMP_STRIDE
    ss = np.arange(ns)[:, None] * SLC_BLOCK
    ov = np.clip(np.minimum(cs + CMP_BLOCK, ss + SLC_BLOCK) - np.maximum(cs, ss), 0, None)
    return jnp.asarray(ov.astype(np.float32) / CMP_BLOCK, dtype=BF16)


def _nsa(qt, kc, vct, ks, vst, kw, vwt, gates):
    B, _, S = qt.shape
    G, R, dh, tq = NSA_GROUPS, NSA_REP, HEAD_DIM, NSA_TQ
    nc = kc.shape[2]
    ns = S // SLC_BLOCK
    ov = _overlap_matrix(nc, ns)
    kern = functools.partial(_nsa_kernel, slopes=tuple(_alibi_slopes(NSA_HEADS)))
    assert S % SWEEP_TK == 0 and S >= NSA_WINDOW + tq
    per_bg = lambda shape: pl.BlockSpec((None, None) + shape, lambda b, g, i: (b, g, 0, 0))
    return pl.pallas_call(
        kern,
        grid=(B, G, S // tq),
        in_specs=[
            pl.BlockSpec((None, R * dh, tq), lambda b, g, i: (b, g, i)),
            per_bg((nc, dh)), per_bg((dh, nc)),
            per_bg((S, K_AUG)), per_bg((V_ROWS, S)),
            per_bg((S, K_AUG)), per_bg((V_ROWS, S)),
            pl.BlockSpec((None, GATE_ROWS, tq), lambda b, g, i: (b, g, i)),
            pl.BlockSpec(ov.shape, lambda b, g, i: (0, 0)),
        ],
        out_specs=pl.BlockSpec((None, tq, R * dh), lambda b, g, i: (b, i, g)),
        out_shape=jax.ShapeDtypeStruct((B, S, G * R * dh), BF16),
        scratch_shapes=[pltpu.VMEM((S // SWEEP_TK, 2 * AUG_NBLK, R * tq), BF16),
                        pltpu.VMEM((dh, R * tq), F32)],
        compiler_params=_params("parallel", "parallel", "arbitrary"),
        name="nsa_attention",
    )(qt, kc, vct, ks, vst, kw, vwt, gates, ov)


BAND_TQ = 128
BAND = 128


def _banded_kernel(q_ref, k_ref, v_ref, o_ref, lse_ref, *, slopes):
    i = pl.program_id(1)
    tq, dh = BAND_TQ, HEAD_DIM
    tk = tq + BAND
    k0 = pl.multiple_of(jnp.maximum(i - 1, 0) * tq, tq)
    q = q_ref[...]
    k = k_ref[pl.ds(k0, tk), :]
    v = v_ref[pl.ds(k0, tk), :]
    qpos = i * tq + lax.broadcasted_iota(jnp.int32, (tq, tk), 0)
    kpos = k0 + lax.broadcasted_iota(jnp.int32, (tq, tk), 1)
    d = qpos - kpos
    mask = (d >= 0) & (d <= BAND)
    df = d.astype(F32)
    outs, lses = [], []
    for h in range(DIL_HEADS_PER_GROUP):
        sl = slice(h * dh, (h + 1) * dh)
        s = _dot_nt(q[:, sl], k[:, sl]) - slopes[h] * df
        s = jnp.where(mask, s, NEG_INF)
        mx = jnp.max(s, axis=-1, keepdims=True)
        p = jnp.where(mask, jnp.exp(s - mx), 0.0)
        den = jnp.sum(p, axis=-1, keepdims=True)
        pr = p * (1.0 / jnp.maximum(den, 1e-30))
        outs.append(_dot(pr.astype(BF16), v[:, sl]))
        lses.append(jnp.broadcast_to(mx + jnp.log(den), (tq, dh)))
    o_ref[...] = jnp.concatenate(outs, axis=1)
    lse_ref[...] = jnp.concatenate(lses, axis=1)


def _banded(q, k, v, slopes):
    nseq, n, width = q.shape
    tq = BAND_TQ
    kern = functools.partial(_banded_kernel, slopes=tuple(slopes))
    seq = pl.BlockSpec((None, n, width), lambda s, i: (s, 0, 0))
    tile = pl.BlockSpec((None, tq, width), lambda s, i: (s, i, 0))
    return pl.pallas_call(
        kern,
        grid=(nseq, n // tq),
        in_specs=[tile, seq, seq],
        out_specs=(tile, tile),
        out_shape=(jax.ShapeDtypeStruct((nseq, n, width), F32),) * 2,
        compiler_params=_params("parallel", "arbitrary"),
        name="banded_attention",
    )(q, k, v)


def _dilated(dil):
    B, S, _ = dil.shape
    width = DIL_HEADS_PER_GROUP * HEAD_DIM
    slopes = _alibi_slopes(DIL_HEADS)
    outs, lses = [], []
    for gi, (w, r) in enumerate(DIL_CONFIGS):
        assert w // r == BAND
        n = S // r

        def classes(which):
            c0 = which * DIL_HEADS * HEAD_DIM + gi * width
            a = dil[:, :, c0:c0 + width].reshape(B, n, r, width)
            return a.transpose(0, 2, 1, 3).reshape(B * r, n, width)

        sl = [s_ * r for s_ in slopes[gi * DIL_HEADS_PER_GROUP:(gi + 1) * DIL_HEADS_PER_GROUP]]
        o, lse = _banded(classes(0), classes(1), classes(2), sl)
        back = lambda a: a.reshape(B, r, n, width).transpose(0, 2, 1, 3).reshape(B, S, width)
        outs.append(back(o))
        lses.append(back(lse))
    return outs, lses


MERGE_TM = 256


def _merge_kernel(x_ref, oa_ref, o0_ref, o1_ref, o2_ref, l0_ref, l1_ref, l2_ref,
                  gmix_ref, wm_ref, wpn_ref, wpd_ref, wo_ref, gffn_ref, x1_ref, h2_ref):
    x = x_ref[...]
    D = x.shape[1]
    h = _rms(x, gmix_ref[...]).astype(BF16)
    gm = _sigmoid(_dot(h, wm_ref[...]))
    l0, l1, l2 = l0_ref[...], l1_ref[...], l2_ref[...]
    mx = jnp.maximum(jnp.maximum(l0, l1), l2)
    e0, e1, e2 = jnp.exp(l0 - mx), jnp.exp(l1 - mx), jnp.exp(l2 - mx)
    inv = 1.0 / (e0 + e1 + e2)
    ob = o0_ref[...] * (e0 * inv) + o1_ref[...] * (e1 * inv) + o2_ref[...] * (e2 * inv)
    a = _dot(oa_ref[...], wpn_ref[...])
    d = _dot(ob.astype(BF16), wpd_ref[...])
    mixed = gm[:, :D] * a + gm[:, D:] * d
    x1 = x + _dot(mixed.astype(BF16), wo_ref[...])
    x1_ref[...] = x1
    h2_ref[...] = _rms(x1, gffn_ref[...]).astype(BF16)


def _merge(x2, oa2, outs, lses, g_mix, w_merge, w_proj_nsa, w_proj_dil, w_out, g_ffn):
    T, D = x2.shape
    tm = MERGE_TM
    row = lambda a: pl.BlockSpec((tm, a.shape[1]), lambda i: (i, 0))
    full = lambda a: pl.BlockSpec(a.shape, lambda i: (0,) * a.ndim)
    ws = [w_merge.astype(BF16), w_proj_nsa.astype(BF16), w_proj_dil.astype(BF16), w_out.astype(BF16)]
    args = [x2, oa2, *outs, *lses, g_mix, *ws, g_ffn]
    in_specs = [row(a) for a in args[:8]] + [full(a) for a in args[8:]]
    return pl.pallas_call(
        _merge_kernel,
        grid=(T // tm,),
        in_specs=in_specs,
        out_specs=(pl.BlockSpec((tm, D), lambda i: (i, 0)),) * 2,
        out_shape=(jax.ShapeDtypeStruct((T, D), F32), jax.ShapeDtypeStruct((T, D), BF16)),
        compiler_params=_params("parallel"),
        name="merge_proj",
    )(*args)


FFN_TM = 256
FFN_TN = 256
HALO = 16


def _ffn_kernel(h_ref, halo_ref, x1_ref, wu_ref, wg_ref, cw_ref, cb_ref, wd_ref, gfin_ref, o_ref, acc_ref):
    i = pl.program_id(1)
    h = h_ref[...]
    halo = halo_ref[...]
    tm = h.shape[0]
    row = lax.broadcasted_iota(jnp.int32, (tm, FFN_TN), 0)
    live = (i > 0).astype(F32)
    acc_ref[...] = jnp.zeros_like(acc_ref)

    def body(j, _):
        wu = wu_ref[j]
        u = _dot(h, wu)
        uh = _dot(halo, wu) * live
        gate = _dot(h, wg_ref[j])
        p1 = jnp.broadcast_to(uh[HALO - 1:HALO, :], (tm, FFN_TN))
        p2 = jnp.broadcast_to(uh[HALO - 2:HALO - 1, :], (tm, FFN_TN))
        u1 = jnp.where(row == 0, p1, pltpu.roll(u, 1, 0))
        u2 = jnp.where(row == 0, p2, jnp.where(row == 1, p1, pltpu.roll(u, 2, 0)))
        cw = cw_ref[j]
        uc = cb_ref[j] + cw[0:1, :] * u2
        uc = uc + cw[1:2, :] * u1
        uc = uc + cw[2:3, :] * u
        act = (jax.nn.gelu(uc) * gate).astype(BF16)
        acc_ref[...] += _dot(act, wd_ref[j])
        return 0

    lax.fori_loop(0, wu_ref.shape[0], body, 0)
    o_ref[...] = _rms(x1_ref[...] + acc_ref[...], gfin_ref[...])


def _ffn(h2, x1, w_up, conv_w, conv_b, w_down, g_final):
    B, S, D = h2.shape
    tm, tn = FFN_TM, FFN_TN
    nj = D_FF // tn
    wu = w_up[:, :D_FF].reshape(D, nj, tn).transpose(1, 0, 2).astype(BF16)
    wg = w_up[:, D_FF:].reshape(D, nj, tn).transpose(1, 0, 2).astype(BF16)
    cw = jnp.pad(conv_w, ((0, 8 - CONV_WIDTH), (0, 0))).reshape(8, nj, tn).transpose(1, 0, 2)
    cb = conv_b.reshape(nj, 1, tn)
    wd = w_down.reshape(nj, tn, D).astype(BF16)
    gfin = g_final.reshape(1, D)
    full = lambda a: pl.BlockSpec(a.shape, lambda b, i: (0,) * a.ndim, pipeline_mode=pl.Buffered(1))
    tile = pl.BlockSpec((None, tm, D), lambda b, i: (b, i, 0))
    halo = pl.BlockSpec((None, HALO, D), lambda b, i: (b, jnp.maximum(i * (tm // HALO) - 1, 0), 0))
    return pl.pallas_call(
        _ffn_kernel,
        grid=(B, S // tm),
        in_specs=[tile, halo, tile, full(wu), full(wg), full(cw), full(cb), full(wd), full(gfin)],
        out_specs=tile,
        out_shape=jax.ShapeDtypeStruct((B, S, D), F32),
        scratch_shapes=[pltpu.VMEM((tm, D), F32)],
        compiler_params=_params("parallel", "parallel"),
        name="conv_ffn",
    )(h2, h2, x1, wu, wg, cw, cb, wd, gfin)


@jax.jit
def _layer(x, g_mix, w_in, pe_cmp_k, w_cmp_k1, w_cmp_k2, pe_cmp_v, w_cmp_v1, w_cmp_v2,
           w_proj_nsa, w_proj_dil, w_out, g_ffn, w_up, conv_w, conv_b, w_down, g_final):
    B, S, D = x.shape
    depth = g_mix.shape[0]
    for l in range(depth):
        gm = g_mix[l].reshape(1, D)
        kvc, ks, kw, dil, qt, vst, vwt, gates = _in_proj(x, gm, w_in[l])
        nch = S // CMP_STRIDE
        half = NSA_GROUPS * HEAD_DIM
        kpart = kvc[:, :, :half].reshape(B, nch, CMP_STRIDE * half)
        vpart = kvc[:, :, half:].reshape(B, nch, CMP_STRIDE * half)
        kc, _ = _compress(kpart, pe_cmp_k[l], w_cmp_k1[l], w_cmp_k2[l])
        _, vct = _compress(vpart, pe_cmp_v[l], w_cmp_v1[l], w_cmp_v2[l])
        o_a = _nsa(qt, kc, vct, ks, vst, kw, vwt, gates)
        outs, lses = _dilated(dil)
        flat = lambda a: a.reshape(B * S, a.shape[-1])
        merge_cols = w_in[l][:, w_in.shape[2] - 2 * D:]
        x1, h2 = _merge(flat(x), flat(o_a), [flat(a) for a in outs], [flat(a) for a in lses],
                        gm, merge_cols, w_proj_nsa[l], w_proj_dil[l], w_out[l], g_ffn[l].reshape(1, D))
        x = _ffn(h2.reshape(B, S, D), x1.reshape(B, S, D), w_up[l], conv_w[l], conv_b[l], w_down[l], g_final)
        assert depth == 1
    return x


def kernel(x, g_mix, w_in, pe_cmp_k, w_cmp_k1, w_cmp_k2, pe_cmp_v, w_cmp_v1, w_cmp_v2, w_proj_nsa, w_proj_dil, w_out, g_ffn, w_up, conv_w, conv_b, w_down, g_final):
    return _layer(x, g_mix, w_in, pe_cmp_k, w_cmp_k1, w_cmp_k2, pe_cmp_v, w_cmp_v1, w_cmp_v2,
                  w_proj_nsa, w_proj_dil, w_out, g_ffn, w_up, conv_w, conv_b, w_down, g_final)
```

```python
import functools
import math

import numpy as np
import jax
import jax.numpy as jnp
from jax import lax
from jax.experimental import pallas as pl
from jax.experimental.pallas import tpu as pltpu

HEAD_DIM = 64
NSA_HEADS = 8
NSA_GROUPS = 2
NSA_REP = NSA_HEADS // NSA_GROUPS
CMP_BLOCK = 32
CMP_STRIDE = 16
CMP_HIDDEN = 128
SLC_BLOCK = 64
SLC_TOP = 16
NSA_WINDOW = 512
FORCE_SCORE = 1.0e4
DIL_CONFIGS = ((128, 1), (512, 4), (2048, 16))
DIL_GROUPS = 3
DIL_HEADS_PER_GROUP = 4
DIL_HEADS = DIL_GROUPS * DIL_HEADS_PER_GROUP
D_FF = 2816
CONV_WIDTH = 3
RMS_EPS = 1e-6
NEG_INF = -1e30

LANES = 128
VMEM_LIMIT_BYTES = 56 * 1024 * 1024

F32 = jnp.float32
BF16 = jnp.bfloat16
NT_DIMS = (((1,), (1,)), ((), ()))


def _alibi_slopes(n):
    return [float(2.0 ** (-8.0 * i / n)) for i in range(1, n + 1)]


def _rms(xf, g):
    ms = jnp.mean(xf * xf, axis=-1, keepdims=True)
    return xf * lax.rsqrt(ms + RMS_EPS) * g


def _dot(a, b):
    return jnp.dot(a, b, preferred_element_type=F32)


def _dot_nt(a, b):
    return lax.dot_general(a, b, NT_DIMS, preferred_element_type=F32)


def _sigmoid(z):
    return 1.0 / (1.0 + jnp.exp(-z))


def _params(*sem):
    return pltpu.CompilerParams(dimension_semantics=sem, vmem_limit_bytes=VMEM_LIMIT_BYTES)


IN_TM = 512
N_KVC = 4 * HEAD_DIM
N_KSEL = NSA_GROUPS * HEAD_DIM
N_DIL = 3 * DIL_HEADS * HEAD_DIM
DIL_WIDTH = DIL_HEADS_PER_GROUP * HEAD_DIM
T_Q = NSA_HEADS * HEAD_DIM
T_V = NSA_GROUPS * HEAD_DIM
GATE_ROWS = 16
K_AUG = 2 * HEAD_DIM
SLC_SHIFT = int(math.log2(SLC_BLOCK))
AUG_NBLK = 8
AUG_HI, AUG_LO, AUG_ONE = AUG_NBLK, AUG_NBLK + 1, AUG_NBLK + 2
V_PAD = 16
V_ROWS = HEAD_DIM + V_PAD


def _key_position_columns(pos0, rows):
    pos = pos0 + lax.broadcasted_iota(jnp.int32, (rows, HEAD_DIM), 0)
    col = lax.broadcasted_iota(jnp.int32, (rows, HEAD_DIM), 1)
    blk = jnp.bitwise_and(lax.shift_right_logical(pos, SLC_SHIFT), AUG_NBLK - 1)
    hi = lax.shift_left(lax.shift_right_logical(pos, 7), 7).astype(F32)
    lo = jnp.bitwise_and(pos, 127).astype(F32)
    c = jnp.where(col == AUG_HI, hi, jnp.where(col == AUG_LO, lo, jnp.where(col == AUG_ONE, 1.0, 0.0)))
    return jnp.where((col < AUG_NBLK) & (blk == col), 1.0, c)


def _in_proj_kernel(x_ref, g_ref, wn_ref, wt_ref,
                    kcmp_ref, vcmp_ref, ks_ref, kw_ref, d0_ref, d1_ref, d2_ref, qt_ref, vst_ref, vwt_ref, gate_ref,
                    slab_ref):
    tm = x_ref.shape[0]
    h = _rms(x_ref[...], g_ref[...]).astype(BF16)
    c0 = 0
    kvc = _dot(h, wn_ref[:, c0:c0 + N_KVC])
    kcmp_ref[...] = kvc[:, :N_KSEL]
    vcmp_ref[...] = kvc[:, N_KSEL:]
    c0 += N_KVC
    ks = _dot(h, wn_ref[:, c0:c0 + N_KSEL]).astype(BF16)
    c0 += N_KSEL
    kw = _dot(h, wn_ref[:, c0:c0 + N_KSEL]).astype(BF16)
    c0 += N_KSEL
    aug = _key_position_columns(pl.program_id(1) * tm, tm).astype(BF16)
    for g in range(NSA_GROUPS):
        ks_ref[g] = jnp.concatenate([ks[:, g * HEAD_DIM:(g + 1) * HEAD_DIM], aug], axis=1)
        kw_ref[g] = jnp.concatenate([kw[:, g * HEAD_DIM:(g + 1) * HEAD_DIM], aug], axis=1)
    seg = DIL_HEADS * HEAD_DIM
    for which in range(3):
        y = _dot(h, wn_ref[:, c0 + which * seg:c0 + (which + 1) * seg])
        for gi, (d_ref, (_, r)) in enumerate(zip((d0_ref, d1_ref, d2_ref), DIL_CONFIGS)):
            yg = y[:, gi * DIL_WIDTH:(gi + 1) * DIL_WIDTH]
            if r == 1:
                d_ref[which, 0] = yg.astype(BF16)
                continue
            for s in range(DIL_WIDTH // LANES):
                slab_ref[s] = yg[:, s * LANES:(s + 1) * LANES]
            for c in range(r):
                d_ref[which, c] = jnp.concatenate(
                    [slab_ref[s, pl.ds(c, tm // r, stride=r), :] for s in range(DIL_WIDTH // LANES)],
                    axis=1).astype(BF16)
    r0 = 0
    qt_ref[...] = _dot_nt(wt_ref[r0:r0 + T_Q, :], h).astype(BF16)
    r0 += T_Q
    ones = jnp.where(lax.broadcasted_iota(jnp.int32, (V_PAD, tm), 0) == 0, 1.0, 0.0).astype(BF16)
    for ref in (vst_ref, vwt_ref):
        vt = _dot_nt(wt_ref[r0:r0 + T_V, :], h).astype(BF16)
        r0 += T_V
        for g in range(NSA_GROUPS):
            ref[g] = jnp.concatenate([vt[g * HEAD_DIM:(g + 1) * HEAD_DIM, :], ones], axis=0)
    gate_ref[...] = _sigmoid(_dot_nt(wt_ref[r0:r0 + NSA_GROUPS * GATE_ROWS, :], h))


def _in_proj(x, g_mix, w_in):
    B, S, D = x.shape
    scale = HEAD_DIM ** -0.5
    o_q, o_kv = 0, T_Q
    o_gate = o_kv + 6 * N_KSEL
    o_dil = o_gate + 3 * NSA_HEADS
    o_merge = o_dil + N_DIL
    kv = w_in[:, o_kv:o_gate]

    def kind(k):
        return kv[:, k * N_KSEL:(k + 1) * N_KSEL]

    dil = w_in[:, o_dil:o_merge]
    dil = jnp.concatenate([dil[:, :DIL_HEADS * HEAD_DIM] * scale, dil[:, DIL_HEADS * HEAD_DIM:]], axis=1)
    wn = jnp.concatenate([kind(0), kind(1), kind(2), kind(4), dil], axis=1).astype(BF16)
    wg = w_in[:, o_gate:o_dil].reshape(D, NSA_GROUPS, 3 * NSA_REP)
    wg = jnp.pad(wg, ((0, 0), (0, 0), (0, GATE_ROWS - 3 * NSA_REP))).reshape(D, NSA_GROUPS * GATE_ROWS)
    wt = jnp.concatenate([w_in[:, o_q:o_kv] * scale, kind(3), kind(5), wg], axis=1).T.astype(BF16)
    tm = IN_TM
    grid = (B, S // tm)
    full = lambda a: pl.BlockSpec(a.shape, lambda b, i: (0,) * a.ndim)
    k_shape = jax.ShapeDtypeStruct((B, NSA_GROUPS, S, K_AUG), BF16)
    v_shape = jax.ShapeDtypeStruct((B, NSA_GROUPS, V_ROWS, S), BF16)
    k_spec = pl.BlockSpec((None, NSA_GROUPS, tm, K_AUG), lambda b, i: (b, 0, i, 0))
    v_spec = pl.BlockSpec((None, NSA_GROUPS, V_ROWS, tm), lambda b, i: (b, 0, 0, i))
    c_shape = jax.ShapeDtypeStruct((B, S, N_KSEL), F32)
    c_spec = pl.BlockSpec((None, tm, N_KSEL), lambda b, i: (b, i, 0))
    d_shapes = tuple(jax.ShapeDtypeStruct((B, 3, r, S // r, DIL_WIDTH), BF16) for _, r in DIL_CONFIGS)
    d_specs = tuple(pl.BlockSpec((None, 3, r, tm // r, DIL_WIDTH), lambda b, i: (b, 0, 0, i, 0))
                    for _, r in DIL_CONFIGS)
    out_shape = (
        c_shape, c_shape, k_shape, k_shape, *d_shapes,
        jax.ShapeDtypeStruct((B, T_Q, S), BF16),
        v_shape,
        v_shape,
        jax.ShapeDtypeStruct((B, NSA_GROUPS * GATE_ROWS, S), F32),
    )
    out_specs = (
        c_spec, c_spec, k_spec, k_spec, *d_specs,
        pl.BlockSpec((None, T_Q, tm), lambda b, i: (b, 0, i)),
        v_spec,
        v_spec,
        pl.BlockSpec((None, NSA_GROUPS * GATE_ROWS, tm), lambda b, i: (b, 0, i)),
    )
    return pl.pallas_call(
        _in_proj_kernel,
        grid=grid,
        in_specs=[pl.BlockSpec((None, tm, D), lambda b, i: (b, i, 0)), full(g_mix), full(wn), full(wt)],
        out_specs=out_specs,
        out_shape=out_shape,
        scratch_shapes=[pltpu.VMEM((DIL_WIDTH // LANES, tm, LANES), F32)],
        compiler_params=_params("parallel", "parallel"),
        name="in_proj",
    )(x, g_mix, wn, wt)


def _compress_kernel(x_ref, pe_ref, w1_ref, w2_ref, w2t_ref, c_ref, ct_ref):
    nch = x_ref.shape[0] // CMP_STRIDE
    ulo = jnp.zeros((nch, NSA_GROUPS * CMP_HIDDEN), F32)
    uhi = jnp.zeros((nch, NSA_GROUPS * CMP_HIDDEN), F32)
    for j in range(CMP_STRIDE):
        xj = x_ref[pl.ds(j, nch, stride=CMP_STRIDE), :]
        ulo = ulo + _dot((xj + pe_ref[j:j + 1, :]).astype(BF16), w1_ref[j])
        uhi = uhi + _dot((xj + pe_ref[CMP_STRIDE + j:CMP_STRIDE + j + 1, :]).astype(BF16), w1_ref[CMP_STRIDE + j])
    pre = ulo + pltpu.roll(uhi, nch - 1, 0)
    hid = jax.nn.gelu(pre).astype(BF16)
    for g in range(NSA_GROUPS):
        hg = hid[:, g * CMP_HIDDEN:(g + 1) * CMP_HIDDEN]
        c_ref[g] = _dot(hg, w2_ref[...]).astype(BF16)
        ct_ref[g] = _dot_nt(w2t_ref[...], hg).astype(BF16)


def _compress(xc, pe, w1, w2):
    B, S, width = xc.shape
    nch = S // CMP_STRIDE
    G, dh, hid = NSA_GROUPS, HEAD_DIM, CMP_HIDDEN
    pe_t = jnp.broadcast_to(pe.reshape(CMP_BLOCK, 1, dh), (CMP_BLOCK, G, dh)).reshape(CMP_BLOCK, width)
    eye = jnp.eye(G, dtype=w1.dtype)
    wexp = jnp.einsum('pdn,ge->pgden', w1.reshape(CMP_BLOCK, dh, hid), eye).reshape(CMP_BLOCK, width, G * hid)
    wexp = wexp.astype(BF16)
    full = lambda a: pl.BlockSpec(a.shape, lambda b: (0,) * a.ndim)
    w2b = w2.astype(BF16)
    w2t = w2.T.astype(BF16)
    return pl.pallas_call(
        _compress_kernel,
        grid=(B,),
        in_specs=[pl.BlockSpec((None, S, width), lambda b: (b, 0, 0)), full(pe_t), full(wexp), full(w2b), full(w2t)],
        out_specs=(pl.BlockSpec((None, G, nch, dh), lambda b: (b, 0, 0, 0)),
                   pl.BlockSpec((None, G, dh, nch), lambda b: (b, 0, 0, 0))),
        out_shape=(jax.ShapeDtypeStruct((B, G, nch, dh), BF16),
                   jax.ShapeDtypeStruct((B, G, dh, nch), BF16)),
        compiler_params=_params("parallel"),
        name="compress",
    )(xc, pe_t, wexp, w2b, w2t)


NSA_TQ = 128
SWEEP_TK = AUG_NBLK * SLC_BLOCK


def _rowmax8(s):
    return jnp.max(s.reshape(s.shape[0] // 8, 8, s.shape[1]), axis=0)


def _nsa_kernel(qt_ref, kc_ref, vct_ref, ks_ref, vst_ref, kw_ref, vwt_ref, gate_ref, ov_ref,
                o_ref, sbt_ref, osel_ref, *, slopes):
    g = pl.program_id(1)
    i = pl.program_id(2)
    R, dh, tq, tk = NSA_REP, HEAD_DIM, NSA_TQ, SWEEP_TK
    L = R * tq
    t0 = i * tq
    nc = kc_ref.shape[0]
    ns = ov_ref.shape[0]
    n_tiles = ns // AUG_NBLK
    tile4 = lambda a: jnp.concatenate([a] * R, axis=1)

    qt = qt_ref[...]
    qs = jnp.concatenate([qt[r * dh:(r + 1) * dh, :] for r in range(R)], axis=1)
    slope = [jnp.where(g == 0, slopes[r], slopes[R + r]).astype(F32) for r in range(R)]
    slope_row = jnp.concatenate([jnp.full((1, tq), 1.0, F32) * slope[r] for r in range(R)], axis=1)
    t_row = t0 + lax.broadcasted_iota(jnp.int32, (1, tq), 1)

    r8 = lax.broadcasted_iota(jnp.int32, (AUG_NBLK, L), 0)
    alibi8 = jnp.where(r8 < 2, slope_row, jnp.where(r8 == 2, -slope_row * t0.astype(F32), 0.0))
    q_pad = jnp.zeros((K_AUG - dh - 2 * AUG_NBLK, L), BF16)
    q_plain = jnp.concatenate([qs, jnp.concatenate([jnp.zeros((AUG_NBLK, L), F32), alibi8], axis=0).astype(BF16),
                               q_pad], axis=0)

    sc = _dot(kc_ref[...], qs)
    cmp_end = lax.broadcasted_iota(jnp.int32, (nc, tq), 0) * CMP_STRIDE + (CMP_BLOCK - 1)
    d_cmp = t_row - cmp_end
    m_cmp = d_cmp >= 0
    d_cmpf = d_cmp.astype(F32)
    probs = []
    psum = jnp.zeros((nc, tq), F32)
    for r in range(R):
        s = sc[:, r * tq:(r + 1) * tq] - slope[r] * d_cmpf
        s = jnp.where(m_cmp, s, NEG_INF)
        mx = jnp.max(s, axis=0, keepdims=True)
        p = jnp.where(m_cmp, jnp.exp(s - mx), 0.0)
        den = jnp.sum(p, axis=0, keepdims=True)
        pr = p * (1.0 / jnp.maximum(den, 1e-30))
        probs.append(pr.astype(BF16))
        psum = psum + pr
    o_cmp = _dot(vct_ref[...], jnp.concatenate(probs, axis=1))

    p_hi = psum.astype(BF16)
    p_lo = (psum - p_hi.astype(F32)).astype(BF16)
    imp = _dot(ov_ref[...], p_hi) + _dot(ov_ref[...], p_lo)
    blk = lax.broadcasted_iota(jnp.int32, (ns, tq), 0)
    cur = lax.shift_right_logical(t_row, SLC_SHIFT)
    val = jnp.where((blk == cur) | (blk == 0), FORCE_SCORE, imp)
    val = jnp.where(blk <= cur, val, -1.0)
    vals = [val[8 * v:8 * v + 8, :] for v in range(ns // 8)]
    ranks = [jnp.zeros((8, tq), F32) for _ in vals]
    row8 = lax.broadcasted_iota(jnp.int32, (8, tq), 0)
    for j in range(ns):
        vj = jnp.broadcast_to(val[j:j + 1, :], (8, tq))
        for v in range(len(vals)):
            if 8 * v > j:
                ahead = vj >= vals[v]
            elif 8 * v + 7 <= j:
                ahead = vj > vals[v]
            else:
                ahead = (vj > vals[v]) | ((vj == vals[v]) & (row8 > j - 8 * v))
            ranks[v] = ranks[v] + jnp.where(ahead, 1.0, 0.0)
    for T in range(n_tiles):
        selb = jnp.where((ranks[T] < float(SLC_TOP)) & (vals[T] >= 0.0), 0.0, NEG_INF)
        sbt_ref[T] = jnp.concatenate([tile4(selb), alibi8], axis=0).astype(BF16)

    a0 = pl.multiple_of(jnp.maximum(t0 - NSA_WINDOW, 0), tq)
    d0 = pl.multiple_of(t0, tq)
    s_a = _dot(kw_ref[pl.ds(a0, NSA_WINDOW), :], q_plain)
    s_d = _dot(kw_ref[pl.ds(d0, tq), :], q_plain)
    kpos_a = a0 + lax.broadcasted_iota(jnp.int32, (NSA_WINDOW, tq), 0)
    t_a = t0 + lax.broadcasted_iota(jnp.int32, (NSA_WINDOW, tq), 1)
    bias_a = jnp.where((kpos_a < t0) & (t_a - kpos_a < NSA_WINDOW), 0.0, NEG_INF)
    causal_d = jnp.where(lax.broadcasted_iota(jnp.int32, (tq, tq), 0)
                         <= lax.broadcasted_iota(jnp.int32, (tq, tq), 1), 0.0, NEG_INF)
    s_a = s_a + tile4(bias_a)
    s_d = s_d + tile4(causal_d)
    m_w = jnp.max(jnp.maximum(_rowmax8(s_a), _rowmax8(s_d)), axis=0, keepdims=True)
    acc_w = (_dot(vwt_ref[:, pl.ds(a0, NSA_WINDOW)], jnp.exp(s_a - m_w).astype(BF16))
             + _dot(vwt_ref[:, pl.ds(d0, tq)], jnp.exp(s_d - m_w).astype(BF16)))
    o_win = acc_w[:dh] * (1.0 / acc_w[dh:dh + 1])

    td = lax.div(t0, tk)
    for c in range(n_tiles):
        @pl.when(td == c)
        def _(c=c):
            def scores(T):
                q_aug = jnp.concatenate([qs, sbt_ref[T], q_pad], axis=0)
                return _dot(ks_ref[T * tk:(T + 1) * tk, :], q_aug)

            kpos = c * tk + lax.broadcasted_iota(jnp.int32, (tk, tq), 0)
            t_q = t0 + lax.broadcasted_iota(jnp.int32, (tk, tq), 1)
            s_all = [scores(T) for T in range(c)]
            s_all.append(scores(c) + tile4(jnp.where(kpos <= t_q, 0.0, NEG_INF)))
            m8 = _rowmax8(s_all[0])
            for s in s_all[1:]:
                m8 = jnp.maximum(m8, _rowmax8(s))
            m = jnp.max(m8, axis=0, keepdims=True)
            p = jnp.concatenate([jnp.exp(s - m).astype(BF16) for s in s_all], axis=0)
            acc = _dot(vst_ref[:, 0:(c + 1) * tk], p)
            osel_ref[...] = acc[:dh] * (1.0 / acc[dh:dh + 1])

    o_sel = osel_ref[...]
    gates = gate_ref[...]
    tiles = []
    for r in range(R):
        sl = slice(r * tq, (r + 1) * tq)
        tiles.append(gates[3 * r:3 * r + 1, :] * o_cmp[:, sl]
                     + gates[3 * r + 1:3 * r + 2, :] * o_sel[:, sl]
                     + gates[3 * r + 2:3 * r + 3, :] * o_win[:, sl])
    o_ref[...] = jnp.concatenate(tiles, axis=0).T.astype(o_ref.dtype)


def _overlap_matrix(nc, ns):
    cs = np.arange(nc)[None, :] * CMP_STRIDE
    ss = np.arange(ns)[:, None] * SLC_BLOCK
    ov = np.clip(np.minimum(cs + CMP_BLOCK, ss + SLC_BLOCK) - np.maximum(cs, ss), 0, None)
    return jnp.asarray(ov.astype(np.float32) / CMP_BLOCK, dtype=BF16)


def _nsa(qt, kc, vct, ks, vst, kw, vwt, gates):
    B, _, S = qt.shape
    G, R, dh, tq = NSA_GROUPS, NSA_REP, HEAD_DIM, NSA_TQ
    nc = kc.shape[2]
    ns = S // SLC_BLOCK
    ov = _overlap_matrix(nc, ns)
    kern = functools.partial(_nsa_kernel, slopes=tuple(_alibi_slopes(NSA_HEADS)))
    assert S % SWEEP_TK == 0 and S >= NSA_WINDOW + tq
    per_bg = lambda shape: pl.BlockSpec((None, None) + shape, lambda b, g, i: (b, g, 0, 0))
    return pl.pallas_call(
        kern,
        grid=(B, G, S // tq),
        in_specs=[
            pl.BlockSpec((None, R * dh, tq), lambda b, g, i: (b, g, i)),
            per_bg((nc, dh)), per_bg((dh, nc)),
            per_bg((S, K_AUG)), per_bg((V_ROWS, S)),
            per_bg((S, K_AUG)), per_bg((V_ROWS, S)),
            pl.BlockSpec((None, GATE_ROWS, tq), lambda b, g, i: (b, g, i)),
            pl.BlockSpec(ov.shape, lambda b, g, i: (0, 0)),
        ],
        out_specs=pl.BlockSpec((None, tq, R * dh), lambda b, g, i: (b, i, g)),
        out_shape=jax.ShapeDtypeStruct((B, S, G * R * dh), BF16),
        scratch_shapes=[pltpu.VMEM((S // SWEEP_TK, 2 * AUG_NBLK, R * tq), BF16),
                        pltpu.VMEM((dh, R * tq), F32)],
        compiler_params=_params("parallel", "parallel", "arbitrary"),
        name="nsa_attention",
    )(qt, kc, vct, ks, vst, kw, vwt, gates, ov)


BAND_TQ = 128
BAND = 128


def _banded_kernel(q_ref, k_ref, v_ref, o_ref, lse_ref, *, slopes):
    i = pl.program_id(2)
    tq, dh = BAND_TQ, HEAD_DIM
    tk = tq + BAND
    k0 = pl.multiple_of(jnp.maximum(i - 1, 0) * tq, tq)
    q = q_ref[...]
    k = k_ref[pl.ds(k0, tk), :]
    v = v_ref[pl.ds(k0, tk), :]
    qpos = i * tq + lax.broadcasted_iota(jnp.int32, (tq, tk), 0)
    kpos = k0 + lax.broadcasted_iota(jnp.int32, (tq, tk), 1)
    d = qpos - kpos
    mask_bias = jnp.where((d >= 0) & (d <= BAND), 0.0, NEG_INF)
    neg_d = -d.astype(F32)
    first = lax.broadcasted_iota(jnp.int32, (tq, LANES), 1) < dh
    ones = jnp.ones((tk, LANES), BF16)
    outs, lses = [], []
    for pair in range(DIL_HEADS_PER_GROUP * dh // LANES):
        sl = slice(pair * LANES, (pair + 1) * LANES)
        qp, kp, vp = q[:, sl], k[:, sl], v[:, sl]
        res = []
        for half in range(LANES // dh):
            mine = first if half == 0 else jnp.logical_not(first)
            s = _dot_nt(jnp.where(mine, qp, jnp.zeros_like(qp)), kp)
            s = s + (slopes[pair * (LANES // dh) + half] * neg_d + mask_bias)
            mx = jnp.max(s, axis=-1, keepdims=True)
            p = jnp.exp(s - mx).astype(BF16)
            den = _dot(p, ones)
            res.append((_dot(p, vp) * (1.0 / den), mx + jnp.log(den)))
        outs.append(jnp.where(first, res[0][0], res[1][0]))
        lses.append(jnp.where(first, res[0][1], res[1][1]))
    o_ref[...] = jnp.concatenate(outs, axis=1)
    lse_ref[...] = jnp.concatenate(lses, axis=1)


def _banded(d, slopes):
    B, _, r, n, width = d.shape
    tq = BAND_TQ
    assert n >= tq + BAND
    kern = functools.partial(_banded_kernel, slopes=tuple(slopes))
    tile = pl.BlockSpec((None, None, None, tq, width), lambda b, c, i: (b, 0, c, i, 0))
    kseq = pl.BlockSpec((None, None, None, n, width), lambda b, c, i: (b, 1, c, 0, 0))
    vseq = pl.BlockSpec((None, None, None, n, width), lambda b, c, i: (b, 2, c, 0, 0))
    out = pl.BlockSpec((None, None, tq, width), lambda b, c, i: (b, c, i, 0))
    return pl.pallas_call(
        kern,
        grid=(B, r, n // tq),
        in_specs=[tile, kseq, vseq],
        out_specs=(out, out),
        out_shape=(jax.ShapeDtypeStruct((B, r, n, width), F32),) * 2,
        compiler_params=_params("parallel", "parallel", "arbitrary"),
        name="banded_attention",
    )(d, d, d)


def _dilated(dils):
    slopes = _alibi_slopes(DIL_HEADS)
    outs, lses = [], []
    for gi, (w, r) in enumerate(DIL_CONFIGS):
        assert w // r == BAND
        sl = [s_ * r for s_ in slopes[gi * DIL_HEADS_PER_GROUP:(gi + 1) * DIL_HEADS_PER_GROUP]]
        o, lse = _banded(dils[gi], sl)
        outs.append(o)
        lses.append(lse)
    return outs, lses


MERGE_TM = 512


def _token_order(ref, slab_ref):
    r, rows, width = ref.shape
    if r == 1:
        return ref[0]
    for c in range(r):
        blk = ref[c]
        for s in range(width // LANES):
            slab_ref[s, pl.ds(c, rows, stride=r), :] = blk[:, s * LANES:(s + 1) * LANES]
    return jnp.concatenate([slab_ref[s] for s in range(width // LANES)], axis=1)


def _merge_kernel(x_ref, oa_ref, o0_ref, o1_ref, o2_ref, l0_ref, l1_ref, l2_ref,
                  gmix_ref, wm_ref, wpn_ref, wpd_ref, wo_ref, gffn_ref, x1_ref, h2_ref, slab_ref):
    x = x_ref[...]
    D = x.shape[1]
    h = _rms(x, gmix_ref[...]).astype(BF16)
    gm = _sigmoid(_dot(h, wm_ref[...]))
    o0, o1, o2 = [_token_order(r_, slab_ref) for r_ in (o0_ref, o1_ref, o2_ref)]
    l0, l1, l2 = [_token_order(r_, slab_ref) for r_ in (l0_ref, l1_ref, l2_ref)]
    mx = jnp.maximum(jnp.maximum(l0, l1), l2)
    e0, e1, e2 = jnp.exp(l0 - mx), jnp.exp(l1 - mx), jnp.exp(l2 - mx)
    inv = 1.0 / (e0 + e1 + e2)
    ob = o0 * (e0 * inv) + o1 * (e1 * inv) + o2 * (e2 * inv)
    a = _dot(oa_ref[...], wpn_ref[...])
    d = _dot(ob.astype(BF16), wpd_ref[...])
    mixed = gm[:, :D] * a + gm[:, D:] * d
    x1 = x + _dot(mixed.astype(BF16), wo_ref[...])
    x1_ref[...] = x1
    h2_ref[...] = _rms(x1, gffn_ref[...]).astype(BF16)


def _merge(x, o_a, outs, lses, g_mix, w_merge, w_proj_nsa, w_proj_dil, w_out, g_ffn):
    B, S, D = x.shape
    tm = MERGE_TM
    row = lambda a: pl.BlockSpec((None, tm, a.shape[2]), lambda b, i: (b, i, 0))
    cls = lambda a: pl.BlockSpec((None, a.shape[1], tm // a.shape[1], a.shape[3]), lambda b, i: (b, 0, i, 0))
    full = lambda a: pl.BlockSpec(a.shape, lambda b, i: (0,) * a.ndim)
    ws = [w_merge.astype(BF16), w_proj_nsa.astype(BF16), w_proj_dil.astype(BF16), w_out.astype(BF16)]
    consts = [g_mix, *ws, g_ffn]
    in_specs = [row(x), row(o_a)] + [cls(a) for a in (*outs, *lses)] + [full(a) for a in consts]
    return pl.pallas_call(
        _merge_kernel,
        grid=(B, S // tm),
        in_specs=in_specs,
        out_specs=(pl.BlockSpec((None, tm, D), lambda b, i: (b, i, 0)),) * 2,
        out_shape=(jax.ShapeDtypeStruct((B, S, D), F32), jax.ShapeDtypeStruct((B, S, D), BF16)),
        scratch_shapes=[pltpu.VMEM((DIL_WIDTH // LANES, tm, LANES), F32)],
        compiler_params=_params("parallel", "parallel"),
        name="merge_proj",
    )(x, o_a, *outs, *lses, *consts)


FFN_TM = 512
FFN_TN = 256
HALO = 16


def _ffn_kernel(h_ref, halo_ref, x1_ref, wu_ref, wg_ref, cw_ref, cb_ref, wd_ref, gfin_ref, o_ref, act_ref):
    i = pl.program_id(1)
    h = h_ref[...]
    halo = halo_ref[...]
    tm = h.shape[0]
    row = lax.broadcasted_iota(jnp.int32, (tm, FFN_TN), 0)
    live = (i > 0).astype(F32)
    for j in range(wu_ref.shape[0]):
        wu = wu_ref[j]
        u = _dot(h, wu)
        uh = _dot(halo, wu) * live
        gate = _dot(h, wg_ref[j])
        p1 = jnp.broadcast_to(uh[HALO - 1:HALO, :], (tm, FFN_TN))
        p2 = jnp.broadcast_to(uh[HALO - 2:HALO - 1, :], (tm, FFN_TN))
        u1 = jnp.where(row == 0, p1, pltpu.roll(u, 1, 0))
        u2 = jnp.where(row == 0, p2, jnp.where(row == 1, p1, pltpu.roll(u, 2, 0)))
        cw = cw_ref[j]
        uc = cb_ref[j] + cw[0:1, :] * u2
        uc = uc + cw[1:2, :] * u1
        uc = uc + cw[2:3, :] * u
        act_ref[:, j * FFN_TN:(j + 1) * FFN_TN] = (jax.nn.gelu(uc) * gate).astype(BF16)
    y = _dot(act_ref[...], wd_ref[...])
    o_ref[...] = _rms(x1_ref[...] + y, gfin_ref[...])


def _ffn(h2, x1, w_up, conv_w, conv_b, w_down, g_final):
    B, S, D = h2.shape
    tm, tn = FFN_TM, FFN_TN
    nj = D_FF // tn
    wu = w_up[:, :D_FF].reshape(D, nj, tn).transpose(1, 0, 2).astype(BF16)
    wg = w_up[:, D_FF:].reshape(D, nj, tn).transpose(1, 0, 2).astype(BF16)
    cw = jnp.pad(conv_w, ((0, 8 - CONV_WIDTH), (0, 0))).reshape(8, nj, tn).transpose(1, 0, 2)
    cb = conv_b.reshape(nj, 1, tn)
    wd = w_down.astype(BF16)
    gfin = g_final.reshape(1, D)
    full = lambda a: pl.BlockSpec(a.shape, lambda b, i: (0,) * a.ndim, pipeline_mode=pl.Buffered(1))
    tile = pl.BlockSpec((None, tm, D), lambda b, i: (b, i, 0))
    halo = pl.BlockSpec((None, HALO, D), lambda b, i: (b, jnp.maximum(i * (tm // HALO) - 1, 0), 0))
    return pl.pallas_call(
        _ffn_kernel,
        grid=(B, S // tm),
        in_specs=[tile, halo, tile, full(wu), full(wg), full(cw), full(cb), full(wd), full(gfin)],
        out_specs=tile,
        out_shape=jax.ShapeDtypeStruct((B, S, D), F32),
        scratch_shapes=[pltpu.VMEM((tm, D_FF), BF16)],
        compiler_params=_params("parallel", "parallel"),
        name="conv_ffn",
    )(h2, h2, x1, wu, wg, cw, cb, wd, gfin)


@jax.jit
def _layer(x, g_mix, w_in, pe_cmp_k, w_cmp_k1, w_cmp_k2, pe_cmp_v, w_cmp_v1, w_cmp_v2,
           w_proj_nsa, w_proj_dil, w_out, g_ffn, w_up, conv_w, conv_b, w_down, g_final):
    B, S, D = x.shape
    depth = g_mix.shape[0]
    for l in range(depth):
        gm = g_mix[l].reshape(1, D)
        kcmp, vcmp, ks, kw, d0, d1, d2, qt, vst, vwt, gates = _in_proj(x, gm, w_in[l])
        kc, _ = _compress(kcmp, pe_cmp_k[l], w_cmp_k1[l], w_cmp_k2[l])
        _, vct = _compress(vcmp, pe_cmp_v[l], w_cmp_v1[l], w_cmp_v2[l])
        o_a = _nsa(qt, kc, vct, ks, vst, kw, vwt, gates)
        outs, lses = _dilated((d0, d1, d2))
        merge_cols = w_in[l][:, w_in.shape[2] - 2 * D:]
        x1, h2 = _merge(x, o_a, outs, lses, gm, merge_cols, w_proj_nsa[l], w_proj_dil[l], w_out[l],
                        g_ffn[l].reshape(1, D))
        x = _ffn(h2, x1, w_up[l], conv_w[l], conv_b[l], w_down[l], g_final)
        assert depth == 1
    return x


def kernel(x, g_mix, w_in, pe_cmp_k, w_cmp_k1, w_cmp_k2, pe_cmp_v, w_cmp_v1, w_cmp_v2, w_proj_nsa, w_proj_dil, w_out, g_ffn, w_up, conv_w, conv_b, w_down, g_final):
    return _layer(x, g_mix, w_in, pe_cmp_k, w_cmp_k1, w_cmp_k2, pe_cmp_v, w_cmp_v1, w_cmp_v2,
                  w_proj_nsa, w_proj_dil, w_out, g_ffn, w_up, conv_w, conv_b, w_down, g_final)
```

```python
import functools
import math

import numpy as np
import jax
import jax.numpy as jnp
from jax import lax
from jax.experimental import pallas as pl
from jax.experimental.pallas import tpu as pltpu

HEAD_DIM = 64
NSA_HEADS = 8
NSA_GROUPS = 2
NSA_REP = NSA_HEADS // NSA_GROUPS
CMP_BLOCK = 32
CMP_STRIDE = 16
CMP_HIDDEN = 128
SLC_BLOCK = 64
SLC_TOP = 16
NSA_WINDOW = 512
FORCE_SCORE = 1.0e4
DIL_CONFIGS = ((128, 1), (512, 4), (2048, 16))
DIL_GROUPS = 3
DIL_HEADS_PER_GROUP = 4
DIL_HEADS = DIL_GROUPS * DIL_HEADS_PER_GROUP
D_FF = 2816
CONV_WIDTH = 3
RMS_EPS = 1e-6
NEG_INF = -1e30

LANES = 128
VMEM_LIMIT_BYTES = 56 * 1024 * 1024

F32 = jnp.float32
BF16 = jnp.bfloat16
NT_DIMS = (((1,), (1,)), ((), ()))


def _alibi_slopes(n):
    return [float(2.0 ** (-8.0 * i / n)) for i in range(1, n + 1)]


def _rms(xf, g):
    ms = jnp.mean(xf * xf, axis=-1, keepdims=True)
    return xf * lax.rsqrt(ms + RMS_EPS) * g


def _dot(a, b):
    return jnp.dot(a, b, preferred_element_type=F32)


def _dot_nt(a, b):
    return lax.dot_general(a, b, NT_DIMS, preferred_element_type=F32)


def _sigmoid(z):
    return 1.0 / (1.0 + jnp.exp(-z))


def _params(*sem):
    return pltpu.CompilerParams(dimension_semantics=sem, vmem_limit_bytes=VMEM_LIMIT_BYTES)


IN_TM = 512
N_KVC = 4 * HEAD_DIM
N_KSEL = NSA_GROUPS * HEAD_DIM
N_DIL = 3 * DIL_HEADS * HEAD_DIM
DIL_WIDTH = DIL_HEADS_PER_GROUP * HEAD_DIM
T_Q = NSA_HEADS * HEAD_DIM
T_V = NSA_GROUPS * HEAD_DIM
GATE_ROWS = 16
K_AUG = 2 * HEAD_DIM
SLC_SHIFT = int(math.log2(SLC_BLOCK))
AUG_NBLK = 8
AUG_HI, AUG_LO, AUG_ONE = AUG_NBLK, AUG_NBLK + 1, AUG_NBLK + 2
V_PAD = 16
V_ROWS = HEAD_DIM + V_PAD


def _key_position_columns(pos0, rows, step=1):
    pos = pos0 + step * lax.broadcasted_iota(jnp.int32, (rows, HEAD_DIM), 0)
    col = lax.broadcasted_iota(jnp.int32, (rows, HEAD_DIM), 1)
    blk = jnp.bitwise_and(lax.shift_right_logical(pos, SLC_SHIFT), AUG_NBLK - 1)
    hi = lax.shift_left(lax.shift_right_logical(pos, 7), 7).astype(F32)
    lo = jnp.bitwise_and(pos, 127).astype(F32)
    c = jnp.where(col == AUG_HI, hi, jnp.where(col == AUG_LO, lo, jnp.where(col == AUG_ONE, 1.0, 0.0)))
    return jnp.where((col < AUG_NBLK) & (blk == col), 1.0, c)


def _in_proj_kernel(x_ref, g_ref, wn_ref, wt_ref,
                    kcmp_ref, vcmp_ref, ks_ref, kw_ref, d0_ref, d1_ref, d2_ref, qt_ref, vst_ref, vwt_ref, gate_ref,
                    slab_ref):
    tm = x_ref.shape[0]
    h = _rms(x_ref[...], g_ref[...]).astype(BF16)
    c0 = 0
    kvc = _dot(h, wn_ref[:, c0:c0 + N_KVC])
    kcmp_ref[...] = kvc[:, :N_KSEL]
    vcmp_ref[...] = kvc[:, N_KSEL:]
    c0 += N_KVC
    ks = _dot(h, wn_ref[:, c0:c0 + N_KSEL]).astype(BF16)
    c0 += N_KSEL
    kw = _dot(h, wn_ref[:, c0:c0 + N_KSEL]).astype(BF16)
    c0 += N_KSEL
    aug = _key_position_columns(pl.program_id(1) * tm, tm).astype(BF16)
    for g in range(NSA_GROUPS):
        ks_ref[g] = jnp.concatenate([ks[:, g * HEAD_DIM:(g + 1) * HEAD_DIM], aug], axis=1)
        kw_ref[g] = jnp.concatenate([kw[:, g * HEAD_DIM:(g + 1) * HEAD_DIM], aug], axis=1)
    seg = DIL_HEADS * HEAD_DIM
    for which in range(3):
        y = _dot(h, wn_ref[:, c0 + which * seg:c0 + (which + 1) * seg])
        for gi, (d_ref, (_, r)) in enumerate(zip((d0_ref, d1_ref, d2_ref), DIL_CONFIGS)):
            yg = y[:, gi * DIL_WIDTH:(gi + 1) * DIL_WIDTH]
            if r == 1:
                d_ref[which, 0] = yg.astype(BF16)
                continue
            for s in range(DIL_WIDTH // LANES):
                slab_ref[s] = yg[:, s * LANES:(s + 1) * LANES]
            for c in range(r):
                d_ref[which, c] = jnp.concatenate(
                    [slab_ref[s, pl.ds(c, tm // r, stride=r), :] for s in range(DIL_WIDTH // LANES)],
                    axis=1).astype(BF16)
    r0 = 0
    qt_ref[...] = _dot_nt(wt_ref[r0:r0 + T_Q, :], h).astype(BF16)
    r0 += T_Q
    ones = jnp.where(lax.broadcasted_iota(jnp.int32, (V_PAD, tm), 0) == 0, 1.0, 0.0).astype(BF16)
    for ref in (vst_ref, vwt_ref):
        vt = _dot_nt(wt_ref[r0:r0 + T_V, :], h).astype(BF16)
        r0 += T_V
        for g in range(NSA_GROUPS):
            ref[g] = jnp.concatenate([vt[g * HEAD_DIM:(g + 1) * HEAD_DIM, :], ones], axis=0)
    gate_ref[...] = _sigmoid(_dot_nt(wt_ref[r0:r0 + NSA_GROUPS * GATE_ROWS, :], h))


def _in_proj(x, g_mix, w_in):
    B, S, D = x.shape
    scale = HEAD_DIM ** -0.5
    o_q, o_kv = 0, T_Q
    o_gate = o_kv + 6 * N_KSEL
    o_dil = o_gate + 3 * NSA_HEADS
    o_merge = o_dil + N_DIL
    kv = w_in[:, o_kv:o_gate]

    def kind(k):
        return kv[:, k * N_KSEL:(k + 1) * N_KSEL]

    dil = w_in[:, o_dil:o_merge]
    dil = jnp.concatenate([dil[:, :DIL_HEADS * HEAD_DIM] * scale, dil[:, DIL_HEADS * HEAD_DIM:]], axis=1)
    wn = jnp.concatenate([kind(0), kind(1), kind(2), kind(4), dil], axis=1).astype(BF16)
    wg = w_in[:, o_gate:o_dil].reshape(D, NSA_GROUPS, 3 * NSA_REP)
    wg = jnp.pad(wg, ((0, 0), (0, 0), (0, GATE_ROWS - 3 * NSA_REP))).reshape(D, NSA_GROUPS * GATE_ROWS)
    wt = jnp.concatenate([w_in[:, o_q:o_kv] * scale, kind(3), kind(5), wg], axis=1).T.astype(BF16)
    tm = IN_TM
    grid = (B, S // tm)
    full = lambda a: pl.BlockSpec(a.shape, lambda b, i: (0,) * a.ndim)
    k_shape = jax.ShapeDtypeStruct((B, NSA_GROUPS, S, K_AUG), BF16)
    v_shape = jax.ShapeDtypeStruct((B, NSA_GROUPS, V_ROWS, S), BF16)
    k_spec = pl.BlockSpec((None, NSA_GROUPS, tm, K_AUG), lambda b, i: (b, 0, i, 0))
    v_spec = pl.BlockSpec((None, NSA_GROUPS, V_ROWS, tm), lambda b, i: (b, 0, 0, i))
    c_shape = jax.ShapeDtypeStruct((B, S, N_KSEL), F32)
    c_spec = pl.BlockSpec((None, tm, N_KSEL), lambda b, i: (b, i, 0))
    d_shapes = tuple(jax.ShapeDtypeStruct((B, 3, r, S // r, DIL_WIDTH), BF16) for _, r in DIL_CONFIGS)
    d_specs = tuple(pl.BlockSpec((None, 3, r, tm // r, DIL_WIDTH), lambda b, i: (b, 0, 0, i, 0))
                    for _, r in DIL_CONFIGS)
    out_shape = (
        c_shape, c_shape, k_shape, k_shape, *d_shapes,
        jax.ShapeDtypeStruct((B, T_Q, S), BF16),
        v_shape,
        v_shape,
        jax.ShapeDtypeStruct((B, NSA_GROUPS * GATE_ROWS, S), F32),
    )
    out_specs = (
        c_spec, c_spec, k_spec, k_spec, *d_specs,
        pl.BlockSpec((None, T_Q, tm), lambda b, i: (b, 0, i)),
        v_spec,
        v_spec,
        pl.BlockSpec((None, NSA_GROUPS * GATE_ROWS, tm), lambda b, i: (b, 0, i)),
    )
    return pl.pallas_call(
        _in_proj_kernel,
        grid=grid,
        in_specs=[pl.BlockSpec((None, tm, D), lambda b, i: (b, i, 0)), full(g_mix), full(wn), full(wt)],
        out_specs=out_specs,
        out_shape=out_shape,
        scratch_shapes=[pltpu.VMEM((DIL_WIDTH // LANES, tm, LANES), F32)],
        compiler_params=_params("parallel", "parallel"),
        name="in_proj",
    )(x, g_mix, wn, wt)


def _compress_kernel(x_ref, pe_ref, w1_ref, w2_ref, w2t_ref, c_ref, ct_ref):
    nch = x_ref.shape[0] // CMP_STRIDE
    ulo = jnp.zeros((nch, NSA_GROUPS * CMP_HIDDEN), F32)
    uhi = jnp.zeros((nch, NSA_GROUPS * CMP_HIDDEN), F32)
    for j in range(CMP_STRIDE):
        xj = x_ref[pl.ds(j, nch, stride=CMP_STRIDE), :]
        ulo = ulo + _dot((xj + pe_ref[j:j + 1, :]).astype(BF16), w1_ref[j])
        uhi = uhi + _dot((xj + pe_ref[CMP_STRIDE + j:CMP_STRIDE + j + 1, :]).astype(BF16), w1_ref[CMP_STRIDE + j])
    pre = ulo + pltpu.roll(uhi, nch - 1, 0)
    hid = jax.nn.gelu(pre).astype(BF16)
    aug = _key_position_columns(CMP_BLOCK - 1, nch, CMP_STRIDE).astype(BF16)
    for g in range(NSA_GROUPS):
        hg = hid[:, g * CMP_HIDDEN:(g + 1) * CMP_HIDDEN]
        c_ref[g] = jnp.concatenate([_dot(hg, w2_ref[...]).astype(BF16), aug], axis=1)
        ct_ref[g] = _dot_nt(w2t_ref[...], hg).astype(BF16)


def _compress(xc, pe, w1, w2):
    B, S, width = xc.shape
    nch = S // CMP_STRIDE
    G, dh, hid = NSA_GROUPS, HEAD_DIM, CMP_HIDDEN
    pe_t = jnp.broadcast_to(pe.reshape(CMP_BLOCK, 1, dh), (CMP_BLOCK, G, dh)).reshape(CMP_BLOCK, width)
    eye = jnp.eye(G, dtype=w1.dtype)
    wexp = jnp.einsum('pdn,ge->pgden', w1.reshape(CMP_BLOCK, dh, hid), eye).reshape(CMP_BLOCK, width, G * hid)
    wexp = wexp.astype(BF16)
    full = lambda a: pl.BlockSpec(a.shape, lambda b: (0,) * a.ndim)
    w2b = w2.astype(BF16)
    w2t = w2.T.astype(BF16)
    return pl.pallas_call(
        _compress_kernel,
        grid=(B,),
        in_specs=[pl.BlockSpec((None, S, width), lambda b: (b, 0, 0)), full(pe_t), full(wexp), full(w2b), full(w2t)],
        out_specs=(pl.BlockSpec((None, G, nch, K_AUG), lambda b: (b, 0, 0, 0)),
                   pl.BlockSpec((None, G, dh, nch), lambda b: (b, 0, 0, 0))),
        out_shape=(jax.ShapeDtypeStruct((B, G, nch, K_AUG), BF16),
                   jax.ShapeDtypeStruct((B, G, dh, nch), BF16)),
        compiler_params=_params("parallel"),
        name="compress",
    )(xc, pe_t, wexp, w2b, w2t)


NSA_TQ = 128
SWEEP_TK = AUG_NBLK * SLC_BLOCK


def _rowmax8(s):
    return jnp.max(s.reshape(s.shape[0] // 8, 8, s.shape[1]), axis=0)


def _nsa_kernel(qt_ref, kc_ref, vct_ref, ks_ref, vst_ref, kw_ref, vwt_ref, gate_ref, ov_ref,
                o_ref, sbt_ref, osel_ref, *, slopes):
    g = pl.program_id(1)
    i = pl.program_id(2)
    R, dh, tq, tk = NSA_REP, HEAD_DIM, NSA_TQ, SWEEP_TK
    L = R * tq
    t0 = i * tq
    nc = kc_ref.shape[0]
    ns = ov_ref.shape[0]
    n_tiles = ns // AUG_NBLK
    tile4 = lambda a: jnp.concatenate([a] * R, axis=1)

    qt = qt_ref[...]
    qs = jnp.concatenate([qt[r * dh:(r + 1) * dh, :] for r in range(R)], axis=1)
    slope = [jnp.where(g == 0, slopes[r], slopes[R + r]).astype(F32) for r in range(R)]
    slope_row = jnp.concatenate([jnp.full((1, tq), 1.0, F32) * slope[r] for r in range(R)], axis=1)
    t_row = t0 + lax.broadcasted_iota(jnp.int32, (1, tq), 1)

    r8 = lax.broadcasted_iota(jnp.int32, (AUG_NBLK, L), 0)
    alibi8 = jnp.where(r8 < 2, slope_row, jnp.where(r8 == 2, -slope_row * t0.astype(F32), 0.0))
    q_pad = jnp.zeros((K_AUG - dh - 2 * AUG_NBLK, L), BF16)
    q_plain = jnp.concatenate([qs, jnp.concatenate([jnp.zeros((AUG_NBLK, L), F32), alibi8], axis=0).astype(BF16),
                               q_pad], axis=0)

    cmp_end = lax.broadcasted_iota(jnp.int32, (nc, tq), 0) * CMP_STRIDE + (CMP_BLOCK - 1)
    m_cmp = t_row >= cmp_end
    sc = _dot(kc_ref[...], q_plain) + tile4(jnp.where(m_cmp, 0.0, NEG_INF))
    mx = jnp.max(_rowmax8(sc), axis=0, keepdims=True)
    p = jnp.exp(sc - mx) * tile4(jnp.where(m_cmp, 1.0, 0.0))
    den = jnp.sum(jnp.sum(p.reshape(nc // 8, 8, L), axis=0), axis=0, keepdims=True)
    pr = p * (1.0 / jnp.maximum(den, 1e-30))
    o_cmp = _dot(vct_ref[...], pr.astype(BF16))
    psum = pr[:, 0:tq]
    for r in range(1, R):
        psum = psum + pr[:, r * tq:(r + 1) * tq]

    p_hi = psum.astype(BF16)
    p_lo = (psum - p_hi.astype(F32)).astype(BF16)
    imp = _dot(ov_ref[...], p_hi) + _dot(ov_ref[...], p_lo)
    blk = lax.broadcasted_iota(jnp.int32, (ns, tq), 0)
    cur = lax.shift_right_logical(t_row, SLC_SHIFT)
    val = jnp.where((blk == cur) | (blk == 0), FORCE_SCORE, imp)
    val = jnp.where(blk <= cur, val, -1.0)
    vals = [val[8 * v:8 * v + 8, :] for v in range(ns // 8)]
    ranks = [jnp.zeros((8, tq), F32) for _ in vals]
    row8 = lax.broadcasted_iota(jnp.int32, (8, tq), 0)
    for j in range(ns):
        vj = jnp.broadcast_to(val[j:j + 1, :], (8, tq))
        for v in range(len(vals)):
            if 8 * v > j:
                ahead = vj >= vals[v]
            elif 8 * v + 7 <= j:
                ahead = vj > vals[v]
            else:
                ahead = (vj > vals[v]) | ((vj == vals[v]) & (row8 > j - 8 * v))
            ranks[v] = ranks[v] + jnp.where(ahead, 1.0, 0.0)
    for T in range(n_tiles):
        selb = jnp.where((ranks[T] < float(SLC_TOP)) & (vals[T] >= 0.0), 0.0, NEG_INF)
        sbt_ref[T] = jnp.concatenate([tile4(selb), alibi8], axis=0).astype(BF16)

    a0 = pl.multiple_of(jnp.maximum(t0 - NSA_WINDOW, 0), tq)
    d0 = pl.multiple_of(t0, tq)
    s_a = _dot(kw_ref[pl.ds(a0, NSA_WINDOW), :], q_plain)
    s_d = _dot(kw_ref[pl.ds(d0, tq), :], q_plain)
    kpos_a = a0 + lax.broadcasted_iota(jnp.int32, (NSA_WINDOW, tq), 0)
    t_a = t0 + lax.broadcasted_iota(jnp.int32, (NSA_WINDOW, tq), 1)
    bias_a = jnp.where((kpos_a < t0) & (t_a - kpos_a < NSA_WINDOW), 0.0, NEG_INF)
    causal_d = jnp.where(lax.broadcasted_iota(jnp.int32, (tq, tq), 0)
                         <= lax.broadcasted_iota(jnp.int32, (tq, tq), 1), 0.0, NEG_INF)
    s_a = s_a + tile4(bias_a)
    s_d = s_d + tile4(causal_d)
    m_w = jnp.max(jnp.maximum(_rowmax8(s_a), _rowmax8(s_d)), axis=0, keepdims=True)
    acc_w = (_dot(vwt_ref[:, pl.ds(a0, NSA_WINDOW)], jnp.exp(s_a - m_w).astype(BF16))
             + _dot(vwt_ref[:, pl.ds(d0, tq)], jnp.exp(s_d - m_w).astype(BF16)))
    o_win = acc_w[:dh] * (1.0 / acc_w[dh:dh + 1])

    td = lax.div(t0, tk)
    for c in range(n_tiles):
        @pl.when(td == c)
        def _(c=c):
            def scores(T):
                q_aug = jnp.concatenate([qs, sbt_ref[T], q_pad], axis=0)
                return _dot(ks_ref[T * tk:(T + 1) * tk, :], q_aug)

            kpos = c * tk + lax.broadcasted_iota(jnp.int32, (tk, tq), 0)
            t_q = t0 + lax.broadcasted_iota(jnp.int32, (tk, tq), 1)
            s_all = [scores(T) for T in range(c)]
            s_all.append(scores(c) + tile4(jnp.where(kpos <= t_q, 0.0, NEG_INF)))
            m8 = _rowmax8(s_all[0])
            for s in s_all[1:]:
                m8 = jnp.maximum(m8, _rowmax8(s))
            m = jnp.max(m8, axis=0, keepdims=True)
            p = jnp.concatenate([jnp.exp(s - m).astype(BF16) for s in s_all], axis=0)
            acc = _dot(vst_ref[:, 0:(c + 1) * tk], p)
            osel_ref[...] = acc[:dh] * (1.0 / acc[dh:dh + 1])

    o_sel = osel_ref[...]
    gates = gate_ref[...]
    tiles = []
    for r in range(R):
        sl = slice(r * tq, (r + 1) * tq)
        tiles.append(gates[3 * r:3 * r + 1, :] * o_cmp[:, sl]
                     + gates[3 * r + 1:3 * r + 2, :] * o_sel[:, sl]
                     + gates[3 * r + 2:3 * r + 3, :] * o_win[:, sl])
    o_ref[...] = jnp.concatenate(tiles, axis=0).T.astype(o_ref.dtype)


def _overlap_matrix(nc, ns):
    cs = np.arange(nc)[None, :] * CMP_STRIDE
    ss = np.arange(ns)[:, None] * SLC_BLOCK
    ov = np.clip(np.minimum(cs + CMP_BLOCK, ss + SLC_BLOCK) - np.maximum(cs, ss), 0, None)
    return jnp.asarray(ov.astype(np.float32) / CMP_BLOCK, dtype=BF16)


def _nsa(qt, kc, vct, ks, vst, kw, vwt, gates):
    B, _, S = qt.shape
    G, R, dh, tq = NSA_GROUPS, NSA_REP, HEAD_DIM, NSA_TQ
    nc = kc.shape[2]
    ns = S // SLC_BLOCK
    ov = _overlap_matrix(nc, ns)
    kern = functools.partial(_nsa_kernel, slopes=tuple(_alibi_slopes(NSA_HEADS)))
    assert S % SWEEP_TK == 0 and S >= NSA_WINDOW + tq
    per_bg = lambda shape: pl.BlockSpec((None, None) + shape, lambda b, g, i: (b, g, 0, 0))
    return pl.pallas_call(
        kern,
        grid=(B, G, S // tq),
        in_specs=[
            pl.BlockSpec((None, R * dh, tq), lambda b, g, i: (b, g, i)),
            per_bg((nc, K_AUG)), per_bg((dh, nc)),
            per_bg((S, K_AUG)), per_bg((V_ROWS, S)),
            per_bg((S, K_AUG)), per_bg((V_ROWS, S)),
            pl.BlockSpec((None, GATE_ROWS, tq), lambda b, g, i: (b, g, i)),
            pl.BlockSpec(ov.shape, lambda b, g, i: (0, 0)),
        ],
        out_specs=pl.BlockSpec((None, tq, R * dh), lambda b, g, i: (b, i, g)),
        out_shape=jax.ShapeDtypeStruct((B, S, G * R * dh), BF16),
        scratch_shapes=[pltpu.VMEM((S // SWEEP_TK, 2 * AUG_NBLK, R * tq), BF16),
                        pltpu.VMEM((dh, R * tq), F32)],
        compiler_params=_params("parallel", "parallel", "arbitrary"),
        name="nsa_attention",
    )(qt, kc, vct, ks, vst, kw, vwt, gates, ov)


BAND_TQ = 128
BAND_SUB = 2
BAND = 128


def _banded_kernel(q_ref, k_ref, v_ref, o_ref, lse_ref, *, slopes):
    tq, dh, nh = BAND_TQ, HEAD_DIM, DIL_HEADS_PER_GROUP
    tk = tq + BAND
    width = nh * dh
    head_of_lane = lambda rows: lax.shift_right_logical(
        lax.broadcasted_iota(jnp.int32, (rows, width), 1), int(math.log2(dh)))
    lane_head, q_head = head_of_lane(tk), head_of_lane(tq)
    keep = [jnp.where(lane_head == h, 1.0, 0.0).astype(BF16) for h in range(nh)]

    def per_head(a):
        return jnp.concatenate([a * keep[h] for h in range(nh)], axis=0)

    ones_h = jnp.concatenate(keep, axis=0)
    for sub in range(BAND_SUB):
        i = pl.program_id(2) * BAND_SUB + sub
        k0 = pl.multiple_of(jnp.maximum(i - 1, 0) * tq, tq)
        q = q_ref[sub * tq:(sub + 1) * tq, :]
        k = k_ref[pl.ds(k0, tk), :]
        v = v_ref[pl.ds(k0, tk), :]
        d = (i * tq - k0) + (lax.broadcasted_iota(jnp.int32, (tq, tk), 0)
                             - lax.broadcasted_iota(jnp.int32, (tq, tk), 1))
        mask_bias = jnp.where((d >= 0) & (d <= BAND), 0.0, NEG_INF)
        neg_d = -d.astype(F32)
        s = _dot_nt(q, per_head(k))
        ps, mxs = [], []
        for h in range(nh):
            sh = s[:, h * tk:(h + 1) * tk] + (slopes[h] * neg_d + mask_bias)
            mx = jnp.max(sh, axis=-1, keepdims=True)
            ps.append(jnp.exp(sh - mx).astype(BF16))
            mxs.append(mx)
        p = jnp.concatenate(ps, axis=1)
        den = _dot(p, ones_h)
        mx_all = mxs[nh - 1]
        for h in range(nh - 2, -1, -1):
            mx_all = jnp.where(q_head == h, mxs[h], mx_all)
        o_ref[sub * tq:(sub + 1) * tq, :] = _dot(p, per_head(v)) * (1.0 / den)
        lse_ref[sub * tq:(sub + 1) * tq, :] = mx_all + jnp.log(den)


def _banded(d, slopes):
    B, _, r, n, width = d.shape
    tq = BAND_TQ * BAND_SUB
    assert n >= BAND_TQ + BAND and n % tq == 0
    kern = functools.partial(_banded_kernel, slopes=tuple(slopes))
    tile = pl.BlockSpec((None, None, None, tq, width), lambda b, c, i: (b, 0, c, i, 0))
    kseq = pl.BlockSpec((None, None, None, n, width), lambda b, c, i: (b, 1, c, 0, 0))
    vseq = pl.BlockSpec((None, None, None, n, width), lambda b, c, i: (b, 2, c, 0, 0))
    out = pl.BlockSpec((None, None, tq, width), lambda b, c, i: (b, c, i, 0))
    return pl.pallas_call(
        kern,
        grid=(B, r, n // tq),
        in_specs=[tile, kseq, vseq],
        out_specs=(out, out),
        out_shape=(jax.ShapeDtypeStruct((B, r, n, width), F32),) * 2,
        compiler_params=_params("parallel", "parallel", "arbitrary"),
        name="banded_attention",
    )(d, d, d)


def _dilated(dils):
    slopes = _alibi_slopes(DIL_HEADS)
    outs, lses = [], []
    for gi, (w, r) in enumerate(DIL_CONFIGS):
        assert w // r == BAND
        sl = [s_ * r for s_ in slopes[gi * DIL_HEADS_PER_GROUP:(gi + 1) * DIL_HEADS_PER_GROUP]]
        o, lse = _banded(dils[gi], sl)
        outs.append(o)
        lses.append(lse)
    return outs, lses


MERGE_TM = 512


def _token_order(ref, slab_ref):
    r, rows, width = ref.shape
    if r == 1:
        return ref[0]
    for c in range(r):
        blk = ref[c]
        for s in range(width // LANES):
            slab_ref[s, pl.ds(c, rows, stride=r), :] = blk[:, s * LANES:(s + 1) * LANES]
    return jnp.concatenate([slab_ref[s] for s in range(width // LANES)], axis=1)


def _merge_kernel(x_ref, oa_ref, o0_ref, o1_ref, o2_ref, l0_ref, l1_ref, l2_ref,
                  gmix_ref, wm_ref, wpn_ref, wpd_ref, wo_ref, gffn_ref, x1_ref, h2_ref, slab_ref):
    x = x_ref[...]
    D = x.shape[1]
    h = _rms(x, gmix_ref[...]).astype(BF16)
    gm = _sigmoid(_dot(h, wm_ref[...]))
    o0, o1, o2 = [_token_order(r_, slab_ref) for r_ in (o0_ref, o1_ref, o2_ref)]
    l0, l1, l2 = [_token_order(r_, slab_ref) for r_ in (l0_ref, l1_ref, l2_ref)]
    mx = jnp.maximum(jnp.maximum(l0, l1), l2)
    e0, e1, e2 = jnp.exp(l0 - mx), jnp.exp(l1 - mx), jnp.exp(l2 - mx)
    inv = 1.0 / (e0 + e1 + e2)
    ob = o0 * (e0 * inv) + o1 * (e1 * inv) + o2 * (e2 * inv)
    a = _dot(oa_ref[...], wpn_ref[...])
    d = _dot(ob.astype(BF16), wpd_ref[...])
    mixed = gm[:, :D] * a + gm[:, D:] * d
    x1 = x + _dot(mixed.astype(BF16), wo_ref[...])
    x1_ref[...] = x1
    h2_ref[...] = _rms(x1, gffn_ref[...]).astype(BF16)


def _merge(x, o_a, outs, lses, g_mix, w_merge, w_proj_nsa, w_proj_dil, w_out, g_ffn):
    B, S, D = x.shape
    tm = MERGE_TM
    row = lambda a: pl.BlockSpec((None, tm, a.shape[2]), lambda b, i: (b, i, 0))
    cls = lambda a: pl.BlockSpec((None, a.shape[1], tm // a.shape[1], a.shape[3]), lambda b, i: (b, 0, i, 0))
    full = lambda a: pl.BlockSpec(a.shape, lambda b, i: (0,) * a.ndim)
    ws = [w_merge.astype(BF16), w_proj_nsa.astype(BF16), w_proj_dil.astype(BF16), w_out.astype(BF16)]
    consts = [g_mix, *ws, g_ffn]
    in_specs = [row(x), row(o_a)] + [cls(a) for a in (*outs, *lses)] + [full(a) for a in consts]
    return pl.pallas_call(
        _merge_kernel,
        grid=(B, S // tm),
        in_specs=in_specs,
        out_specs=(pl.BlockSpec((None, tm, D), lambda b, i: (b, i, 0)),) * 2,
        out_shape=(jax.ShapeDtypeStruct((B, S, D), F32), jax.ShapeDtypeStruct((B, S, D), BF16)),
        scratch_shapes=[pltpu.VMEM((DIL_WIDTH // LANES, tm, LANES), F32)],
        compiler_params=_params("parallel", "parallel"),
        name="merge_proj",
    )(x, o_a, *outs, *lses, *consts)


FFN_TM = 512
FFN_TN = 256
HALO = 16


def _ffn_kernel(h_ref, halo_ref, x1_ref, wu_ref, wg_ref, cw_ref, cb_ref, wd_ref, gfin_ref, o_ref, act_ref):
    i = pl.program_id(1)
    h = h_ref[...]
    halo = halo_ref[...]
    tm = h.shape[0]
    row = lax.broadcasted_iota(jnp.int32, (tm, FFN_TN), 0)
    live = (i > 0).astype(F32)
    for j in range(wu_ref.shape[0]):
        wu = wu_ref[j]
        u = _dot(h, wu)
        uh = _dot(halo, wu) * live
        gate = _dot(h, wg_ref[j])
        p1 = jnp.broadcast_to(uh[HALO - 1:HALO, :], (tm, FFN_TN))
        p2 = jnp.broadcast_to(uh[HALO - 2:HALO - 1, :], (tm, FFN_TN))
        u1 = jnp.where(row == 0, p1, pltpu.roll(u, 1, 0))
        u2 = jnp.where(row == 0, p2, jnp.where(row == 1, p1, pltpu.roll(u, 2, 0)))
        cw = cw_ref[j]
        uc = cb_ref[j] + cw[0:1, :] * u2
        uc = uc + cw[1:2, :] * u1
        uc = uc + cw[2:3, :] * u
        act_ref[:, j * FFN_TN:(j + 1) * FFN_TN] = (jax.nn.gelu(uc) * gate).astype(BF16)
    y = _dot(act_ref[...], wd_ref[...])
    o_ref[...] = _rms(x1_ref[...] + y, gfin_ref[...])


def _ffn(h2, x1, w_up, conv_w, conv_b, w_down, g_final):
    B, S, D = h2.shape
    tm, tn = FFN_TM, FFN_TN
    nj = D_FF // tn
    wu = w_up[:, :D_FF].reshape(D, nj, tn).transpose(1, 0, 2).astype(BF16)
    wg = w_up[:, D_FF:].reshape(D, nj, tn).transpose(1, 0, 2).astype(BF16)
    cw = jnp.pad(conv_w, ((0, 8 - CONV_WIDTH), (0, 0))).reshape(8, nj, tn).transpose(1, 0, 2)
    cb = conv_b.reshape(nj, 1, tn)
    wd = w_down.astype(BF16)
    gfin = g_final.reshape(1, D)
    full = lambda a: pl.BlockSpec(a.shape, lambda b, i: (0,) * a.ndim, pipeline_mode=pl.Buffered(1))
    tile = pl.BlockSpec((None, tm, D), lambda b, i: (b, i, 0))
    halo = pl.BlockSpec((None, HALO, D), lambda b, i: (b, jnp.maximum(i * (tm // HALO) - 1, 0), 0))
    return pl.pallas_call(
        _ffn_kernel,
        grid=(B, S // tm),
        in_specs=[tile, halo, tile, full(wu), full(wg), full(cw), full(cb), full(wd), full(gfin)],
        out_specs=tile,
        out_shape=jax.ShapeDtypeStruct((B, S, D), F32),
        scratch_shapes=[pltpu.VMEM((tm, D_FF), BF16)],
        compiler_params=_params("parallel", "parallel"),
        name="conv_ffn",
    )(h2, h2, x1, wu, wg, cw, cb, wd, gfin)


@jax.jit
def _layer(x, g_mix, w_in, pe_cmp_k, w_cmp_k1, w_cmp_k2, pe_cmp_v, w_cmp_v1, w_cmp_v2,
           w_proj_nsa, w_proj_dil, w_out, g_ffn, w_up, conv_w, conv_b, w_down, g_final):
    B, S, D = x.shape
    depth = g_mix.shape[0]
    for l in range(depth):
        gm = g_mix[l].reshape(1, D)
        kcmp, vcmp, ks, kw, d0, d1, d2, qt, vst, vwt, gates = _in_proj(x, gm, w_in[l])
        kc, _ = _compress(kcmp, pe_cmp_k[l], w_cmp_k1[l], w_cmp_k2[l])
        _, vct = _compress(vcmp, pe_cmp_v[l], w_cmp_v1[l], w_cmp_v2[l])
        o_a = _nsa(qt, kc, vct, ks, vst, kw, vwt, gates)
        outs, lses = _dilated((d0, d1, d2))
        merge_cols = w_in[l][:, w_in.shape[2] - 2 * D:]
        x1, h2 = _merge(x, o_a, outs, lses, gm, merge_cols, w_proj_nsa[l], w_proj_dil[l], w_out[l],
                        g_ffn[l].reshape(1, D))
        x = _ffn(h2, x1, w_up[l], conv_w[l], conv_b[l], w_down[l], g_final)
        assert depth == 1
    return x


def kernel(x, g_mix, w_in, pe_cmp_k, w_cmp_k1, w_cmp_k2, pe_cmp_v, w_cmp_v1, w_cmp_v2, w_proj_nsa, w_proj_dil, w_out, g_ffn, w_up, conv_w, conv_b, w_down, g_final):
    return _layer(x, g_mix, w_in, pe_cmp_k, w_cmp_k1, w_cmp_k2, pe_cmp_v, w_cmp_v1, w_cmp_v2,
                  w_proj_nsa, w_proj_dil, w_out, g_ffn, w_up, conv_w, conv_b, w_down, g_final)
```

```python
import functools
import math

import numpy as np
import jax
import jax.numpy as jnp
from jax import lax
from jax.experimental import pallas as pl
from jax.experimental.pallas import tpu as pltpu

HEAD_DIM = 64
NSA_HEADS = 8
NSA_GROUPS = 2
NSA_REP = NSA_HEADS // NSA_GROUPS
CMP_BLOCK = 32
CMP_STRIDE = 16
CMP_HIDDEN = 128
SLC_BLOCK = 64
SLC_TOP = 16
NSA_WINDOW = 512
FORCE_SCORE = 1.0e4
DIL_CONFIGS = ((128, 1), (512, 4), (2048, 16))
DIL_GROUPS = 3
DIL_HEADS_PER_GROUP = 4
DIL_HEADS = DIL_GROUPS * DIL_HEADS_PER_GROUP
D_FF = 2816
CONV_WIDTH = 3
RMS_EPS = 1e-6
NEG_INF = -1e30

LANES = 128
VMEM_LIMIT_BYTES = 56 * 1024 * 1024

F32 = jnp.float32
BF16 = jnp.bfloat16
NT_DIMS = (((1,), (1,)), ((), ()))


def _alibi_slopes(n):
    return [float(2.0 ** (-8.0 * i / n)) for i in range(1, n + 1)]


def _rms(xf, g):
    ms = jnp.mean(xf * xf, axis=-1, keepdims=True)
    return xf * lax.rsqrt(ms + RMS_EPS) * g


def _dot(a, b):
    return jnp.dot(a, b, preferred_element_type=F32)


def _dot_nt(a, b):
    return lax.dot_general(a, b, NT_DIMS, preferred_element_type=F32)


def _sigmoid(z):
    return 1.0 / (1.0 + jnp.exp(-z))


def _params(*sem):
    return pltpu.CompilerParams(dimension_semantics=sem, vmem_limit_bytes=VMEM_LIMIT_BYTES)


IN_TM = 512
N_KVC = 4 * HEAD_DIM
N_KSEL = NSA_GROUPS * HEAD_DIM
N_DIL = 3 * DIL_HEADS * HEAD_DIM
DIL_WIDTH = DIL_HEADS_PER_GROUP * HEAD_DIM
T_Q = NSA_HEADS * HEAD_DIM
T_V = NSA_GROUPS * HEAD_DIM
GATE_ROWS = 16
K_AUG = 2 * HEAD_DIM
SLC_SHIFT = int(math.log2(SLC_BLOCK))
AUG_NBLK = 8
AUG_HI, AUG_LO, AUG_ONE = AUG_NBLK, AUG_NBLK + 1, AUG_NBLK + 2
V_PAD = 16
V_ROWS = HEAD_DIM + V_PAD


def _key_position_columns(pos0, rows, step=1):
    pos = pos0 + step * lax.broadcasted_iota(jnp.int32, (rows, HEAD_DIM), 0)
    col = lax.broadcasted_iota(jnp.int32, (rows, HEAD_DIM), 1)
    blk = jnp.bitwise_and(lax.shift_right_logical(pos, SLC_SHIFT), AUG_NBLK - 1)
    hi = lax.shift_left(lax.shift_right_logical(pos, 7), 7).astype(F32)
    lo = jnp.bitwise_and(pos, 127).astype(F32)
    c = jnp.where(col == AUG_HI, hi, jnp.where(col == AUG_LO, lo, jnp.where(col == AUG_ONE, 1.0, 0.0)))
    return jnp.where((col < AUG_NBLK) & (blk == col), 1.0, c)


def _in_proj_kernel(x_ref, g_ref, wn_ref, wt_ref,
                    kcmp_ref, vcmp_ref, ks_ref, kw_ref, d0_ref, d1_ref, d2_ref, qt_ref, vst_ref, vwt_ref, gate_ref,
                    slab_ref):
    tm = x_ref.shape[0]
    h = _rms(x_ref[...], g_ref[...]).astype(BF16)
    c0 = 0
    kvc = _dot(h, wn_ref[:, c0:c0 + N_KVC])
    kcmp_ref[...] = kvc[:, :N_KSEL]
    vcmp_ref[...] = kvc[:, N_KSEL:]
    c0 += N_KVC
    ks = _dot(h, wn_ref[:, c0:c0 + N_KSEL]).astype(BF16)
    c0 += N_KSEL
    kw = _dot(h, wn_ref[:, c0:c0 + N_KSEL]).astype(BF16)
    c0 += N_KSEL
    aug = _key_position_columns(pl.program_id(1) * tm, tm).astype(BF16)
    for g in range(NSA_GROUPS):
        ks_ref[g] = jnp.concatenate([ks[:, g * HEAD_DIM:(g + 1) * HEAD_DIM], aug], axis=1)
        kw_ref[g] = jnp.concatenate([kw[:, g * HEAD_DIM:(g + 1) * HEAD_DIM], aug], axis=1)
    seg = DIL_HEADS * HEAD_DIM
    for which in range(3):
        y = _dot(h, wn_ref[:, c0 + which * seg:c0 + (which + 1) * seg])
        for gi, (d_ref, (_, r)) in enumerate(zip((d0_ref, d1_ref, d2_ref), DIL_CONFIGS)):
            yg = y[:, gi * DIL_WIDTH:(gi + 1) * DIL_WIDTH]
            if r == 1:
                d_ref[which, 0] = yg.astype(BF16)
                continue
            for s in range(DIL_WIDTH // LANES):
                slab_ref[s] = yg[:, s * LANES:(s + 1) * LANES]
            for c in range(r):
                d_ref[which, c] = jnp.concatenate(
                    [slab_ref[s, pl.ds(c, tm // r, stride=r), :] for s in range(DIL_WIDTH // LANES)],
                    axis=1).astype(BF16)
    r0 = 0
    qt_ref[...] = _dot_nt(wt_ref[r0:r0 + T_Q, :], h).astype(BF16)
    r0 += T_Q
    ones = jnp.where(lax.broadcasted_iota(jnp.int32, (V_PAD, tm), 0) == 0, 1.0, 0.0).astype(BF16)
    for ref in (vst_ref, vwt_ref):
        vt = _dot_nt(wt_ref[r0:r0 + T_V, :], h).astype(BF16)
        r0 += T_V
        for g in range(NSA_GROUPS):
            ref[g] = jnp.concatenate([vt[g * HEAD_DIM:(g + 1) * HEAD_DIM, :], ones], axis=0)
    gate_ref[...] = _sigmoid(_dot_nt(wt_ref[r0:r0 + NSA_GROUPS * GATE_ROWS, :], h))


def _in_proj(x, g_mix, w_in):
    B, S, D = x.shape
    scale = HEAD_DIM ** -0.5
    o_q, o_kv = 0, T_Q
    o_gate = o_kv + 6 * N_KSEL
    o_dil = o_gate + 3 * NSA_HEADS
    o_merge = o_dil + N_DIL
    kv = w_in[:, o_kv:o_gate]

    def kind(k):
        return kv[:, k * N_KSEL:(k + 1) * N_KSEL]

    dil = w_in[:, o_dil:o_merge]
    dil = jnp.concatenate([dil[:, :DIL_HEADS * HEAD_DIM] * scale, dil[:, DIL_HEADS * HEAD_DIM:]], axis=1)
    wn = jnp.concatenate([kind(0), kind(1), kind(2), kind(4), dil], axis=1).astype(BF16)
    wg = w_in[:, o_gate:o_dil].reshape(D, NSA_GROUPS, 3 * NSA_REP)
    wg = jnp.pad(wg, ((0, 0), (0, 0), (0, GATE_ROWS - 3 * NSA_REP))).reshape(D, NSA_GROUPS * GATE_ROWS)
    wt = jnp.concatenate([w_in[:, o_q:o_kv] * scale, kind(3), kind(5), wg], axis=1).T.astype(BF16)
    tm = IN_TM
    grid = (B, S // tm)
    full = lambda a: pl.BlockSpec(a.shape, lambda b, i: (0,) * a.ndim)
    k_shape = jax.ShapeDtypeStruct((B, NSA_GROUPS, S, K_AUG), BF16)
    v_shape = jax.ShapeDtypeStruct((B, NSA_GROUPS, V_ROWS, S), BF16)
    k_spec = pl.BlockSpec((None, NSA_GROUPS, tm, K_AUG), lambda b, i: (b, 0, i, 0))
    v_spec = pl.BlockSpec((None, NSA_GROUPS, V_ROWS, tm), lambda b, i: (b, 0, 0, i))
    c_shape = jax.ShapeDtypeStruct((B, S, N_KSEL), F32)
    c_spec = pl.BlockSpec((None, tm, N_KSEL), lambda b, i: (b, i, 0))
    d_shapes = tuple(jax.ShapeDtypeStruct((B, 3, r, S // r, DIL_WIDTH), BF16) for _, r in DIL_CONFIGS)
    d_specs = tuple(pl.BlockSpec((None, 3, r, tm // r, DIL_WIDTH), lambda b, i: (b, 0, 0, i, 0))
                    for _, r in DIL_CONFIGS)
    out_shape = (
        c_shape, c_shape, k_shape, k_shape, *d_shapes,
        jax.ShapeDtypeStruct((B, T_Q, S), BF16),
        v_shape,
        v_shape,
        jax.ShapeDtypeStruct((B, NSA_GROUPS * GATE_ROWS, S), F32),
    )
    out_specs = (
        c_spec, c_spec, k_spec, k_spec, *d_specs,
        pl.BlockSpec((None, T_Q, tm), lambda b, i: (b, 0, i)),
        v_spec,
        v_spec,
        pl.BlockSpec((None, NSA_GROUPS * GATE_ROWS, tm), lambda b, i: (b, 0, i)),
    )
    return pl.pallas_call(
        _in_proj_kernel,
        grid=grid,
        in_specs=[pl.BlockSpec((None, tm, D), lambda b, i: (b, i, 0)), full(g_mix), full(wn), full(wt)],
        out_specs=out_specs,
        out_shape=out_shape,
        scratch_shapes=[pltpu.VMEM((DIL_WIDTH // LANES, tm, LANES), F32)],
        compiler_params=_params("parallel", "parallel"),
        name="in_proj",
    )(x, g_mix, wn, wt)


def _compress_kernel(x_ref, pe_ref, w1_ref, w2_ref, w2t_ref, c_ref, ct_ref):
    nch = x_ref.shape[0] // CMP_STRIDE
    ulo = jnp.zeros((nch, NSA_GROUPS * CMP_HIDDEN), F32)
    uhi = jnp.zeros((nch, NSA_GROUPS * CMP_HIDDEN), F32)
    for j in range(CMP_STRIDE):
        xj = x_ref[pl.ds(j, nch, stride=CMP_STRIDE), :]
        ulo = ulo + _dot((xj + pe_ref[j:j + 1, :]).astype(BF16), w1_ref[j])
        uhi = uhi + _dot((xj + pe_ref[CMP_STRIDE + j:CMP_STRIDE + j + 1, :]).astype(BF16), w1_ref[CMP_STRIDE + j])
    pre = ulo + pltpu.roll(uhi, nch - 1, 0)
    hid = jax.nn.gelu(pre).astype(BF16)
    aug = _key_position_columns(CMP_BLOCK - 1, nch, CMP_STRIDE).astype(BF16)
    for g in range(NSA_GROUPS):
        hg = hid[:, g * CMP_HIDDEN:(g + 1) * CMP_HIDDEN]
        c_ref[g] = jnp.concatenate([_dot(hg, w2_ref[...]).astype(BF16), aug], axis=1)
        ct_ref[g] = _dot_nt(w2t_ref[...], hg).astype(BF16)


def _compress(xc, pe, w1, w2):
    B, S, width = xc.shape
    nch = S // CMP_STRIDE
    G, dh, hid = NSA_GROUPS, HEAD_DIM, CMP_HIDDEN
    pe_t = jnp.broadcast_to(pe.reshape(CMP_BLOCK, 1, dh), (CMP_BLOCK, G, dh)).reshape(CMP_BLOCK, width)
    eye = jnp.eye(G, dtype=w1.dtype)
    wexp = jnp.einsum('pdn,ge->pgden', w1.reshape(CMP_BLOCK, dh, hid), eye).reshape(CMP_BLOCK, width, G * hid)
    wexp = wexp.astype(BF16)
    full = lambda a: pl.BlockSpec(a.shape, lambda b: (0,) * a.ndim)
    w2b = w2.astype(BF16)
    w2t = w2.T.astype(BF16)
    return pl.pallas_call(
        _compress_kernel,
        grid=(B,),
        in_specs=[pl.BlockSpec((None, S, width), lambda b: (b, 0, 0)), full(pe_t), full(wexp), full(w2b), full(w2t)],
        out_specs=(pl.BlockSpec((None, G, nch, K_AUG), lambda b: (b, 0, 0, 0)),
                   pl.BlockSpec((None, G, dh, nch), lambda b: (b, 0, 0, 0))),
        out_shape=(jax.ShapeDtypeStruct((B, G, nch, K_AUG), BF16),
                   jax.ShapeDtypeStruct((B, G, dh, nch), BF16)),
        compiler_params=_params("parallel"),
        name="compress",
    )(xc, pe_t, wexp, w2b, w2t)


NSA_TQ = 128
SWEEP_TK = AUG_NBLK * SLC_BLOCK
SEL_SUB = 256
SEL_AHEAD = 3


def _rowmax8(s):
    return jnp.max(s.reshape(s.shape[0] // 8, 8, s.shape[1]), axis=0)


def _nsa_kernel(qt_ref, kc_ref, vct_ref, ks_ref, vst_ref, kw_ref, vwt_ref, gate_ref, ov_ref,
                o_ref, sbt_ref, osel_ref, *, slopes):
    g = pl.program_id(1)
    i = pl.program_id(2)
    R, dh, tq, tk = NSA_REP, HEAD_DIM, NSA_TQ, SWEEP_TK
    L = R * tq
    t0 = i * tq
    nc = kc_ref.shape[0]
    ns = ov_ref.shape[0]
    n_tiles = ns // AUG_NBLK
    tile4 = lambda a: jnp.concatenate([a] * R, axis=1)

    qt = qt_ref[...]
    qs = jnp.concatenate([qt[r * dh:(r + 1) * dh, :] for r in range(R)], axis=1)
    slope = [jnp.where(g == 0, slopes[r], slopes[R + r]).astype(F32) for r in range(R)]
    slope_row = jnp.concatenate([jnp.full((1, tq), 1.0, F32) * slope[r] for r in range(R)], axis=1)
    t_row = t0 + lax.broadcasted_iota(jnp.int32, (1, tq), 1)

    r8 = lax.broadcasted_iota(jnp.int32, (AUG_NBLK, L), 0)
    alibi8 = jnp.where(r8 < 2, slope_row, jnp.where(r8 == 2, -slope_row * t0.astype(F32), 0.0))
    q_pad = jnp.zeros((K_AUG - dh - 2 * AUG_NBLK, L), BF16)
    q_plain = jnp.concatenate([qs, jnp.concatenate([jnp.zeros((AUG_NBLK, L), F32), alibi8], axis=0).astype(BF16),
                               q_pad], axis=0)

    cmp_end = lax.broadcasted_iota(jnp.int32, (nc, tq), 0) * CMP_STRIDE + (CMP_BLOCK - 1)
    m_cmp = t_row >= cmp_end
    sc = _dot(kc_ref[...], q_plain) + tile4(jnp.where(m_cmp, 0.0, NEG_INF))
    mx = jnp.max(_rowmax8(sc), axis=0, keepdims=True)
    p = jnp.exp(sc - mx) * tile4(jnp.where(m_cmp, 1.0, 0.0))
    den = jnp.sum(jnp.sum(p.reshape(nc // 8, 8, L), axis=0), axis=0, keepdims=True)
    pr = p * (1.0 / jnp.maximum(den, 1e-30))
    o_cmp = _dot(vct_ref[...], pr.astype(BF16))
    psum = pr[:, 0:tq]
    for r in range(1, R):
        psum = psum + pr[:, r * tq:(r + 1) * tq]

    p_hi = psum.astype(BF16)
    p_lo = (psum - p_hi.astype(F32)).astype(BF16)
    imp = _dot(ov_ref[...], p_hi) + _dot(ov_ref[...], p_lo)
    blk = lax.broadcasted_iota(jnp.int32, (ns, tq), 0)
    cur = lax.shift_right_logical(t_row, SLC_SHIFT)
    val = jnp.where((blk == cur) | (blk == 0), FORCE_SCORE, imp)
    val = jnp.where(blk <= cur, val, -1.0)
    vals = [val[8 * v:8 * v + 8, :] for v in range(ns // 8)]
    ranks = [jnp.zeros((8, tq), F32) for _ in vals]
    row8 = lax.broadcasted_iota(jnp.int32, (8, tq), 0)
    for j in range(ns):
        vj = jnp.broadcast_to(val[j:j + 1, :], (8, tq))
        for v in range(len(vals)):
            if 8 * v > j:
                ahead = vj >= vals[v]
            elif 8 * v + 7 <= j:
                ahead = vj > vals[v]
            else:
                ahead = (vj > vals[v]) | ((vj == vals[v]) & (row8 > j - 8 * v))
            ranks[v] = ranks[v] + jnp.where(ahead, 1.0, 0.0)
    for T in range(n_tiles):
        selb = jnp.where((ranks[T] < float(SLC_TOP)) & (vals[T] >= 0.0), 0.0, NEG_INF)
        sbt_ref[T] = jnp.concatenate([tile4(selb), alibi8], axis=0).astype(BF16)

    a0 = pl.multiple_of(jnp.maximum(t0 - NSA_WINDOW, 0), tq)
    d0 = pl.multiple_of(t0, tq)
    s_a = _dot(kw_ref[pl.ds(a0, NSA_WINDOW), :], q_plain)
    s_d = _dot(kw_ref[pl.ds(d0, tq), :], q_plain)
    kpos_a = a0 + lax.broadcasted_iota(jnp.int32, (NSA_WINDOW, tq), 0)
    t_a = t0 + lax.broadcasted_iota(jnp.int32, (NSA_WINDOW, tq), 1)
    bias_a = jnp.where((kpos_a < t0) & (t_a - kpos_a < NSA_WINDOW), 0.0, NEG_INF)
    causal_d = jnp.where(lax.broadcasted_iota(jnp.int32, (tq, tq), 0)
                         <= lax.broadcasted_iota(jnp.int32, (tq, tq), 1), 0.0, NEG_INF)
    s_a = s_a + tile4(bias_a)
    s_d = s_d + tile4(causal_d)
    m_w = jnp.max(jnp.maximum(_rowmax8(s_a), _rowmax8(s_d)), axis=0, keepdims=True)
    acc_w = (_dot(vwt_ref[:, pl.ds(a0, NSA_WINDOW)], jnp.exp(s_a - m_w).astype(BF16))
             + _dot(vwt_ref[:, pl.ds(d0, tq)], jnp.exp(s_d - m_w).astype(BF16)))
    o_win = acc_w[:dh] * (1.0 / acc_w[dh:dh + 1])

    td = lax.div(t0, tk)
    for c in range(n_tiles):
        @pl.when(td == c)
        def _(c=c):
            kpos = c * tk + lax.broadcasted_iota(jnp.int32, (tk, tq), 0)
            t_q = t0 + lax.broadcasted_iota(jnp.int32, (tk, tq), 1)
            causal = tile4(jnp.where(kpos <= t_q, 0.0, NEG_INF))
            q_augs = [jnp.concatenate([qs, sbt_ref[T], q_pad], axis=0) for T in range(c + 1)]
            chunks = [(T, j * SEL_SUB) for T in range(c + 1) for j in range(tk // SEL_SUB)]

            def chunk_scores(T, r):
                s = _dot(ks_ref[T * tk + r:T * tk + r + SEL_SUB, :], q_augs[T])
                return s + causal[r:r + SEL_SUB] if T == c else s

            pending = [chunk_scores(*ch) for ch in chunks[:SEL_AHEAD]]
            m = acc = None
            for n, (T, r) in enumerate(chunks):
                if n + SEL_AHEAD < len(chunks):
                    pending.append(chunk_scores(*chunks[n + SEL_AHEAD]))
                s = pending.pop(0)
                v_t = vst_ref[:, T * tk + r:T * tk + r + SEL_SUB]
                m_c = jnp.max(_rowmax8(s), axis=0, keepdims=True)
                if m is None:
                    m = m_c
                    acc = _dot(v_t, jnp.exp(s - m).astype(BF16))
                else:
                    m_new = jnp.maximum(m, m_c)
                    acc = jnp.exp(m - m_new) * acc + _dot(v_t, jnp.exp(s - m_new).astype(BF16))
                    m = m_new
            osel_ref[...] = acc[:dh] * (1.0 / acc[dh:dh + 1])

    o_sel = osel_ref[...]
    gates = gate_ref[...]
    tiles = []
    for r in range(R):
        sl = slice(r * tq, (r + 1) * tq)
        tiles.append(gates[3 * r:3 * r + 1, :] * o_cmp[:, sl]
                     + gates[3 * r + 1:3 * r + 2, :] * o_sel[:, sl]
                     + gates[3 * r + 2:3 * r + 3, :] * o_win[:, sl])
    o_ref[...] = jnp.concatenate(tiles, axis=0).T.astype(o_ref.dtype)


def _overlap_matrix(nc, ns):
    cs = np.arange(nc)[None, :] * CMP_STRIDE
    ss = np.arange(ns)[:, None] * SLC_BLOCK
    ov = np.clip(np.minimum(cs + CMP_BLOCK, ss + SLC_BLOCK) - np.maximum(cs, ss), 0, None)
    return jnp.asarray(ov.astype(np.float32) / CMP_BLOCK, dtype=BF16)


def _nsa(qt, kc, vct, ks, vst, kw, vwt, gates):
    B, _, S = qt.shape
    G, R, dh, tq = NSA_GROUPS, NSA_REP, HEAD_DIM, NSA_TQ
    nc = kc.shape[2]
    ns = S // SLC_BLOCK
    ov = _overlap_matrix(nc, ns)
    kern = functools.partial(_nsa_kernel, slopes=tuple(_alibi_slopes(NSA_HEADS)))
    assert S % SWEEP_TK == 0 and S >= NSA_WINDOW + tq
    per_bg = lambda shape: pl.BlockSpec((None, None) + shape, lambda b, g, i: (b, g, 0, 0))
    return pl.pallas_call(
        kern,
        grid=(B, G, S // tq),
        in_specs=[
            pl.BlockSpec((None, R * dh, tq), lambda b, g, i: (b, g, i)),
            per_bg((nc, K_AUG)), per_bg((dh, nc)),
            per_bg((S, K_AUG)), per_bg((V_ROWS, S)),
            per_bg((S, K_AUG)), per_bg((V_ROWS, S)),
            pl.BlockSpec((None, GATE_ROWS, tq), lambda b, g, i: (b, g, i)),
            pl.BlockSpec(ov.shape, lambda b, g, i: (0, 0)),
        ],
        out_specs=pl.BlockSpec((None, tq, R * dh), lambda b, g, i: (b, i, g)),
        out_shape=jax.ShapeDtypeStruct((B, S, G * R * dh), BF16),
        scratch_shapes=[pltpu.VMEM((S // SWEEP_TK, 2 * AUG_NBLK, R * tq), BF16),
                        pltpu.VMEM((dh, R * tq), F32)],
        compiler_params=_params("parallel", "parallel", "arbitrary"),
        name="nsa_attention",
    )(qt, kc, vct, ks, vst, kw, vwt, gates, ov)


BAND_TQ = 128
BAND_SUB = 2
BAND = 128


def _banded_kernel(q_ref, k_ref, v_ref, o_ref, lse_ref, *, slopes):
    tq, dh, nh = BAND_TQ, HEAD_DIM, DIL_HEADS_PER_GROUP
    tk = tq + BAND
    width = nh * dh
    head_of_lane = lambda rows: lax.shift_right_logical(
        lax.broadcasted_iota(jnp.int32, (rows, width), 1), int(math.log2(dh)))
    lane_head, q_head = head_of_lane(tk), head_of_lane(tq)
    keep = [jnp.where(lane_head == h, 1.0, 0.0).astype(BF16) for h in range(nh)]

    def per_head(a):
        return jnp.concatenate([a * keep[h] for h in range(nh)], axis=0)

    ones_h = jnp.concatenate(keep, axis=0)
    for sub in range(BAND_SUB):
        i = pl.program_id(2) * BAND_SUB + sub
        k0 = pl.multiple_of(jnp.maximum(i - 1, 0) * tq, tq)
        q = q_ref[sub * tq:(sub + 1) * tq, :]
        k = k_ref[pl.ds(k0, tk), :]
        v = v_ref[pl.ds(k0, tk), :]
        d = (i * tq - k0) + (lax.broadcasted_iota(jnp.int32, (tq, tk), 0)
                             - lax.broadcasted_iota(jnp.int32, (tq, tk), 1))
        mask_bias = jnp.where((d >= 0) & (d <= BAND), 0.0, NEG_INF)
        neg_d = -d.astype(F32)
        s = _dot_nt(q, per_head(k))
        ps, mxs = [], []
        for h in range(nh):
            sh = s[:, h * tk:(h + 1) * tk] + (slopes[h] * neg_d + mask_bias)
            mx = jnp.max(sh, axis=-1, keepdims=True)
            ps.append(jnp.exp(sh - mx).astype(BF16))
            mxs.append(mx)
        p = jnp.concatenate(ps, axis=1)
        den = _dot(p, ones_h)
        mx_all = mxs[nh - 1]
        for h in range(nh - 2, -1, -1):
            mx_all = jnp.where(q_head == h, mxs[h], mx_all)
        o_ref[sub * tq:(sub + 1) * tq, :] = _dot(p, per_head(v)) * (1.0 / den)
        lse_ref[sub * tq:(sub + 1) * tq, :] = mx_all + jnp.log(den)


def _banded(d, slopes):
    B, _, r, n, width = d.shape
    tq = BAND_TQ * BAND_SUB
    assert n >= BAND_TQ + BAND and n % tq == 0
    kern = functools.partial(_banded_kernel, slopes=tuple(slopes))
    tile = pl.BlockSpec((None, None, None, tq, width), lambda b, c, i: (b, 0, c, i, 0))
    kseq = pl.BlockSpec((None, None, None, n, width), lambda b, c, i: (b, 1, c, 0, 0))
    vseq = pl.BlockSpec((None, None, None, n, width), lambda b, c, i: (b, 2, c, 0, 0))
    out = pl.BlockSpec((None, None, tq, width), lambda b, c, i: (b, c, i, 0))
    return pl.pallas_call(
        kern,
        grid=(B, r, n // tq),
        in_specs=[tile, kseq, vseq],
        out_specs=(out, out),
        out_shape=(jax.ShapeDtypeStruct((B, r, n, width), F32),) * 2,
        compiler_params=_params("parallel", "parallel", "arbitrary"),
        name="banded_attention",
    )(d, d, d)


def _dilated(dils):
    slopes = _alibi_slopes(DIL_HEADS)
    outs, lses = [], []
    for gi, (w, r) in enumerate(DIL_CONFIGS):
        assert w // r == BAND
        sl = [s_ * r for s_ in slopes[gi * DIL_HEADS_PER_GROUP:(gi + 1) * DIL_HEADS_PER_GROUP]]
        o, lse = _banded(dils[gi], sl)
        outs.append(o)
        lses.append(lse)
    return outs, lses


MERGE_TM = 512


def _token_order(ref, slab_ref):
    r, rows, width = ref.shape
    if r == 1:
        return ref[0]
    for c in range(r):
        blk = ref[c]
        for s in range(width // LANES):
            slab_ref[s, pl.ds(c, rows, stride=r), :] = blk[:, s * LANES:(s + 1) * LANES]
    return jnp.concatenate([slab_ref[s] for s in range(width // LANES)], axis=1)


def _merge_kernel(x_ref, oa_ref, o0_ref, o1_ref, o2_ref, l0_ref, l1_ref, l2_ref,
                  gmix_ref, wm_ref, wpn_ref, wpd_ref, wo_ref, gffn_ref, x1_ref, h2_ref, slab_ref):
    x = x_ref[...]
    D = x.shape[1]
    h = _rms(x, gmix_ref[...]).astype(BF16)
    gm = _sigmoid(_dot(h, wm_ref[...]))
    o0, o1, o2 = [_token_order(r_, slab_ref) for r_ in (o0_ref, o1_ref, o2_ref)]
    l0, l1, l2 = [_token_order(r_, slab_ref) for r_ in (l0_ref, l1_ref, l2_ref)]
    mx = jnp.maximum(jnp.maximum(l0, l1), l2)
    e0, e1, e2 = jnp.exp(l0 - mx), jnp.exp(l1 - mx), jnp.exp(l2 - mx)
    inv = 1.0 / (e0 + e1 + e2)
    ob = o0 * (e0 * inv) + o1 * (e1 * inv) + o2 * (e2 * inv)
    a = _dot(oa_ref[...], wpn_ref[...])
    d = _dot(ob.astype(BF16), wpd_ref[...])
    mixed = gm[:, :D] * a + gm[:, D:] * d
    x1 = x + _dot(mixed.astype(BF16), wo_ref[...])
    x1_ref[...] = x1
    h2_ref[...] = _rms(x1, gffn_ref[...]).astype(BF16)


def _merge(x, o_a, outs, lses, g_mix, w_merge, w_proj_nsa, w_proj_dil, w_out, g_ffn):
    B, S, D = x.shape
    tm = MERGE_TM
    row = lambda a: pl.BlockSpec((None, tm, a.shape[2]), lambda b, i: (b, i, 0))
    cls = lambda a: pl.BlockSpec((None, a.shape[1], tm // a.shape[1], a.shape[3]), lambda b, i: (b, 0, i, 0))
    full = lambda a: pl.BlockSpec(a.shape, lambda b, i: (0,) * a.ndim)
    ws = [w_merge.astype(BF16), w_proj_nsa.astype(BF16), w_proj_dil.astype(BF16), w_out.astype(BF16)]
    consts = [g_mix, *ws, g_ffn]
    in_specs = [row(x), row(o_a)] + [cls(a) for a in (*outs, *lses)] + [full(a) for a in consts]
    return pl.pallas_call(
        _merge_kernel,
        grid=(B, S // tm),
        in_specs=in_specs,
        out_specs=(pl.BlockSpec((None, tm, D), lambda b, i: (b, i, 0)),) * 2,
        out_shape=(jax.ShapeDtypeStruct((B, S, D), F32), jax.ShapeDtypeStruct((B, S, D), BF16)),
        scratch_shapes=[pltpu.VMEM((DIL_WIDTH // LANES, tm, LANES), F32)],
        compiler_params=_params("parallel", "parallel"),
        name="merge_proj",
    )(x, o_a, *outs, *lses, *consts)


FFN_TM = 512
FFN_TN = 256
HALO = 16


def _ffn_kernel(h_ref, halo_ref, x1_ref, wu_ref, wg_ref, cw_ref, cb_ref, wd_ref, gfin_ref, o_ref, act_ref):
    i = pl.program_id(1)
    h = h_ref[...]
    halo = halo_ref[...]
    tm = h.shape[0]
    row = lax.broadcasted_iota(jnp.int32, (tm, FFN_TN), 0)
    live = (i > 0).astype(F32)
    for j in range(wu_ref.shape[0]):
        wu = wu_ref[j]
        u = _dot(h, wu)
        uh = _dot(halo, wu) * live
        gate = _dot(h, wg_ref[j])
        p1 = jnp.broadcast_to(uh[HALO - 1:HALO, :], (tm, FFN_TN))
        p2 = jnp.broadcast_to(uh[HALO - 2:HALO - 1, :], (tm, FFN_TN))
        u1 = jnp.where(row == 0, p1, pltpu.roll(u, 1, 0))
        u2 = jnp.where(row == 0, p2, jnp.where(row == 1, p1, pltpu.roll(u, 2, 0)))
        cw = cw_ref[j]
        uc = cb_ref[j] + cw[0:1, :] * u2
        uc = uc + cw[1:2, :] * u1
        uc = uc + cw[2:3, :] * u
        act_ref[:, j * FFN_TN:(j + 1) * FFN_TN] = (jax.nn.gelu(uc) * gate).astype(BF16)
    y = _dot(act_ref[...], wd_ref[...])
    o_ref[...] = _rms(x1_ref[...] + y, gfin_ref[...])


def _ffn(h2, x1, w_up, conv_w, conv_b, w_down, g_final):
    B, S, D = h2.shape
    tm, tn = FFN_TM, FFN_TN
    nj = D_FF // tn
    wu = w_up[:, :D_FF].reshape(D, nj, tn).transpose(1, 0, 2).astype(BF16)
    wg = w_up[:, D_FF:].reshape(D, nj, tn).transpose(1, 0, 2).astype(BF16)
    cw = jnp.pad(conv_w, ((0, 8 - CONV_WIDTH), (0, 0))).reshape(8, nj, tn).transpose(1, 0, 2)
    cb = conv_b.reshape(nj, 1, tn)
    wd = w_down.astype(BF16)
    gfin = g_final.reshape(1, D)
    full = lambda a: pl.BlockSpec(a.shape, lambda b, i: (0,) * a.ndim, pipeline_mode=pl.Buffered(1))
    tile = pl.BlockSpec((None, tm, D), lambda b, i: (b, i, 0))
    halo = pl.BlockSpec((None, HALO, D), lambda b, i: (b, jnp.maximum(i * (tm // HALO) - 1, 0), 0))
    return pl.pallas_call(
        _ffn_kernel,
        grid=(B, S // tm),
        in_specs=[tile, halo, tile, full(wu), full(wg), full(cw), full(cb), full(wd), full(gfin)],
        out_specs=tile,
        out_shape=jax.ShapeDtypeStruct((B, S, D), F32),
        scratch_shapes=[pltpu.VMEM((tm, D_FF), BF16)],
        compiler_params=_params("parallel", "parallel"),
        name="conv_ffn",
    )(h2, h2, x1, wu, wg, cw, cb, wd, gfin)


@jax.jit
def _layer(x, g_mix, w_in, pe_cmp_k, w_cmp_k1, w_cmp_k2, pe_cmp_v, w_cmp_v1, w_cmp_v2,
           w_proj_nsa, w_proj_dil, w_out, g_ffn, w_up, conv_w, conv_b, w_down, g_final):
    B, S, D = x.shape
    depth = g_mix.shape[0]
    for l in range(depth):
        gm = g_mix[l].reshape(1, D)
        kcmp, vcmp, ks, kw, d0, d1, d2, qt, vst, vwt, gates = _in_proj(x, gm, w_in[l])
        kc, _ = _compress(kcmp, pe_cmp_k[l], w_cmp_k1[l], w_cmp_k2[l])
        _, vct = _compress(vcmp, pe_cmp_v[l], w_cmp_v1[l], w_cmp_v2[l])
        o_a = _nsa(qt, kc, vct, ks, vst, kw, vwt, gates)
        outs, lses = _dilated((d0, d1, d2))
        merge_cols = w_in[l][:, w_in.shape[2] - 2 * D:]
        x1, h2 = _merge(x, o_a, outs, lses, gm, merge_cols, w_proj_nsa[l], w_proj_dil[l], w_out[l],
                        g_ffn[l].reshape(1, D))
        x = _ffn(h2, x1, w_up[l], conv_w[l], conv_b[l], w_down[l], g_final)
        assert depth == 1
    return x


def kernel(x, g_mix, w_in, pe_cmp_k, w_cmp_k1, w_cmp_k2, pe_cmp_v, w_cmp_v1, w_cmp_v2, w_proj_nsa, w_proj_dil, w_out, g_ffn, w_up, conv_w, conv_b, w_down, g_final):
    return _layer(x, g_mix, w_in, pe_cmp_k, w_cmp_k1, w_cmp_k2, pe_cmp_v, w_cmp_v1, w_cmp_v2,
                  w_proj_nsa, w_proj_dil, w_out, g_ffn, w_up, conv_w, conv_b, w_down, g_final)
```

```python
import functools
import math

import numpy as np
import jax
import jax.numpy as jnp
from jax import lax
from jax.experimental import pallas as pl
from jax.experimental.pallas import tpu as pltpu

HEAD_DIM = 64
NSA_HEADS = 8
NSA_GROUPS = 2
NSA_REP = NSA_HEADS // NSA_GROUPS
CMP_BLOCK = 32
CMP_STRIDE = 16
CMP_HIDDEN = 128
SLC_BLOCK = 64
SLC_TOP = 16
NSA_WINDOW = 512
FORCE_SCORE = 1.0e4
DIL_CONFIGS = ((128, 1), (512, 4), (2048, 16))
DIL_GROUPS = 3
DIL_HEADS_PER_GROUP = 4
DIL_HEADS = DIL_GROUPS * DIL_HEADS_PER_GROUP
D_FF = 2816
CONV_WIDTH = 3
RMS_EPS = 1e-6
NEG_INF = -1e30

LANES = 128
VMEM_LIMIT_BYTES = 56 * 1024 * 1024

F32 = jnp.float32
BF16 = jnp.bfloat16
NT_DIMS = (((1,), (1,)), ((), ()))


def _alibi_slopes(n):
    return [float(2.0 ** (-8.0 * i / n)) for i in range(1, n + 1)]


def _rms(xf, g):
    ms = jnp.mean(xf * xf, axis=-1, keepdims=True)
    return xf * lax.rsqrt(ms + RMS_EPS) * g


def _dot(a, b):
    return jnp.dot(a, b, preferred_element_type=F32)


def _dot_nt(a, b):
    return lax.dot_general(a, b, NT_DIMS, preferred_element_type=F32)


def _sigmoid(z):
    return 1.0 / (1.0 + jnp.exp(-z))


def _params(*sem):
    return pltpu.CompilerParams(dimension_semantics=sem, vmem_limit_bytes=VMEM_LIMIT_BYTES)


IN_TM = 512
N_KVC = 4 * HEAD_DIM
N_KSEL = NSA_GROUPS * HEAD_DIM
N_DIL = 3 * DIL_HEADS * HEAD_DIM
DIL_WIDTH = DIL_HEADS_PER_GROUP * HEAD_DIM
T_Q = NSA_HEADS * HEAD_DIM
T_V = NSA_GROUPS * HEAD_DIM
GATE_ROWS = 16
K_AUG = 2 * HEAD_DIM
SLC_SHIFT = int(math.log2(SLC_BLOCK))
AUG_NBLK = 8
AUG_HI, AUG_LO, AUG_ONE = AUG_NBLK, AUG_NBLK + 1, AUG_NBLK + 2
V_PAD = 16
V_ROWS = HEAD_DIM + V_PAD


def _key_position_columns(pos0, rows, step=1):
    pos = pos0 + step * lax.broadcasted_iota(jnp.int32, (rows, HEAD_DIM), 0)
    col = lax.broadcasted_iota(jnp.int32, (rows, HEAD_DIM), 1)
    blk = jnp.bitwise_and(lax.shift_right_logical(pos, SLC_SHIFT), AUG_NBLK - 1)
    hi = lax.shift_left(lax.shift_right_logical(pos, 7), 7).astype(F32)
    lo = jnp.bitwise_and(pos, 127).astype(F32)
    c = jnp.where(col == AUG_HI, hi, jnp.where(col == AUG_LO, lo, jnp.where(col == AUG_ONE, 1.0, 0.0)))
    return jnp.where((col < AUG_NBLK) & (blk == col), 1.0, c)


def _in_proj_kernel(x_ref, g_ref, wn_ref, wt_ref,
                    kcmp_ref, vcmp_ref, ks_ref, kw_ref, d0_ref, d1_ref, d2_ref, qt_ref, vst_ref, vwt_ref, gate_ref,
                    slab_ref):
    tm = x_ref.shape[0]
    h = _rms(x_ref[...], g_ref[...]).astype(BF16)
    c0 = 0
    kvc = _dot(h, wn_ref[:, c0:c0 + N_KVC])
    kcmp_ref[...] = kvc[:, :N_KSEL]
    vcmp_ref[...] = kvc[:, N_KSEL:]
    c0 += N_KVC
    ks = _dot(h, wn_ref[:, c0:c0 + N_KSEL]).astype(BF16)
    c0 += N_KSEL
    kw = _dot(h, wn_ref[:, c0:c0 + N_KSEL]).astype(BF16)
    c0 += N_KSEL
    aug = _key_position_columns(pl.program_id(1) * tm, tm).astype(BF16)
    for g in range(NSA_GROUPS):
        ks_ref[g] = jnp.concatenate([ks[:, g * HEAD_DIM:(g + 1) * HEAD_DIM], aug], axis=1)
        kw_ref[g] = jnp.concatenate([kw[:, g * HEAD_DIM:(g + 1) * HEAD_DIM], aug], axis=1)
    seg = DIL_HEADS * HEAD_DIM
    for which in range(3):
        y = _dot(h, wn_ref[:, c0 + which * seg:c0 + (which + 1) * seg])
        for gi, (d_ref, (_, r)) in enumerate(zip((d0_ref, d1_ref, d2_ref), DIL_CONFIGS)):
            yg = y[:, gi * DIL_WIDTH:(gi + 1) * DIL_WIDTH]
            if r == 1:
                d_ref[which, 0] = yg.astype(BF16)
                continue
            for s in range(DIL_WIDTH // LANES):
                slab_ref[s] = yg[:, s * LANES:(s + 1) * LANES]
            for c in range(r):
                d_ref[which, c] = jnp.concatenate(
                    [slab_ref[s, pl.ds(c, tm // r, stride=r), :] for s in range(DIL_WIDTH // LANES)],
                    axis=1).astype(BF16)
    r0 = 0
    qt_ref[...] = _dot_nt(wt_ref[r0:r0 + T_Q, :], h).astype(BF16)
    r0 += T_Q
    ones = jnp.where(lax.broadcasted_iota(jnp.int32, (V_PAD, tm), 0) == 0, 1.0, 0.0).astype(BF16)
    for ref in (vst_ref, vwt_ref):
        vt = _dot_nt(wt_ref[r0:r0 + T_V, :], h).astype(BF16)
        r0 += T_V
        for g in range(NSA_GROUPS):
            ref[g] = jnp.concatenate([vt[g * HEAD_DIM:(g + 1) * HEAD_DIM, :], ones], axis=0)
    gate_ref[...] = _sigmoid(_dot_nt(wt_ref[r0:r0 + NSA_GROUPS * GATE_ROWS, :], h))


def _in_proj(x, g_mix, w_in):
    B, S, D = x.shape
    scale = HEAD_DIM ** -0.5
    o_q, o_kv = 0, T_Q
    o_gate = o_kv + 6 * N_KSEL
    o_dil = o_gate + 3 * NSA_HEADS
    o_merge = o_dil + N_DIL
    kv = w_in[:, o_kv:o_gate]

    def kind(k):
        return kv[:, k * N_KSEL:(k + 1) * N_KSEL]

    dil = w_in[:, o_dil:o_merge]
    dil = jnp.concatenate([dil[:, :DIL_HEADS * HEAD_DIM] * scale, dil[:, DIL_HEADS * HEAD_DIM:]], axis=1)
    wn = jnp.concatenate([kind(0), kind(1), kind(2), kind(4), dil], axis=1).astype(BF16)
    wg = w_in[:, o_gate:o_dil].reshape(D, NSA_GROUPS, 3 * NSA_REP)
    wg = jnp.pad(wg, ((0, 0), (0, 0), (0, GATE_ROWS - 3 * NSA_REP))).reshape(D, NSA_GROUPS * GATE_ROWS)
    wt = jnp.concatenate([w_in[:, o_q:o_kv] * scale, kind(3), kind(5), wg], axis=1).T.astype(BF16)
    tm = IN_TM
    grid = (B, S // tm)
    full = lambda a: pl.BlockSpec(a.shape, lambda b, i: (0,) * a.ndim)
    k_shape = jax.ShapeDtypeStruct((B, NSA_GROUPS, S, K_AUG), BF16)
    v_shape = jax.ShapeDtypeStruct((B, NSA_GROUPS, V_ROWS, S), BF16)
    k_spec = pl.BlockSpec((None, NSA_GROUPS, tm, K_AUG), lambda b, i: (b, 0, i, 0))
    v_spec = pl.BlockSpec((None, NSA_GROUPS, V_ROWS, tm), lambda b, i: (b, 0, 0, i))
    c_shape = jax.ShapeDtypeStruct((B, S, N_KSEL), F32)
    c_spec = pl.BlockSpec((None, tm, N_KSEL), lambda b, i: (b, i, 0))
    d_shapes = tuple(jax.ShapeDtypeStruct((B, 3, r, S // r, DIL_WIDTH), BF16) for _, r in DIL_CONFIGS)
    d_specs = tuple(pl.BlockSpec((None, 3, r, tm // r, DIL_WIDTH), lambda b, i: (b, 0, 0, i, 0))
                    for _, r in DIL_CONFIGS)
    out_shape = (
        c_shape, c_shape, k_shape, k_shape, *d_shapes,
        jax.ShapeDtypeStruct((B, T_Q, S), BF16),
        v_shape,
        v_shape,
        jax.ShapeDtypeStruct((B, NSA_GROUPS * GATE_ROWS, S), F32),
    )
    out_specs = (
        c_spec, c_spec, k_spec, k_spec, *d_specs,
        pl.BlockSpec((None, T_Q, tm), lambda b, i: (b, 0, i)),
        v_spec,
        v_spec,
        pl.BlockSpec((None, NSA_GROUPS * GATE_ROWS, tm), lambda b, i: (b, 0, i)),
    )
    return pl.pallas_call(
        _in_proj_kernel,
        grid=grid,
        in_specs=[pl.BlockSpec((None, tm, D), lambda b, i: (b, i, 0)), full(g_mix), full(wn), full(wt)],
        out_specs=out_specs,
        out_shape=out_shape,
        scratch_shapes=[pltpu.VMEM((DIL_WIDTH // LANES, tm, LANES), F32)],
        compiler_params=_params("parallel", "parallel"),
        name="in_proj",
    )(x, g_mix, wn, wt)


def _compress_kernel(x_ref, pe_ref, w1_ref, w2_ref, w2t_ref, c_ref, ct_ref):
    nch = x_ref.shape[0] // CMP_STRIDE
    ulo = jnp.zeros((nch, NSA_GROUPS * CMP_HIDDEN), F32)
    uhi = jnp.zeros((nch, NSA_GROUPS * CMP_HIDDEN), F32)
    for j in range(CMP_STRIDE):
        xj = x_ref[pl.ds(j, nch, stride=CMP_STRIDE), :]
        ulo = ulo + _dot((xj + pe_ref[j:j + 1, :]).astype(BF16), w1_ref[j])
        uhi = uhi + _dot((xj + pe_ref[CMP_STRIDE + j:CMP_STRIDE + j + 1, :]).astype(BF16), w1_ref[CMP_STRIDE + j])
    pre = ulo + pltpu.roll(uhi, nch - 1, 0)
    hid = jax.nn.gelu(pre).astype(BF16)
    aug = _key_position_columns(CMP_BLOCK - 1, nch, CMP_STRIDE).astype(BF16)
    for g in range(NSA_GROUPS):
        hg = hid[:, g * CMP_HIDDEN:(g + 1) * CMP_HIDDEN]
        c_ref[g] = jnp.concatenate([_dot(hg, w2_ref[...]).astype(BF16), aug], axis=1)
        ct_ref[g] = _dot_nt(w2t_ref[...], hg).astype(BF16)


def _compress(xc, pe, w1, w2):
    B, S, width = xc.shape
    nch = S // CMP_STRIDE
    G, dh, hid = NSA_GROUPS, HEAD_DIM, CMP_HIDDEN
    pe_t = jnp.broadcast_to(pe.reshape(CMP_BLOCK, 1, dh), (CMP_BLOCK, G, dh)).reshape(CMP_BLOCK, width)
    eye = jnp.eye(G, dtype=w1.dtype)
    wexp = jnp.einsum('pdn,ge->pgden', w1.reshape(CMP_BLOCK, dh, hid), eye).reshape(CMP_BLOCK, width, G * hid)
    wexp = wexp.astype(BF16)
    full = lambda a: pl.BlockSpec(a.shape, lambda b: (0,) * a.ndim)
    w2b = w2.astype(BF16)
    w2t = w2.T.astype(BF16)
    return pl.pallas_call(
        _compress_kernel,
        grid=(B,),
        in_specs=[pl.BlockSpec((None, S, width), lambda b: (b, 0, 0)), full(pe_t), full(wexp), full(w2b), full(w2t)],
        out_specs=(pl.BlockSpec((None, G, nch, K_AUG), lambda b: (b, 0, 0, 0)),
                   pl.BlockSpec((None, G, dh, nch), lambda b: (b, 0, 0, 0))),
        out_shape=(jax.ShapeDtypeStruct((B, G, nch, K_AUG), BF16),
                   jax.ShapeDtypeStruct((B, G, dh, nch), BF16)),
        compiler_params=_params("parallel"),
        name="compress",
    )(xc, pe_t, wexp, w2b, w2t)


NSA_TQ = 128
SWEEP_TK = AUG_NBLK * SLC_BLOCK
SEL_SUB = 256
SEL_AHEAD = 3


def _rowmax8(s):
    return jnp.max(s.reshape(s.shape[0] // 8, 8, s.shape[1]), axis=0)


def _nsa_kernel(qt_ref, kc_ref, vct_ref, ks_ref, vst_ref, kw_ref, vwt_ref, gate_ref, ov_ref,
                o_ref, sbt_ref, osel_ref, *, slopes):
    g = pl.program_id(1)
    i = pl.program_id(2)
    R, dh, tq, tk = NSA_REP, HEAD_DIM, NSA_TQ, SWEEP_TK
    L = R * tq
    t0 = i * tq
    nc = kc_ref.shape[0]
    ns = ov_ref.shape[0]
    n_tiles = ns // AUG_NBLK
    tile4 = lambda a: jnp.concatenate([a] * R, axis=1)

    qt = qt_ref[...]
    qs = jnp.concatenate([qt[r * dh:(r + 1) * dh, :] for r in range(R)], axis=1)
    slope = [jnp.where(g == 0, slopes[r], slopes[R + r]).astype(F32) for r in range(R)]
    slope_row = jnp.concatenate([jnp.full((1, tq), 1.0, F32) * slope[r] for r in range(R)], axis=1)
    t_row = t0 + lax.broadcasted_iota(jnp.int32, (1, tq), 1)

    r8 = lax.broadcasted_iota(jnp.int32, (AUG_NBLK, L), 0)
    alibi8 = jnp.where(r8 < 2, slope_row, jnp.where(r8 == 2, -slope_row * t0.astype(F32), 0.0))
    q_pad = jnp.zeros((K_AUG - dh - 2 * AUG_NBLK, L), BF16)
    q_plain = jnp.concatenate([qs, jnp.concatenate([jnp.zeros((AUG_NBLK, L), F32), alibi8], axis=0).astype(BF16),
                               q_pad], axis=0)

    cmp_end = lax.broadcasted_iota(jnp.int32, (nc, tq), 0) * CMP_STRIDE + (CMP_BLOCK - 1)
    m_cmp = t_row >= cmp_end
    sc = _dot(kc_ref[...], q_plain) + tile4(jnp.where(m_cmp, 0.0, NEG_INF))

    a0 = pl.multiple_of(jnp.maximum(t0 - NSA_WINDOW, 0), tq)
    d0 = pl.multiple_of(t0, tq)
    s_a = _dot(kw_ref[pl.ds(a0, NSA_WINDOW), :], q_plain)
    s_d = _dot(kw_ref[pl.ds(d0, tq), :], q_plain)
    kpos_a = a0 + lax.broadcasted_iota(jnp.int32, (NSA_WINDOW, tq), 0)
    t_a = t0 + lax.broadcasted_iota(jnp.int32, (NSA_WINDOW, tq), 1)
    bias_a = jnp.where((kpos_a < t0) & (t_a - kpos_a < NSA_WINDOW), 0.0, NEG_INF)
    causal_d = jnp.where(lax.broadcasted_iota(jnp.int32, (tq, tq), 0)
                         <= lax.broadcasted_iota(jnp.int32, (tq, tq), 1), 0.0, NEG_INF)
    s_a = s_a + tile4(bias_a)
    s_d = s_d + tile4(causal_d)
    m_w = jnp.max(jnp.maximum(_rowmax8(s_a), _rowmax8(s_d)), axis=0, keepdims=True)
    acc_w = (_dot(vwt_ref[:, pl.ds(a0, NSA_WINDOW)], jnp.exp(s_a - m_w).astype(BF16))
             + _dot(vwt_ref[:, pl.ds(d0, tq)], jnp.exp(s_d - m_w).astype(BF16)))
    o_win = acc_w[:dh] * (1.0 / acc_w[dh:dh + 1])

    mx = jnp.max(_rowmax8(sc), axis=0, keepdims=True)
    p = jnp.exp(sc - mx) * tile4(jnp.where(m_cmp, 1.0, 0.0))
    den = jnp.sum(jnp.sum(p.reshape(nc // 8, 8, L), axis=0), axis=0, keepdims=True)
    pr = p * (1.0 / jnp.maximum(den, 1e-30))
    o_cmp = _dot(vct_ref[...], pr.astype(BF16))
    psum = pr[:, 0:tq]
    for r in range(1, R):
        psum = psum + pr[:, r * tq:(r + 1) * tq]

    p_hi = psum.astype(BF16)
    p_lo = (psum - p_hi.astype(F32)).astype(BF16)
    imp = _dot(ov_ref[...], p_hi) + _dot(ov_ref[...], p_lo)
    blk = lax.broadcasted_iota(jnp.int32, (ns, tq), 0)
    cur = lax.shift_right_logical(t_row, SLC_SHIFT)
    val = jnp.where((blk == cur) | (blk == 0), FORCE_SCORE, imp)
    val = jnp.where(blk <= cur, val, -1.0)
    vals = [val[8 * v:8 * v + 8, :] for v in range(ns // 8)]
    ranks = [jnp.zeros((8, tq), F32) for _ in vals]
    row8 = lax.broadcasted_iota(jnp.int32, (8, tq), 0)
    for j in range(ns):
        vj = jnp.broadcast_to(val[j:j + 1, :], (8, tq))
        for v in range(len(vals)):
            if 8 * v > j:
                ahead = vj >= vals[v]
            elif 8 * v + 7 <= j:
                ahead = vj > vals[v]
            else:
                ahead = (vj > vals[v]) | ((vj == vals[v]) & (row8 > j - 8 * v))
            ranks[v] = ranks[v] + jnp.where(ahead, 1.0, 0.0)
    for T in range(n_tiles):
        selb = jnp.where((ranks[T] < float(SLC_TOP)) & (vals[T] >= 0.0), 0.0, NEG_INF)
        sbt_ref[T] = jnp.concatenate([tile4(selb), alibi8], axis=0).astype(BF16)

    td = lax.div(t0, tk)
    for c in range(n_tiles):
        @pl.when(td == c)
        def _(c=c):
            kpos = c * tk + lax.broadcasted_iota(jnp.int32, (tk, tq), 0)
            t_q = t0 + lax.broadcasted_iota(jnp.int32, (tk, tq), 1)
            causal = tile4(jnp.where(kpos <= t_q, 0.0, NEG_INF))
            q_augs = [jnp.concatenate([qs, sbt_ref[T], q_pad], axis=0) for T in range(c + 1)]
            chunks = [(T, j * SEL_SUB) for T in range(c + 1) for j in range(tk // SEL_SUB)]

            def chunk_scores(T, r):
                s = _dot(ks_ref[T * tk + r:T * tk + r + SEL_SUB, :], q_augs[T])
                return s + causal[r:r + SEL_SUB] if T == c else s

            pending = [chunk_scores(*ch) for ch in chunks[:SEL_AHEAD]]
            m = acc = None
            for n, (T, r) in enumerate(chunks):
                if n + SEL_AHEAD < len(chunks):
                    pending.append(chunk_scores(*chunks[n + SEL_AHEAD]))
                s = pending.pop(0)
                v_t = vst_ref[:, T * tk + r:T * tk + r + SEL_SUB]
                m_c = jnp.max(_rowmax8(s), axis=0, keepdims=True)
                if m is None:
                    m = m_c
                    acc = _dot(v_t, jnp.exp(s - m).astype(BF16))
                else:
                    m_new = jnp.maximum(m, m_c)
                    acc = jnp.exp(m - m_new) * acc + _dot(v_t, jnp.exp(s - m_new).astype(BF16))
                    m = m_new
            osel_ref[...] = acc[:dh] * (1.0 / acc[dh:dh + 1])

    o_sel = osel_ref[...]
    gates = gate_ref[...]
    tiles = []
    for r in range(R):
        sl = slice(r * tq, (r + 1) * tq)
        tiles.append(gates[3 * r:3 * r + 1, :] * o_cmp[:, sl]
                     + gates[3 * r + 1:3 * r + 2, :] * o_sel[:, sl]
                     + gates[3 * r + 2:3 * r + 3, :] * o_win[:, sl])
    o_ref[...] = jnp.concatenate(tiles, axis=0).T.astype(o_ref.dtype)


def _overlap_matrix(nc, ns):
    cs = np.arange(nc)[None, :] * CMP_STRIDE
    ss = np.arange(ns)[:, None] * SLC_BLOCK
    ov = np.clip(np.minimum(cs + CMP_BLOCK, ss + SLC_BLOCK) - np.maximum(cs, ss), 0, None)
    return jnp.asarray(ov.astype(np.float32) / CMP_BLOCK, dtype=BF16)


def _nsa(qt, kc, vct, ks, vst, kw, vwt, gates):
    B, _, S = qt.shape
    G, R, dh, tq = NSA_GROUPS, NSA_REP, HEAD_DIM, NSA_TQ
    nc = kc.shape[2]
    ns = S // SLC_BLOCK
    ov = _overlap_matrix(nc, ns)
    kern = functools.partial(_nsa_kernel, slopes=tuple(_alibi_slopes(NSA_HEADS)))
    assert S % SWEEP_TK == 0 and S >= NSA_WINDOW + tq
    per_bg = lambda shape: pl.BlockSpec((None, None) + shape, lambda b, g, i: (b, g, 0, 0))
    return pl.pallas_call(
        kern,
        grid=(B, G, S // tq),
        in_specs=[
            pl.BlockSpec((None, R * dh, tq), lambda b, g, i: (b, g, i)),
            per_bg((nc, K_AUG)), per_bg((dh, nc)),
            per_bg((S, K_AUG)), per_bg((V_ROWS, S)),
            per_bg((S, K_AUG)), per_bg((V_ROWS, S)),
            pl.BlockSpec((None, GATE_ROWS, tq), lambda b, g, i: (b, g, i)),
            pl.BlockSpec(ov.shape, lambda b, g, i: (0, 0)),
        ],
        out_specs=pl.BlockSpec((None, tq, R * dh), lambda b, g, i: (b, i, g)),
        out_shape=jax.ShapeDtypeStruct((B, S, G * R * dh), BF16),
        scratch_shapes=[pltpu.VMEM((S // SWEEP_TK, 2 * AUG_NBLK, R * tq), BF16),
                        pltpu.VMEM((dh, R * tq), F32)],
        compiler_params=_params("parallel", "parallel", "arbitrary"),
        name="nsa_attention",
    )(qt, kc, vct, ks, vst, kw, vwt, gates, ov)


BAND_TQ = 128
BAND_SUB = 4
BAND = 128


def _banded_kernel(q_ref, k_ref, v_ref, o_ref, lse_ref, *, slopes, nsub):
    tq, dh, nh = BAND_TQ, HEAD_DIM, DIL_HEADS_PER_GROUP
    tk = tq + BAND
    width = nh * dh
    head_of_lane = lambda rows: lax.shift_right_logical(
        lax.broadcasted_iota(jnp.int32, (rows, width), 1), int(math.log2(dh)))
    lane_head, q_head = head_of_lane(tk), head_of_lane(tq)
    keep = [jnp.where(lane_head == h, 1.0, 0.0).astype(BF16) for h in range(nh)]

    def per_head(a):
        return jnp.concatenate([a * keep[h] for h in range(nh)], axis=0)

    ones_h = jnp.concatenate(keep, axis=0)
    def biases(first_key_offset):
        d = first_key_offset + (lax.broadcasted_iota(jnp.int32, (tq, tk), 0)
                                - lax.broadcasted_iota(jnp.int32, (tq, tk), 1))
        mask_bias = jnp.where((d >= 0) & (d <= BAND), 0.0, NEG_INF)
        neg_d = -d.astype(F32)
        return [slopes[h] * neg_d + mask_bias for h in range(nh)]

    subs = []
    for sub in range(nsub):
        i = pl.program_id(2) * nsub + sub
        k0 = pl.multiple_of(jnp.maximum(i - 1, 0) * tq, tq)
        q = q_ref[sub * tq:(sub + 1) * tq, :]
        subs.append((i * tq - k0, k0, _dot_nt(q, per_head(k_ref[pl.ds(k0, tk), :]))))
    inner_bias = biases(tq) if nsub > 1 else None
    probs = []
    for sub, (off, k0, s) in enumerate(subs):
        bias = biases(off) if sub == 0 else inner_bias
        ps, mxs = [], []
        for h in range(nh):
            sh = s[:, h * tk:(h + 1) * tk] + bias[h]
            mx = jnp.max(sh, axis=-1, keepdims=True)
            ps.append(jnp.exp(sh - mx).astype(BF16))
            mxs.append(mx)
        probs.append((k0, jnp.concatenate(ps, axis=1), mxs))
    for sub, (k0, p, mxs) in enumerate(probs):
        den = _dot(p, ones_h)
        mx_all = mxs[nh - 1]
        for h in range(nh - 2, -1, -1):
            mx_all = jnp.where(q_head == h, mxs[h], mx_all)
        o_ref[sub * tq:(sub + 1) * tq, :] = _dot(p, per_head(v_ref[pl.ds(k0, tk), :])) * (1.0 / den)
        lse_ref[sub * tq:(sub + 1) * tq, :] = mx_all + jnp.log(den)


def _banded(d, slopes):
    B, _, r, n, width = d.shape
    nsub = min(BAND_SUB, n // BAND_TQ)
    tq = BAND_TQ * nsub
    assert n >= BAND_TQ + BAND and n % tq == 0
    kern = functools.partial(_banded_kernel, slopes=tuple(slopes), nsub=nsub)
    tile = pl.BlockSpec((None, None, None, tq, width), lambda b, c, i: (b, 0, c, i, 0))
    kseq = pl.BlockSpec((None, None, None, n, width), lambda b, c, i: (b, 1, c, 0, 0))
    vseq = pl.BlockSpec((None, None, None, n, width), lambda b, c, i: (b, 2, c, 0, 0))
    out = pl.BlockSpec((None, None, tq, width), lambda b, c, i: (b, c, i, 0))
    return pl.pallas_call(
        kern,
        grid=(B, r, n // tq),
        in_specs=[tile, kseq, vseq],
        out_specs=(out, out),
        out_shape=(jax.ShapeDtypeStruct((B, r, n, width), F32),) * 2,
        compiler_params=_params("parallel", "parallel", "arbitrary"),
        name="banded_attention",
    )(d, d, d)


def _dilated(dils):
    slopes = _alibi_slopes(DIL_HEADS)
    outs, lses = [], []
    for gi, (w, r) in enumerate(DIL_CONFIGS):
        assert w // r == BAND
        sl = [s_ * r for s_ in slopes[gi * DIL_HEADS_PER_GROUP:(gi + 1) * DIL_HEADS_PER_GROUP]]
        o, lse = _banded(dils[gi], sl)
        outs.append(o)
        lses.append(lse)
    return outs, lses


MERGE_TM = 512


def _token_order(ref, slab_ref):
    r, rows, width = ref.shape
    if r == 1:
        return ref[0]
    for c in range(r):
        blk = ref[c]
        for s in range(width // LANES):
            slab_ref[s, pl.ds(c, rows, stride=r), :] = blk[:, s * LANES:(s + 1) * LANES]
    return jnp.concatenate([slab_ref[s] for s in range(width // LANES)], axis=1)


def _merge_kernel(x_ref, oa_ref, o0_ref, o1_ref, o2_ref, l0_ref, l1_ref, l2_ref,
                  gmix_ref, wm_ref, wpn_ref, wpd_ref, wo_ref, gffn_ref, x1_ref, h2_ref, slab_ref):
    x = x_ref[...]
    D = x.shape[1]
    h = _rms(x, gmix_ref[...]).astype(BF16)
    gm = _sigmoid(_dot(h, wm_ref[...]))
    o0, o1, o2 = [_token_order(r_, slab_ref) for r_ in (o0_ref, o1_ref, o2_ref)]
    l0, l1, l2 = [_token_order(r_, slab_ref) for r_ in (l0_ref, l1_ref, l2_ref)]
    mx = jnp.maximum(jnp.maximum(l0, l1), l2)
    e0, e1, e2 = jnp.exp(l0 - mx), jnp.exp(l1 - mx), jnp.exp(l2 - mx)
    inv = 1.0 / (e0 + e1 + e2)
    ob = o0 * (e0 * inv) + o1 * (e1 * inv) + o2 * (e2 * inv)
    a = _dot(oa_ref[...], wpn_ref[...])
    d = _dot(ob.astype(BF16), wpd_ref[...])
    mixed = gm[:, :D] * a + gm[:, D:] * d
    x1 = x + _dot(mixed.astype(BF16), wo_ref[...])
    x1_ref[...] = x1
    h2_ref[...] = _rms(x1, gffn_ref[...]).astype(BF16)


def _merge(x, o_a, outs, lses, g_mix, w_merge, w_proj_nsa, w_proj_dil, w_out, g_ffn):
    B, S, D = x.shape
    tm = MERGE_TM
    row = lambda a: pl.BlockSpec((None, tm, a.shape[2]), lambda b, i: (b, i, 0))
    cls = lambda a: pl.BlockSpec((None, a.shape[1], tm // a.shape[1], a.shape[3]), lambda b, i: (b, 0, i, 0))
    full = lambda a: pl.BlockSpec(a.shape, lambda b, i: (0,) * a.ndim)
    ws = [w_merge.astype(BF16), w_proj_nsa.astype(BF16), w_proj_dil.astype(BF16), w_out.astype(BF16)]
    consts = [g_mix, *ws, g_ffn]
    in_specs = [row(x), row(o_a)] + [cls(a) for a in (*outs, *lses)] + [full(a) for a in consts]
    return pl.pallas_call(
        _merge_kernel,
        grid=(B, S // tm),
        in_specs=in_specs,
        out_specs=(pl.BlockSpec((None, tm, D), lambda b, i: (b, i, 0)),) * 2,
        out_shape=(jax.ShapeDtypeStruct((B, S, D), F32), jax.ShapeDtypeStruct((B, S, D), BF16)),
        scratch_shapes=[pltpu.VMEM((DIL_WIDTH // LANES, tm, LANES), F32)],
        compiler_params=_params("parallel", "parallel"),
        name="merge_proj",
    )(x, o_a, *outs, *lses, *consts)


FFN_TM = 512
FFN_TN = 256
HALO = 16


def _ffn_kernel(h_ref, halo_ref, x1_ref, wu_ref, wg_ref, cw_ref, cb_ref, wd_ref, gfin_ref, o_ref, act_ref):
    i = pl.program_id(1)
    h = h_ref[...]
    halo = halo_ref[...]
    tm = h.shape[0]
    row = lax.broadcasted_iota(jnp.int32, (tm, FFN_TN), 0)
    live = (i > 0).astype(F32)
    for j in range(wu_ref.shape[0]):
        wu = wu_ref[j]
        u = _dot(h, wu)
        uh = _dot(halo, wu) * live
        gate = _dot(h, wg_ref[j])
        p1 = jnp.broadcast_to(uh[HALO - 1:HALO, :], (tm, FFN_TN))
        p2 = jnp.broadcast_to(uh[HALO - 2:HALO - 1, :], (tm, FFN_TN))
        u1 = jnp.where(row == 0, p1, pltpu.roll(u, 1, 0))
        u2 = jnp.where(row == 0, p2, jnp.where(row == 1, p1, pltpu.roll(u, 2, 0)))
        cw = cw_ref[j]
        uc = cb_ref[j] + cw[0:1, :] * u2
        uc = uc + cw[1:2, :] * u1
        uc = uc + cw[2:3, :] * u
        act_ref[:, j * FFN_TN:(j + 1) * FFN_TN] = (jax.nn.gelu(uc) * gate).astype(BF16)
    y = _dot(act_ref[...], wd_ref[...])
    o_ref[...] = _rms(x1_ref[...] + y, gfin_ref[...])


def _ffn(h2, x1, w_up, conv_w, conv_b, w_down, g_final):
    B, S, D = h2.shape
    tm, tn = FFN_TM, FFN_TN
    nj = D_FF // tn
    wu = w_up[:, :D_FF].reshape(D, nj, tn).transpose(1, 0, 2).astype(BF16)
    wg = w_up[:, D_FF:].reshape(D, nj, tn).transpose(1, 0, 2).astype(BF16)
    cw = jnp.pad(conv_w, ((0, 8 - CONV_WIDTH), (0, 0))).reshape(8, nj, tn).transpose(1, 0, 2)
    cb = conv_b.reshape(nj, 1, tn)
    wd = w_down.astype(BF16)
    gfin = g_final.reshape(1, D)
    full = lambda a: pl.BlockSpec(a.shape, lambda b, i: (0,) * a.ndim, pipeline_mode=pl.Buffered(1))
    tile = pl.BlockSpec((None, tm, D), lambda b, i: (b, i, 0))
    halo = pl.BlockSpec((None, HALO, D), lambda b, i: (b, jnp.maximum(i * (tm // HALO) - 1, 0), 0))
    return pl.pallas_call(
        _ffn_kernel,
        grid=(B, S // tm),
        in_specs=[tile, halo, tile, full(wu), full(wg), full(cw), full(cb), full(wd), full(gfin)],
        out_specs=tile,
        out_shape=jax.ShapeDtypeStruct((B, S, D), F32),
        scratch_shapes=[pltpu.VMEM((tm, D_FF), BF16)],
        compiler_params=_params("parallel", "parallel"),
        name="conv_ffn",
    )(h2, h2, x1, wu, wg, cw, cb, wd, gfin)


@jax.jit
def _layer(x, g_mix, w_in, pe_cmp_k, w_cmp_k1, w_cmp_k2, pe_cmp_v, w_cmp_v1, w_cmp_v2,
           w_proj_nsa, w_proj_dil, w_out, g_ffn, w_up, conv_w, conv_b, w_down, g_final):
    B, S, D = x.shape
    depth = g_mix.shape[0]
    for l in range(depth):
        gm = g_mix[l].reshape(1, D)
        kcmp, vcmp, ks, kw, d0, d1, d2, qt, vst, vwt, gates = _in_proj(x, gm, w_in[l])
        kc, _ = _compress(kcmp, pe_cmp_k[l], w_cmp_k1[l], w_cmp_k2[l])
        _, vct = _compress(vcmp, pe_cmp_v[l], w_cmp_v1[l], w_cmp_v2[l])
        o_a = _nsa(qt, kc, vct, ks, vst, kw, vwt, gates)
        outs, lses = _dilated((d0, d1, d2))
        merge_cols = w_in[l][:, w_in.shape[2] - 2 * D:]
        x1, h2 = _merge(x, o_a, outs, lses, gm, merge_cols, w_proj_nsa[l], w_proj_dil[l], w_out[l],
                        g_ffn[l].reshape(1, D))
        x = _ffn(h2, x1, w_up[l], conv_w[l], conv_b[l], w_down[l], g_final)
        assert depth == 1
    return x


def kernel(x, g_mix, w_in, pe_cmp_k, w_cmp_k1, w_cmp_k2, pe_cmp_v, w_cmp_v1, w_cmp_v2, w_proj_nsa, w_proj_dil, w_out, g_ffn, w_up, conv_w, conv_b, w_down, g_final):
    return _layer(x, g_mix, w_in, pe_cmp_k, w_cmp_k1, w_cmp_k2, pe_cmp_v, w_cmp_v1, w_cmp_v2,
                  w_proj_nsa, w_proj_dil, w_out, g_ffn, w_up, conv_w, conv_b, w_down, g_final)
```

```python
import functools
import math

import numpy as np
import jax
import jax.numpy as jnp
from jax import lax
from jax.experimental import pallas as pl
from jax.experimental.pallas import tpu as pltpu

HEAD_DIM = 64
NSA_HEADS = 8
NSA_GROUPS = 2
NSA_REP = NSA_HEADS // NSA_GROUPS
CMP_BLOCK = 32
CMP_STRIDE = 16
CMP_HIDDEN = 128
SLC_BLOCK = 64
SLC_TOP = 16
NSA_WINDOW = 512
FORCE_SCORE = 1.0e4
DIL_CONFIGS = ((128, 1), (512, 4), (2048, 16))
DIL_GROUPS = 3
DIL_HEADS_PER_GROUP = 4
DIL_HEADS = DIL_GROUPS * DIL_HEADS_PER_GROUP
D_FF = 2816
CONV_WIDTH = 3
RMS_EPS = 1e-6
NEG_INF = -1e30

LANES = 128
VMEM_LIMIT_BYTES = 56 * 1024 * 1024

F32 = jnp.float32
BF16 = jnp.bfloat16
NT_DIMS = (((1,), (1,)), ((), ()))


def _alibi_slopes(n):
    return [float(2.0 ** (-8.0 * i / n)) for i in range(1, n + 1)]


def _rms(xf, g):
    ms = jnp.mean(xf * xf, axis=-1, keepdims=True)
    return xf * lax.rsqrt(ms + RMS_EPS) * g


def _dot(a, b):
    return jnp.dot(a, b, preferred_element_type=F32)


def _dot_nt(a, b):
    return lax.dot_general(a, b, NT_DIMS, preferred_element_type=F32)


def _sigmoid(z):
    return 1.0 / (1.0 + jnp.exp(-z))


def _params(*sem):
    return pltpu.CompilerParams(dimension_semantics=sem, vmem_limit_bytes=VMEM_LIMIT_BYTES)


IN_TM = 512
N_KVC = 4 * HEAD_DIM
N_KSEL = NSA_GROUPS * HEAD_DIM
N_DIL = 3 * DIL_HEADS * HEAD_DIM
DIL_WIDTH = DIL_HEADS_PER_GROUP * HEAD_DIM
T_Q = NSA_HEADS * HEAD_DIM
T_V = NSA_GROUPS * HEAD_DIM
GATE_ROWS = 16
K_AUG = 2 * HEAD_DIM
SLC_SHIFT = int(math.log2(SLC_BLOCK))
AUG_NBLK = 8
AUG_HI, AUG_LO, AUG_ONE = AUG_NBLK, AUG_NBLK + 1, AUG_NBLK + 2
V_PAD = 16
V_ROWS = HEAD_DIM + V_PAD


def _key_position_columns(pos0, rows, step=1):
    pos = pos0 + step * lax.broadcasted_iota(jnp.int32, (rows, HEAD_DIM), 0)
    col = lax.broadcasted_iota(jnp.int32, (rows, HEAD_DIM), 1)
    blk = jnp.bitwise_and(lax.shift_right_logical(pos, SLC_SHIFT), AUG_NBLK - 1)
    hi = lax.shift_left(lax.shift_right_logical(pos, 7), 7).astype(F32)
    lo = jnp.bitwise_and(pos, 127).astype(F32)
    c = jnp.where(col == AUG_HI, hi, jnp.where(col == AUG_LO, lo, jnp.where(col == AUG_ONE, 1.0, 0.0)))
    return jnp.where((col < AUG_NBLK) & (blk == col), 1.0, c)


def _in_proj_kernel(x_ref, g_ref, wn_ref, wt_ref,
                    kcmp_ref, vcmp_ref, ks_ref, kw_ref, d0_ref, d1_ref, d2_ref, qt_ref, vst_ref, vwt_ref, gate_ref,
                    slab_ref):
    tm = x_ref.shape[0]
    h = _rms(x_ref[...], g_ref[...]).astype(BF16)
    c0 = 0
    kvc = _dot(h, wn_ref[:, c0:c0 + N_KVC])
    kcmp_ref[...] = kvc[:, :N_KSEL]
    vcmp_ref[...] = kvc[:, N_KSEL:]
    c0 += N_KVC
    ks = _dot(h, wn_ref[:, c0:c0 + N_KSEL]).astype(BF16)
    c0 += N_KSEL
    kw = _dot(h, wn_ref[:, c0:c0 + N_KSEL]).astype(BF16)
    c0 += N_KSEL
    aug = _key_position_columns(pl.program_id(1) * tm, tm).astype(BF16)
    for g in range(NSA_GROUPS):
        ks_ref[g] = jnp.concatenate([ks[:, g * HEAD_DIM:(g + 1) * HEAD_DIM], aug], axis=1)
        kw_ref[g] = jnp.concatenate([kw[:, g * HEAD_DIM:(g + 1) * HEAD_DIM], aug], axis=1)
    seg = DIL_HEADS * HEAD_DIM
    for which in range(3):
        y = _dot(h, wn_ref[:, c0 + which * seg:c0 + (which + 1) * seg])
        for gi, (d_ref, (_, r)) in enumerate(zip((d0_ref, d1_ref, d2_ref), DIL_CONFIGS)):
            yg = y[:, gi * DIL_WIDTH:(gi + 1) * DIL_WIDTH]
            if r == 1:
                d_ref[which, 0] = yg.astype(BF16)
                continue
            for s in range(DIL_WIDTH // LANES):
                slab_ref[s] = yg[:, s * LANES:(s + 1) * LANES]
            for c in range(r):
                d_ref[which, c] = jnp.concatenate(
                    [slab_ref[s, pl.ds(c, tm // r, stride=r), :] for s in range(DIL_WIDTH // LANES)],
                    axis=1).astype(BF16)
    yt = _dot_nt(wt_ref[...], h)
    qt_ref[...] = yt[0:T_Q].astype(BF16)
    r0 = T_Q
    ones = jnp.where(lax.broadcasted_iota(jnp.int32, (V_PAD, tm), 0) == 0, 1.0, 0.0).astype(BF16)
    for ref in (vst_ref, vwt_ref):
        vt = yt[r0:r0 + T_V].astype(BF16)
        r0 += T_V
        for g in range(NSA_GROUPS):
            ref[g] = jnp.concatenate([vt[g * HEAD_DIM:(g + 1) * HEAD_DIM, :], ones], axis=0)
    gate_ref[...] = _sigmoid(yt[r0:r0 + NSA_GROUPS * GATE_ROWS])


def _in_proj(x, g_mix, w_in):
    B, S, D = x.shape
    scale = HEAD_DIM ** -0.5
    o_q, o_kv = 0, T_Q
    o_gate = o_kv + 6 * N_KSEL
    o_dil = o_gate + 3 * NSA_HEADS
    o_merge = o_dil + N_DIL
    kv = w_in[:, o_kv:o_gate]

    def kind(k):
        return kv[:, k * N_KSEL:(k + 1) * N_KSEL]

    dil = w_in[:, o_dil:o_merge]
    dil = jnp.concatenate([dil[:, :DIL_HEADS * HEAD_DIM] * scale, dil[:, DIL_HEADS * HEAD_DIM:]], axis=1)
    wn = jnp.concatenate([kind(0), kind(1), kind(2), kind(4), dil], axis=1).astype(BF16)
    wg = w_in[:, o_gate:o_dil].reshape(D, NSA_GROUPS, 3 * NSA_REP)
    wg = jnp.pad(wg, ((0, 0), (0, 0), (0, GATE_ROWS - 3 * NSA_REP))).reshape(D, NSA_GROUPS * GATE_ROWS)
    wt = jnp.concatenate([w_in[:, o_q:o_kv] * scale, kind(3), kind(5), wg], axis=1).T.astype(BF16)
    tm = IN_TM
    grid = (B, S // tm)
    full = lambda a: pl.BlockSpec(a.shape, lambda b, i: (0,) * a.ndim)
    k_shape = jax.ShapeDtypeStruct((B, NSA_GROUPS, S, K_AUG), BF16)
    v_shape = jax.ShapeDtypeStruct((B, NSA_GROUPS, V_ROWS, S), BF16)
    k_spec = pl.BlockSpec((None, NSA_GROUPS, tm, K_AUG), lambda b, i: (b, 0, i, 0))
    v_spec = pl.BlockSpec((None, NSA_GROUPS, V_ROWS, tm), lambda b, i: (b, 0, 0, i))
    c_shape = jax.ShapeDtypeStruct((B, S, N_KSEL), F32)
    c_spec = pl.BlockSpec((None, tm, N_KSEL), lambda b, i: (b, i, 0))
    d_shapes = tuple(jax.ShapeDtypeStruct((B, 3, r, S // r, DIL_WIDTH), BF16) for _, r in DIL_CONFIGS)
    d_specs = tuple(pl.BlockSpec((None, 3, r, tm // r, DIL_WIDTH), lambda b, i: (b, 0, 0, i, 0))
                    for _, r in DIL_CONFIGS)
    out_shape = (
        c_shape, c_shape, k_shape, k_shape, *d_shapes,
        jax.ShapeDtypeStruct((B, T_Q, S), BF16),
        v_shape,
        v_shape,
        jax.ShapeDtypeStruct((B, NSA_GROUPS * GATE_ROWS, S), F32),
    )
    out_specs = (
        c_spec, c_spec, k_spec, k_spec, *d_specs,
        pl.BlockSpec((None, T_Q, tm), lambda b, i: (b, 0, i)),
        v_spec,
        v_spec,
        pl.BlockSpec((None, NSA_GROUPS * GATE_ROWS, tm), lambda b, i: (b, 0, i)),
    )
    return pl.pallas_call(
        _in_proj_kernel,
        grid=grid,
        in_specs=[pl.BlockSpec((None, tm, D), lambda b, i: (b, i, 0)), full(g_mix), full(wn), full(wt)],
        out_specs=out_specs,
        out_shape=out_shape,
        scratch_shapes=[pltpu.VMEM((DIL_WIDTH // LANES, tm, LANES), F32)],
        compiler_params=_params("parallel", "parallel"),
        name="in_proj",
    )(x, g_mix, wn, wt)


def _compress_kernel(x_ref, pe_ref, w1_ref, w2_ref, w2t_ref, c_ref, ct_ref):
    nch = x_ref.shape[0] // CMP_STRIDE
    ulo = jnp.zeros((nch, NSA_GROUPS * CMP_HIDDEN), F32)
    uhi = jnp.zeros((nch, NSA_GROUPS * CMP_HIDDEN), F32)
    for j in range(CMP_STRIDE):
        xj = x_ref[pl.ds(j, nch, stride=CMP_STRIDE), :]
        ulo = ulo + _dot((xj + pe_ref[j:j + 1, :]).astype(BF16), w1_ref[j])
        uhi = uhi + _dot((xj + pe_ref[CMP_STRIDE + j:CMP_STRIDE + j + 1, :]).astype(BF16), w1_ref[CMP_STRIDE + j])
    pre = ulo + pltpu.roll(uhi, nch - 1, 0)
    hid = jax.nn.gelu(pre).astype(BF16)
    aug = _key_position_columns(CMP_BLOCK - 1, nch, CMP_STRIDE).astype(BF16)
    for g in range(NSA_GROUPS):
        hg = hid[:, g * CMP_HIDDEN:(g + 1) * CMP_HIDDEN]
        c_ref[g] = jnp.concatenate([_dot(hg, w2_ref[...]).astype(BF16), aug], axis=1)
        ct_ref[g] = _dot_nt(w2t_ref[...], hg).astype(BF16)


def _compress(xc, pe, w1, w2):
    B, S, width = xc.shape
    nch = S // CMP_STRIDE
    G, dh, hid = NSA_GROUPS, HEAD_DIM, CMP_HIDDEN
    pe_t = jnp.broadcast_to(pe.reshape(CMP_BLOCK, 1, dh), (CMP_BLOCK, G, dh)).reshape(CMP_BLOCK, width)
    eye = jnp.eye(G, dtype=w1.dtype)
    wexp = jnp.einsum('pdn,ge->pgden', w1.reshape(CMP_BLOCK, dh, hid), eye).reshape(CMP_BLOCK, width, G * hid)
    wexp = wexp.astype(BF16)
    full = lambda a: pl.BlockSpec(a.shape, lambda b: (0,) * a.ndim)
    w2b = w2.astype(BF16)
    w2t = w2.T.astype(BF16)
    return pl.pallas_call(
        _compress_kernel,
        grid=(B,),
        in_specs=[pl.BlockSpec((None, S, width), lambda b: (b, 0, 0)), full(pe_t), full(wexp), full(w2b), full(w2t)],
        out_specs=(pl.BlockSpec((None, G, nch, K_AUG), lambda b: (b, 0, 0, 0)),
                   pl.BlockSpec((None, G, dh, nch), lambda b: (b, 0, 0, 0))),
        out_shape=(jax.ShapeDtypeStruct((B, G, nch, K_AUG), BF16),
                   jax.ShapeDtypeStruct((B, G, dh, nch), BF16)),
        compiler_params=_params("parallel"),
        name="compress",
    )(xc, pe_t, wexp, w2b, w2t)


NSA_TQ = 128
SWEEP_TK = AUG_NBLK * SLC_BLOCK
SEL_SUB = 256
SEL_AHEAD = 3


def _rowmax8(s):
    return jnp.max(s.reshape(s.shape[0] // 8, 8, s.shape[1]), axis=0)


def _nsa_kernel(qt_ref, kc_ref, vct_ref, ks_ref, vst_ref, kw_ref, vwt_ref, gate_ref, ov_ref,
                o_ref, sbt_ref, osel_ref, *, slopes):
    g = pl.program_id(1)
    i = pl.program_id(2)
    R, dh, tq, tk = NSA_REP, HEAD_DIM, NSA_TQ, SWEEP_TK
    L = R * tq
    t0 = i * tq
    nc = kc_ref.shape[0]
    ns = ov_ref.shape[0]
    n_tiles = ns // AUG_NBLK
    tile4 = lambda a: jnp.concatenate([a] * R, axis=1)

    qt = qt_ref[...]
    qs = jnp.concatenate([qt[r * dh:(r + 1) * dh, :] for r in range(R)], axis=1)
    slope = [jnp.where(g == 0, slopes[r], slopes[R + r]).astype(F32) for r in range(R)]
    slope_row = jnp.concatenate([jnp.full((1, tq), 1.0, F32) * slope[r] for r in range(R)], axis=1)
    t_row = t0 + lax.broadcasted_iota(jnp.int32, (1, tq), 1)

    r8 = lax.broadcasted_iota(jnp.int32, (AUG_NBLK, L), 0)
    alibi8 = jnp.where(r8 < 2, slope_row, jnp.where(r8 == 2, -slope_row * t0.astype(F32), 0.0))
    q_pad = jnp.zeros((K_AUG - dh - 2 * AUG_NBLK, L), BF16)
    q_plain = jnp.concatenate([qs, jnp.concatenate([jnp.zeros((AUG_NBLK, L), F32), alibi8], axis=0).astype(BF16),
                               q_pad], axis=0)

    cmp_end = lax.broadcasted_iota(jnp.int32, (nc, tq), 0) * CMP_STRIDE + (CMP_BLOCK - 1)
    m_cmp = t_row >= cmp_end
    sc = _dot(kc_ref[...], q_plain) + tile4(jnp.where(m_cmp, 0.0, NEG_INF))

    a0 = pl.multiple_of(jnp.maximum(t0 - NSA_WINDOW, 0), tq)
    d0 = pl.multiple_of(t0, tq)
    s_a = _dot(kw_ref[pl.ds(a0, NSA_WINDOW), :], q_plain)
    s_d = _dot(kw_ref[pl.ds(d0, tq), :], q_plain)
    kpos_a = a0 + lax.broadcasted_iota(jnp.int32, (NSA_WINDOW, tq), 0)
    t_a = t0 + lax.broadcasted_iota(jnp.int32, (NSA_WINDOW, tq), 1)
    bias_a = jnp.where((kpos_a < t0) & (t_a - kpos_a < NSA_WINDOW), 0.0, NEG_INF)
    causal_d = jnp.where(lax.broadcasted_iota(jnp.int32, (tq, tq), 0)
                         <= lax.broadcasted_iota(jnp.int32, (tq, tq), 1), 0.0, NEG_INF)
    s_a = s_a + tile4(bias_a)
    s_d = s_d + tile4(causal_d)
    m_w = jnp.max(jnp.maximum(_rowmax8(s_a), _rowmax8(s_d)), axis=0, keepdims=True)
    acc_w = (_dot(vwt_ref[:, pl.ds(a0, NSA_WINDOW)], jnp.exp(s_a - m_w).astype(BF16))
             + _dot(vwt_ref[:, pl.ds(d0, tq)], jnp.exp(s_d - m_w).astype(BF16)))
    o_win = acc_w[:dh] * (1.0 / acc_w[dh:dh + 1])

    mx = jnp.max(_rowmax8(sc), axis=0, keepdims=True)
    p = jnp.exp(sc - mx) * tile4(jnp.where(m_cmp, 1.0, 0.0))
    den = jnp.sum(jnp.sum(p.reshape(nc // 8, 8, L), axis=0), axis=0, keepdims=True)
    pr = p * (1.0 / jnp.maximum(den, 1e-30))
    o_cmp = _dot(vct_ref[...], pr.astype(BF16))
    psum = pr[:, 0:tq]
    for r in range(1, R):
        psum = psum + pr[:, r * tq:(r + 1) * tq]

    p_hi = psum.astype(BF16)
    p_lo = (psum - p_hi.astype(F32)).astype(BF16)
    imp = _dot(ov_ref[...], p_hi) + _dot(ov_ref[...], p_lo)
    blk = lax.broadcasted_iota(jnp.int32, (ns, tq), 0)
    cur = lax.shift_right_logical(t_row, SLC_SHIFT)
    val = jnp.where((blk == cur) | (blk == 0), FORCE_SCORE, imp)
    val = jnp.where(blk <= cur, val, -1.0)
    vals = [val[8 * v:8 * v + 8, :] for v in range(ns // 8)]
    ranks = [jnp.zeros((8, tq), F32) for _ in vals]
    row8 = lax.broadcasted_iota(jnp.int32, (8, tq), 0)
    for j in range(ns):
        vj = jnp.broadcast_to(val[j:j + 1, :], (8, tq))
        for v in range(len(vals)):
            if 8 * v > j:
                ahead = vj >= vals[v]
            elif 8 * v + 7 <= j:
                ahead = vj > vals[v]
            else:
                ahead = (vj > vals[v]) | ((vj == vals[v]) & (row8 > j - 8 * v))
            ranks[v] = ranks[v] + jnp.where(ahead, 1.0, 0.0)
    for T in range(n_tiles):
        selb = jnp.where((ranks[T] < float(SLC_TOP)) & (vals[T] >= 0.0), 0.0, NEG_INF)
        sbt_ref[T] = jnp.concatenate([tile4(selb), alibi8], axis=0).astype(BF16)

    td = lax.div(t0, tk)
    for c in range(n_tiles):
        @pl.when(td == c)
        def _(c=c):
            kpos = c * tk + lax.broadcasted_iota(jnp.int32, (tk, tq), 0)
            t_q = t0 + lax.broadcasted_iota(jnp.int32, (tk, tq), 1)
            causal = tile4(jnp.where(kpos <= t_q, 0.0, NEG_INF))
            q_augs = [jnp.concatenate([qs, sbt_ref[T], q_pad], axis=0) for T in range(c + 1)]
            chunks = [(T, j * SEL_SUB) for T in range(c + 1) for j in range(tk // SEL_SUB)]

            def chunk_scores(T, r):
                s = _dot(ks_ref[T * tk + r:T * tk + r + SEL_SUB, :], q_augs[T])
                return s + causal[r:r + SEL_SUB] if T == c else s

            pending = [chunk_scores(*ch) for ch in chunks[:SEL_AHEAD]]
            m = acc = None
            for n, (T, r) in enumerate(chunks):
                if n + SEL_AHEAD < len(chunks):
                    pending.append(chunk_scores(*chunks[n + SEL_AHEAD]))
                s = pending.pop(0)
                v_t = vst_ref[:, T * tk + r:T * tk + r + SEL_SUB]
                m_c = jnp.max(_rowmax8(s), axis=0, keepdims=True)
                if m is None:
                    m = m_c
                    acc = _dot(v_t, jnp.exp(s - m).astype(BF16))
                else:
                    m_new = jnp.maximum(m, m_c)
                    acc = jnp.exp(m - m_new) * acc + _dot(v_t, jnp.exp(s - m_new).astype(BF16))
                    m = m_new
            osel_ref[...] = acc[:dh] * (1.0 / acc[dh:dh + 1])

    o_sel = osel_ref[...]
    gates = gate_ref[...]
    tiles = []
    for r in range(R):
        sl = slice(r * tq, (r + 1) * tq)
        tiles.append(gates[3 * r:3 * r + 1, :] * o_cmp[:, sl]
                     + gates[3 * r + 1:3 * r + 2, :] * o_sel[:, sl]
                     + gates[3 * r + 2:3 * r + 3, :] * o_win[:, sl])
    o_ref[...] = jnp.concatenate(tiles, axis=0).T.astype(o_ref.dtype)


def _overlap_matrix(nc, ns):
    cs = np.arange(nc)[None, :] * CMP_STRIDE
    ss = np.arange(ns)[:, None] * SLC_BLOCK
    ov = np.clip(np.minimum(cs + CMP_BLOCK, ss + SLC_BLOCK) - np.maximum(cs, ss), 0, None)
    return jnp.asarray(ov.astype(np.float32) / CMP_BLOCK, dtype=BF16)


def _nsa(qt, kc, vct, ks, vst, kw, vwt, gates):
    B, _, S = qt.shape
    G, R, dh, tq = NSA_GROUPS, NSA_REP, HEAD_DIM, NSA_TQ
    nc = kc.shape[2]
    ns = S // SLC_BLOCK
    ov = _overlap_matrix(nc, ns)
    kern = functools.partial(_nsa_kernel, slopes=tuple(_alibi_slopes(NSA_HEADS)))
    assert S % SWEEP_TK == 0 and S >= NSA_WINDOW + tq
    per_bg = lambda shape: pl.BlockSpec((None, None) + shape, lambda b, g, i: (b, g, 0, 0))
    return pl.pallas_call(
        kern,
        grid=(B, G, S // tq),
        in_specs=[
            pl.BlockSpec((None, R * dh, tq), lambda b, g, i: (b, g, i)),
            per_bg((nc, K_AUG)), per_bg((dh, nc)),
            per_bg((S, K_AUG)), per_bg((V_ROWS, S)),
            per_bg((S, K_AUG)), per_bg((V_ROWS, S)),
            pl.BlockSpec((None, GATE_ROWS, tq), lambda b, g, i: (b, g, i)),
            pl.BlockSpec(ov.shape, lambda b, g, i: (0, 0)),
        ],
        out_specs=pl.BlockSpec((None, tq, R * dh), lambda b, g, i: (b, i, g)),
        out_shape=jax.ShapeDtypeStruct((B, S, G * R * dh), BF16),
        scratch_shapes=[pltpu.VMEM((S // SWEEP_TK, 2 * AUG_NBLK, R * tq), BF16),
                        pltpu.VMEM((dh, R * tq), F32)],
        compiler_params=_params("parallel", "parallel", "arbitrary"),
        name="nsa_attention",
    )(qt, kc, vct, ks, vst, kw, vwt, gates, ov)


BAND_TQ = 128
BAND_SUB = 4
BAND = 128


def _banded_kernel(q_ref, k_ref, v_ref, o_ref, lse_ref, *, slopes, nsub):
    tq, dh, nh = BAND_TQ, HEAD_DIM, DIL_HEADS_PER_GROUP
    tk = tq + BAND
    width = nh * dh
    head_of_lane = lambda rows: lax.shift_right_logical(
        lax.broadcasted_iota(jnp.int32, (rows, width), 1), int(math.log2(dh)))
    lane_head, q_head = head_of_lane(tk), head_of_lane(tq)
    keep = [jnp.where(lane_head == h, 1.0, 0.0).astype(BF16) for h in range(nh)]

    def per_head(a):
        return jnp.concatenate([a * keep[h] for h in range(nh)], axis=0)

    ones_h = jnp.concatenate(keep, axis=0)
    def biases(first_key_offset):
        d = first_key_offset + (lax.broadcasted_iota(jnp.int32, (tq, tk), 0)
                                - lax.broadcasted_iota(jnp.int32, (tq, tk), 1))
        mask_bias = jnp.where((d >= 0) & (d <= BAND), 0.0, NEG_INF)
        neg_d = -d.astype(F32)
        return [slopes[h] * neg_d + mask_bias for h in range(nh)]

    subs = []
    for sub in range(nsub):
        i = pl.program_id(2) * nsub + sub
        k0 = pl.multiple_of(jnp.maximum(i - 1, 0) * tq, tq)
        q = q_ref[sub * tq:(sub + 1) * tq, :]
        subs.append((i * tq - k0, k0, _dot_nt(q, per_head(k_ref[pl.ds(k0, tk), :]))))
    inner_bias = biases(tq) if nsub > 1 else None
    probs = []
    for sub, (off, k0, s) in enumerate(subs):
        bias = biases(off) if sub == 0 else inner_bias
        ps, mxs = [], []
        for h in range(nh):
            sh = s[:, h * tk:(h + 1) * tk] + bias[h]
            mx = jnp.max(sh, axis=-1, keepdims=True)
            ps.append(jnp.exp(sh - mx).astype(BF16))
            mxs.append(mx)
        probs.append((k0, jnp.concatenate(ps, axis=1), mxs))
    for sub, (k0, p, mxs) in enumerate(probs):
        den = _dot(p, ones_h)
        mx_all = mxs[nh - 1]
        for h in range(nh - 2, -1, -1):
            mx_all = jnp.where(q_head == h, mxs[h], mx_all)
        o_ref[sub * tq:(sub + 1) * tq, :] = _dot(p, per_head(v_ref[pl.ds(k0, tk), :])) * (1.0 / den)
        lse_ref[sub * tq:(sub + 1) * tq, :] = mx_all + jnp.log(den)


def _banded(d, slopes):
    B, _, r, n, width = d.shape
    nsub = min(BAND_SUB, n // BAND_TQ)
    tq = BAND_TQ * nsub
    assert n >= BAND_TQ + BAND and n % tq == 0
    kern = functools.partial(_banded_kernel, slopes=tuple(slopes), nsub=nsub)
    tile = pl.BlockSpec((None, None, None, tq, width), lambda b, c, i: (b, 0, c, i, 0))
    kseq = pl.BlockSpec((None, None, None, n, width), lambda b, c, i: (b, 1, c, 0, 0))
    vseq = pl.BlockSpec((None, None, None, n, width), lambda b, c, i: (b, 2, c, 0, 0))
    out = pl.BlockSpec((None, None, tq, width), lambda b, c, i: (b, c, i, 0))
    return pl.pallas_call(
        kern,
        grid=(B, r, n // tq),
        in_specs=[tile, kseq, vseq],
        out_specs=(out, out),
        out_shape=(jax.ShapeDtypeStruct((B, r, n, width), F32),) * 2,
        compiler_params=_params("parallel", "parallel", "arbitrary"),
        name="banded_attention",
    )(d, d, d)


def _dilated(dils):
    slopes = _alibi_slopes(DIL_HEADS)
    outs, lses = [], []
    for gi, (w, r) in enumerate(DIL_CONFIGS):
        assert w // r == BAND
        sl = [s_ * r for s_ in slopes[gi * DIL_HEADS_PER_GROUP:(gi + 1) * DIL_HEADS_PER_GROUP]]
        o, lse = _banded(dils[gi], sl)
        outs.append(o)
        lses.append(lse)
    return outs, lses


MERGE_TM = 512


def _token_order(ref, slab_ref):
    r, rows, width = ref.shape
    if r == 1:
        return ref[0]
    for c in range(r):
        blk = ref[c]
        for s in range(width // LANES):
            slab_ref[s, pl.ds(c, rows, stride=r), :] = blk[:, s * LANES:(s + 1) * LANES]
    return jnp.concatenate([slab_ref[s] for s in range(width // LANES)], axis=1)


def _merge_kernel(x_ref, oa_ref, o0_ref, o1_ref, o2_ref, l0_ref, l1_ref, l2_ref,
                  gmix_ref, wm_ref, wpn_ref, wpd_ref, wo_ref, gffn_ref, x1_ref, h2_ref, slab_ref):
    x = x_ref[...]
    D = x.shape[1]
    h = _rms(x, gmix_ref[...]).astype(BF16)
    gm = _sigmoid(_dot(h, wm_ref[...]))
    o0, o1, o2 = [_token_order(r_, slab_ref) for r_ in (o0_ref, o1_ref, o2_ref)]
    l0, l1, l2 = [_token_order(r_, slab_ref) for r_ in (l0_ref, l1_ref, l2_ref)]
    mx = jnp.maximum(jnp.maximum(l0, l1), l2)
    e0, e1, e2 = jnp.exp(l0 - mx), jnp.exp(l1 - mx), jnp.exp(l2 - mx)
    inv = 1.0 / (e0 + e1 + e2)
    ob = o0 * (e0 * inv) + o1 * (e1 * inv) + o2 * (e2 * inv)
    a = _dot(oa_ref[...], wpn_ref[...])
    d = _dot(ob.astype(BF16), wpd_ref[...])
    mixed = gm[:, :D] * a + gm[:, D:] * d
    x1 = x + _dot(mixed.astype(BF16), wo_ref[...])
    x1_ref[...] = x1
    h2_ref[...] = _rms(x1, gffn_ref[...]).astype(BF16)


def _merge(x, o_a, outs, lses, g_mix, w_merge, w_proj_nsa, w_proj_dil, w_out, g_ffn):
    B, S, D = x.shape
    tm = MERGE_TM
    row = lambda a: pl.BlockSpec((None, tm, a.shape[2]), lambda b, i: (b, i, 0))
    cls = lambda a: pl.BlockSpec((None, a.shape[1], tm // a.shape[1], a.shape[3]), lambda b, i: (b, 0, i, 0))
    full = lambda a: pl.BlockSpec(a.shape, lambda b, i: (0,) * a.ndim)
    ws = [w_merge.astype(BF16), w_proj_nsa.astype(BF16), w_proj_dil.astype(BF16), w_out.astype(BF16)]
    consts = [g_mix, *ws, g_ffn]
    in_specs = [row(x), row(o_a)] + [cls(a) for a in (*outs, *lses)] + [full(a) for a in consts]
    return pl.pallas_call(
        _merge_kernel,
        grid=(B, S // tm),
        in_specs=in_specs,
        out_specs=(pl.BlockSpec((None, tm, D), lambda b, i: (b, i, 0)),) * 2,
        out_shape=(jax.ShapeDtypeStruct((B, S, D), F32), jax.ShapeDtypeStruct((B, S, D), BF16)),
        scratch_shapes=[pltpu.VMEM((DIL_WIDTH // LANES, tm, LANES), F32)],
        compiler_params=_params("parallel", "parallel"),
        name="merge_proj",
    )(x, o_a, *outs, *lses, *consts)


FFN_TM = 512
FFN_TN = 256
HALO = 16


def _ffn_kernel(h_ref, halo_ref, x1_ref, wup_ref, cw_ref, cb_ref, wd_ref, gfin_ref, o_ref, act_ref):
    i = pl.program_id(1)
    h = h_ref[...]
    halo = halo_ref[...]
    tm = h.shape[0]
    row = lax.broadcasted_iota(jnp.int32, (tm, FFN_TN), 0)
    live = (i > 0).astype(F32)
    for j in range(D_FF // FFN_TN):
        cols = slice(j * FFN_TN, (j + 1) * FFN_TN)
        wu = wup_ref[:, cols]
        u = _dot(h, wu)
        uh = _dot(halo, wu) * live
        gate = _dot(h, wup_ref[:, D_FF + j * FFN_TN:D_FF + (j + 1) * FFN_TN])
        p1 = jnp.broadcast_to(uh[HALO - 1:HALO, :], (tm, FFN_TN))
        p2 = jnp.broadcast_to(uh[HALO - 2:HALO - 1, :], (tm, FFN_TN))
        u1 = jnp.where(row == 0, p1, pltpu.roll(u, 1, 0))
        u2 = jnp.where(row == 0, p2, jnp.where(row == 1, p1, pltpu.roll(u, 2, 0)))
        uc = cb_ref[:, cols] + cw_ref[0:1, cols] * u2
        uc = uc + cw_ref[1:2, cols] * u1
        uc = uc + cw_ref[2:3, cols] * u
        act_ref[:, j * FFN_TN:(j + 1) * FFN_TN] = (jax.nn.gelu(uc) * gate).astype(BF16)
    y = _dot(act_ref[...], wd_ref[...])
    o_ref[...] = _rms(x1_ref[...] + y, gfin_ref[...])


def _ffn(h2, x1, w_up, conv_w, conv_b, w_down, g_final):
    B, S, D = h2.shape
    tm = FFN_TM
    assert D_FF % FFN_TN == 0
    wup = w_up.astype(BF16)
    cw = conv_w
    cb = conv_b.reshape(1, D_FF)
    wd = w_down.astype(BF16)
    gfin = g_final.reshape(1, D)
    full = lambda a: pl.BlockSpec(a.shape, lambda b, i: (0,) * a.ndim, pipeline_mode=pl.Buffered(1))
    tile = pl.BlockSpec((None, tm, D), lambda b, i: (b, i, 0))
    halo = pl.BlockSpec((None, HALO, D), lambda b, i: (b, jnp.maximum(i * (tm // HALO) - 1, 0), 0))
    return pl.pallas_call(
        _ffn_kernel,
        grid=(B, S // tm),
        in_specs=[tile, halo, tile, full(wup), full(cw), full(cb), full(wd), full(gfin)],
        out_specs=tile,
        out_shape=jax.ShapeDtypeStruct((B, S, D), F32),
        scratch_shapes=[pltpu.VMEM((tm, D_FF), BF16)],
        compiler_params=_params("parallel", "parallel"),
        name="conv_ffn",
    )(h2, h2, x1, wup, cw, cb, wd, gfin)


@jax.jit
def _layer(x, g_mix, w_in, pe_cmp_k, w_cmp_k1, w_cmp_k2, pe_cmp_v, w_cmp_v1, w_cmp_v2,
           w_proj_nsa, w_proj_dil, w_out, g_ffn, w_up, conv_w, conv_b, w_down, g_final):
    B, S, D = x.shape
    depth = g_mix.shape[0]
    for l in range(depth):
        gm = g_mix[l].reshape(1, D)
        kcmp, vcmp, ks, kw, d0, d1, d2, qt, vst, vwt, gates = _in_proj(x, gm, w_in[l])
        kc, _ = _compress(kcmp, pe_cmp_k[l], w_cmp_k1[l], w_cmp_k2[l])
        _, vct = _compress(vcmp, pe_cmp_v[l], w_cmp_v1[l], w_cmp_v2[l])
        o_a = _nsa(qt, kc, vct, ks, vst, kw, vwt, gates)
        outs, lses = _dilated((d0, d1, d2))
        merge_cols = w_in[l][:, w_in.shape[2] - 2 * D:]
        x1, h2 = _merge(x, o_a, outs, lses, gm, merge_cols, w_proj_nsa[l], w_proj_dil[l], w_out[l],
                        g_ffn[l].reshape(1, D))
        x = _ffn(h2, x1, w_up[l], conv_w[l], conv_b[l], w_down[l], g_final)
        assert depth == 1
    return x


def kernel(x, g_mix, w_in, pe_cmp_k, w_cmp_k1, w_cmp_k2, pe_cmp_v, w_cmp_v1, w_cmp_v2, w_proj_nsa, w_proj_dil, w_out, g_ffn, w_up, conv_w, conv_b, w_down, g_final):
    return _layer(x, g_mix, w_in, pe_cmp_k, w_cmp_k1, w_cmp_k2, pe_cmp_v, w_cmp_v1, w_cmp_v2,
                  w_proj_nsa, w_proj_dil, w_out, g_ffn, w_up, conv_w, conv_b, w_down, g_final)
```

```python
import functools
import math

import numpy as np
import jax
import jax.numpy as jnp
from jax import lax
from jax.experimental import pallas as pl
from jax.experimental.pallas import tpu as pltpu

HEAD_DIM = 64
NSA_HEADS = 8
NSA_GROUPS = 2
NSA_REP = NSA_HEADS // NSA_GROUPS
CMP_BLOCK = 32
CMP_STRIDE = 16
CMP_HIDDEN = 128
SLC_BLOCK = 64
SLC_TOP = 16
NSA_WINDOW = 512
FORCE_SCORE = 1.0e4
DIL_CONFIGS = ((128, 1), (512, 4), (2048, 16))
DIL_GROUPS = 3
DIL_HEADS_PER_GROUP = 4
DIL_HEADS = DIL_GROUPS * DIL_HEADS_PER_GROUP
D_FF = 2816
CONV_WIDTH = 3
RMS_EPS = 1e-6
NEG_INF = -1e30

LANES = 128
VMEM_LIMIT_BYTES = 56 * 1024 * 1024

F32 = jnp.float32
BF16 = jnp.bfloat16
NT_DIMS = (((1,), (1,)), ((), ()))


def _alibi_slopes(n):
    return [float(2.0 ** (-8.0 * i / n)) for i in range(1, n + 1)]


def _rms(xf, g):
    ms = jnp.mean(xf * xf, axis=-1, keepdims=True)
    return xf * lax.rsqrt(ms + RMS_EPS) * g


def _dot(a, b):
    return jnp.dot(a, b, preferred_element_type=F32)


def _dot_nt(a, b):
    return lax.dot_general(a, b, NT_DIMS, preferred_element_type=F32)


def _sigmoid(z):
    return 1.0 / (1.0 + jnp.exp(-z))


def _params(*sem):
    return pltpu.CompilerParams(dimension_semantics=sem, vmem_limit_bytes=VMEM_LIMIT_BYTES)


IN_TM = 512
N_KVC = 4 * HEAD_DIM
N_KSEL = NSA_GROUPS * HEAD_DIM
N_DIL = 3 * DIL_HEADS * HEAD_DIM
DIL_WIDTH = DIL_HEADS_PER_GROUP * HEAD_DIM
T_Q = NSA_HEADS * HEAD_DIM
T_V = NSA_GROUPS * HEAD_DIM
GATE_ROWS = 16
K_AUG = 2 * HEAD_DIM
SLC_SHIFT = int(math.log2(SLC_BLOCK))
AUG_NBLK = 8
AUG_HI, AUG_LO, AUG_ONE = AUG_NBLK, AUG_NBLK + 1, AUG_NBLK + 2
V_PAD = 16
V_ROWS = HEAD_DIM + V_PAD


def _key_position_columns(pos0, rows, step=1):
    pos = pos0 + step * lax.broadcasted_iota(jnp.int32, (rows, HEAD_DIM), 0)
    col = lax.broadcasted_iota(jnp.int32, (rows, HEAD_DIM), 1)
    blk = jnp.bitwise_and(lax.shift_right_logical(pos, SLC_SHIFT), AUG_NBLK - 1)
    hi = lax.shift_left(lax.shift_right_logical(pos, 7), 7).astype(F32)
    lo = jnp.bitwise_and(pos, 127).astype(F32)
    c = jnp.where(col == AUG_HI, hi, jnp.where(col == AUG_LO, lo, jnp.where(col == AUG_ONE, 1.0, 0.0)))
    return jnp.where((col < AUG_NBLK) & (blk == col), 1.0, c)


def _in_proj_kernel(x_ref, g_ref, wn_ref, wt_ref,
                    kcmp_ref, vcmp_ref, ks_ref, kw_ref, d0_ref, d1_ref, d2_ref, qt_ref, vst_ref, vwt_ref, gate_ref,
                    slab_ref):
    tm = x_ref.shape[0]
    h = _rms(x_ref[...], g_ref[...]).astype(BF16)
    c0 = 0
    kvc = _dot(h, wn_ref[:, c0:c0 + N_KVC])
    kcmp_ref[...] = kvc[:, :N_KSEL]
    vcmp_ref[...] = kvc[:, N_KSEL:]
    c0 += N_KVC
    ks = _dot(h, wn_ref[:, c0:c0 + N_KSEL]).astype(BF16)
    c0 += N_KSEL
    kw = _dot(h, wn_ref[:, c0:c0 + N_KSEL]).astype(BF16)
    c0 += N_KSEL
    aug = _key_position_columns(pl.program_id(1) * tm, tm).astype(BF16)
    for g in range(NSA_GROUPS):
        ks_ref[g] = jnp.concatenate([ks[:, g * HEAD_DIM:(g + 1) * HEAD_DIM], aug], axis=1)
        kw_ref[g] = jnp.concatenate([kw[:, g * HEAD_DIM:(g + 1) * HEAD_DIM], aug], axis=1)
    seg = DIL_HEADS * HEAD_DIM
    for which in range(3):
        y = _dot(h, wn_ref[:, c0 + which * seg:c0 + (which + 1) * seg])
        for gi, (d_ref, (_, r)) in enumerate(zip((d0_ref, d1_ref, d2_ref), DIL_CONFIGS)):
            yg = y[:, gi * DIL_WIDTH:(gi + 1) * DIL_WIDTH]
            if r == 1:
                d_ref[which, 0] = yg.astype(BF16)
                continue
            for s in range(DIL_WIDTH // LANES):
                slab_ref[s] = yg[:, s * LANES:(s + 1) * LANES]
            for c in range(r):
                d_ref[which, c] = jnp.concatenate(
                    [slab_ref[s, pl.ds(c, tm // r, stride=r), :] for s in range(DIL_WIDTH // LANES)],
                    axis=1).astype(BF16)
    yt = _dot_nt(wt_ref[...], h)
    qt_ref[...] = yt[0:T_Q].astype(BF16)
    r0 = T_Q
    ones = jnp.where(lax.broadcasted_iota(jnp.int32, (V_PAD, tm), 0) == 0, 1.0, 0.0).astype(BF16)
    for ref in (vst_ref, vwt_ref):
        vt = yt[r0:r0 + T_V].astype(BF16)
        r0 += T_V
        for g in range(NSA_GROUPS):
            ref[g] = jnp.concatenate([vt[g * HEAD_DIM:(g + 1) * HEAD_DIM, :], ones], axis=0)
    gate_ref[...] = _sigmoid(yt[r0:r0 + NSA_GROUPS * GATE_ROWS])


def _transpose_kernel(w_ref, o_ref):
    o_ref[...] = w_ref[...].T.astype(o_ref.dtype)


def _transpose_to_bf16(w):
    D, N = w.shape
    return pl.pallas_call(
        _transpose_kernel,
        grid=(N // LANES,),
        in_specs=[pl.BlockSpec((D, LANES), lambda j: (0, j))],
        out_specs=pl.BlockSpec((LANES, D), lambda j: (j, 0)),
        out_shape=jax.ShapeDtypeStruct((N, D), BF16),
        compiler_params=_params("parallel"),
        name="transpose_weights",
    )(w)


def _in_proj(x, g_mix, w_in):
    B, S, D = x.shape
    scale = HEAD_DIM ** -0.5
    o_q, o_kv = 0, T_Q
    o_gate = o_kv + 6 * N_KSEL
    o_dil = o_gate + 3 * NSA_HEADS
    o_merge = o_dil + N_DIL
    kv = w_in[:, o_kv:o_gate]

    def kind(k):
        return kv[:, k * N_KSEL:(k + 1) * N_KSEL]

    dil = w_in[:, o_dil:o_merge]
    dil = jnp.concatenate([dil[:, :DIL_HEADS * HEAD_DIM] * scale, dil[:, DIL_HEADS * HEAD_DIM:]], axis=1)
    wn = jnp.concatenate([kind(0), kind(1), kind(2), kind(4), dil], axis=1).astype(BF16)
    wg = w_in[:, o_gate:o_dil].reshape(D, NSA_GROUPS, 3 * NSA_REP)
    wg = jnp.pad(wg, ((0, 0), (0, 0), (0, GATE_ROWS - 3 * NSA_REP))).reshape(D, NSA_GROUPS * GATE_ROWS)
    wt_src = jnp.concatenate([w_in[:, o_q:o_kv] * scale, kind(3), kind(5), wg], axis=1)
    wt_src = jnp.pad(wt_src, ((0, 0), (0, -wt_src.shape[1] % LANES)))
    wt = _transpose_to_bf16(wt_src)
    tm = IN_TM
    grid = (B, S // tm)
    full = lambda a: pl.BlockSpec(a.shape, lambda b, i: (0,) * a.ndim)
    k_shape = jax.ShapeDtypeStruct((B, NSA_GROUPS, S, K_AUG), BF16)
    v_shape = jax.ShapeDtypeStruct((B, NSA_GROUPS, V_ROWS, S), BF16)
    k_spec = pl.BlockSpec((None, NSA_GROUPS, tm, K_AUG), lambda b, i: (b, 0, i, 0))
    v_spec = pl.BlockSpec((None, NSA_GROUPS, V_ROWS, tm), lambda b, i: (b, 0, 0, i))
    c_shape = jax.ShapeDtypeStruct((B, S, N_KSEL), F32)
    c_spec = pl.BlockSpec((None, tm, N_KSEL), lambda b, i: (b, i, 0))
    d_shapes = tuple(jax.ShapeDtypeStruct((B, 3, r, S // r, DIL_WIDTH), BF16) for _, r in DIL_CONFIGS)
    d_specs = tuple(pl.BlockSpec((None, 3, r, tm // r, DIL_WIDTH), lambda b, i: (b, 0, 0, i, 0))
                    for _, r in DIL_CONFIGS)
    out_shape = (
        c_shape, c_shape, k_shape, k_shape, *d_shapes,
        jax.ShapeDtypeStruct((B, T_Q, S), BF16),
        v_shape,
        v_shape,
        jax.ShapeDtypeStruct((B, NSA_GROUPS * GATE_ROWS, S), F32),
    )
    out_specs = (
        c_spec, c_spec, k_spec, k_spec, *d_specs,
        pl.BlockSpec((None, T_Q, tm), lambda b, i: (b, 0, i)),
        v_spec,
        v_spec,
        pl.BlockSpec((None, NSA_GROUPS * GATE_ROWS, tm), lambda b, i: (b, 0, i)),
    )
    return pl.pallas_call(
        _in_proj_kernel,
        grid=grid,
        in_specs=[pl.BlockSpec((None, tm, D), lambda b, i: (b, i, 0)), full(g_mix), full(wn), full(wt)],
        out_specs=out_specs,
        out_shape=out_shape,
        scratch_shapes=[pltpu.VMEM((DIL_WIDTH // LANES, tm, LANES), F32)],
        compiler_params=_params("parallel", "parallel"),
        name="in_proj",
    )(x, g_mix, wn, wt)


def _compress_kernel(x_ref, pe_ref, w1_ref, w2_ref, w2t_ref, c_ref, ct_ref):
    nch = x_ref.shape[0] // CMP_STRIDE
    ulo = jnp.zeros((nch, NSA_GROUPS * CMP_HIDDEN), F32)
    uhi = jnp.zeros((nch, NSA_GROUPS * CMP_HIDDEN), F32)
    for j in range(CMP_STRIDE):
        xj = x_ref[pl.ds(j, nch, stride=CMP_STRIDE), :]
        ulo = ulo + _dot((xj + pe_ref[j:j + 1, :]).astype(BF16), w1_ref[j])
        uhi = uhi + _dot((xj + pe_ref[CMP_STRIDE + j:CMP_STRIDE + j + 1, :]).astype(BF16), w1_ref[CMP_STRIDE + j])
    pre = ulo + pltpu.roll(uhi, nch - 1, 0)
    hid = jax.nn.gelu(pre).astype(BF16)
    aug = _key_position_columns(CMP_BLOCK - 1, nch, CMP_STRIDE).astype(BF16)
    for g in range(NSA_GROUPS):
        hg = hid[:, g * CMP_HIDDEN:(g + 1) * CMP_HIDDEN]
        c_ref[g] = jnp.concatenate([_dot(hg, w2_ref[...]).astype(BF16), aug], axis=1)
        ct_ref[g] = _dot_nt(w2t_ref[...], hg).astype(BF16)


def _compress(xc, pe, w1, w2):
    B, S, width = xc.shape
    nch = S // CMP_STRIDE
    G, dh, hid = NSA_GROUPS, HEAD_DIM, CMP_HIDDEN
    pe_t = jnp.broadcast_to(pe.reshape(CMP_BLOCK, 1, dh), (CMP_BLOCK, G, dh)).reshape(CMP_BLOCK, width)
    eye = jnp.eye(G, dtype=w1.dtype)
    wexp = jnp.einsum('pdn,ge->pgden', w1.reshape(CMP_BLOCK, dh, hid), eye).reshape(CMP_BLOCK, width, G * hid)
    wexp = wexp.astype(BF16)
    full = lambda a: pl.BlockSpec(a.shape, lambda b: (0,) * a.ndim)
    w2b = w2.astype(BF16)
    w2t = w2.T.astype(BF16)
    return pl.pallas_call(
        _compress_kernel,
        grid=(B,),
        in_specs=[pl.BlockSpec((None, S, width), lambda b: (b, 0, 0)), full(pe_t), full(wexp), full(w2b), full(w2t)],
        out_specs=(pl.BlockSpec((None, G, nch, K_AUG), lambda b: (b, 0, 0, 0)),
                   pl.BlockSpec((None, G, dh, nch), lambda b: (b, 0, 0, 0))),
        out_shape=(jax.ShapeDtypeStruct((B, G, nch, K_AUG), BF16),
                   jax.ShapeDtypeStruct((B, G, dh, nch), BF16)),
        compiler_params=_params("parallel"),
        name="compress",
    )(xc, pe_t, wexp, w2b, w2t)


NSA_TQ = 256
SWEEP_TK = AUG_NBLK * SLC_BLOCK
SEL_SUB = 128
SEL_AHEAD = 4


def _rowmax8(s):
    return jnp.max(s.reshape(s.shape[0] // 8, 8, s.shape[1]), axis=0)


def _nsa_kernel(qt_ref, kc_ref, vct_ref, ks_ref, vst_ref, kw_ref, vwt_ref, gate_ref, ov_ref,
                o_ref, sbt_ref, osel_ref, *, slopes):
    g = pl.program_id(1)
    i = pl.program_id(2)
    R, dh, tq, tk = NSA_REP, HEAD_DIM, NSA_TQ, SWEEP_TK
    L = R * tq
    t0 = i * tq
    nc = kc_ref.shape[0]
    ns = ov_ref.shape[0]
    n_tiles = ns // AUG_NBLK
    tile4 = lambda a: jnp.concatenate([a] * R, axis=1)

    qt = qt_ref[...]
    qs = jnp.concatenate([qt[r * dh:(r + 1) * dh, :] for r in range(R)], axis=1)
    slope = [jnp.where(g == 0, slopes[r], slopes[R + r]).astype(F32) for r in range(R)]
    slope_row = jnp.concatenate([jnp.full((1, tq), 1.0, F32) * slope[r] for r in range(R)], axis=1)
    t_row = t0 + lax.broadcasted_iota(jnp.int32, (1, tq), 1)

    r8 = lax.broadcasted_iota(jnp.int32, (AUG_NBLK, L), 0)
    alibi8 = jnp.where(r8 < 2, slope_row, jnp.where(r8 == 2, -slope_row * t0.astype(F32), 0.0))
    q_pad = jnp.zeros((K_AUG - dh - 2 * AUG_NBLK, L), BF16)
    q_plain = jnp.concatenate([qs, jnp.concatenate([jnp.zeros((AUG_NBLK, L), F32), alibi8], axis=0).astype(BF16),
                               q_pad], axis=0)

    cmp_end = lax.broadcasted_iota(jnp.int32, (nc, tq), 0) * CMP_STRIDE + (CMP_BLOCK - 1)
    m_cmp = t_row >= cmp_end
    sc = _dot(kc_ref[...], q_plain) + tile4(jnp.where(m_cmp, 0.0, NEG_INF))

    a0 = pl.multiple_of(jnp.maximum(t0 - NSA_WINDOW, 0), tq)
    d0 = pl.multiple_of(t0, tq)
    s_a = _dot(kw_ref[pl.ds(a0, NSA_WINDOW), :], q_plain)
    s_d = _dot(kw_ref[pl.ds(d0, tq), :], q_plain)
    kpos_a = a0 + lax.broadcasted_iota(jnp.int32, (NSA_WINDOW, tq), 0)
    t_a = t0 + lax.broadcasted_iota(jnp.int32, (NSA_WINDOW, tq), 1)
    bias_a = jnp.where((kpos_a < t0) & (t_a - kpos_a < NSA_WINDOW), 0.0, NEG_INF)
    causal_d = jnp.where(lax.broadcasted_iota(jnp.int32, (tq, tq), 0)
                         <= lax.broadcasted_iota(jnp.int32, (tq, tq), 1), 0.0, NEG_INF)
    s_a = s_a + tile4(bias_a)
    s_d = s_d + tile4(causal_d)
    m_w = jnp.max(jnp.maximum(_rowmax8(s_a), _rowmax8(s_d)), axis=0, keepdims=True)
    acc_w = (_dot(vwt_ref[:, pl.ds(a0, NSA_WINDOW)], jnp.exp(s_a - m_w).astype(BF16))
             + _dot(vwt_ref[:, pl.ds(d0, tq)], jnp.exp(s_d - m_w).astype(BF16)))
    o_win = acc_w[:dh] * (1.0 / acc_w[dh:dh + 1])

    mx = jnp.max(_rowmax8(sc), axis=0, keepdims=True)
    p = jnp.exp(sc - mx) * tile4(jnp.where(m_cmp, 1.0, 0.0))
    den = jnp.sum(jnp.sum(p.reshape(nc // 8, 8, L), axis=0), axis=0, keepdims=True)
    pr = p * (1.0 / jnp.maximum(den, 1e-30))
    o_cmp = _dot(vct_ref[...], pr.astype(BF16))
    psum = pr[:, 0:tq]
    for r in range(1, R):
        psum = psum + pr[:, r * tq:(r + 1) * tq]

    p_hi = psum.astype(BF16)
    p_lo = (psum - p_hi.astype(F32)).astype(BF16)
    imp = _dot(ov_ref[...], p_hi) + _dot(ov_ref[...], p_lo)
    blk = lax.broadcasted_iota(jnp.int32, (ns, tq), 0)
    cur = lax.shift_right_logical(t_row, SLC_SHIFT)
    val = jnp.where((blk == cur) | (blk == 0), FORCE_SCORE, imp)
    val = jnp.where(blk <= cur, val, -1.0)
    vals = [val[8 * v:8 * v + 8, :] for v in range(ns // 8)]
    ranks = [jnp.zeros((8, tq), F32) for _ in vals]
    row8 = lax.broadcasted_iota(jnp.int32, (8, tq), 0)
    for j in range(ns):
        vj = jnp.broadcast_to(val[j:j + 1, :], (8, tq))
        for v in range(len(vals)):
            if 8 * v > j:
                ahead = vj >= vals[v]
            elif 8 * v + 7 <= j:
                ahead = vj > vals[v]
            else:
                ahead = (vj > vals[v]) | ((vj == vals[v]) & (row8 > j - 8 * v))
            ranks[v] = ranks[v] + jnp.where(ahead, 1.0, 0.0)
    for T in range(n_tiles):
        selb = jnp.where((ranks[T] < float(SLC_TOP)) & (vals[T] >= 0.0), 0.0, NEG_INF)
        sbt_ref[T] = jnp.concatenate([tile4(selb), alibi8], axis=0).astype(BF16)

    td = lax.div(t0, tk)
    for c in range(n_tiles):
        @pl.when(td == c)
        def _(c=c):
            kpos = c * tk + lax.broadcasted_iota(jnp.int32, (tk, tq), 0)
            t_q = t0 + lax.broadcasted_iota(jnp.int32, (tk, tq), 1)
            causal = tile4(jnp.where(kpos <= t_q, 0.0, NEG_INF))
            q_augs = [jnp.concatenate([qs, sbt_ref[T], q_pad], axis=0) for T in range(c + 1)]
            chunks = [(T, j * SEL_SUB) for T in range(c + 1) for j in range(tk // SEL_SUB)]

            def chunk_scores(T, r):
                s = _dot(ks_ref[T * tk + r:T * tk + r + SEL_SUB, :], q_augs[T])
                return s + causal[r:r + SEL_SUB] if T == c else s

            pending = [chunk_scores(*ch) for ch in chunks[:SEL_AHEAD]]
            m = acc = None
            for n, (T, r) in enumerate(chunks):
                if n + SEL_AHEAD < len(chunks):
                    pending.append(chunk_scores(*chunks[n + SEL_AHEAD]))
                s = pending.pop(0)
                v_t = vst_ref[:, T * tk + r:T * tk + r + SEL_SUB]
                m_c = jnp.max(_rowmax8(s), axis=0, keepdims=True)
                if m is None:
                    m = m_c
                    acc = _dot(v_t, jnp.exp(s - m).astype(BF16))
                else:
                    m_new = jnp.maximum(m, m_c)
                    acc = jnp.exp(m - m_new) * acc + _dot(v_t, jnp.exp(s - m_new).astype(BF16))
                    m = m_new
            osel_ref[...] = acc[:dh] * (1.0 / acc[dh:dh + 1])

    o_sel = osel_ref[...]
    gates = gate_ref[...]
    tiles = []
    for r in range(R):
        sl = slice(r * tq, (r + 1) * tq)
        tiles.append(gates[3 * r:3 * r + 1, :] * o_cmp[:, sl]
                     + gates[3 * r + 1:3 * r + 2, :] * o_sel[:, sl]
                     + gates[3 * r + 2:3 * r + 3, :] * o_win[:, sl])
    o_ref[...] = jnp.concatenate(tiles, axis=0).T.astype(o_ref.dtype)


def _overlap_matrix(nc, ns):
    cs = np.arange(nc)[None, :] * CMP_STRIDE
    ss = np.arange(ns)[:, None] * SLC_BLOCK
    ov = np.clip(np.minimum(cs + CMP_BLOCK, ss + SLC_BLOCK) - np.maximum(cs, ss), 0, None)
    return jnp.asarray(ov.astype(np.float32) / CMP_BLOCK, dtype=BF16)


def _nsa(qt, kc, vct, ks, vst, kw, vwt, gates):
    B, _, S = qt.shape
    G, R, dh, tq = NSA_GROUPS, NSA_REP, HEAD_DIM, NSA_TQ
    nc = kc.shape[2]
    ns = S // SLC_BLOCK
    ov = _overlap_matrix(nc, ns)
    kern = functools.partial(_nsa_kernel, slopes=tuple(_alibi_slopes(NSA_HEADS)))
    assert S % SWEEP_TK == 0 and S >= NSA_WINDOW + tq
    per_bg = lambda shape: pl.BlockSpec((None, None) + shape, lambda b, g, i: (b, g, 0, 0))
    return pl.pallas_call(
        kern,
        grid=(B, G, S // tq),
        in_specs=[
            pl.BlockSpec((None, R * dh, tq), lambda b, g, i: (b, g, i)),
            per_bg((nc, K_AUG)), per_bg((dh, nc)),
            per_bg((S, K_AUG)), per_bg((V_ROWS, S)),
            per_bg((S, K_AUG)), per_bg((V_ROWS, S)),
            pl.BlockSpec((None, GATE_ROWS, tq), lambda b, g, i: (b, g, i)),
            pl.BlockSpec(ov.shape, lambda b, g, i: (0, 0)),
        ],
        out_specs=pl.BlockSpec((None, tq, R * dh), lambda b, g, i: (b, i, g)),
        out_shape=jax.ShapeDtypeStruct((B, S, G * R * dh), BF16),
        scratch_shapes=[pltpu.VMEM((S // SWEEP_TK, 2 * AUG_NBLK, R * tq), BF16),
                        pltpu.VMEM((dh, R * tq), F32)],
        compiler_params=_params("parallel", "parallel", "arbitrary"),
        name="nsa_attention",
    )(qt, kc, vct, ks, vst, kw, vwt, gates, ov)


BAND_TQ = 128
BAND_SUB = 4
BAND = 128


def _banded_kernel(q_ref, k_ref, v_ref, o_ref, lse_ref, *, slopes, nsub):
    tq, dh, nh = BAND_TQ, HEAD_DIM, DIL_HEADS_PER_GROUP
    tk = tq + BAND
    width = nh * dh
    head_of_lane = lambda rows: lax.shift_right_logical(
        lax.broadcasted_iota(jnp.int32, (rows, width), 1), int(math.log2(dh)))
    lane_head, q_head = head_of_lane(tk), head_of_lane(tq)
    keep = [jnp.where(lane_head == h, 1.0, 0.0).astype(BF16) for h in range(nh)]

    def per_head(a):
        return jnp.concatenate([a * keep[h] for h in range(nh)], axis=0)

    ones_h = jnp.concatenate(keep, axis=0)
    def biases(first_key_offset):
        d = first_key_offset + (lax.broadcasted_iota(jnp.int32, (tq, tk), 0)
                                - lax.broadcasted_iota(jnp.int32, (tq, tk), 1))
        mask_bias = jnp.where((d >= 0) & (d <= BAND), 0.0, NEG_INF)
        neg_d = -d.astype(F32)
        return [slopes[h] * neg_d + mask_bias for h in range(nh)]

    subs = []
    for sub in range(nsub):
        i = pl.program_id(2) * nsub + sub
        k0 = pl.multiple_of(jnp.maximum(i - 1, 0) * tq, tq)
        q = q_ref[sub * tq:(sub + 1) * tq, :]
        subs.append((i * tq - k0, k0, _dot_nt(q, per_head(k_ref[pl.ds(k0, tk), :]))))
    inner_bias = biases(tq) if nsub > 1 else None
    probs = []
    for sub, (off, k0, s) in enumerate(subs):
        bias = biases(off) if sub == 0 else inner_bias
        ps, mxs = [], []
        for h in range(nh):
            sh = s[:, h * tk:(h + 1) * tk] + bias[h]
            mx = jnp.max(sh, axis=-1, keepdims=True)
            ps.append(jnp.exp(sh - mx).astype(BF16))
            mxs.append(mx)
        probs.append((k0, jnp.concatenate(ps, axis=1), mxs))
    for sub, (k0, p, mxs) in enumerate(probs):
        den = _dot(p, ones_h)
        mx_all = mxs[nh - 1]
        for h in range(nh - 2, -1, -1):
            mx_all = jnp.where(q_head == h, mxs[h], mx_all)
        o_ref[sub * tq:(sub + 1) * tq, :] = _dot(p, per_head(v_ref[pl.ds(k0, tk), :])) * (1.0 / den)
        lse_ref[sub * tq:(sub + 1) * tq, :] = mx_all + jnp.log(den)


def _banded(d, slopes):
    B, _, r, n, width = d.shape
    nsub = min(BAND_SUB, n // BAND_TQ)
    tq = BAND_TQ * nsub
    assert n >= BAND_TQ + BAND and n % tq == 0
    kern = functools.partial(_banded_kernel, slopes=tuple(slopes), nsub=nsub)
    tile = pl.BlockSpec((None, None, None, tq, width), lambda b, c, i: (b, 0, c, i, 0))
    kseq = pl.BlockSpec((None, None, None, n, width), lambda b, c, i: (b, 1, c, 0, 0))
    vseq = pl.BlockSpec((None, None, None, n, width), lambda b, c, i: (b, 2, c, 0, 0))
    out = pl.BlockSpec((None, None, tq, width), lambda b, c, i: (b, c, i, 0))
    return pl.pallas_call(
        kern,
        grid=(B, r, n // tq),
        in_specs=[tile, kseq, vseq],
        out_specs=(out, out),
        out_shape=(jax.ShapeDtypeStruct((B, r, n, width), F32),) * 2,
        compiler_params=_params("parallel", "parallel", "arbitrary"),
        name="banded_attention",
    )(d, d, d)


def _dilated(dils):
    slopes = _alibi_slopes(DIL_HEADS)
    outs, lses = [], []
    for gi, (w, r) in enumerate(DIL_CONFIGS):
        assert w // r == BAND
        sl = [s_ * r for s_ in slopes[gi * DIL_HEADS_PER_GROUP:(gi + 1) * DIL_HEADS_PER_GROUP]]
        o, lse = _banded(dils[gi], sl)
        outs.append(o)
        lses.append(lse)
    return outs, lses


MERGE_TM = 512


def _token_order(ref, slab_ref):
    r, rows, width = ref.shape
    if r == 1:
        return ref[0]
    for c in range(r):
        blk = ref[c]
        for s in range(width // LANES):
            slab_ref[s, pl.ds(c, rows, stride=r), :] = blk[:, s * LANES:(s + 1) * LANES]
    return jnp.concatenate([slab_ref[s] for s in range(width // LANES)], axis=1)


def _merge_kernel(x_ref, oa_ref, o0_ref, o1_ref, o2_ref, l0_ref, l1_ref, l2_ref,
                  gmix_ref, wm_ref, wpn_ref, wpd_ref, wo_ref, gffn_ref, x1_ref, h2_ref, slab_ref):
    x = x_ref[...]
    D = x.shape[1]
    h = _rms(x, gmix_ref[...]).astype(BF16)
    gm = _sigmoid(_dot(h, wm_ref[...]))
    o0, o1, o2 = [_token_order(r_, slab_ref) for r_ in (o0_ref, o1_ref, o2_ref)]
    l0, l1, l2 = [_token_order(r_, slab_ref) for r_ in (l0_ref, l1_ref, l2_ref)]
    mx = jnp.maximum(jnp.maximum(l0, l1), l2)
    e0, e1, e2 = jnp.exp(l0 - mx), jnp.exp(l1 - mx), jnp.exp(l2 - mx)
    inv = 1.0 / (e0 + e1 + e2)
    ob = o0 * (e0 * inv) + o1 * (e1 * inv) + o2 * (e2 * inv)
    a = _dot(oa_ref[...], wpn_ref[...])
    d = _dot(ob.astype(BF16), wpd_ref[...])
    mixed = gm[:, :D] * a + gm[:, D:] * d
    x1 = x + _dot(mixed.astype(BF16), wo_ref[...])
    x1_ref[...] = x1
    h2_ref[...] = _rms(x1, gffn_ref[...]).astype(BF16)


def _merge(x, o_a, outs, lses, g_mix, w_merge, w_proj_nsa, w_proj_dil, w_out, g_ffn):
    B, S, D = x.shape
    tm = MERGE_TM
    row = lambda a: pl.BlockSpec((None, tm, a.shape[2]), lambda b, i: (b, i, 0))
    cls = lambda a: pl.BlockSpec((None, a.shape[1], tm // a.shape[1], a.shape[3]), lambda b, i: (b, 0, i, 0))
    full = lambda a: pl.BlockSpec(a.shape, lambda b, i: (0,) * a.ndim)
    ws = [w_merge.astype(BF16), w_proj_nsa.astype(BF16), w_proj_dil.astype(BF16), w_out.astype(BF16)]
    consts = [g_mix, *ws, g_ffn]
    in_specs = [row(x), row(o_a)] + [cls(a) for a in (*outs, *lses)] + [full(a) for a in consts]
    return pl.pallas_call(
        _merge_kernel,
        grid=(B, S // tm),
        in_specs=in_specs,
        out_specs=(pl.BlockSpec((None, tm, D), lambda b, i: (b, i, 0)),) * 2,
        out_shape=(jax.ShapeDtypeStruct((B, S, D), F32), jax.ShapeDtypeStruct((B, S, D), BF16)),
        scratch_shapes=[pltpu.VMEM((DIL_WIDTH // LANES, tm, LANES), F32)],
        compiler_params=_params("parallel", "parallel"),
        name="merge_proj",
    )(x, o_a, *outs, *lses, *consts)


FFN_TM = 512
FFN_TN = 256
HALO = 16


def _ffn_kernel(h_ref, halo_ref, x1_ref, wup_ref, cw_ref, cb_ref, wd_ref, gfin_ref, o_ref, act_ref):
    i = pl.program_id(1)
    h = h_ref[...]
    halo = halo_ref[...]
    tm = h.shape[0]
    row = lax.broadcasted_iota(jnp.int32, (tm, FFN_TN), 0)
    live = (i > 0).astype(F32)
    for j in range(D_FF // FFN_TN):
        cols = slice(j * FFN_TN, (j + 1) * FFN_TN)
        wu = wup_ref[:, cols]
        u = _dot(h, wu)
        uh = _dot(halo, wu) * live
        gate = _dot(h, wup_ref[:, D_FF + j * FFN_TN:D_FF + (j + 1) * FFN_TN])
        p1 = jnp.broadcast_to(uh[HALO - 1:HALO, :], (tm, FFN_TN))
        p2 = jnp.broadcast_to(uh[HALO - 2:HALO - 1, :], (tm, FFN_TN))
        u1 = jnp.where(row == 0, p1, pltpu.roll(u, 1, 0))
        u2 = jnp.where(row == 0, p2, jnp.where(row == 1, p1, pltpu.roll(u, 2, 0)))
        uc = cb_ref[:, cols] + cw_ref[0:1, cols] * u2
        uc = uc + cw_ref[1:2, cols] * u1
        uc = uc + cw_ref[2:3, cols] * u
        act_ref[:, j * FFN_TN:(j + 1) * FFN_TN] = (jax.nn.gelu(uc) * gate).astype(BF16)
    y = _dot(act_ref[...], wd_ref[...])
    o_ref[...] = _rms(x1_ref[...] + y, gfin_ref[...])


def _ffn(h2, x1, w_up, conv_w, conv_b, w_down, g_final):
    B, S, D = h2.shape
    tm = FFN_TM
    assert D_FF % FFN_TN == 0
    wup = w_up.astype(BF16)
    cw = conv_w
    cb = conv_b.reshape(1, D_FF)
    wd = w_down.astype(BF16)
    gfin = g_final.reshape(1, D)
    full = lambda a: pl.BlockSpec(a.shape, lambda b, i: (0,) * a.ndim, pipeline_mode=pl.Buffered(1))
    tile = pl.BlockSpec((None, tm, D), lambda b, i: (b, i, 0))
    halo = pl.BlockSpec((None, HALO, D), lambda b, i: (b, jnp.maximum(i * (tm // HALO) - 1, 0), 0))
    return pl.pallas_call(
        _ffn_kernel,
        grid=(B, S // tm),
        in_specs=[tile, halo, tile, full(wup), full(cw), full(cb), full(wd), full(gfin)],
        out_specs=tile,
        out_shape=jax.ShapeDtypeStruct((B, S, D), F32),
        scratch_shapes=[pltpu.VMEM((tm, D_FF), BF16)],
        compiler_params=_params("parallel", "parallel"),
        name="conv_ffn",
    )(h2, h2, x1, wup, cw, cb, wd, gfin)


@jax.jit
def _layer(x, g_mix, w_in, pe_cmp_k, w_cmp_k1, w_cmp_k2, pe_cmp_v, w_cmp_v1, w_cmp_v2,
           w_proj_nsa, w_proj_dil, w_out, g_ffn, w_up, conv_w, conv_b, w_down, g_final):
    B, S, D = x.shape
    depth = g_mix.shape[0]
    for l in range(depth):
        gm = g_mix[l].reshape(1, D)
        kcmp, vcmp, ks, kw, d0, d1, d2, qt, vst, vwt, gates = _in_proj(x, gm, w_in[l])
        kc, _ = _compress(kcmp, pe_cmp_k[l], w_cmp_k1[l], w_cmp_k2[l])
        _, vct = _compress(vcmp, pe_cmp_v[l], w_cmp_v1[l], w_cmp_v2[l])
        o_a = _nsa(qt, kc, vct, ks, vst, kw, vwt, gates)
        outs, lses = _dilated((d0, d1, d2))
        merge_cols = w_in[l][:, w_in.shape[2] - 2 * D:]
        x1, h2 = _merge(x, o_a, outs, lses, gm, merge_cols, w_proj_nsa[l], w_proj_dil[l], w_out[l],
                        g_ffn[l].reshape(1, D))
        x = _ffn(h2, x1, w_up[l], conv_w[l], conv_b[l], w_down[l], g_final)
        assert depth == 1
    return x


def kernel(x, g_mix, w_in, pe_cmp_k, w_cmp_k1, w_cmp_k2, pe_cmp_v, w_cmp_v1, w_cmp_v2, w_proj_nsa, w_proj_dil, w_out, g_ffn, w_up, conv_w, conv_b, w_down, g_final):
    return _layer(x, g_mix, w_in, pe_cmp_k, w_cmp_k1, w_cmp_k2, pe_cmp_v, w_cmp_v1, w_cmp_v2,
                  w_proj_nsa, w_proj_dil, w_out, g_ffn, w_up, conv_w, conv_b, w_down, g_final)
```

```python
import functools
import math

import numpy as np
import jax
import jax.numpy as jnp
from jax import lax
from jax.experimental import pallas as pl
from jax.experimental.pallas import tpu as pltpu

HEAD_DIM = 64
NSA_HEADS = 8
NSA_GROUPS = 2
NSA_REP = NSA_HEADS // NSA_GROUPS
CMP_BLOCK = 32
CMP_STRIDE = 16
CMP_HIDDEN = 128
SLC_BLOCK = 64
SLC_TOP = 16
NSA_WINDOW = 512
FORCE_SCORE = 1.0e4
DIL_CONFIGS = ((128, 1), (512, 4), (2048, 16))
DIL_GROUPS = 3
DIL_HEADS_PER_GROUP = 4
DIL_HEADS = DIL_GROUPS * DIL_HEADS_PER_GROUP
D_FF = 2816
CONV_WIDTH = 3
RMS_EPS = 1e-6
NEG_INF = -1e30

LANES = 128
VMEM_LIMIT_BYTES = 56 * 1024 * 1024

F32 = jnp.float32
BF16 = jnp.bfloat16
NT_DIMS = (((1,), (1,)), ((), ()))


def _alibi_slopes(n):
    return [float(2.0 ** (-8.0 * i / n)) for i in range(1, n + 1)]


def _rms(xf, g):
    ms = jnp.mean(xf * xf, axis=-1, keepdims=True)
    return xf * lax.rsqrt(ms + RMS_EPS) * g


def _dot(a, b):
    return jnp.dot(a, b, preferred_element_type=F32)


def _dot_nt(a, b):
    return lax.dot_general(a, b, NT_DIMS, preferred_element_type=F32)


def _sigmoid(z):
    return 1.0 / (1.0 + jnp.exp(-z))


def _params(*sem):
    return pltpu.CompilerParams(dimension_semantics=sem, vmem_limit_bytes=VMEM_LIMIT_BYTES)


IN_TM = 512
N_KVC = 4 * HEAD_DIM
N_KSEL = NSA_GROUPS * HEAD_DIM
N_DIL = 3 * DIL_HEADS * HEAD_DIM
DIL_WIDTH = DIL_HEADS_PER_GROUP * HEAD_DIM
T_Q = NSA_HEADS * HEAD_DIM
T_V = NSA_GROUPS * HEAD_DIM
GATE_ROWS = 16
K_AUG = 2 * HEAD_DIM
SLC_SHIFT = int(math.log2(SLC_BLOCK))
AUG_NBLK = 8
AUG_HI, AUG_LO, AUG_ONE = AUG_NBLK, AUG_NBLK + 1, AUG_NBLK + 2
V_PAD = 16
V_ROWS = HEAD_DIM + V_PAD


def _key_position_columns(pos0, rows, step=1):
    pos = pos0 + step * lax.broadcasted_iota(jnp.int32, (rows, HEAD_DIM), 0)
    col = lax.broadcasted_iota(jnp.int32, (rows, HEAD_DIM), 1)
    blk = jnp.bitwise_and(lax.shift_right_logical(pos, SLC_SHIFT), AUG_NBLK - 1)
    hi = lax.shift_left(lax.shift_right_logical(pos, 7), 7).astype(F32)
    lo = jnp.bitwise_and(pos, 127).astype(F32)
    c = jnp.where(col == AUG_HI, hi, jnp.where(col == AUG_LO, lo, jnp.where(col == AUG_ONE, 1.0, 0.0)))
    return jnp.where((col < AUG_NBLK) & (blk == col), 1.0, c)


def _in_proj_kernel(x_ref, g_ref, wn_ref, wt_ref,
                    kcmp_ref, vcmp_ref, ks_ref, kw_ref, d0_ref, d1_ref, d2_ref, qt_ref, vst_ref, vwt_ref, gate_ref,
                    slab_ref):
    tm = x_ref.shape[0]
    h = _rms(x_ref[...], g_ref[...]).astype(BF16)
    c0 = 0
    kvc = _dot_nt(h, wn_ref[c0:c0 + N_KVC, :])
    kcmp_ref[...] = kvc[:, :N_KSEL]
    vcmp_ref[...] = kvc[:, N_KSEL:]
    c0 += N_KVC
    ks = _dot_nt(h, wn_ref[c0:c0 + N_KSEL, :]).astype(BF16)
    c0 += N_KSEL
    kw = _dot_nt(h, wn_ref[c0:c0 + N_KSEL, :]).astype(BF16)
    c0 += N_KSEL
    aug = _key_position_columns(pl.program_id(1) * tm, tm).astype(BF16)
    for g in range(NSA_GROUPS):
        ks_ref[g] = jnp.concatenate([ks[:, g * HEAD_DIM:(g + 1) * HEAD_DIM], aug], axis=1)
        kw_ref[g] = jnp.concatenate([kw[:, g * HEAD_DIM:(g + 1) * HEAD_DIM], aug], axis=1)
    seg = DIL_HEADS * HEAD_DIM
    for which in range(3):
        y = _dot_nt(h, wn_ref[c0 + which * seg:c0 + (which + 1) * seg, :])
        for gi, (d_ref, (_, r)) in enumerate(zip((d0_ref, d1_ref, d2_ref), DIL_CONFIGS)):
            yg = y[:, gi * DIL_WIDTH:(gi + 1) * DIL_WIDTH]
            if r == 1:
                d_ref[which, 0] = yg.astype(BF16)
                continue
            for s in range(DIL_WIDTH // LANES):
                slab_ref[s] = yg[:, s * LANES:(s + 1) * LANES]
            for c in range(r):
                d_ref[which, c] = jnp.concatenate(
                    [slab_ref[s, pl.ds(c, tm // r, stride=r), :] for s in range(DIL_WIDTH // LANES)],
                    axis=1).astype(BF16)
    yt = _dot_nt(wt_ref[...], h)
    qt_ref[...] = yt[0:T_Q].astype(BF16)
    r0 = T_Q
    ones = jnp.where(lax.broadcasted_iota(jnp.int32, (V_PAD, tm), 0) == 0, 1.0, 0.0).astype(BF16)
    for ref in (vst_ref, vwt_ref):
        vt = yt[r0:r0 + T_V].astype(BF16)
        r0 += T_V
        for g in range(NSA_GROUPS):
            ref[g] = jnp.concatenate([vt[g * HEAD_DIM:(g + 1) * HEAD_DIM, :], ones], axis=0)
    gate_ref[...] = _sigmoid(yt[r0:r0 + NSA_GROUPS * GATE_ROWS])


def _in_proj(x, g_mix, w_in):
    B, S, D = x.shape
    scale = HEAD_DIM ** -0.5
    o_q, o_kv = 0, T_Q
    o_gate = o_kv + 6 * N_KSEL
    o_dil = o_gate + 3 * NSA_HEADS
    o_merge = o_dil + N_DIL
    w_t = w_in.T
    kv = w_t[o_kv:o_gate]

    def kind(k):
        return kv[k * N_KSEL:(k + 1) * N_KSEL]

    dil = w_t[o_dil:o_merge]
    dil = jnp.concatenate([dil[:DIL_HEADS * HEAD_DIM] * scale, dil[DIL_HEADS * HEAD_DIM:]], axis=0)
    wn = jnp.concatenate([kind(0), kind(1), kind(2), kind(4), dil], axis=0).astype(BF16)
    wg = w_t[o_gate:o_dil].reshape(NSA_GROUPS, 3 * NSA_REP, D)
    wg = jnp.pad(wg, ((0, 0), (0, GATE_ROWS - 3 * NSA_REP), (0, 0))).reshape(NSA_GROUPS * GATE_ROWS, D)
    wt = jnp.concatenate([w_t[o_q:o_kv] * scale, kind(3), kind(5), wg], axis=0).astype(BF16)
    tm = IN_TM
    grid = (B, S // tm)
    full = lambda a: pl.BlockSpec(a.shape, lambda b, i: (0,) * a.ndim)
    k_shape = jax.ShapeDtypeStruct((B, NSA_GROUPS, S, K_AUG), BF16)
    v_shape = jax.ShapeDtypeStruct((B, NSA_GROUPS, V_ROWS, S), BF16)
    k_spec = pl.BlockSpec((None, NSA_GROUPS, tm, K_AUG), lambda b, i: (b, 0, i, 0))
    v_spec = pl.BlockSpec((None, NSA_GROUPS, V_ROWS, tm), lambda b, i: (b, 0, 0, i))
    c_shape = jax.ShapeDtypeStruct((B, S, N_KSEL), F32)
    c_spec = pl.BlockSpec((None, tm, N_KSEL), lambda b, i: (b, i, 0))
    d_shapes = tuple(jax.ShapeDtypeStruct((B, 3, r, S // r, DIL_WIDTH), BF16) for _, r in DIL_CONFIGS)
    d_specs = tuple(pl.BlockSpec((None, 3, r, tm // r, DIL_WIDTH), lambda b, i: (b, 0, 0, i, 0))
                    for _, r in DIL_CONFIGS)
    out_shape = (
        c_shape, c_shape, k_shape, k_shape, *d_shapes,
        jax.ShapeDtypeStruct((B, T_Q, S), BF16),
        v_shape,
        v_shape,
        jax.ShapeDtypeStruct((B, NSA_GROUPS * GATE_ROWS, S), F32),
    )
    out_specs = (
        c_spec, c_spec, k_spec, k_spec, *d_specs,
        pl.BlockSpec((None, T_Q, tm), lambda b, i: (b, 0, i)),
        v_spec,
        v_spec,
        pl.BlockSpec((None, NSA_GROUPS * GATE_ROWS, tm), lambda b, i: (b, 0, i)),
    )
    return pl.pallas_call(
        _in_proj_kernel,
        grid=grid,
        in_specs=[pl.BlockSpec((None, tm, D), lambda b, i: (b, i, 0)), full(g_mix), full(wn), full(wt)],
        out_specs=out_specs,
        out_shape=out_shape,
        scratch_shapes=[pltpu.VMEM((DIL_WIDTH // LANES, tm, LANES), F32)],
        compiler_params=_params("parallel", "parallel"),
        name="in_proj",
    )(x, g_mix, wn, wt)


def _compress_kernel(x_ref, pe_ref, w1_ref, w2_ref, w2t_ref, c_ref, ct_ref):
    nch = x_ref.shape[0] // CMP_STRIDE
    ulo = jnp.zeros((nch, NSA_GROUPS * CMP_HIDDEN), F32)
    uhi = jnp.zeros((nch, NSA_GROUPS * CMP_HIDDEN), F32)
    for j in range(CMP_STRIDE):
        xj = x_ref[pl.ds(j, nch, stride=CMP_STRIDE), :]
        ulo = ulo + _dot((xj + pe_ref[j:j + 1, :]).astype(BF16), w1_ref[j])
        uhi = uhi + _dot((xj + pe_ref[CMP_STRIDE + j:CMP_STRIDE + j + 1, :]).astype(BF16), w1_ref[CMP_STRIDE + j])
    pre = ulo + pltpu.roll(uhi, nch - 1, 0)
    hid = jax.nn.gelu(pre).astype(BF16)
    aug = _key_position_columns(CMP_BLOCK - 1, nch, CMP_STRIDE).astype(BF16)
    for g in range(NSA_GROUPS):
        hg = hid[:, g * CMP_HIDDEN:(g + 1) * CMP_HIDDEN]
        c_ref[g] = jnp.concatenate([_dot(hg, w2_ref[...]).astype(BF16), aug], axis=1)
        ct_ref[g] = _dot_nt(w2t_ref[...], hg).astype(BF16)


def _compress(xc, pe, w1, w2):
    B, S, width = xc.shape
    nch = S // CMP_STRIDE
    G, dh, hid = NSA_GROUPS, HEAD_DIM, CMP_HIDDEN
    pe_t = jnp.broadcast_to(pe.reshape(CMP_BLOCK, 1, dh), (CMP_BLOCK, G, dh)).reshape(CMP_BLOCK, width)
    eye = jnp.eye(G, dtype=w1.dtype)
    wexp = jnp.einsum('pdn,ge->pgden', w1.reshape(CMP_BLOCK, dh, hid), eye).reshape(CMP_BLOCK, width, G * hid)
    wexp = wexp.astype(BF16)
    full = lambda a: pl.BlockSpec(a.shape, lambda b: (0,) * a.ndim)
    w2b = w2.astype(BF16)
    w2t = w2.T.astype(BF16)
    return pl.pallas_call(
        _compress_kernel,
        grid=(B,),
        in_specs=[pl.BlockSpec((None, S, width), lambda b: (b, 0, 0)), full(pe_t), full(wexp), full(w2b), full(w2t)],
        out_specs=(pl.BlockSpec((None, G, nch, K_AUG), lambda b: (b, 0, 0, 0)),
                   pl.BlockSpec((None, G, dh, nch), lambda b: (b, 0, 0, 0))),
        out_shape=(jax.ShapeDtypeStruct((B, G, nch, K_AUG), BF16),
                   jax.ShapeDtypeStruct((B, G, dh, nch), BF16)),
        compiler_params=_params("parallel"),
        name="compress",
    )(xc, pe_t, wexp, w2b, w2t)


NSA_TQ = 256
SWEEP_TK = AUG_NBLK * SLC_BLOCK
SEL_SUB = 128
SEL_AHEAD = 4


def _rowmax8(s):
    return jnp.max(s.reshape(s.shape[0] // 8, 8, s.shape[1]), axis=0)


def _online_softmax(chunks, scores, values):
    pending = [scores(ch) for ch in chunks[:SEL_AHEAD]]
    m = acc = None
    for n, ch in enumerate(chunks):
        if n + SEL_AHEAD < len(chunks):
            pending.append(scores(chunks[n + SEL_AHEAD]))
        s = pending.pop(0)
        m_c = jnp.max(_rowmax8(s), axis=0, keepdims=True)
        if m is None:
            m = m_c
            acc = _dot(values(ch), jnp.exp(s - m).astype(BF16))
        else:
            m_new = jnp.maximum(m, m_c)
            acc = jnp.exp(m - m_new) * acc + _dot(values(ch), jnp.exp(s - m_new).astype(BF16))
            m = m_new
    return acc


def _nsa_kernel(qt_ref, kc_ref, vct_ref, ks_ref, vst_ref, kw_ref, vwt_ref, gate_ref, ov_ref,
                o_ref, sbt_ref, osel_ref, *, slopes):
    g = pl.program_id(1)
    i = pl.program_id(2)
    R, dh, tq, tk = NSA_REP, HEAD_DIM, NSA_TQ, SWEEP_TK
    L = R * tq
    t0 = i * tq
    nc = kc_ref.shape[0]
    ns = ov_ref.shape[0]
    n_tiles = ns // AUG_NBLK
    tile4 = lambda a: jnp.concatenate([a] * R, axis=1)

    qt = qt_ref[...]
    qs = jnp.concatenate([qt[r * dh:(r + 1) * dh, :] for r in range(R)], axis=1)
    slope = [jnp.where(g == 0, slopes[r], slopes[R + r]).astype(F32) for r in range(R)]
    slope_row = jnp.concatenate([jnp.full((1, tq), 1.0, F32) * slope[r] for r in range(R)], axis=1)
    t_row = t0 + lax.broadcasted_iota(jnp.int32, (1, tq), 1)

    r8 = lax.broadcasted_iota(jnp.int32, (AUG_NBLK, L), 0)
    alibi8 = jnp.where(r8 < 2, slope_row, jnp.where(r8 == 2, -slope_row * t0.astype(F32), 0.0))
    q_pad = jnp.zeros((K_AUG - dh - 2 * AUG_NBLK, L), BF16)
    q_plain = jnp.concatenate([qs, jnp.concatenate([jnp.zeros((AUG_NBLK, L), F32), alibi8], axis=0).astype(BF16),
                               q_pad], axis=0)

    cmp_end = lax.broadcasted_iota(jnp.int32, (nc, tq), 0) * CMP_STRIDE + (CMP_BLOCK - 1)
    m_cmp = t_row >= cmp_end
    sc = _dot(kc_ref[...], q_plain) + tile4(jnp.where(m_cmp, 0.0, NEG_INF))

    a0 = jnp.maximum(t0 - NSA_WINDOW, 0)
    kpos_w = lax.broadcasted_iota(jnp.int32, (SEL_SUB, tq), 0)
    t_w = t0 + lax.broadcasted_iota(jnp.int32, (SEL_SUB, tq), 1)

    def win_scores(ch):
        diag, r = ch
        k0 = pl.multiple_of((t0 if diag else a0) + r, SEL_SUB)
        kpos = k0 + kpos_w
        valid = (kpos <= t_w) if diag else ((kpos < t0) & (t_w - kpos < NSA_WINDOW))
        return _dot(kw_ref[pl.ds(k0, SEL_SUB), :], q_plain) + tile4(jnp.where(valid, 0.0, NEG_INF))

    def win_values(ch):
        diag, r = ch
        return vwt_ref[:, pl.ds(pl.multiple_of((t0 if diag else a0) + r, SEL_SUB), SEL_SUB)]

    win_chunks = ([(True, r) for r in range(0, tq, SEL_SUB)]
                  + [(False, r) for r in range(0, NSA_WINDOW, SEL_SUB)])
    acc_w = _online_softmax(win_chunks, win_scores, win_values)
    o_win = acc_w[:dh] * (1.0 / acc_w[dh:dh + 1])

    mx = jnp.max(_rowmax8(sc), axis=0, keepdims=True)
    p = jnp.exp(sc - mx) * tile4(jnp.where(m_cmp, 1.0, 0.0))
    den = jnp.sum(jnp.sum(p.reshape(nc // 8, 8, L), axis=0), axis=0, keepdims=True)
    pr = p * (1.0 / jnp.maximum(den, 1e-30))
    o_cmp = _dot(vct_ref[...], pr.astype(BF16))
    psum = pr[:, 0:tq]
    for r in range(1, R):
        psum = psum + pr[:, r * tq:(r + 1) * tq]

    p_hi = psum.astype(BF16)
    p_lo = (psum - p_hi.astype(F32)).astype(BF16)
    imp = _dot(ov_ref[...], p_hi) + _dot(ov_ref[...], p_lo)
    blk = lax.broadcasted_iota(jnp.int32, (ns, tq), 0)
    cur = lax.shift_right_logical(t_row, SLC_SHIFT)
    val = jnp.where((blk == cur) | (blk == 0), FORCE_SCORE, imp)
    val = jnp.where(blk <= cur, val, -1.0)
    vals = [val[8 * v:8 * v + 8, :] for v in range(ns // 8)]
    ranks = [jnp.zeros((8, tq), F32) for _ in vals]
    row8 = lax.broadcasted_iota(jnp.int32, (8, tq), 0)
    for j in range(ns):
        vj = jnp.broadcast_to(val[j:j + 1, :], (8, tq))
        for v in range(len(vals)):
            if 8 * v > j:
                ahead = vj >= vals[v]
            elif 8 * v + 7 <= j:
                ahead = vj > vals[v]
            else:
                ahead = (vj > vals[v]) | ((vj == vals[v]) & (row8 > j - 8 * v))
            ranks[v] = ranks[v] + jnp.where(ahead, 1.0, 0.0)
    for T in range(n_tiles):
        selb = jnp.where((ranks[T] < float(SLC_TOP)) & (vals[T] >= 0.0), 0.0, NEG_INF)
        sbt_ref[T] = jnp.concatenate([tile4(selb), alibi8], axis=0).astype(BF16)

    td = lax.div(t0, tk)
    for c in range(n_tiles):
        @pl.when(td == c)
        def _(c=c):
            kpos = c * tk + lax.broadcasted_iota(jnp.int32, (tk, tq), 0)
            t_q = t0 + lax.broadcasted_iota(jnp.int32, (tk, tq), 1)
            causal = tile4(jnp.where(kpos <= t_q, 0.0, NEG_INF))
            q_augs = [jnp.concatenate([qs, sbt_ref[T], q_pad], axis=0) for T in range(c + 1)]
            chunks = [(T, j * SEL_SUB) for T in range(c + 1) for j in range(tk // SEL_SUB)]

            def chunk_scores(T, r):
                s = _dot(ks_ref[T * tk + r:T * tk + r + SEL_SUB, :], q_augs[T])
                return s + causal[r:r + SEL_SUB] if T == c else s

            acc = _online_softmax(chunks, lambda ch: chunk_scores(*ch),
                                  lambda ch: vst_ref[:, ch[0] * tk + ch[1]:ch[0] * tk + ch[1] + SEL_SUB])
            osel_ref[...] = acc[:dh] * (1.0 / acc[dh:dh + 1])

    o_sel = osel_ref[...]
    gates = gate_ref[...]
    tiles = []
    for r in range(R):
        sl = slice(r * tq, (r + 1) * tq)
        tiles.append(gates[3 * r:3 * r + 1, :] * o_cmp[:, sl]
                     + gates[3 * r + 1:3 * r + 2, :] * o_sel[:, sl]
                     + gates[3 * r + 2:3 * r + 3, :] * o_win[:, sl])
    o_ref[...] = jnp.concatenate(tiles, axis=0).T.astype(o_ref.dtype)


def _overlap_matrix(nc, ns):
    cs = np.arange(nc)[None, :] * CMP_STRIDE
    ss = np.arange(ns)[:, None] * SLC_BLOCK
    ov = np.clip(np.minimum(cs + CMP_BLOCK, ss + SLC_BLOCK) - np.maximum(cs, ss), 0, None)
    return jnp.asarray(ov.astype(np.float32) / CMP_BLOCK, dtype=BF16)


def _nsa(qt, kc, vct, ks, vst, kw, vwt, gates):
    B, _, S = qt.shape
    G, R, dh, tq = NSA_GROUPS, NSA_REP, HEAD_DIM, NSA_TQ
    nc = kc.shape[2]
    ns = S // SLC_BLOCK
    ov = _overlap_matrix(nc, ns)
    kern = functools.partial(_nsa_kernel, slopes=tuple(_alibi_slopes(NSA_HEADS)))
    assert S % SWEEP_TK == 0 and S >= NSA_WINDOW + tq
    per_bg = lambda shape: pl.BlockSpec((None, None) + shape, lambda b, g, i: (b, g, 0, 0))
    return pl.pallas_call(
        kern,
        grid=(B, G, S // tq),
        in_specs=[
            pl.BlockSpec((None, R * dh, tq), lambda b, g, i: (b, g, i)),
            per_bg((nc, K_AUG)), per_bg((dh, nc)),
            per_bg((S, K_AUG)), per_bg((V_ROWS, S)),
            per_bg((S, K_AUG)), per_bg((V_ROWS, S)),
            pl.BlockSpec((None, GATE_ROWS, tq), lambda b, g, i: (b, g, i)),
            pl.BlockSpec(ov.shape, lambda b, g, i: (0, 0)),
        ],
        out_specs=pl.BlockSpec((None, tq, R * dh), lambda b, g, i: (b, i, g)),
        out_shape=jax.ShapeDtypeStruct((B, S, G * R * dh), BF16),
        scratch_shapes=[pltpu.VMEM((S // SWEEP_TK, 2 * AUG_NBLK, R * tq), BF16),
                        pltpu.VMEM((dh, R * tq), F32)],
        compiler_params=_params("parallel", "parallel", "arbitrary"),
        name="nsa_attention",
    )(qt, kc, vct, ks, vst, kw, vwt, gates, ov)


BAND_TQ = 128
BAND_SUB = 4
BAND = 128


def _banded_kernel(q_ref, k_ref, v_ref, o_ref, lse_ref, *, slopes, nsub):
    tq, dh, nh = BAND_TQ, HEAD_DIM, DIL_HEADS_PER_GROUP
    tk = tq + BAND
    width = nh * dh
    head_of_lane = lambda rows: lax.shift_right_logical(
        lax.broadcasted_iota(jnp.int32, (rows, width), 1), int(math.log2(dh)))
    lane_head, q_head = head_of_lane(tk), head_of_lane(tq)
    keep = [jnp.where(lane_head == h, 1.0, 0.0).astype(BF16) for h in range(nh)]

    def per_head(a):
        return jnp.concatenate([a * keep[h] for h in range(nh)], axis=0)

    ones_h = jnp.concatenate(keep, axis=0)
    def biases(first_key_offset):
        d = first_key_offset + (lax.broadcasted_iota(jnp.int32, (tq, tk), 0)
                                - lax.broadcasted_iota(jnp.int32, (tq, tk), 1))
        mask_bias = jnp.where((d >= 0) & (d <= BAND), 0.0, NEG_INF)
        neg_d = -d.astype(F32)
        return [slopes[h] * neg_d + mask_bias for h in range(nh)]

    subs = []
    for sub in range(nsub):
        i = pl.program_id(2) * nsub + sub
        k0 = pl.multiple_of(jnp.maximum(i - 1, 0) * tq, tq)
        q = q_ref[sub * tq:(sub + 1) * tq, :]
        subs.append((i * tq - k0, k0, _dot_nt(q, per_head(k_ref[pl.ds(k0, tk), :]))))
    inner_bias = biases(tq) if nsub > 1 else None
    probs = []
    for sub, (off, k0, s) in enumerate(subs):
        bias = biases(off) if sub == 0 else inner_bias
        ps, mxs = [], []
        for h in range(nh):
            sh = s[:, h * tk:(h + 1) * tk] + bias[h]
            mx = jnp.max(sh, axis=-1, keepdims=True)
            ps.append(jnp.exp(sh - mx).astype(BF16))
            mxs.append(mx)
        probs.append((k0, jnp.concatenate(ps, axis=1), mxs))
    for sub, (k0, p, mxs) in enumerate(probs):
        den = _dot(p, ones_h)
        mx_all = mxs[nh - 1]
        for h in range(nh - 2, -1, -1):
            mx_all = jnp.where(q_head == h, mxs[h], mx_all)
        o_ref[sub * tq:(sub + 1) * tq, :] = _dot(p, per_head(v_ref[pl.ds(k0, tk), :])) * (1.0 / den)
        lse_ref[sub * tq:(sub + 1) * tq, :] = mx_all + jnp.log(den)


def _banded(d, slopes):
    B, _, r, n, width = d.shape
    nsub = min(BAND_SUB, n // BAND_TQ)
    tq = BAND_TQ * nsub
    assert n >= BAND_TQ + BAND and n % tq == 0
    kern = functools.partial(_banded_kernel, slopes=tuple(slopes), nsub=nsub)
    tile = pl.BlockSpec((None, None, None, tq, width), lambda b, c, i: (b, 0, c, i, 0))
    kseq = pl.BlockSpec((None, None, None, n, width), lambda b, c, i: (b, 1, c, 0, 0))
    vseq = pl.BlockSpec((None, None, None, n, width), lambda b, c, i: (b, 2, c, 0, 0))
    out = pl.BlockSpec((None, None, tq, width), lambda b, c, i: (b, c, i, 0))
    return pl.pallas_call(
        kern,
        grid=(B, r, n // tq),
        in_specs=[tile, kseq, vseq],
        out_specs=(out, out),
        out_shape=(jax.ShapeDtypeStruct((B, r, n, width), F32),) * 2,
        compiler_params=_params("parallel", "parallel", "arbitrary"),
        name="banded_attention",
    )(d, d, d)


def _dilated(dils):
    slopes = _alibi_slopes(DIL_HEADS)
    outs, lses = [], []
    for gi, (w, r) in enumerate(DIL_CONFIGS):
        assert w // r == BAND
        sl = [s_ * r for s_ in slopes[gi * DIL_HEADS_PER_GROUP:(gi + 1) * DIL_HEADS_PER_GROUP]]
        o, lse = _banded(dils[gi], sl)
        outs.append(o)
        lses.append(lse)
    return outs, lses


MERGE_TM = 512


def _token_order(ref, slab_ref):
    r, rows, width = ref.shape
    if r == 1:
        return ref[0]
    for c in range(r):
        blk = ref[c]
        for s in range(width // LANES):
            slab_ref[s, pl.ds(c, rows, stride=r), :] = blk[:, s * LANES:(s + 1) * LANES]
    return jnp.concatenate([slab_ref[s] for s in range(width // LANES)], axis=1)


def _merge_kernel(x_ref, oa_ref, o0_ref, o1_ref, o2_ref, l0_ref, l1_ref, l2_ref,
                  gmix_ref, wm_ref, wpn_ref, wpd_ref, wo_ref, gffn_ref, x1_ref, h2_ref, slab_ref):
    x = x_ref[...]
    D = x.shape[1]
    h = _rms(x, gmix_ref[...]).astype(BF16)
    gm = _sigmoid(_dot_nt(h, wm_ref[...]))
    o0, o1, o2 = [_token_order(r_, slab_ref) for r_ in (o0_ref, o1_ref, o2_ref)]
    l0, l1, l2 = [_token_order(r_, slab_ref) for r_ in (l0_ref, l1_ref, l2_ref)]
    mx = jnp.maximum(jnp.maximum(l0, l1), l2)
    e0, e1, e2 = jnp.exp(l0 - mx), jnp.exp(l1 - mx), jnp.exp(l2 - mx)
    inv = 1.0 / (e0 + e1 + e2)
    ob = o0 * (e0 * inv) + o1 * (e1 * inv) + o2 * (e2 * inv)
    a = _dot(oa_ref[...], wpn_ref[...])
    d = _dot(ob.astype(BF16), wpd_ref[...])
    mixed = gm[:, :D] * a + gm[:, D:] * d
    x1 = x + _dot(mixed.astype(BF16), wo_ref[...])
    x1_ref[...] = x1
    h2_ref[...] = _rms(x1, gffn_ref[...]).astype(BF16)


def _merge(x, o_a, outs, lses, g_mix, w_merge, w_proj_nsa, w_proj_dil, w_out, g_ffn):
    B, S, D = x.shape
    tm = MERGE_TM
    row = lambda a: pl.BlockSpec((None, tm, a.shape[2]), lambda b, i: (b, i, 0))
    cls = lambda a: pl.BlockSpec((None, a.shape[1], tm // a.shape[1], a.shape[3]), lambda b, i: (b, 0, i, 0))
    full = lambda a: pl.BlockSpec(a.shape, lambda b, i: (0,) * a.ndim)
    ws = [w_merge.astype(BF16), w_proj_nsa.astype(BF16), w_proj_dil.astype(BF16), w_out.astype(BF16)]
    consts = [g_mix, *ws, g_ffn]
    in_specs = [row(x), row(o_a)] + [cls(a) for a in (*outs, *lses)] + [full(a) for a in consts]
    return pl.pallas_call(
        _merge_kernel,
        grid=(B, S // tm),
        in_specs=in_specs,
        out_specs=(pl.BlockSpec((None, tm, D), lambda b, i: (b, i, 0)),) * 2,
        out_shape=(jax.ShapeDtypeStruct((B, S, D), F32), jax.ShapeDtypeStruct((B, S, D), BF16)),
        scratch_shapes=[pltpu.VMEM((DIL_WIDTH // LANES, tm, LANES), F32)],
        compiler_params=_params("parallel", "parallel"),
        name="merge_proj",
    )(x, o_a, *outs, *lses, *consts)


FFN_TM = 512
FFN_TN = 256
HALO = 16


def _ffn_kernel(h_ref, halo_ref, x1_ref, wup_ref, cw_ref, cb_ref, wd_ref, gfin_ref, o_ref, act_ref):
    i = pl.program_id(1)
    h = h_ref[...]
    halo = halo_ref[...]
    tm = h.shape[0]
    row = lax.broadcasted_iota(jnp.int32, (tm, FFN_TN), 0)
    live = (i > 0).astype(F32)
    for j in range(D_FF // FFN_TN):
        cols = slice(j * FFN_TN, (j + 1) * FFN_TN)
        wu = wup_ref[:, cols]
        u = _dot(h, wu)
        uh = _dot(halo, wu) * live
        gate = _dot(h, wup_ref[:, D_FF + j * FFN_TN:D_FF + (j + 1) * FFN_TN])
        p1 = jnp.broadcast_to(uh[HALO - 1:HALO, :], (tm, FFN_TN))
        p2 = jnp.broadcast_to(uh[HALO - 2:HALO - 1, :], (tm, FFN_TN))
        u1 = jnp.where(row == 0, p1, pltpu.roll(u, 1, 0))
        u2 = jnp.where(row == 0, p2, jnp.where(row == 1, p1, pltpu.roll(u, 2, 0)))
        uc = cb_ref[:, cols] + cw_ref[0:1, cols] * u2
        uc = uc + cw_ref[1:2, cols] * u1
        uc = uc + cw_ref[2:3, cols] * u
        act_ref[:, j * FFN_TN:(j + 1) * FFN_TN] = (jax.nn.gelu(uc) * gate).astype(BF16)
    y = _dot(act_ref[...], wd_ref[...])
    o_ref[...] = _rms(x1_ref[...] + y, gfin_ref[...])


def _ffn(h2, x1, w_up, conv_w, conv_b, w_down, g_final):
    B, S, D = h2.shape
    tm = FFN_TM
    assert D_FF % FFN_TN == 0
    wup = w_up.astype(BF16)
    cw = conv_w
    cb = conv_b.reshape(1, D_FF)
    wd = w_down.astype(BF16)
    gfin = g_final.reshape(1, D)
    full = lambda a: pl.BlockSpec(a.shape, lambda b, i: (0,) * a.ndim, pipeline_mode=pl.Buffered(1))
    tile = pl.BlockSpec((None, tm, D), lambda b, i: (b, i, 0))
    halo = pl.BlockSpec((None, HALO, D), lambda b, i: (b, jnp.maximum(i * (tm // HALO) - 1, 0), 0))
    return pl.pallas_call(
        _ffn_kernel,
        grid=(B, S // tm),
        in_specs=[tile, halo, tile, full(wup), full(cw), full(cb), full(wd), full(gfin)],
        out_specs=tile,
        out_shape=jax.ShapeDtypeStruct((B, S, D), F32),
        scratch_shapes=[pltpu.VMEM((tm, D_FF), BF16)],
        compiler_params=_params("parallel", "parallel"),
        name="conv_ffn",
    )(h2, h2, x1, wup, cw, cb, wd, gfin)


@jax.jit
def _layer(x, g_mix, w_in, pe_cmp_k, w_cmp_k1, w_cmp_k2, pe_cmp_v, w_cmp_v1, w_cmp_v2,
           w_proj_nsa, w_proj_dil, w_out, g_ffn, w_up, conv_w, conv_b, w_down, g_final):
    B, S, D = x.shape
    depth = g_mix.shape[0]
    for l in range(depth):
        gm = g_mix[l].reshape(1, D)
        kcmp, vcmp, ks, kw, d0, d1, d2, qt, vst, vwt, gates = _in_proj(x, gm, w_in[l])
        kc, _ = _compress(kcmp, pe_cmp_k[l], w_cmp_k1[l], w_cmp_k2[l])
        _, vct = _compress(vcmp, pe_cmp_v[l], w_cmp_v1[l], w_cmp_v2[l])
        o_a = _nsa(qt, kc, vct, ks, vst, kw, vwt, gates)
        outs, lses = _dilated((d0, d1, d2))
        merge_cols = w_in[l].T[w_in.shape[2] - 2 * D:]
        x1, h2 = _merge(x, o_a, outs, lses, gm, merge_cols, w_proj_nsa[l], w_proj_dil[l], w_out[l],
                        g_ffn[l].reshape(1, D))
        x = _ffn(h2, x1, w_up[l], conv_w[l], conv_b[l], w_down[l], g_final)
        assert depth == 1
    return x


def kernel(x, g_mix, w_in, pe_cmp_k, w_cmp_k1, w_cmp_k2, pe_cmp_v, w_cmp_v1, w_cmp_v2, w_proj_nsa, w_proj_dil, w_out, g_ffn, w_up, conv_w, conv_b, w_down, g_final):
    return _layer(x, g_mix, w_in, pe_cmp_k, w_cmp_k1, w_cmp_k2, pe_cmp_v, w_cmp_v1, w_cmp_v2,
                  w_proj_nsa, w_proj_dil, w_out, g_ffn, w_up, conv_w, conv_b, w_down, g_final)
```

```python
import functools
import math

import numpy as np
import jax
import jax.numpy as jnp
from jax import lax
from jax.experimental import pallas as pl
from jax.experimental.pallas import tpu as pltpu

HEAD_DIM = 64
NSA_HEADS = 8
NSA_GROUPS = 2
NSA_REP = NSA_HEADS // NSA_GROUPS
CMP_BLOCK = 32
CMP_STRIDE = 16
CMP_HIDDEN = 128
SLC_BLOCK = 64
SLC_TOP = 16
NSA_WINDOW = 512
FORCE_SCORE = 1.0e4
DIL_CONFIGS = ((128, 1), (512, 4), (2048, 16))
DIL_GROUPS = 3
DIL_HEADS_PER_GROUP = 4
DIL_HEADS = DIL_GROUPS * DIL_HEADS_PER_GROUP
D_FF = 2816
CONV_WIDTH = 3
RMS_EPS = 1e-6
NEG_INF = -1e30

LANES = 128
VMEM_LIMIT_BYTES = 56 * 1024 * 1024

F32 = jnp.float32
BF16 = jnp.bfloat16
NT_DIMS = (((1,), (1,)), ((), ()))


def _alibi_slopes(n):
    return [float(2.0 ** (-8.0 * i / n)) for i in range(1, n + 1)]


def _rms(xf, g):
    ms = jnp.mean(xf * xf, axis=-1, keepdims=True)
    return xf * lax.rsqrt(ms + RMS_EPS) * g


def _dot(a, b):
    return jnp.dot(a, b, preferred_element_type=F32)


def _dot_nt(a, b):
    return lax.dot_general(a, b, NT_DIMS, preferred_element_type=F32)


def _sigmoid(z):
    return 1.0 / (1.0 + jnp.exp(-z))


def _params(*sem):
    return pltpu.CompilerParams(dimension_semantics=sem, vmem_limit_bytes=VMEM_LIMIT_BYTES)


IN_TM = 512
N_KVC = 4 * HEAD_DIM
N_KSEL = NSA_GROUPS * HEAD_DIM
N_DIL = 3 * DIL_HEADS * HEAD_DIM
DIL_WIDTH = DIL_HEADS_PER_GROUP * HEAD_DIM
T_Q = NSA_HEADS * HEAD_DIM
T_V = NSA_GROUPS * HEAD_DIM
GATE_ROWS = 16
K_AUG = 2 * HEAD_DIM
SLC_SHIFT = int(math.log2(SLC_BLOCK))
AUG_NBLK = 8
AUG_HI, AUG_LO = AUG_NBLK, AUG_NBLK + 1
LOG2E = math.log2(math.e)
LN2 = math.log(2.0)
V_PAD = 16
V_ROWS = HEAD_DIM + V_PAD


def _key_position_columns(pos0, rows, step=1):
    pos = pos0 + step * lax.broadcasted_iota(jnp.int32, (rows, HEAD_DIM), 0)
    col = lax.broadcasted_iota(jnp.int32, (rows, HEAD_DIM), 1)
    blk = jnp.bitwise_and(lax.shift_right_logical(pos, SLC_SHIFT), AUG_NBLK - 1)
    hi = lax.shift_left(lax.shift_right_logical(pos, 7), 7).astype(F32)
    lo = jnp.bitwise_and(pos, 127).astype(F32)
    c = jnp.where((col == AUG_HI) | (col == AUG_HI + 2), hi,
                  jnp.where((col == AUG_LO) | (col == AUG_LO + 2), lo, 0.0))
    return jnp.where((col < AUG_NBLK) & (blk == col), 1.0, c)


def _in_proj_kernel(x_ref, g_ref, wn_ref, wt_ref,
                    kcmp_ref, vcmp_ref, ks_ref, kw_ref, d0_ref, d1_ref, d2_ref, qt_ref, vst_ref, vwt_ref, gate_ref,
                    slab_ref):
    tm = x_ref.shape[0]
    h = _rms(x_ref[...], g_ref[...]).astype(BF16)
    c0 = 0
    kvc = _dot_nt(h, wn_ref[c0:c0 + N_KVC, :])
    kcmp_ref[...] = kvc[:, :N_KSEL]
    vcmp_ref[...] = kvc[:, N_KSEL:]
    c0 += N_KVC
    ks = _dot_nt(h, wn_ref[c0:c0 + N_KSEL, :]).astype(BF16)
    c0 += N_KSEL
    kw = _dot_nt(h, wn_ref[c0:c0 + N_KSEL, :]).astype(BF16)
    c0 += N_KSEL
    aug = _key_position_columns(pl.program_id(1) * tm, tm).astype(BF16)
    for g in range(NSA_GROUPS):
        ks_ref[g] = jnp.concatenate([ks[:, g * HEAD_DIM:(g + 1) * HEAD_DIM], aug], axis=1)
        kw_ref[g] = jnp.concatenate([kw[:, g * HEAD_DIM:(g + 1) * HEAD_DIM], aug], axis=1)
    seg = DIL_HEADS * HEAD_DIM
    for which in range(3):
        y = _dot_nt(h, wn_ref[c0 + which * seg:c0 + (which + 1) * seg, :])
        for gi, (d_ref, (_, r)) in enumerate(zip((d0_ref, d1_ref, d2_ref), DIL_CONFIGS)):
            yg = y[:, gi * DIL_WIDTH:(gi + 1) * DIL_WIDTH]
            if r == 1:
                d_ref[which, 0] = yg.astype(BF16)
                continue
            for s in range(DIL_WIDTH // LANES):
                slab_ref[s] = yg[:, s * LANES:(s + 1) * LANES]
            for c in range(r):
                d_ref[which, c] = jnp.concatenate(
                    [slab_ref[s, pl.ds(c, tm // r, stride=r), :] for s in range(DIL_WIDTH // LANES)],
                    axis=1).astype(BF16)
    yt = _dot_nt(wt_ref[...], h)
    qt_ref[...] = yt[0:T_Q].astype(BF16)
    r0 = T_Q
    ones = jnp.where(lax.broadcasted_iota(jnp.int32, (V_PAD, tm), 0) == 0, 1.0, 0.0).astype(BF16)
    for ref in (vst_ref, vwt_ref):
        vt = yt[r0:r0 + T_V].astype(BF16)
        r0 += T_V
        for g in range(NSA_GROUPS):
            ref[g] = jnp.concatenate([vt[g * HEAD_DIM:(g + 1) * HEAD_DIM, :], ones], axis=0)
    gate_ref[...] = _sigmoid(yt[r0:r0 + NSA_GROUPS * GATE_ROWS])


def _in_proj(x, g_mix, w_in):
    B, S, D = x.shape
    scale = HEAD_DIM ** -0.5 * LOG2E
    o_q, o_kv = 0, T_Q
    o_gate = o_kv + 6 * N_KSEL
    o_dil = o_gate + 3 * NSA_HEADS
    o_merge = o_dil + N_DIL
    w_t = w_in.T
    kv = w_t[o_kv:o_gate]

    def kind(k):
        return kv[k * N_KSEL:(k + 1) * N_KSEL]

    dil = w_t[o_dil:o_merge]
    dil = jnp.concatenate([dil[:DIL_HEADS * HEAD_DIM] * scale, dil[DIL_HEADS * HEAD_DIM:]], axis=0)
    wn = jnp.concatenate([kind(0), kind(1), kind(2), kind(4), dil], axis=0).astype(BF16)
    wg = w_t[o_gate:o_dil].reshape(NSA_GROUPS, 3 * NSA_REP, D)
    wg = jnp.pad(wg, ((0, 0), (0, GATE_ROWS - 3 * NSA_REP), (0, 0))).reshape(NSA_GROUPS * GATE_ROWS, D)
    wt = jnp.concatenate([w_t[o_q:o_kv] * scale, kind(3), kind(5), wg], axis=0).astype(BF16)
    tm = IN_TM
    grid = (B, S // tm)
    full = lambda a: pl.BlockSpec(a.shape, lambda b, i: (0,) * a.ndim)
    k_shape = jax.ShapeDtypeStruct((B, NSA_GROUPS, S, K_AUG), BF16)
    v_shape = jax.ShapeDtypeStruct((B, NSA_GROUPS, V_ROWS, S), BF16)
    k_spec = pl.BlockSpec((None, NSA_GROUPS, tm, K_AUG), lambda b, i: (b, 0, i, 0))
    v_spec = pl.BlockSpec((None, NSA_GROUPS, V_ROWS, tm), lambda b, i: (b, 0, 0, i))
    c_shape = jax.ShapeDtypeStruct((B, S, N_KSEL), F32)
    c_spec = pl.BlockSpec((None, tm, N_KSEL), lambda b, i: (b, i, 0))
    d_shapes = tuple(jax.ShapeDtypeStruct((B, 3, r, S // r, DIL_WIDTH), BF16) for _, r in DIL_CONFIGS)
    d_specs = tuple(pl.BlockSpec((None, 3, r, tm // r, DIL_WIDTH), lambda b, i: (b, 0, 0, i, 0))
                    for _, r in DIL_CONFIGS)
    out_shape = (
        c_shape, c_shape, k_shape, k_shape, *d_shapes,
        jax.ShapeDtypeStruct((B, T_Q, S), BF16),
        v_shape,
        v_shape,
        jax.ShapeDtypeStruct((B, NSA_GROUPS * GATE_ROWS, S), F32),
    )
    out_specs = (
        c_spec, c_spec, k_spec, k_spec, *d_specs,
        pl.BlockSpec((None, T_Q, tm), lambda b, i: (b, 0, i)),
        v_spec,
        v_spec,
        pl.BlockSpec((None, NSA_GROUPS * GATE_ROWS, tm), lambda b, i: (b, 0, i)),
    )
    return pl.pallas_call(
        _in_proj_kernel,
        grid=grid,
        in_specs=[pl.BlockSpec((None, tm, D), lambda b, i: (b, i, 0)), full(g_mix), full(wn), full(wt)],
        out_specs=out_specs,
        out_shape=out_shape,
        scratch_shapes=[pltpu.VMEM((DIL_WIDTH // LANES, tm, LANES), F32)],
        compiler_params=_params("parallel", "parallel"),
        name="in_proj",
    )(x, g_mix, wn, wt)


def _compress_kernel(x_ref, pe_ref, w1_ref, w2_ref, w2t_ref, c_ref, ct_ref):
    nch = x_ref.shape[0] // CMP_STRIDE
    ulo = jnp.zeros((nch, NSA_GROUPS * CMP_HIDDEN), F32)
    uhi = jnp.zeros((nch, NSA_GROUPS * CMP_HIDDEN), F32)
    for j in range(CMP_STRIDE):
        xj = x_ref[pl.ds(j, nch, stride=CMP_STRIDE), :]
        ulo = ulo + _dot((xj + pe_ref[j:j + 1, :]).astype(BF16), w1_ref[j])
        uhi = uhi + _dot((xj + pe_ref[CMP_STRIDE + j:CMP_STRIDE + j + 1, :]).astype(BF16), w1_ref[CMP_STRIDE + j])
    pre = ulo + pltpu.roll(uhi, nch - 1, 0)
    hid = jax.nn.gelu(pre).astype(BF16)
    aug = _key_position_columns(CMP_BLOCK - 1, nch, CMP_STRIDE).astype(BF16)
    for g in range(NSA_GROUPS):
        hg = hid[:, g * CMP_HIDDEN:(g + 1) * CMP_HIDDEN]
        c_ref[g] = jnp.concatenate([_dot(hg, w2_ref[...]).astype(BF16), aug], axis=1)
        ct_ref[g] = _dot_nt(w2t_ref[...], hg).astype(BF16)


def _compress(xc, pe, w1, w2):
    B, S, width = xc.shape
    nch = S // CMP_STRIDE
    G, dh, hid = NSA_GROUPS, HEAD_DIM, CMP_HIDDEN
    pe_t = jnp.broadcast_to(pe.reshape(CMP_BLOCK, 1, dh), (CMP_BLOCK, G, dh)).reshape(CMP_BLOCK, width)
    eye = jnp.eye(G, dtype=w1.dtype)
    wexp = jnp.einsum('pdn,ge->pgden', w1.reshape(CMP_BLOCK, dh, hid), eye).reshape(CMP_BLOCK, width, G * hid)
    wexp = wexp.astype(BF16)
    full = lambda a: pl.BlockSpec(a.shape, lambda b: (0,) * a.ndim)
    w2b = w2.astype(BF16)
    w2t = w2.T.astype(BF16)
    return pl.pallas_call(
        _compress_kernel,
        grid=(B,),
        in_specs=[pl.BlockSpec((None, S, width), lambda b: (b, 0, 0)), full(pe_t), full(wexp), full(w2b), full(w2t)],
        out_specs=(pl.BlockSpec((None, G, nch, K_AUG), lambda b: (b, 0, 0, 0)),
                   pl.BlockSpec((None, G, dh, nch), lambda b: (b, 0, 0, 0))),
        out_shape=(jax.ShapeDtypeStruct((B, G, nch, K_AUG), BF16),
                   jax.ShapeDtypeStruct((B, G, dh, nch), BF16)),
        compiler_params=_params("parallel"),
        name="compress",
    )(xc, pe_t, wexp, w2b, w2t)


NSA_TQ = 256
SWEEP_TK = AUG_NBLK * SLC_BLOCK
SEL_SUB = 128
SEL_AHEAD = 4


def _rowmax8(s):
    return jnp.max(s.reshape(s.shape[0] // 8, 8, s.shape[1]), axis=0)


def _online_softmax(chunks, scores, values):
    pending = [scores(ch) for ch in chunks[:SEL_AHEAD]]
    m = acc = None
    for n, ch in enumerate(chunks):
        if n + SEL_AHEAD < len(chunks):
            pending.append(scores(chunks[n + SEL_AHEAD]))
        s = pending.pop(0)
        m_c = jnp.max(_rowmax8(s), axis=0, keepdims=True)
        if m is None:
            m = m_c
            acc = _dot(values(ch), jnp.exp2(s - m).astype(BF16))
        else:
            m_new = jnp.maximum(m, m_c)
            acc = jnp.exp2(m - m_new) * acc + _dot(values(ch), jnp.exp2(s - m_new).astype(BF16))
            m = m_new
    return acc


def _nsa_kernel(qt_ref, kc_ref, vct_ref, ks_ref, vst_ref, kw_ref, vwt_ref, gate_ref, ov_ref,
                o_ref, sbt_ref, osel_ref, *, slopes):
    g = pl.program_id(1)
    i = pl.program_id(2)
    R, dh, tq, tk = NSA_REP, HEAD_DIM, NSA_TQ, SWEEP_TK
    L = R * tq
    t0 = i * tq
    nc = kc_ref.shape[0]
    ns = ov_ref.shape[0]
    n_tiles = ns // AUG_NBLK
    tile4 = lambda a: jnp.concatenate([a] * R, axis=1)

    qt = qt_ref[...]
    qs = jnp.concatenate([qt[r * dh:(r + 1) * dh, :] for r in range(R)], axis=1)
    slope = [jnp.where(g == 0, slopes[r], slopes[R + r]).astype(F32) for r in range(R)]
    slope_row = jnp.concatenate([jnp.full((1, tq), 1.0, F32) * slope[r] for r in range(R)], axis=1)
    t_row = t0 + lax.broadcasted_iota(jnp.int32, (1, tq), 1)

    r8 = lax.broadcasted_iota(jnp.int32, (AUG_NBLK, L), 0)
    s_full = slope_row * LOG2E
    s_hi = s_full.astype(BF16).astype(F32)
    alibi8 = jnp.where(r8 < 2, s_hi, jnp.where(r8 < 4, s_full - s_hi, 0.0))
    q_pad = jnp.zeros((K_AUG - dh - 2 * AUG_NBLK, L), BF16)
    q_plain = jnp.concatenate([qs, jnp.concatenate([jnp.zeros((AUG_NBLK, L), F32), alibi8], axis=0).astype(BF16),
                               q_pad], axis=0)

    cmp_end = lax.broadcasted_iota(jnp.int32, (nc, tq), 0) * CMP_STRIDE + (CMP_BLOCK - 1)
    m_cmp = t_row >= cmp_end
    sc = _dot(kc_ref[...], q_plain) + tile4(jnp.where(m_cmp, 0.0, NEG_INF))

    a0 = jnp.maximum(t0 - NSA_WINDOW, 0)
    kpos_w = lax.broadcasted_iota(jnp.int32, (SEL_SUB, tq), 0)
    t_w = t0 + lax.broadcasted_iota(jnp.int32, (SEL_SUB, tq), 1)

    def win_scores(ch):
        diag, r = ch
        k0 = pl.multiple_of((t0 if diag else a0) + r, SEL_SUB)
        kpos = k0 + kpos_w
        valid = (kpos <= t_w) if diag else ((kpos < t0) & (t_w - kpos < NSA_WINDOW))
        return _dot(kw_ref[pl.ds(k0, SEL_SUB), :], q_plain) + tile4(jnp.where(valid, 0.0, NEG_INF))

    def win_values(ch):
        diag, r = ch
        return vwt_ref[:, pl.ds(pl.multiple_of((t0 if diag else a0) + r, SEL_SUB), SEL_SUB)]

    win_chunks = ([(True, r) for r in range(0, tq, SEL_SUB)]
                  + [(False, r) for r in range(0, NSA_WINDOW, SEL_SUB)])
    acc_w = _online_softmax(win_chunks, win_scores, win_values)
    o_win = acc_w[:dh] * (1.0 / acc_w[dh:dh + 1])

    mx = jnp.max(_rowmax8(sc), axis=0, keepdims=True)
    p = jnp.exp2(sc - mx) * tile4(jnp.where(m_cmp, 1.0, 0.0))
    den = jnp.sum(jnp.sum(p.reshape(nc // 8, 8, L), axis=0), axis=0, keepdims=True)
    pr = p * (1.0 / jnp.maximum(den, 1e-30))
    o_cmp = _dot(vct_ref[...], pr.astype(BF16))
    psum = pr[:, 0:tq]
    for r in range(1, R):
        psum = psum + pr[:, r * tq:(r + 1) * tq]

    p_hi = psum.astype(BF16)
    p_lo = (psum - p_hi.astype(F32)).astype(BF16)
    imp = _dot(ov_ref[...], p_hi) + _dot(ov_ref[...], p_lo)
    blk = lax.broadcasted_iota(jnp.int32, (ns, tq), 0)
    cur = lax.shift_right_logical(t_row, SLC_SHIFT)
    val = jnp.where((blk == cur) | (blk == 0), FORCE_SCORE, imp)
    val = jnp.where(blk <= cur, val, -1.0)
    vals = [val[8 * v:8 * v + 8, :] for v in range(ns // 8)]
    ranks = [jnp.zeros((8, tq), F32) for _ in vals]
    row8 = lax.broadcasted_iota(jnp.int32, (8, tq), 0)
    for j in range(ns):
        vj = jnp.broadcast_to(val[j:j + 1, :], (8, tq))
        for v in range(len(vals)):
            if 8 * v > j:
                ahead = vj >= vals[v]
            elif 8 * v + 7 <= j:
                ahead = vj > vals[v]
            else:
                ahead = (vj > vals[v]) | ((vj == vals[v]) & (row8 > j - 8 * v))
            ranks[v] = ranks[v] + jnp.where(ahead, 1.0, 0.0)
    for T in range(n_tiles):
        selb = jnp.where((ranks[T] < float(SLC_TOP)) & (vals[T] >= 0.0), 0.0, NEG_INF)
        sbt_ref[T] = jnp.concatenate([tile4(selb), alibi8], axis=0).astype(BF16)

    td = lax.div(t0, tk)
    for c in range(n_tiles):
        @pl.when(td == c)
        def _(c=c):
            kpos = c * tk + lax.broadcasted_iota(jnp.int32, (tk, tq), 0)
            t_q = t0 + lax.broadcasted_iota(jnp.int32, (tk, tq), 1)
            causal = tile4(jnp.where(kpos <= t_q, 0.0, NEG_INF))
            q_augs = [jnp.concatenate([qs, sbt_ref[T], q_pad], axis=0) for T in range(c + 1)]
            chunks = [(T, j * SEL_SUB) for T in range(c + 1) for j in range(tk // SEL_SUB)]

            def chunk_scores(T, r):
                s = _dot(ks_ref[T * tk + r:T * tk + r + SEL_SUB, :], q_augs[T])
                return s + causal[r:r + SEL_SUB] if T == c else s

            acc = _online_softmax(chunks, lambda ch: chunk_scores(*ch),
                                  lambda ch: vst_ref[:, ch[0] * tk + ch[1]:ch[0] * tk + ch[1] + SEL_SUB])
            osel_ref[...] = acc[:dh] * (1.0 / acc[dh:dh + 1])

    o_sel = osel_ref[...]
    gates = gate_ref[...]
    tiles = []
    for r in range(R):
        sl = slice(r * tq, (r + 1) * tq)
        tiles.append(gates[3 * r:3 * r + 1, :] * o_cmp[:, sl]
                     + gates[3 * r + 1:3 * r + 2, :] * o_sel[:, sl]
                     + gates[3 * r + 2:3 * r + 3, :] * o_win[:, sl])
    o_ref[...] = jnp.concatenate(tiles, axis=0).T.astype(o_ref.dtype)


def _overlap_matrix(nc, ns):
    cs = np.arange(nc)[None, :] * CMP_STRIDE
    ss = np.arange(ns)[:, None] * SLC_BLOCK
    ov = np.clip(np.minimum(cs + CMP_BLOCK, ss + SLC_BLOCK) - np.maximum(cs, ss), 0, None)
    return jnp.asarray(ov.astype(np.float32) / CMP_BLOCK, dtype=BF16)


def _nsa(qt, kc, vct, ks, vst, kw, vwt, gates):
    B, _, S = qt.shape
    G, R, dh, tq = NSA_GROUPS, NSA_REP, HEAD_DIM, NSA_TQ
    nc = kc.shape[2]
    ns = S // SLC_BLOCK
    ov = _overlap_matrix(nc, ns)
    kern = functools.partial(_nsa_kernel, slopes=tuple(_alibi_slopes(NSA_HEADS)))
    assert S % SWEEP_TK == 0 and S >= NSA_WINDOW + tq
    per_bg = lambda shape: pl.BlockSpec((None, None) + shape, lambda b, g, i: (b, g, 0, 0))
    return pl.pallas_call(
        kern,
        grid=(B, G, S // tq),
        in_specs=[
            pl.BlockSpec((None, R * dh, tq), lambda b, g, i: (b, g, i)),
            per_bg((nc, K_AUG)), per_bg((dh, nc)),
            per_bg((S, K_AUG)), per_bg((V_ROWS, S)),
            per_bg((S, K_AUG)), per_bg((V_ROWS, S)),
            pl.BlockSpec((None, GATE_ROWS, tq), lambda b, g, i: (b, g, i)),
            pl.BlockSpec(ov.shape, lambda b, g, i: (0, 0)),
        ],
        out_specs=pl.BlockSpec((None, tq, R * dh), lambda b, g, i: (b, i, g)),
        out_shape=jax.ShapeDtypeStruct((B, S, G * R * dh), BF16),
        scratch_shapes=[pltpu.VMEM((S // SWEEP_TK, 2 * AUG_NBLK, R * tq), BF16),
                        pltpu.VMEM((dh, R * tq), F32)],
        compiler_params=_params("parallel", "parallel", "arbitrary"),
        name="nsa_attention",
    )(qt, kc, vct, ks, vst, kw, vwt, gates, ov)


BAND_TQ = 128
BAND_SUB = 4
BAND = 128


def _banded_kernel(q_ref, k_ref, v_ref, o_ref, lse_ref, *, slopes, nsub):
    tq, dh, nh = BAND_TQ, HEAD_DIM, DIL_HEADS_PER_GROUP
    tk = tq + BAND
    width = nh * dh
    head_of_lane = lambda rows: lax.shift_right_logical(
        lax.broadcasted_iota(jnp.int32, (rows, width), 1), int(math.log2(dh)))
    lane_head, q_head = head_of_lane(tk), head_of_lane(tq)
    keep = [jnp.where(lane_head == h, 1.0, 0.0).astype(BF16) for h in range(nh)]

    def per_head(a):
        return jnp.concatenate([a * keep[h] for h in range(nh)], axis=0)

    ones_h = jnp.concatenate(keep, axis=0)
    def biases(first_key_offset):
        d = first_key_offset + (lax.broadcasted_iota(jnp.int32, (tq, tk), 0)
                                - lax.broadcasted_iota(jnp.int32, (tq, tk), 1))
        mask_bias = jnp.where((d >= 0) & (d <= BAND), 0.0, NEG_INF)
        neg_d = -d.astype(F32)
        return [slopes[h] * neg_d + mask_bias for h in range(nh)]

    subs = []
    for sub in range(nsub):
        i = pl.program_id(2) * nsub + sub
        k0 = pl.multiple_of(jnp.maximum(i - 1, 0) * tq, tq)
        q = q_ref[sub * tq:(sub + 1) * tq, :]
        subs.append((i * tq - k0, k0, _dot_nt(q, per_head(k_ref[pl.ds(k0, tk), :]))))
    inner_bias = biases(tq) if nsub > 1 else None
    probs = []
    for sub, (off, k0, s) in enumerate(subs):
        bias = biases(off) if sub == 0 else inner_bias
        ps, mxs = [], []
        for h in range(nh):
            sh = s[:, h * tk:(h + 1) * tk] + bias[h]
            mx = jnp.max(sh, axis=-1, keepdims=True)
            ps.append(jnp.exp2(sh - mx).astype(BF16))
            mxs.append(mx)
        probs.append((k0, jnp.concatenate(ps, axis=1), mxs))
    for sub, (k0, p, mxs) in enumerate(probs):
        den = _dot(p, ones_h)
        mx_all = mxs[nh - 1]
        for h in range(nh - 2, -1, -1):
            mx_all = jnp.where(q_head == h, mxs[h], mx_all)
        o_ref[sub * tq:(sub + 1) * tq, :] = _dot(p, per_head(v_ref[pl.ds(k0, tk), :])) * (1.0 / den)
        lse_ref[sub * tq:(sub + 1) * tq, :] = mx_all * LN2 + jnp.log(den)


def _banded(d, slopes):
    B, _, r, n, width = d.shape
    nsub = min(BAND_SUB, n // BAND_TQ)
    tq = BAND_TQ * nsub
    assert n >= BAND_TQ + BAND and n % tq == 0
    kern = functools.partial(_banded_kernel, slopes=tuple(slopes), nsub=nsub)
    tile = pl.BlockSpec((None, None, None, tq, width), lambda b, c, i: (b, 0, c, i, 0))
    kseq = pl.BlockSpec((None, None, None, n, width), lambda b, c, i: (b, 1, c, 0, 0))
    vseq = pl.BlockSpec((None, None, None, n, width), lambda b, c, i: (b, 2, c, 0, 0))
    out = pl.BlockSpec((None, None, tq, width), lambda b, c, i: (b, c, i, 0))
    return pl.pallas_call(
        kern,
        grid=(B, r, n // tq),
        in_specs=[tile, kseq, vseq],
        out_specs=(out, out),
        out_shape=(jax.ShapeDtypeStruct((B, r, n, width), F32),) * 2,
        compiler_params=_params("parallel", "parallel", "arbitrary"),
        name="banded_attention",
    )(d, d, d)


def _dilated(dils):
    slopes = _alibi_slopes(DIL_HEADS)
    outs, lses = [], []
    for gi, (w, r) in enumerate(DIL_CONFIGS):
        assert w // r == BAND
        sl = [s_ * r * LOG2E for s_ in slopes[gi * DIL_HEADS_PER_GROUP:(gi + 1) * DIL_HEADS_PER_GROUP]]
        o, lse = _banded(dils[gi], sl)
        outs.append(o)
        lses.append(lse)
    return outs, lses


MERGE_TM = 512


def _token_order(ref, slab_ref):
    r, rows, width = ref.shape
    if r == 1:
        return ref[0]
    for c in range(r):
        blk = ref[c]
        for s in range(width // LANES):
            slab_ref[s, pl.ds(c, rows, stride=r), :] = blk[:, s * LANES:(s + 1) * LANES]
    return jnp.concatenate([slab_ref[s] for s in range(width // LANES)], axis=1)


def _merge_kernel(x_ref, oa_ref, o0_ref, o1_ref, o2_ref, l0_ref, l1_ref, l2_ref,
                  gmix_ref, wm_ref, wpn_ref, wpd_ref, wo_ref, gffn_ref, x1_ref, h2_ref, slab_ref):
    x = x_ref[...]
    D = x.shape[1]
    h = _rms(x, gmix_ref[...]).astype(BF16)
    gm = _sigmoid(_dot_nt(h, wm_ref[...]))
    o0, o1, o2 = [_token_order(r_, slab_ref) for r_ in (o0_ref, o1_ref, o2_ref)]
    l0, l1, l2 = [_token_order(r_, slab_ref) for r_ in (l0_ref, l1_ref, l2_ref)]
    mx = jnp.maximum(jnp.maximum(l0, l1), l2)
    e0, e1, e2 = jnp.exp(l0 - mx), jnp.exp(l1 - mx), jnp.exp(l2 - mx)
    inv = 1.0 / (e0 + e1 + e2)
    ob = o0 * (e0 * inv) + o1 * (e1 * inv) + o2 * (e2 * inv)
    a = _dot(oa_ref[...], wpn_ref[...])
    d = _dot(ob.astype(BF16), wpd_ref[...])
    mixed = gm[:, :D] * a + gm[:, D:] * d
    x1 = x + _dot(mixed.astype(BF16), wo_ref[...])
    x1_ref[...] = x1
    h2_ref[...] = _rms(x1, gffn_ref[...]).astype(BF16)


def _merge(x, o_a, outs, lses, g_mix, w_merge, w_proj_nsa, w_proj_dil, w_out, g_ffn):
    B, S, D = x.shape
    tm = MERGE_TM
    row = lambda a: pl.BlockSpec((None, tm, a.shape[2]), lambda b, i: (b, i, 0))
    cls = lambda a: pl.BlockSpec((None, a.shape[1], tm // a.shape[1], a.shape[3]), lambda b, i: (b, 0, i, 0))
    full = lambda a: pl.BlockSpec(a.shape, lambda b, i: (0,) * a.ndim)
    ws = [w_merge.astype(BF16), w_proj_nsa.astype(BF16), w_proj_dil.astype(BF16), w_out.astype(BF16)]
    consts = [g_mix, *ws, g_ffn]
    in_specs = [row(x), row(o_a)] + [cls(a) for a in (*outs, *lses)] + [full(a) for a in consts]
    return pl.pallas_call(
        _merge_kernel,
        grid=(B, S // tm),
        in_specs=in_specs,
        out_specs=(pl.BlockSpec((None, tm, D), lambda b, i: (b, i, 0)),) * 2,
        out_shape=(jax.ShapeDtypeStruct((B, S, D), F32), jax.ShapeDtypeStruct((B, S, D), BF16)),
        scratch_shapes=[pltpu.VMEM((DIL_WIDTH // LANES, tm, LANES), F32)],
        compiler_params=_params("parallel", "parallel"),
        name="merge_proj",
    )(x, o_a, *outs, *lses, *consts)


FFN_TM = 512
FFN_TN = 256
HALO = 16


def _ffn_kernel(h_ref, halo_ref, x1_ref, wup_ref, cw_ref, cb_ref, wd_ref, gfin_ref, o_ref, act_ref):
    i = pl.program_id(1)
    h = h_ref[...]
    halo = halo_ref[...]
    tm = h.shape[0]
    row = lax.broadcasted_iota(jnp.int32, (tm, FFN_TN), 0)
    live = (i > 0).astype(F32)
    for j in range(D_FF // FFN_TN):
        cols = slice(j * FFN_TN, (j + 1) * FFN_TN)
        wu = wup_ref[:, cols]
        u = _dot(h, wu)
        uh = _dot(halo, wu) * live
        gate = _dot(h, wup_ref[:, D_FF + j * FFN_TN:D_FF + (j + 1) * FFN_TN])
        p1 = jnp.broadcast_to(uh[HALO - 1:HALO, :], (tm, FFN_TN))
        p2 = jnp.broadcast_to(uh[HALO - 2:HALO - 1, :], (tm, FFN_TN))
        u1 = jnp.where(row == 0, p1, pltpu.roll(u, 1, 0))
        u2 = jnp.where(row == 0, p2, jnp.where(row == 1, p1, pltpu.roll(u, 2, 0)))
        uc = cb_ref[:, cols] + cw_ref[0:1, cols] * u2
        uc = uc + cw_ref[1:2, cols] * u1
        uc = uc + cw_ref[2:3, cols] * u
        act_ref[:, j * FFN_TN:(j + 1) * FFN_TN] = (jax.nn.gelu(uc) * gate).astype(BF16)
    y = _dot(act_ref[...], wd_ref[...])
    o_ref[...] = _rms(x1_ref[...] + y, gfin_ref[...])


def _ffn(h2, x1, w_up, conv_w, conv_b, w_down, g_final):
    B, S, D = h2.shape
    tm = FFN_TM
    assert D_FF % FFN_TN == 0
    wup = w_up.astype(BF16)
    cw = conv_w
    cb = conv_b.reshape(1, D_FF)
    wd = w_down.astype(BF16)
    gfin = g_final.reshape(1, D)
    full = lambda a: pl.BlockSpec(a.shape, lambda b, i: (0,) * a.ndim, pipeline_mode=pl.Buffered(1))
    tile = pl.BlockSpec((None, tm, D), lambda b, i: (b, i, 0))
    halo = pl.BlockSpec((None, HALO, D), lambda b, i: (b, jnp.maximum(i * (tm // HALO) - 1, 0), 0))
    return pl.pallas_call(
        _ffn_kernel,
        grid=(B, S // tm),
        in_specs=[tile, halo, tile, full(wup), full(cw), full(cb), full(wd), full(gfin)],
        out_specs=tile,
        out_shape=jax.ShapeDtypeStruct((B, S, D), F32),
        scratch_shapes=[pltpu.VMEM((tm, D_FF), BF16)],
        compiler_params=_params("parallel", "parallel"),
        name="conv_ffn",
    )(h2, h2, x1, wup, cw, cb, wd, gfin)


@jax.jit
def _layer(x, g_mix, w_in, pe_cmp_k, w_cmp_k1, w_cmp_k2, pe_cmp_v, w_cmp_v1, w_cmp_v2,
           w_proj_nsa, w_proj_dil, w_out, g_ffn, w_up, conv_w, conv_b, w_down, g_final):
    B, S, D = x.shape
    depth = g_mix.shape[0]
    for l in range(depth):
        gm = g_mix[l].reshape(1, D)
        kcmp, vcmp, ks, kw, d0, d1, d2, qt, vst, vwt, gates = _in_proj(x, gm, w_in[l])
        kc, _ = _compress(kcmp, pe_cmp_k[l], w_cmp_k1[l], w_cmp_k2[l])
        _, vct = _compress(vcmp, pe_cmp_v[l], w_cmp_v1[l], w_cmp_v2[l])
        o_a = _nsa(qt, kc, vct, ks, vst, kw, vwt, gates)
        outs, lses = _dilated((d0, d1, d2))
        merge_cols = w_in[l].T[w_in.shape[2] - 2 * D:]
        x1, h2 = _merge(x, o_a, outs, lses, gm, merge_cols, w_proj_nsa[l], w_proj_dil[l], w_out[l],
                        g_ffn[l].reshape(1, D))
        x = _ffn(h2, x1, w_up[l], conv_w[l], conv_b[l], w_down[l], g_final)
        assert depth == 1
    return x


def kernel(x, g_mix, w_in, pe_cmp_k, w_cmp_k1, w_cmp_k2, pe_cmp_v, w_cmp_v1, w_cmp_v2, w_proj_nsa, w_proj_dil, w_out, g_ffn, w_up, conv_w, conv_b, w_down, g_final):
    return _layer(x, g_mix, w_in, pe_cmp_k, w_cmp_k1, w_cmp_k2, pe_cmp_v, w_cmp_v1, w_cmp_v2,
                  w_proj_nsa, w_proj_dil, w_out, g_ffn, w_up, conv_w, conv_b, w_down, g_final)
```

```python
import functools
import math

import numpy as np
import jax
import jax.numpy as jnp
from jax import lax
from jax.experimental import pallas as pl
from jax.experimental.pallas import tpu as pltpu

HEAD_DIM = 64
NSA_HEADS = 8
NSA_GROUPS = 2
NSA_REP = NSA_HEADS // NSA_GROUPS
CMP_BLOCK = 32
CMP_STRIDE = 16
CMP_HIDDEN = 128
SLC_BLOCK = 64
SLC_TOP = 16
NSA_WINDOW = 512
FORCE_SCORE = 1.0e4
DIL_CONFIGS = ((128, 1), (512, 4), (2048, 16))
DIL_GROUPS = 3
DIL_HEADS_PER_GROUP = 4
DIL_HEADS = DIL_GROUPS * DIL_HEADS_PER_GROUP
D_FF = 2816
CONV_WIDTH = 3
RMS_EPS = 1e-6
NEG_INF = -1e30

LANES = 128
VMEM_LIMIT_BYTES = 56 * 1024 * 1024

F32 = jnp.float32
BF16 = jnp.bfloat16
NT_DIMS = (((1,), (1,)), ((), ()))


def _alibi_slopes(n):
    return [float(2.0 ** (-8.0 * i / n)) for i in range(1, n + 1)]


def _rms(xf, g):
    ms = jnp.mean(xf * xf, axis=-1, keepdims=True)
    return xf * lax.rsqrt(ms + RMS_EPS) * g


def _dot(a, b):
    return jnp.dot(a, b, preferred_element_type=F32)


def _dot_nt(a, b):
    return lax.dot_general(a, b, NT_DIMS, preferred_element_type=F32)


def _sigmoid(z):
    return 1.0 / (1.0 + jnp.exp(-z))


def _params(*sem):
    return pltpu.CompilerParams(dimension_semantics=sem, vmem_limit_bytes=VMEM_LIMIT_BYTES)


IN_TM = 512
N_KVC = 4 * HEAD_DIM
N_KSEL = NSA_GROUPS * HEAD_DIM
N_DIL = 3 * DIL_HEADS * HEAD_DIM
DIL_WIDTH = DIL_HEADS_PER_GROUP * HEAD_DIM
T_Q = NSA_HEADS * HEAD_DIM
T_V = NSA_GROUPS * HEAD_DIM
GATE_ROWS = 16
K_AUG = 2 * HEAD_DIM
SLC_SHIFT = int(math.log2(SLC_BLOCK))
AUG_NBLK = 8
AUG_HI, AUG_LO = AUG_NBLK, AUG_NBLK + 1
LOG2E = math.log2(math.e)
LN2 = math.log(2.0)
V_PAD = 16
V_ROWS = HEAD_DIM + V_PAD


def _key_position_columns(pos0, rows, step=1):
    pos = pos0 + step * lax.broadcasted_iota(jnp.int32, (rows, HEAD_DIM), 0)
    col = lax.broadcasted_iota(jnp.int32, (rows, HEAD_DIM), 1)
    blk = jnp.bitwise_and(lax.shift_right_logical(pos, SLC_SHIFT), AUG_NBLK - 1)
    hi = lax.shift_left(lax.shift_right_logical(pos, 7), 7).astype(F32)
    lo = jnp.bitwise_and(pos, 127).astype(F32)
    c = jnp.where((col == AUG_HI) | (col == AUG_HI + 2), hi,
                  jnp.where((col == AUG_LO) | (col == AUG_LO + 2), lo, 0.0))
    return jnp.where((col < AUG_NBLK) & (blk == col), 1.0, c)


def _in_proj_kernel(x_ref, g_ref, wn_ref, wt_ref,
                    kcmp_ref, vcmp_ref, ks_ref, kw_ref, d0_ref, d1_ref, d2_ref, qt_ref, vst_ref, vwt_ref, gate_ref,
                    slab_ref):
    tm = x_ref.shape[0]
    h = _rms(x_ref[...], g_ref[...]).astype(BF16)
    c0 = 0
    kvc = _dot_nt(h, wn_ref[c0:c0 + N_KVC, :])
    kcmp_ref[...] = kvc[:, :N_KSEL]
    vcmp_ref[...] = kvc[:, N_KSEL:]
    c0 += N_KVC
    ks = _dot_nt(h, wn_ref[c0:c0 + N_KSEL, :]).astype(BF16)
    c0 += N_KSEL
    kw = _dot_nt(h, wn_ref[c0:c0 + N_KSEL, :]).astype(BF16)
    c0 += N_KSEL
    aug = _key_position_columns(pl.program_id(1) * tm, tm).astype(BF16)
    for g in range(NSA_GROUPS):
        ks_ref[g] = jnp.concatenate([ks[:, g * HEAD_DIM:(g + 1) * HEAD_DIM], aug], axis=1)
        kw_ref[g] = jnp.concatenate([kw[:, g * HEAD_DIM:(g + 1) * HEAD_DIM], aug], axis=1)
    seg = DIL_HEADS * HEAD_DIM
    for which in range(3):
        y = _dot_nt(h, wn_ref[c0 + which * seg:c0 + (which + 1) * seg, :])
        for gi, (d_ref, (_, r)) in enumerate(zip((d0_ref, d1_ref, d2_ref), DIL_CONFIGS)):
            yg = y[:, gi * DIL_WIDTH:(gi + 1) * DIL_WIDTH]
            if r == 1:
                d_ref[which, 0] = yg.astype(BF16)
                continue
            for s in range(DIL_WIDTH // LANES):
                slab_ref[s] = yg[:, s * LANES:(s + 1) * LANES]
            for c in range(r):
                d_ref[which, c] = jnp.concatenate(
                    [slab_ref[s, pl.ds(c, tm // r, stride=r), :] for s in range(DIL_WIDTH // LANES)],
                    axis=1).astype(BF16)
    yt = _dot_nt(wt_ref[...], h)
    qt_ref[...] = yt[0:T_Q].astype(BF16)
    r0 = T_Q
    ones = jnp.where(lax.broadcasted_iota(jnp.int32, (V_PAD, tm), 0) == 0, 1.0, 0.0).astype(BF16)
    for ref in (vst_ref, vwt_ref):
        vt = yt[r0:r0 + T_V].astype(BF16)
        r0 += T_V
        for g in range(NSA_GROUPS):
            ref[g] = jnp.concatenate([vt[g * HEAD_DIM:(g + 1) * HEAD_DIM, :], ones], axis=0)
    gate_ref[...] = _sigmoid(yt[r0:r0 + NSA_GROUPS * GATE_ROWS])


def _in_proj(x, g_mix, w_in):
    B, S, D = x.shape
    scale = HEAD_DIM ** -0.5 * LOG2E
    o_q, o_kv = 0, T_Q
    o_gate = o_kv + 6 * N_KSEL
    o_dil = o_gate + 3 * NSA_HEADS
    o_merge = o_dil + N_DIL
    w_t = w_in.T
    kv = w_t[o_kv:o_gate]

    def kind(k):
        return kv[k * N_KSEL:(k + 1) * N_KSEL]

    dil = w_t[o_dil:o_merge]
    dil = jnp.concatenate([dil[:DIL_HEADS * HEAD_DIM] * scale, dil[DIL_HEADS * HEAD_DIM:]], axis=0)
    wn = jnp.concatenate([kind(0), kind(1), kind(2), kind(4), dil], axis=0).astype(BF16)
    wg = w_t[o_gate:o_dil].reshape(NSA_GROUPS, 3 * NSA_REP, D)
    wg = jnp.pad(wg, ((0, 0), (0, GATE_ROWS - 3 * NSA_REP), (0, 0))).reshape(NSA_GROUPS * GATE_ROWS, D)
    wt = jnp.concatenate([w_t[o_q:o_kv] * scale, kind(3), kind(5), wg], axis=0).astype(BF16)
    tm = IN_TM
    grid = (B, S // tm)
    full = lambda a: pl.BlockSpec(a.shape, lambda b, i: (0,) * a.ndim)
    k_shape = jax.ShapeDtypeStruct((B, NSA_GROUPS, S, K_AUG), BF16)
    v_shape = jax.ShapeDtypeStruct((B, NSA_GROUPS, V_ROWS, S), BF16)
    k_spec = pl.BlockSpec((None, NSA_GROUPS, tm, K_AUG), lambda b, i: (b, 0, i, 0))
    v_spec = pl.BlockSpec((None, NSA_GROUPS, V_ROWS, tm), lambda b, i: (b, 0, 0, i))
    c_shape = jax.ShapeDtypeStruct((B, S, N_KSEL), F32)
    c_spec = pl.BlockSpec((None, tm, N_KSEL), lambda b, i: (b, i, 0))
    d_shapes = tuple(jax.ShapeDtypeStruct((B, 3, r, S // r, DIL_WIDTH), BF16) for _, r in DIL_CONFIGS)
    d_specs = tuple(pl.BlockSpec((None, 3, r, tm // r, DIL_WIDTH), lambda b, i: (b, 0, 0, i, 0))
                    for _, r in DIL_CONFIGS)
    out_shape = (
        c_shape, c_shape, k_shape, k_shape, *d_shapes,
        jax.ShapeDtypeStruct((B, T_Q, S), BF16),
        v_shape,
        v_shape,
        jax.ShapeDtypeStruct((B, NSA_GROUPS * GATE_ROWS, S), F32),
    )
    out_specs = (
        c_spec, c_spec, k_spec, k_spec, *d_specs,
        pl.BlockSpec((None, T_Q, tm), lambda b, i: (b, 0, i)),
        v_spec,
        v_spec,
        pl.BlockSpec((None, NSA_GROUPS * GATE_ROWS, tm), lambda b, i: (b, 0, i)),
    )
    return pl.pallas_call(
        _in_proj_kernel,
        grid=grid,
        in_specs=[pl.BlockSpec((None, tm, D), lambda b, i: (b, i, 0)), full(g_mix), full(wn), full(wt)],
        out_specs=out_specs,
        out_shape=out_shape,
        scratch_shapes=[pltpu.VMEM((DIL_WIDTH // LANES, tm, LANES), F32)],
        compiler_params=_params("parallel", "parallel"),
        name="in_proj",
    )(x, g_mix, wn, wt)


def _compress_kernel(x_ref, pe_ref, w1_ref, w2_ref, w2t_ref, c_ref, ct_ref):
    nch = x_ref.shape[0] // CMP_STRIDE
    ulo = jnp.zeros((nch, NSA_GROUPS * CMP_HIDDEN), F32)
    uhi = jnp.zeros((nch, NSA_GROUPS * CMP_HIDDEN), F32)
    for j in range(CMP_STRIDE):
        xj = x_ref[pl.ds(j, nch, stride=CMP_STRIDE), :]
        ulo = ulo + _dot((xj + pe_ref[j:j + 1, :]).astype(BF16), w1_ref[j])
        uhi = uhi + _dot((xj + pe_ref[CMP_STRIDE + j:CMP_STRIDE + j + 1, :]).astype(BF16), w1_ref[CMP_STRIDE + j])
    pre = ulo + pltpu.roll(uhi, nch - 1, 0)
    hid = jax.nn.gelu(pre).astype(BF16)
    aug = _key_position_columns(CMP_BLOCK - 1, nch, CMP_STRIDE).astype(BF16)
    for g in range(NSA_GROUPS):
        hg = hid[:, g * CMP_HIDDEN:(g + 1) * CMP_HIDDEN]
        c_ref[g] = jnp.concatenate([_dot(hg, w2_ref[...]).astype(BF16), aug], axis=1)
        ct_ref[g] = _dot_nt(w2t_ref[...], hg).astype(BF16)


def _compress(xc, pe, w1, w2):
    B, S, width = xc.shape
    nch = S // CMP_STRIDE
    G, dh, hid = NSA_GROUPS, HEAD_DIM, CMP_HIDDEN
    pe_t = jnp.broadcast_to(pe.reshape(CMP_BLOCK, 1, dh), (CMP_BLOCK, G, dh)).reshape(CMP_BLOCK, width)
    eye = jnp.eye(G, dtype=w1.dtype)
    wexp = jnp.einsum('pdn,ge->pgden', w1.reshape(CMP_BLOCK, dh, hid), eye).reshape(CMP_BLOCK, width, G * hid)
    wexp = wexp.astype(BF16)
    full = lambda a: pl.BlockSpec(a.shape, lambda b: (0,) * a.ndim)
    w2b = w2.astype(BF16)
    w2t = w2.T.astype(BF16)
    return pl.pallas_call(
        _compress_kernel,
        grid=(B,),
        in_specs=[pl.BlockSpec((None, S, width), lambda b: (b, 0, 0)), full(pe_t), full(wexp), full(w2b), full(w2t)],
        out_specs=(pl.BlockSpec((None, G, nch, K_AUG), lambda b: (b, 0, 0, 0)),
                   pl.BlockSpec((None, G, dh, nch), lambda b: (b, 0, 0, 0))),
        out_shape=(jax.ShapeDtypeStruct((B, G, nch, K_AUG), BF16),
                   jax.ShapeDtypeStruct((B, G, dh, nch), BF16)),
        compiler_params=_params("parallel"),
        name="compress",
    )(xc, pe_t, wexp, w2b, w2t)


NSA_TQ = 256
SWEEP_TK = AUG_NBLK * SLC_BLOCK
SEL_SUB = 128
SEL_AHEAD = 4


def _rowmax8(s):
    return jnp.max(s.reshape(s.shape[0] // 8, 8, s.shape[1]), axis=0)


def _online_softmax(chunks, scores, values):
    pending = [scores(ch) for ch in chunks[:SEL_AHEAD]]
    m = acc = None
    for n, ch in enumerate(chunks):
        if n + SEL_AHEAD < len(chunks):
            pending.append(scores(chunks[n + SEL_AHEAD]))
        s = pending.pop(0)
        m_c = jnp.max(_rowmax8(s), axis=0, keepdims=True)
        if m is None:
            m = m_c
            acc = _dot(values(ch), jnp.exp2(s - m).astype(BF16))
        else:
            m_new = jnp.maximum(m, m_c)
            acc = jnp.exp2(m - m_new) * acc + _dot(values(ch), jnp.exp2(s - m_new).astype(BF16))
            m = m_new
    return acc


def _nsa_kernel(qt_ref, kc_ref, vct_ref, ks_ref, vst_ref, kw_ref, vwt_ref, gate_ref, ov_ref,
                o_ref, sbt_ref, osel_ref, *, slopes):
    g = pl.program_id(1)
    i = pl.program_id(2)
    R, dh, tq, tk = NSA_REP, HEAD_DIM, NSA_TQ, SWEEP_TK
    L = R * tq
    t0 = i * tq
    nc = kc_ref.shape[0]
    ns = ov_ref.shape[0]
    n_tiles = ns // AUG_NBLK
    tile4 = lambda a: jnp.concatenate([a] * R, axis=1)

    qt = qt_ref[...]
    qs = jnp.concatenate([qt[r * dh:(r + 1) * dh, :] for r in range(R)], axis=1)
    slope = [jnp.where(g == 0, slopes[r], slopes[R + r]).astype(F32) for r in range(R)]
    slope_row = jnp.concatenate([jnp.full((1, tq), 1.0, F32) * slope[r] for r in range(R)], axis=1)
    t_row = t0 + lax.broadcasted_iota(jnp.int32, (1, tq), 1)

    r8 = lax.broadcasted_iota(jnp.int32, (AUG_NBLK, L), 0)
    s_full = slope_row * LOG2E
    s_hi = s_full.astype(BF16).astype(F32)
    alibi8 = jnp.where(r8 < 2, s_hi, jnp.where(r8 < 4, s_full - s_hi, 0.0))
    q_pad = jnp.zeros((K_AUG - dh - 2 * AUG_NBLK, L), BF16)
    q_plain = jnp.concatenate([qs, jnp.concatenate([jnp.zeros((AUG_NBLK, L), F32), alibi8], axis=0).astype(BF16),
                               q_pad], axis=0)

    cmp_end = lax.broadcasted_iota(jnp.int32, (nc, tq), 0) * CMP_STRIDE + (CMP_BLOCK - 1)
    m_cmp = t_row >= cmp_end
    sc = _dot(kc_ref[...], q_plain) + tile4(jnp.where(m_cmp, 0.0, NEG_INF))

    a0 = jnp.maximum(t0 - NSA_WINDOW, 0)
    kpos_w = lax.broadcasted_iota(jnp.int32, (SEL_SUB, tq), 0)
    t_w = t0 + lax.broadcasted_iota(jnp.int32, (SEL_SUB, tq), 1)

    def win_scores(ch):
        diag, r = ch
        k0 = pl.multiple_of((t0 if diag else a0) + r, SEL_SUB)
        kpos = k0 + kpos_w
        valid = (kpos <= t_w) if diag else ((kpos < t0) & (t_w - kpos < NSA_WINDOW))
        return _dot(kw_ref[pl.ds(k0, SEL_SUB), :], q_plain) + tile4(jnp.where(valid, 0.0, NEG_INF))

    def win_values(ch):
        diag, r = ch
        return vwt_ref[:, pl.ds(pl.multiple_of((t0 if diag else a0) + r, SEL_SUB), SEL_SUB)]

    win_chunks = ([(True, r) for r in range(0, tq, SEL_SUB)]
                  + [(False, r) for r in range(0, NSA_WINDOW, SEL_SUB)])
    acc_w = _online_softmax(win_chunks, win_scores, win_values)
    o_win = acc_w[:dh] * (1.0 / acc_w[dh:dh + 1])

    mx = jnp.max(_rowmax8(sc), axis=0, keepdims=True)
    p = jnp.exp2(sc - mx) * tile4(jnp.where(m_cmp, 1.0, 0.0))
    den = jnp.sum(jnp.sum(p.reshape(nc // 8, 8, L), axis=0), axis=0, keepdims=True)
    pr = p * (1.0 / jnp.maximum(den, 1e-30))
    o_cmp = _dot(vct_ref[...], pr.astype(BF16))
    psum = pr[:, 0:tq]
    for r in range(1, R):
        psum = psum + pr[:, r * tq:(r + 1) * tq]

    p_hi = psum.astype(BF16)
    p_lo = (psum - p_hi.astype(F32)).astype(BF16)
    imp = _dot(ov_ref[...], p_hi) + _dot(ov_ref[...], p_lo)
    blk = lax.broadcasted_iota(jnp.int32, (ns, tq), 0)
    cur = lax.shift_right_logical(t_row, SLC_SHIFT)
    val = jnp.where((blk == cur) | (blk == 0), FORCE_SCORE, imp)
    val = jnp.where(blk <= cur, val, -1.0)
    vals = [val[8 * v:8 * v + 8, :] for v in range(ns // 8)]
    ranks = [jnp.zeros((8, tq), F32) for _ in vals]
    row8 = lax.broadcasted_iota(jnp.int32, (8, tq), 0)
    for j in range(ns):
        vj = jnp.broadcast_to(val[j:j + 1, :], (8, tq))
        for v in range(len(vals)):
            if 8 * v > j:
                ahead = vj >= vals[v]
            elif 8 * v + 7 <= j:
                ahead = vj > vals[v]
            else:
                ahead = (vj > vals[v]) | ((vj == vals[v]) & (row8 > j - 8 * v))
            ranks[v] = ranks[v] + jnp.where(ahead, 1.0, 0.0)
    for T in range(n_tiles):
        selb = jnp.where((ranks[T] < float(SLC_TOP)) & (vals[T] >= 0.0), 0.0, NEG_INF)
        sbt_ref[T] = jnp.concatenate([tile4(selb), alibi8], axis=0).astype(BF16)

    td = lax.div(t0, tk)
    for c in range(n_tiles):
        @pl.when(td == c)
        def _(c=c):
            kpos = c * tk + lax.broadcasted_iota(jnp.int32, (tk, tq), 0)
            t_q = t0 + lax.broadcasted_iota(jnp.int32, (tk, tq), 1)
            causal = tile4(jnp.where(kpos <= t_q, 0.0, NEG_INF))
            q_augs = [jnp.concatenate([qs, sbt_ref[T], q_pad], axis=0) for T in range(c + 1)]
            chunks = [(T, j * SEL_SUB) for T in range(c + 1) for j in range(tk // SEL_SUB)]

            def chunk_scores(T, r):
                s = _dot(ks_ref[T * tk + r:T * tk + r + SEL_SUB, :], q_augs[T])
                return s + causal[r:r + SEL_SUB] if T == c else s

            acc = _online_softmax(chunks, lambda ch: chunk_scores(*ch),
                                  lambda ch: vst_ref[:, ch[0] * tk + ch[1]:ch[0] * tk + ch[1] + SEL_SUB])
            osel_ref[...] = acc[:dh] * (1.0 / acc[dh:dh + 1])

    o_sel = osel_ref[...]
    gates = gate_ref[...]
    tiles = []
    for r in range(R):
        sl = slice(r * tq, (r + 1) * tq)
        tiles.append(gates[3 * r:3 * r + 1, :] * o_cmp[:, sl]
                     + gates[3 * r + 1:3 * r + 2, :] * o_sel[:, sl]
                     + gates[3 * r + 2:3 * r + 3, :] * o_win[:, sl])
    o_ref[...] = jnp.concatenate(tiles, axis=0).T.astype(o_ref.dtype)


def _overlap_matrix(nc, ns):
    cs = np.arange(nc)[None, :] * CMP_STRIDE
    ss = np.arange(ns)[:, None] * SLC_BLOCK
    ov = np.clip(np.minimum(cs + CMP_BLOCK, ss + SLC_BLOCK) - np.maximum(cs, ss), 0, None)
    return jnp.asarray(ov.astype(np.float32) / CMP_BLOCK, dtype=BF16)


def _nsa(qt, kc, vct, ks, vst, kw, vwt, gates):
    B, _, S = qt.shape
    G, R, dh, tq = NSA_GROUPS, NSA_REP, HEAD_DIM, NSA_TQ
    nc = kc.shape[2]
    ns = S // SLC_BLOCK
    ov = _overlap_matrix(nc, ns)
    kern = functools.partial(_nsa_kernel, slopes=tuple(_alibi_slopes(NSA_HEADS)))
    assert S % SWEEP_TK == 0 and S >= NSA_WINDOW + tq
    per_bg = lambda shape: pl.BlockSpec((None, None) + shape, lambda b, g, i: (b, g, 0, 0))
    return pl.pallas_call(
        kern,
        grid=(B, G, S // tq),
        in_specs=[
            pl.BlockSpec((None, R * dh, tq), lambda b, g, i: (b, g, i)),
            per_bg((nc, K_AUG)), per_bg((dh, nc)),
            per_bg((S, K_AUG)), per_bg((V_ROWS, S)),
            per_bg((S, K_AUG)), per_bg((V_ROWS, S)),
            pl.BlockSpec((None, GATE_ROWS, tq), lambda b, g, i: (b, g, i)),
            pl.BlockSpec(ov.shape, lambda b, g, i: (0, 0)),
        ],
        out_specs=pl.BlockSpec((None, tq, R * dh), lambda b, g, i: (b, i, g)),
        out_shape=jax.ShapeDtypeStruct((B, S, G * R * dh), BF16),
        scratch_shapes=[pltpu.VMEM((S // SWEEP_TK, 2 * AUG_NBLK, R * tq), BF16),
                        pltpu.VMEM((dh, R * tq), F32)],
        compiler_params=_params("parallel", "parallel", "arbitrary"),
        name="nsa_attention",
    )(qt, kc, vct, ks, vst, kw, vwt, gates, ov)


BAND_TQ = 128
BAND_SUB = 8
BAND_PROBLEMS = 8
BAND = 128


def _banded_kernel(q_ref, k_ref, v_ref, o_ref, lse_ref, *, slopes, nsub, ncls):
    tq, dh, nh = BAND_TQ, HEAD_DIM, DIL_HEADS_PER_GROUP
    tk = tq + BAND
    width = nh * dh
    head_of_lane = lambda rows: lax.shift_right_logical(
        lax.broadcasted_iota(jnp.int32, (rows, width), 1), int(math.log2(dh)))
    lane_head, q_head = head_of_lane(tk), head_of_lane(tq)
    keep = [jnp.where(lane_head == h, 1.0, 0.0).astype(BF16) for h in range(nh)]

    def per_head(a):
        return jnp.concatenate([a * keep[h] for h in range(nh)], axis=0)

    ones_h = jnp.concatenate(keep, axis=0)
    def biases(first_key_offset):
        d = first_key_offset + (lax.broadcasted_iota(jnp.int32, (tq, tk), 0)
                                - lax.broadcasted_iota(jnp.int32, (tq, tk), 1))
        mask_bias = jnp.where((d >= 0) & (d <= BAND), 0.0, NEG_INF)
        neg_d = -d.astype(F32)
        return [slopes[h] * neg_d + mask_bias for h in range(nh)]

    subs = []
    for cls in range(ncls):
        for sub in range(nsub):
            i = pl.program_id(2) * nsub + sub
            k0 = pl.multiple_of(jnp.maximum(i - 1, 0) * tq, tq)
            q = q_ref[cls, sub * tq:(sub + 1) * tq, :]
            scores = _dot_nt(q, per_head(k_ref[cls, pl.ds(k0, tk), :]))
            subs.append((cls, sub, i * tq - k0, k0, scores))
    first_bias = biases(subs[0][2])
    inner_bias = biases(tq) if nsub > 1 else None
    probs = []
    for cls, sub, off, k0, s in subs:
        bias = first_bias if sub == 0 else inner_bias
        ps, mxs = [], []
        for h in range(nh):
            sh = s[:, h * tk:(h + 1) * tk] + bias[h]
            mx = jnp.max(sh, axis=-1, keepdims=True)
            ps.append(jnp.exp2(sh - mx).astype(BF16))
            mxs.append(mx)
        probs.append((cls, sub, k0, jnp.concatenate(ps, axis=1), mxs))
    for cls, sub, k0, p, mxs in probs:
        den = _dot(p, ones_h)
        mx_all = mxs[nh - 1]
        for h in range(nh - 2, -1, -1):
            mx_all = jnp.where(q_head == h, mxs[h], mx_all)
        rows = slice(sub * tq, (sub + 1) * tq)
        o_ref[cls, rows, :] = _dot(p, per_head(v_ref[cls, pl.ds(k0, tk), :])) * (1.0 / den)
        lse_ref[cls, rows, :] = mx_all * LN2 + jnp.log(den)


def _banded(d, slopes):
    B, _, r, n, width = d.shape
    nsub = min(BAND_SUB, n // BAND_TQ)
    ncls = min(r, BAND_PROBLEMS // nsub)
    tq = BAND_TQ * nsub
    assert n >= BAND_TQ + BAND and n % tq == 0 and r % ncls == 0
    kern = functools.partial(_banded_kernel, slopes=tuple(slopes), nsub=nsub, ncls=ncls)
    tile = pl.BlockSpec((None, None, ncls, tq, width), lambda b, c, i: (b, 0, c, i, 0))
    kseq = pl.BlockSpec((None, None, ncls, n, width), lambda b, c, i: (b, 1, c, 0, 0))
    vseq = pl.BlockSpec((None, None, ncls, n, width), lambda b, c, i: (b, 2, c, 0, 0))
    out = pl.BlockSpec((None, ncls, tq, width), lambda b, c, i: (b, c, i, 0))
    return pl.pallas_call(
        kern,
        grid=(B, r // ncls, n // tq),
        in_specs=[tile, kseq, vseq],
        out_specs=(out, out),
        out_shape=(jax.ShapeDtypeStruct((B, r, n, width), F32),) * 2,
        compiler_params=_params("parallel", "parallel", "arbitrary"),
        name="banded_attention",
    )(d, d, d)


def _dilated(dils):
    slopes = _alibi_slopes(DIL_HEADS)
    outs, lses = [], []
    for gi, (w, r) in enumerate(DIL_CONFIGS):
        assert w // r == BAND
        sl = [s_ * r * LOG2E for s_ in slopes[gi * DIL_HEADS_PER_GROUP:(gi + 1) * DIL_HEADS_PER_GROUP]]
        o, lse = _banded(dils[gi], sl)
        outs.append(o)
        lses.append(lse)
    return outs, lses


MERGE_TM = 512


def _token_order(ref, slab_ref):
    r, rows, width = ref.shape
    if r == 1:
        return ref[0]
    for c in range(r):
        blk = ref[c]
        for s in range(width // LANES):
            slab_ref[s, pl.ds(c, rows, stride=r), :] = blk[:, s * LANES:(s + 1) * LANES]
    return jnp.concatenate([slab_ref[s] for s in range(width // LANES)], axis=1)


def _merge_kernel(x_ref, oa_ref, o0_ref, o1_ref, o2_ref, l0_ref, l1_ref, l2_ref,
                  gmix_ref, wm_ref, wpn_ref, wpd_ref, wo_ref, gffn_ref, x1_ref, h2_ref, slab_ref):
    x = x_ref[...]
    D = x.shape[1]
    h = _rms(x, gmix_ref[...]).astype(BF16)
    gm = _sigmoid(_dot_nt(h, wm_ref[...]))
    o0, o1, o2 = [_token_order(r_, slab_ref) for r_ in (o0_ref, o1_ref, o2_ref)]
    l0, l1, l2 = [_token_order(r_, slab_ref) for r_ in (l0_ref, l1_ref, l2_ref)]
    mx = jnp.maximum(jnp.maximum(l0, l1), l2)
    e0, e1, e2 = jnp.exp(l0 - mx), jnp.exp(l1 - mx), jnp.exp(l2 - mx)
    inv = 1.0 / (e0 + e1 + e2)
    ob = o0 * (e0 * inv) + o1 * (e1 * inv) + o2 * (e2 * inv)
    a = _dot(oa_ref[...], wpn_ref[...])
    d = _dot(ob.astype(BF16), wpd_ref[...])
    mixed = gm[:, :D] * a + gm[:, D:] * d
    x1 = x + _dot(mixed.astype(BF16), wo_ref[...])
    x1_ref[...] = x1
    h2_ref[...] = _rms(x1, gffn_ref[...]).astype(BF16)


def _merge(x, o_a, outs, lses, g_mix, w_merge, w_proj_nsa, w_proj_dil, w_out, g_ffn):
    B, S, D = x.shape
    tm = MERGE_TM
    row = lambda a: pl.BlockSpec((None, tm, a.shape[2]), lambda b, i: (b, i, 0))
    cls = lambda a: pl.BlockSpec((None, a.shape[1], tm // a.shape[1], a.shape[3]), lambda b, i: (b, 0, i, 0))
    full = lambda a: pl.BlockSpec(a.shape, lambda b, i: (0,) * a.ndim)
    ws = [w_merge.astype(BF16), w_proj_nsa.astype(BF16), w_proj_dil.astype(BF16), w_out.astype(BF16)]
    consts = [g_mix, *ws, g_ffn]
    in_specs = [row(x), row(o_a)] + [cls(a) for a in (*outs, *lses)] + [full(a) for a in consts]
    return pl.pallas_call(
        _merge_kernel,
        grid=(B, S // tm),
        in_specs=in_specs,
        out_specs=(pl.BlockSpec((None, tm, D), lambda b, i: (b, i, 0)),) * 2,
        out_shape=(jax.ShapeDtypeStruct((B, S, D), F32), jax.ShapeDtypeStruct((B, S, D), BF16)),
        scratch_shapes=[pltpu.VMEM((DIL_WIDTH // LANES, tm, LANES), F32)],
        compiler_params=_params("parallel", "parallel"),
        name="merge_proj",
    )(x, o_a, *outs, *lses, *consts)


FFN_TM = 512
FFN_TN = 256
HALO = 16


def _ffn_kernel(h_ref, halo_ref, x1_ref, wup_ref, cw_ref, cb_ref, wd_ref, gfin_ref, o_ref, act_ref):
    i = pl.program_id(1)
    h = h_ref[...]
    halo = halo_ref[...]
    tm = h.shape[0]
    row = lax.broadcasted_iota(jnp.int32, (tm, FFN_TN), 0)
    live = (i > 0).astype(F32)
    for j in range(D_FF // FFN_TN):
        cols = slice(j * FFN_TN, (j + 1) * FFN_TN)
        wu = wup_ref[:, cols]
        u = _dot(h, wu)
        uh = _dot(halo, wu) * live
        gate = _dot(h, wup_ref[:, D_FF + j * FFN_TN:D_FF + (j + 1) * FFN_TN])
        p1 = jnp.broadcast_to(uh[HALO - 1:HALO, :], (tm, FFN_TN))
        p2 = jnp.broadcast_to(uh[HALO - 2:HALO - 1, :], (tm, FFN_TN))
        u1 = jnp.where(row == 0, p1, pltpu.roll(u, 1, 0))
        u2 = jnp.where(row == 0, p2, jnp.where(row == 1, p1, pltpu.roll(u, 2, 0)))
        uc = cb_ref[:, cols] + cw_ref[0:1, cols] * u2
        uc = uc + cw_ref[1:2, cols] * u1
        uc = uc + cw_ref[2:3, cols] * u
        act_ref[:, j * FFN_TN:(j + 1) * FFN_TN] = (jax.nn.gelu(uc) * gate).astype(BF16)
    y = _dot(act_ref[...], wd_ref[...])
    o_ref[...] = _rms(x1_ref[...] + y, gfin_ref[...])


def _ffn(h2, x1, w_up, conv_w, conv_b, w_down, g_final):
    B, S, D = h2.shape
    tm = FFN_TM
    assert D_FF % FFN_TN == 0
    wup = w_up.astype(BF16)
    cw = conv_w
    cb = conv_b.reshape(1, D_FF)
    wd = w_down.astype(BF16)
    gfin = g_final.reshape(1, D)
    full = lambda a: pl.BlockSpec(a.shape, lambda b, i: (0,) * a.ndim, pipeline_mode=pl.Buffered(1))
    tile = pl.BlockSpec((None, tm, D), lambda b, i: (b, i, 0))
    halo = pl.BlockSpec((None, HALO, D), lambda b, i: (b, jnp.maximum(i * (tm // HALO) - 1, 0), 0))
    return pl.pallas_call(
        _ffn_kernel,
        grid=(B, S // tm),
        in_specs=[tile, halo, tile, full(wup), full(cw), full(cb), full(wd), full(gfin)],
        out_specs=tile,
        out_shape=jax.ShapeDtypeStruct((B, S, D), F32),
        scratch_shapes=[pltpu.VMEM((tm, D_FF), BF16)],
        compiler_params=_params("parallel", "parallel"),
        name="conv_ffn",
    )(h2, h2, x1, wup, cw, cb, wd, gfin)


@jax.jit
def _layer(x, g_mix, w_in, pe_cmp_k, w_cmp_k1, w_cmp_k2, pe_cmp_v, w_cmp_v1, w_cmp_v2,
           w_proj_nsa, w_proj_dil, w_out, g_ffn, w_up, conv_w, conv_b, w_down, g_final):
    B, S, D = x.shape
    depth = g_mix.shape[0]
    for l in range(depth):
        gm = g_mix[l].reshape(1, D)
        kcmp, vcmp, ks, kw, d0, d1, d2, qt, vst, vwt, gates = _in_proj(x, gm, w_in[l])
        kc, _ = _compress(kcmp, pe_cmp_k[l], w_cmp_k1[l], w_cmp_k2[l])
        _, vct = _compress(vcmp, pe_cmp_v[l], w_cmp_v1[l], w_cmp_v2[l])
        o_a = _nsa(qt, kc, vct, ks, vst, kw, vwt, gates)
        outs, lses = _dilated((d0, d1, d2))
        merge_cols = w_in[l].T[w_in.shape[2] - 2 * D:]
        x1, h2 = _merge(x, o_a, outs, lses, gm, merge_cols, w_proj_nsa[l], w_proj_dil[l], w_out[l],
                        g_ffn[l].reshape(1, D))
        x = _ffn(h2, x1, w_up[l], conv_w[l], conv_b[l], w_down[l], g_final)
        assert depth == 1
    return x


def kernel(x, g_mix, w_in, pe_cmp_k, w_cmp_k1, w_cmp_k2, pe_cmp_v, w_cmp_v1, w_cmp_v2, w_proj_nsa, w_proj_dil, w_out, g_ffn, w_up, conv_w, conv_b, w_down, g_final):
    return _layer(x, g_mix, w_in, pe_cmp_k, w_cmp_k1, w_cmp_k2, pe_cmp_v, w_cmp_v1, w_cmp_v2,
                  w_proj_nsa, w_proj_dil, w_out, g_ffn, w_up, conv_w, conv_b, w_down, g_final)
```

```python
import functools
import math

import numpy as np
import jax
import jax.numpy as jnp
from jax import lax
from jax.experimental import pallas as pl
from jax.experimental.pallas import tpu as pltpu

HEAD_DIM = 64
NSA_HEADS = 8
NSA_GROUPS = 2
NSA_REP = NSA_HEADS // NSA_GROUPS
CMP_BLOCK = 32
CMP_STRIDE = 16
CMP_HIDDEN = 128
SLC_BLOCK = 64
SLC_TOP = 16
NSA_WINDOW = 512
FORCE_SCORE = 1.0e4
DIL_CONFIGS = ((128, 1), (512, 4), (2048, 16))
DIL_GROUPS = 3
DIL_HEADS_PER_GROUP = 4
DIL_HEADS = DIL_GROUPS * DIL_HEADS_PER_GROUP
D_FF = 2816
CONV_WIDTH = 3
RMS_EPS = 1e-6
NEG_INF = -1e30

LANES = 128
VMEM_LIMIT_BYTES = 56 * 1024 * 1024

F32 = jnp.float32
BF16 = jnp.bfloat16
NT_DIMS = (((1,), (1,)), ((), ()))


def _alibi_slopes(n):
    return [float(2.0 ** (-8.0 * i / n)) for i in range(1, n + 1)]


def _rms(xf, g):
    ms = jnp.mean(xf * xf, axis=-1, keepdims=True)
    return xf * lax.rsqrt(ms + RMS_EPS) * g


def _dot(a, b):
    return jnp.dot(a, b, preferred_element_type=F32)


def _dot_nt(a, b):
    return lax.dot_general(a, b, NT_DIMS, preferred_element_type=F32)


def _sigmoid(z):
    return 1.0 / (1.0 + jnp.exp(-z))


def _params(*sem):
    return pltpu.CompilerParams(dimension_semantics=sem, vmem_limit_bytes=VMEM_LIMIT_BYTES)


IN_TM = 512
N_KVC = 4 * HEAD_DIM
N_KSEL = NSA_GROUPS * HEAD_DIM
N_DIL = 3 * DIL_HEADS * HEAD_DIM
DIL_WIDTH = DIL_HEADS_PER_GROUP * HEAD_DIM
T_Q = NSA_HEADS * HEAD_DIM
T_V = NSA_GROUPS * HEAD_DIM
GATE_ROWS = 16
K_AUG = 2 * HEAD_DIM
SLC_SHIFT = int(math.log2(SLC_BLOCK))
AUG_NBLK = 8
AUG_HI, AUG_LO = AUG_NBLK, AUG_NBLK + 1
LOG2E = math.log2(math.e)
LN2 = math.log(2.0)
V_PAD = 16
V_ROWS = HEAD_DIM + V_PAD


def _key_position_columns(pos0, rows, step=1):
    pos = pos0 + step * lax.broadcasted_iota(jnp.int32, (rows, HEAD_DIM), 0)
    col = lax.broadcasted_iota(jnp.int32, (rows, HEAD_DIM), 1)
    blk = jnp.bitwise_and(lax.shift_right_logical(pos, SLC_SHIFT), AUG_NBLK - 1)
    hi = lax.shift_left(lax.shift_right_logical(pos, 7), 7).astype(F32)
    lo = jnp.bitwise_and(pos, 127).astype(F32)
    c = jnp.where((col == AUG_HI) | (col == AUG_HI + 2), hi,
                  jnp.where((col == AUG_LO) | (col == AUG_LO + 2), lo, 0.0))
    return jnp.where((col < AUG_NBLK) & (blk == col), 1.0, c)


def _in_proj_kernel(x_ref, g_ref, wn_ref, wt_ref,
                    kcmp_ref, vcmp_ref, ks_ref, kw_ref, d0_ref, d1_ref, d2_ref, qt_ref, vst_ref, vwt_ref, gate_ref,
                    slab_ref):
    tm = x_ref.shape[0]
    h = _rms(x_ref[...], g_ref[...]).astype(BF16)
    c0 = 0
    kvc = _dot_nt(h, wn_ref[c0:c0 + N_KVC, :])
    kcmp_ref[...] = kvc[:, :N_KSEL]
    vcmp_ref[...] = kvc[:, N_KSEL:]
    c0 += N_KVC
    ks = _dot_nt(h, wn_ref[c0:c0 + N_KSEL, :]).astype(BF16)
    c0 += N_KSEL
    kw = _dot_nt(h, wn_ref[c0:c0 + N_KSEL, :]).astype(BF16)
    c0 += N_KSEL
    aug = _key_position_columns(pl.program_id(1) * tm, tm).astype(BF16)
    for g in range(NSA_GROUPS):
        ks_ref[g] = jnp.concatenate([ks[:, g * HEAD_DIM:(g + 1) * HEAD_DIM], aug], axis=1)
        kw_ref[g] = jnp.concatenate([kw[:, g * HEAD_DIM:(g + 1) * HEAD_DIM], aug], axis=1)
    seg = DIL_HEADS * HEAD_DIM
    for which in range(3):
        y = _dot_nt(h, wn_ref[c0 + which * seg:c0 + (which + 1) * seg, :])
        for gi, (d_ref, (_, r)) in enumerate(zip((d0_ref, d1_ref, d2_ref), DIL_CONFIGS)):
            yg = y[:, gi * DIL_WIDTH:(gi + 1) * DIL_WIDTH]
            if r == 1:
                d_ref[which, 0] = yg.astype(BF16)
                continue
            for s in range(DIL_WIDTH // LANES):
                slab_ref[s] = yg[:, s * LANES:(s + 1) * LANES]
            for c in range(r):
                d_ref[which, c] = jnp.concatenate(
                    [slab_ref[s, pl.ds(c, tm // r, stride=r), :] for s in range(DIL_WIDTH // LANES)],
                    axis=1).astype(BF16)
    yt = _dot_nt(wt_ref[...], h)
    qt_ref[...] = yt[0:T_Q].astype(BF16)
    r0 = T_Q
    ones = jnp.where(lax.broadcasted_iota(jnp.int32, (V_PAD, tm), 0) == 0, 1.0, 0.0).astype(BF16)
    for ref in (vst_ref, vwt_ref):
        vt = yt[r0:r0 + T_V].astype(BF16)
        r0 += T_V
        for g in range(NSA_GROUPS):
            ref[g] = jnp.concatenate([vt[g * HEAD_DIM:(g + 1) * HEAD_DIM, :], ones], axis=0)
    gate_ref[...] = _sigmoid(yt[r0:r0 + NSA_GROUPS * GATE_ROWS])


def _in_proj(x, g_mix, w_in):
    B, S, D = x.shape
    scale = HEAD_DIM ** -0.5 * LOG2E
    o_q, o_kv = 0, T_Q
    o_gate = o_kv + 6 * N_KSEL
    o_dil = o_gate + 3 * NSA_HEADS
    o_merge = o_dil + N_DIL
    w_t = w_in.T
    kv = w_t[o_kv:o_gate]

    def kind(k):
        return kv[k * N_KSEL:(k + 1) * N_KSEL]

    dil = w_t[o_dil:o_merge]
    dil = jnp.concatenate([dil[:DIL_HEADS * HEAD_DIM] * scale, dil[DIL_HEADS * HEAD_DIM:]], axis=0)
    wn = jnp.concatenate([kind(0), kind(1), kind(2), kind(4), dil], axis=0).astype(BF16)
    wg = w_t[o_gate:o_dil].reshape(NSA_GROUPS, 3 * NSA_REP, D)
    wg = jnp.pad(wg, ((0, 0), (0, GATE_ROWS - 3 * NSA_REP), (0, 0))).reshape(NSA_GROUPS * GATE_ROWS, D)
    wt = jnp.concatenate([w_t[o_q:o_kv] * scale, kind(3), kind(5), wg], axis=0).astype(BF16)
    tm = IN_TM
    grid = (B, S // tm)
    full = lambda a: pl.BlockSpec(a.shape, lambda b, i: (0,) * a.ndim)
    k_shape = jax.ShapeDtypeStruct((B, NSA_GROUPS, S, K_AUG), BF16)
    v_shape = jax.ShapeDtypeStruct((B, NSA_GROUPS, V_ROWS, S), BF16)
    k_spec = pl.BlockSpec((None, NSA_GROUPS, tm, K_AUG), lambda b, i: (b, 0, i, 0))
    v_spec = pl.BlockSpec((None, NSA_GROUPS, V_ROWS, tm), lambda b, i: (b, 0, 0, i))
    c_shape = jax.ShapeDtypeStruct((B, S, N_KSEL), F32)
    c_spec = pl.BlockSpec((None, tm, N_KSEL), lambda b, i: (b, i, 0))
    d_shapes = tuple(jax.ShapeDtypeStruct((B, 3, r, S // r, DIL_WIDTH), BF16) for _, r in DIL_CONFIGS)
    d_specs = tuple(pl.BlockSpec((None, 3, r, tm // r, DIL_WIDTH), lambda b, i: (b, 0, 0, i, 0))
                    for _, r in DIL_CONFIGS)
    out_shape = (
        c_shape, c_shape, k_shape, k_shape, *d_shapes,
        jax.ShapeDtypeStruct((B, T_Q, S), BF16),
        v_shape,
        v_shape,
        jax.ShapeDtypeStruct((B, NSA_GROUPS * GATE_ROWS, S), F32),
    )
    out_specs = (
        c_spec, c_spec, k_spec, k_spec, *d_specs,
        pl.BlockSpec((None, T_Q, tm), lambda b, i: (b, 0, i)),
        v_spec,
        v_spec,
        pl.BlockSpec((None, NSA_GROUPS * GATE_ROWS, tm), lambda b, i: (b, 0, i)),
    )
    return pl.pallas_call(
        _in_proj_kernel,
        grid=grid,
        in_specs=[pl.BlockSpec((None, tm, D), lambda b, i: (b, i, 0)), full(g_mix), full(wn), full(wt)],
        out_specs=out_specs,
        out_shape=out_shape,
        scratch_shapes=[pltpu.VMEM((DIL_WIDTH // LANES, tm, LANES), F32)],
        compiler_params=_params("parallel", "parallel"),
        name="in_proj",
    )(x, g_mix, wn, wt)


def _compress_kernel(x_ref, pe_ref, w1_ref, w2_ref, w2t_ref, c_ref, ct_ref):
    nch = x_ref.shape[0] // CMP_STRIDE
    ulo = jnp.zeros((nch, NSA_GROUPS * CMP_HIDDEN), F32)
    uhi = jnp.zeros((nch, NSA_GROUPS * CMP_HIDDEN), F32)
    for j in range(CMP_STRIDE):
        xj = x_ref[pl.ds(j, nch, stride=CMP_STRIDE), :]
        ulo = ulo + _dot((xj + pe_ref[j:j + 1, :]).astype(BF16), w1_ref[j])
        uhi = uhi + _dot((xj + pe_ref[CMP_STRIDE + j:CMP_STRIDE + j + 1, :]).astype(BF16), w1_ref[CMP_STRIDE + j])
    pre = ulo + pltpu.roll(uhi, nch - 1, 0)
    hid = jax.nn.gelu(pre).astype(BF16)
    aug = _key_position_columns(CMP_BLOCK - 1, nch, CMP_STRIDE).astype(BF16)
    for g in range(NSA_GROUPS):
        hg = hid[:, g * CMP_HIDDEN:(g + 1) * CMP_HIDDEN]
        c_ref[g] = jnp.concatenate([_dot(hg, w2_ref[...]).astype(BF16), aug], axis=1)
        ct_ref[g] = _dot_nt(w2t_ref[...], hg).astype(BF16)


def _compress(xc, pe, w1, w2):
    B, S, width = xc.shape
    nch = S // CMP_STRIDE
    G, dh, hid = NSA_GROUPS, HEAD_DIM, CMP_HIDDEN
    pe_t = jnp.broadcast_to(pe.reshape(CMP_BLOCK, 1, dh), (CMP_BLOCK, G, dh)).reshape(CMP_BLOCK, width)
    eye = jnp.eye(G, dtype=w1.dtype)
    wexp = jnp.einsum('pdn,ge->pgden', w1.reshape(CMP_BLOCK, dh, hid), eye).reshape(CMP_BLOCK, width, G * hid)
    wexp = wexp.astype(BF16)
    full = lambda a: pl.BlockSpec(a.shape, lambda b: (0,) * a.ndim)
    w2b = w2.astype(BF16)
    w2t = w2.T.astype(BF16)
    return pl.pallas_call(
        _compress_kernel,
        grid=(B,),
        in_specs=[pl.BlockSpec((None, S, width), lambda b: (b, 0, 0)), full(pe_t), full(wexp), full(w2b), full(w2t)],
        out_specs=(pl.BlockSpec((None, G, nch, K_AUG), lambda b: (b, 0, 0, 0)),
                   pl.BlockSpec((None, G, dh, nch), lambda b: (b, 0, 0, 0))),
        out_shape=(jax.ShapeDtypeStruct((B, G, nch, K_AUG), BF16),
                   jax.ShapeDtypeStruct((B, G, dh, nch), BF16)),
        compiler_params=_params("parallel"),
        name="compress",
    )(xc, pe_t, wexp, w2b, w2t)


NSA_TQ = 256
SWEEP_TK = AUG_NBLK * SLC_BLOCK
SEL_SUB = 128
SEL_AHEAD = 4


def _rowmax8(s):
    return jnp.max(s.reshape(s.shape[0] // 8, 8, s.shape[1]), axis=0)


def _online_softmax(chunks, scores, values):
    pending = [scores(ch) for ch in chunks[:SEL_AHEAD]]
    m = acc = None
    for n, ch in enumerate(chunks):
        if n + SEL_AHEAD < len(chunks):
            pending.append(scores(chunks[n + SEL_AHEAD]))
        s = pending.pop(0)
        m_c = jnp.max(_rowmax8(s), axis=0, keepdims=True)
        if m is None:
            m = m_c
            acc = _dot(values(ch), jnp.exp2(s - m).astype(BF16))
        else:
            m_new = jnp.maximum(m, m_c)
            acc = jnp.exp2(m - m_new) * acc + _dot(values(ch), jnp.exp2(s - m_new).astype(BF16))
            m = m_new
    return acc


def _nsa_kernel(qt_ref, kc_ref, vct_ref, ks_ref, vst_ref, kw_ref, vwt_ref, gate_ref, ov_ref,
                o_ref, sbt_ref, osel_ref, idx_ref, *, slopes):
    g = pl.program_id(1)
    i = pl.program_id(2)
    R, dh, tq, tk = NSA_REP, HEAD_DIM, NSA_TQ, SWEEP_TK
    L = R * tq
    t0 = i * tq
    nc = kc_ref.shape[0]
    ns = ov_ref.shape[0]
    n_tiles = ns // AUG_NBLK
    tile4 = lambda a: jnp.concatenate([a] * R, axis=1)

    qt = qt_ref[...]
    qs = jnp.concatenate([qt[r * dh:(r + 1) * dh, :] for r in range(R)], axis=1)
    slope = [jnp.where(g == 0, slopes[r], slopes[R + r]).astype(F32) for r in range(R)]
    slope_row = jnp.concatenate([jnp.full((1, tq), 1.0, F32) * slope[r] for r in range(R)], axis=1)
    t_row = t0 + lax.broadcasted_iota(jnp.int32, (1, tq), 1)

    r8 = lax.broadcasted_iota(jnp.int32, (AUG_NBLK, L), 0)
    s_full = slope_row * LOG2E
    s_hi = s_full.astype(BF16).astype(F32)
    alibi8 = jnp.where(r8 < 2, s_hi, jnp.where(r8 < 4, s_full - s_hi, 0.0))
    q_pad = jnp.zeros((K_AUG - dh - 2 * AUG_NBLK, L), BF16)
    q_plain = jnp.concatenate([qs, jnp.concatenate([jnp.zeros((AUG_NBLK, L), F32), alibi8], axis=0).astype(BF16),
                               q_pad], axis=0)

    cmp_end = lax.broadcasted_iota(jnp.int32, (nc, tq), 0) * CMP_STRIDE + (CMP_BLOCK - 1)
    m_cmp = t_row >= cmp_end
    sc = _dot(kc_ref[...], q_plain) + tile4(jnp.where(m_cmp, 0.0, NEG_INF))

    a0 = jnp.maximum(t0 - NSA_WINDOW, 0)
    kpos_w = lax.broadcasted_iota(jnp.int32, (SEL_SUB, tq), 0)
    t_w = t0 + lax.broadcasted_iota(jnp.int32, (SEL_SUB, tq), 1)

    def win_scores(ch):
        diag, r = ch
        k0 = pl.multiple_of((t0 if diag else a0) + r, SEL_SUB)
        kpos = k0 + kpos_w
        valid = (kpos <= t_w) if diag else ((kpos < t0) & (t_w - kpos < NSA_WINDOW))
        return _dot(kw_ref[pl.ds(k0, SEL_SUB), :], q_plain) + tile4(jnp.where(valid, 0.0, NEG_INF))

    def win_values(ch):
        diag, r = ch
        return vwt_ref[:, pl.ds(pl.multiple_of((t0 if diag else a0) + r, SEL_SUB), SEL_SUB)]

    win_chunks = ([(True, r) for r in range(0, tq, SEL_SUB)]
                  + [(False, r) for r in range(0, NSA_WINDOW, SEL_SUB)])
    acc_w = _online_softmax(win_chunks, win_scores, win_values)
    o_win = acc_w[:dh] * (1.0 / acc_w[dh:dh + 1])

    mx = jnp.max(_rowmax8(sc), axis=0, keepdims=True)
    p = jnp.exp2(sc - mx) * tile4(jnp.where(m_cmp, 1.0, 0.0))
    den = jnp.sum(jnp.sum(p.reshape(nc // 8, 8, L), axis=0), axis=0, keepdims=True)
    pr = p * (1.0 / jnp.maximum(den, 1e-30))
    o_cmp = _dot(vct_ref[...], pr.astype(BF16))
    psum = pr[:, 0:tq]
    for r in range(1, R):
        psum = psum + pr[:, r * tq:(r + 1) * tq]

    p_hi = psum.astype(BF16)
    p_lo = (psum - p_hi.astype(F32)).astype(BF16)
    imp = _dot(ov_ref[...], p_hi) + _dot(ov_ref[...], p_lo)
    blk = lax.broadcasted_iota(jnp.int32, (ns, tq), 0)
    cur = lax.shift_right_logical(t_row, SLC_SHIFT)
    val = jnp.where((blk == cur) | (blk == 0), FORCE_SCORE, imp)
    val = jnp.where(blk <= cur, val, -1.0)
    vals = [val[8 * v:8 * v + 8, :] for v in range(ns // 8)]
    ranks = [jnp.zeros((8, tq), F32) for _ in vals]
    row8 = lax.broadcasted_iota(jnp.int32, (8, tq), 0)
    for j in range(ns):
        vj = jnp.broadcast_to(val[j:j + 1, :], (8, tq))
        for v in range(len(vals)):
            if 8 * v > j:
                ahead = vj >= vals[v]
            elif 8 * v + 7 <= j:
                ahead = vj > vals[v]
            else:
                ahead = (vj > vals[v]) | ((vj == vals[v]) & (row8 > j - 8 * v))
            ranks[v] = ranks[v] + jnp.where(ahead, 1.0, 0.0)
    for T in range(n_tiles):
        selb = jnp.where((ranks[T] < float(SLC_TOP)) & (vals[T] >= 0.0), 0.0, NEG_INF)
        sbt_ref[T] = jnp.concatenate([tile4(selb), alibi8], axis=0).astype(BF16)

    td = lax.div(t0, tk)
    part = lax.div(t0 - td * tk, tq)
    cnt = jnp.int32(0)
    for T in range(n_tiles - 1):
        picked = jnp.where((ranks[T] < float(SLC_TOP)) & (vals[T] >= 0.0), 1.0, 0.0)
        idx_ref[cnt] = jnp.int32(T)
        cnt = cnt + jnp.logical_and(jnp.max(picked) > 0.0, T < td).astype(jnp.int32)
    for k in range(n_tiles):
        for p in range(tk // tq):
            @pl.when(jnp.logical_and(cnt == k, part == p))
            def _(k=k, p=p):
                past = [idx_ref[j] for j in range(k)]
                q_past = [jnp.concatenate([qs, sbt_ref[T], q_pad], axis=0) for T in past]
                q_diag = jnp.concatenate([qs, sbt_ref[td], q_pad], axis=0)
                chunks = ([(j, r) for j in range(k) for r in range(0, tk, SEL_SUB)]
                          + [(None, r) for r in range(0, (p + 1) * tq, SEL_SUB)])

                def key_start(ch):
                    j, r = ch
                    return pl.multiple_of((td if j is None else past[j]) * tk + r, SEL_SUB)

                def chunk_scores(ch):
                    k0 = key_start(ch)
                    if ch[0] is None:
                        causal = jnp.where(k0 + kpos_w <= t_w, 0.0, NEG_INF)
                        return _dot(ks_ref[pl.ds(k0, SEL_SUB), :], q_diag) + tile4(causal)
                    return _dot(ks_ref[pl.ds(k0, SEL_SUB), :], q_past[ch[0]])

                acc = _online_softmax(chunks, chunk_scores, lambda ch: vst_ref[:, pl.ds(key_start(ch), SEL_SUB)])
                osel_ref[...] = acc[:dh] * (1.0 / acc[dh:dh + 1])

    o_sel = osel_ref[...]
    gates = gate_ref[...]
    tiles = []
    for r in range(R):
        sl = slice(r * tq, (r + 1) * tq)
        tiles.append(gates[3 * r:3 * r + 1, :] * o_cmp[:, sl]
                     + gates[3 * r + 1:3 * r + 2, :] * o_sel[:, sl]
                     + gates[3 * r + 2:3 * r + 3, :] * o_win[:, sl])
    o_ref[...] = jnp.concatenate(tiles, axis=0).T.astype(o_ref.dtype)


def _overlap_matrix(nc, ns):
    cs = np.arange(nc)[None, :] * CMP_STRIDE
    ss = np.arange(ns)[:, None] * SLC_BLOCK
    ov = np.clip(np.minimum(cs + CMP_BLOCK, ss + SLC_BLOCK) - np.maximum(cs, ss), 0, None)
    return jnp.asarray(ov.astype(np.float32) / CMP_BLOCK, dtype=BF16)


def _nsa(qt, kc, vct, ks, vst, kw, vwt, gates):
    B, _, S = qt.shape
    G, R, dh, tq = NSA_GROUPS, NSA_REP, HEAD_DIM, NSA_TQ
    nc = kc.shape[2]
    ns = S // SLC_BLOCK
    ov = _overlap_matrix(nc, ns)
    kern = functools.partial(_nsa_kernel, slopes=tuple(_alibi_slopes(NSA_HEADS)))
    assert S % SWEEP_TK == 0 and S >= NSA_WINDOW + tq
    per_bg = lambda shape: pl.BlockSpec((None, None) + shape, lambda b, g, i: (b, g, 0, 0))
    return pl.pallas_call(
        kern,
        grid=(B, G, S // tq),
        in_specs=[
            pl.BlockSpec((None, R * dh, tq), lambda b, g, i: (b, g, i)),
            per_bg((nc, K_AUG)), per_bg((dh, nc)),
            per_bg((S, K_AUG)), per_bg((V_ROWS, S)),
            per_bg((S, K_AUG)), per_bg((V_ROWS, S)),
            pl.BlockSpec((None, GATE_ROWS, tq), lambda b, g, i: (b, g, i)),
            pl.BlockSpec(ov.shape, lambda b, g, i: (0, 0)),
        ],
        out_specs=pl.BlockSpec((None, tq, R * dh), lambda b, g, i: (b, i, g)),
        out_shape=jax.ShapeDtypeStruct((B, S, G * R * dh), BF16),
        scratch_shapes=[pltpu.VMEM((S // SWEEP_TK, 2 * AUG_NBLK, R * tq), BF16),
                        pltpu.VMEM((dh, R * tq), F32),
                        pltpu.SMEM((S // SWEEP_TK,), jnp.int32)],
        compiler_params=_params("parallel", "parallel", "arbitrary"),
        name="nsa_attention",
    )(qt, kc, vct, ks, vst, kw, vwt, gates, ov)


BAND_TQ = 128
BAND_SUB = 8
BAND_PROBLEMS = 8
BAND = 128


def _banded_kernel(q_ref, k_ref, v_ref, o_ref, lse_ref, *, slopes, nsub, ncls):
    tq, dh, nh = BAND_TQ, HEAD_DIM, DIL_HEADS_PER_GROUP
    tk = tq + BAND
    width = nh * dh
    head_of_lane = lambda rows: lax.shift_right_logical(
        lax.broadcasted_iota(jnp.int32, (rows, width), 1), int(math.log2(dh)))
    lane_head, q_head = head_of_lane(tk), head_of_lane(tq)
    keep = [jnp.where(lane_head == h, 1.0, 0.0).astype(BF16) for h in range(nh)]

    def per_head(a):
        return jnp.concatenate([a * keep[h] for h in range(nh)], axis=0)

    ones_h = jnp.concatenate(keep, axis=0)
    def biases(first_key_offset):
        d = first_key_offset + (lax.broadcasted_iota(jnp.int32, (tq, tk), 0)
                                - lax.broadcasted_iota(jnp.int32, (tq, tk), 1))
        mask_bias = jnp.where((d >= 0) & (d <= BAND), 0.0, NEG_INF)
        neg_d = -d.astype(F32)
        return [slopes[h] * neg_d + mask_bias for h in range(nh)]

    subs = []
    for cls in range(ncls):
        for sub in range(nsub):
            i = pl.program_id(2) * nsub + sub
            k0 = pl.multiple_of(jnp.maximum(i - 1, 0) * tq, tq)
            q = q_ref[cls, sub * tq:(sub + 1) * tq, :]
            scores = _dot_nt(q, per_head(k_ref[cls, pl.ds(k0, tk), :]))
            subs.append((cls, sub, i * tq - k0, k0, scores))
    first_bias = biases(subs[0][2])
    inner_bias = biases(tq) if nsub > 1 else None
    probs = []
    for cls, sub, off, k0, s in subs:
        bias = first_bias if sub == 0 else inner_bias
        ps, mxs = [], []
        for h in range(nh):
            sh = s[:, h * tk:(h + 1) * tk] + bias[h]
            mx = jnp.max(sh, axis=-1, keepdims=True)
            ps.append(jnp.exp2(sh - mx).astype(BF16))
            mxs.append(mx)
        probs.append((cls, sub, k0, jnp.concatenate(ps, axis=1), mxs))
    for cls, sub, k0, p, mxs in probs:
        den = _dot(p, ones_h)
        mx_all = mxs[nh - 1]
        for h in range(nh - 2, -1, -1):
            mx_all = jnp.where(q_head == h, mxs[h], mx_all)
        rows = slice(sub * tq, (sub + 1) * tq)
        o_ref[cls, rows, :] = _dot(p, per_head(v_ref[cls, pl.ds(k0, tk), :])) * (1.0 / den)
        lse_ref[cls, rows, :] = mx_all * LN2 + jnp.log(den)


def _banded(d, slopes):
    B, _, r, n, width = d.shape
    nsub = min(BAND_SUB, n // BAND_TQ)
    ncls = min(r, BAND_PROBLEMS // nsub)
    tq = BAND_TQ * nsub
    assert n >= BAND_TQ + BAND and n % tq == 0 and r % ncls == 0
    kern = functools.partial(_banded_kernel, slopes=tuple(slopes), nsub=nsub, ncls=ncls)
    tile = pl.BlockSpec((None, None, ncls, tq, width), lambda b, c, i: (b, 0, c, i, 0))
    kseq = pl.BlockSpec((None, None, ncls, n, width), lambda b, c, i: (b, 1, c, 0, 0))
    vseq = pl.BlockSpec((None, None, ncls, n, width), lambda b, c, i: (b, 2, c, 0, 0))
    out = pl.BlockSpec((None, ncls, tq, width), lambda b, c, i: (b, c, i, 0))
    return pl.pallas_call(
        kern,
        grid=(B, r // ncls, n // tq),
        in_specs=[tile, kseq, vseq],
        out_specs=(out, out),
        out_shape=(jax.ShapeDtypeStruct((B, r, n, width), F32),) * 2,
        compiler_params=_params("parallel", "parallel", "arbitrary"),
        name="banded_attention",
    )(d, d, d)


def _dilated(dils):
    slopes = _alibi_slopes(DIL_HEADS)
    outs, lses = [], []
    for gi, (w, r) in enumerate(DIL_CONFIGS):
        assert w // r == BAND
        sl = [s_ * r * LOG2E for s_ in slopes[gi * DIL_HEADS_PER_GROUP:(gi + 1) * DIL_HEADS_PER_GROUP]]
        o, lse = _banded(dils[gi], sl)
        outs.append(o)
        lses.append(lse)
    return outs, lses


MERGE_TM = 512


def _token_order(ref, slab_ref):
    r, rows, width = ref.shape
    if r == 1:
        return ref[0]
    for c in range(r):
        blk = ref[c]
        for s in range(width // LANES):
            slab_ref[s, pl.ds(c, rows, stride=r), :] = blk[:, s * LANES:(s + 1) * LANES]
    return jnp.concatenate([slab_ref[s] for s in range(width // LANES)], axis=1)


def _merge_kernel(x_ref, oa_ref, o0_ref, o1_ref, o2_ref, l0_ref, l1_ref, l2_ref,
                  gmix_ref, wm_ref, wpn_ref, wpd_ref, wo_ref, gffn_ref, x1_ref, h2_ref, slab_ref):
    x = x_ref[...]
    D = x.shape[1]
    h = _rms(x, gmix_ref[...]).astype(BF16)
    gm = _sigmoid(_dot_nt(h, wm_ref[...]))
    o0, o1, o2 = [_token_order(r_, slab_ref) for r_ in (o0_ref, o1_ref, o2_ref)]
    l0, l1, l2 = [_token_order(r_, slab_ref) for r_ in (l0_ref, l1_ref, l2_ref)]
    mx = jnp.maximum(jnp.maximum(l0, l1), l2)
    e0, e1, e2 = jnp.exp(l0 - mx), jnp.exp(l1 - mx), jnp.exp(l2 - mx)
    inv = 1.0 / (e0 + e1 + e2)
    ob = o0 * (e0 * inv) + o1 * (e1 * inv) + o2 * (e2 * inv)
    a = _dot(oa_ref[...], wpn_ref[...])
    d = _dot(ob.astype(BF16), wpd_ref[...])
    mixed = gm[:, :D] * a + gm[:, D:] * d
    x1 = x + _dot(mixed.astype(BF16), wo_ref[...])
    x1_ref[...] = x1
    h2_ref[...] = _rms(x1, gffn_ref[...]).astype(BF16)


def _merge(x, o_a, outs, lses, g_mix, w_merge, w_proj_nsa, w_proj_dil, w_out, g_ffn):
    B, S, D = x.shape
    tm = MERGE_TM
    row = lambda a: pl.BlockSpec((None, tm, a.shape[2]), lambda b, i: (b, i, 0))
    cls = lambda a: pl.BlockSpec((None, a.shape[1], tm // a.shape[1], a.shape[3]), lambda b, i: (b, 0, i, 0))
    full = lambda a: pl.BlockSpec(a.shape, lambda b, i: (0,) * a.ndim)
    ws = [w_merge.astype(BF16), w_proj_nsa.astype(BF16), w_proj_dil.astype(BF16), w_out.astype(BF16)]
    consts = [g_mix, *ws, g_ffn]
    in_specs = [row(x), row(o_a)] + [cls(a) for a in (*outs, *lses)] + [full(a) for a in consts]
    return pl.pallas_call(
        _merge_kernel,
        grid=(B, S // tm),
        in_specs=in_specs,
        out_specs=(pl.BlockSpec((None, tm, D), lambda b, i: (b, i, 0)),) * 2,
        out_shape=(jax.ShapeDtypeStruct((B, S, D), F32), jax.ShapeDtypeStruct((B, S, D), BF16)),
        scratch_shapes=[pltpu.VMEM((DIL_WIDTH // LANES, tm, LANES), F32)],
        compiler_params=_params("parallel", "parallel"),
        name="merge_proj",
    )(x, o_a, *outs, *lses, *consts)


FFN_TM = 512
FFN_TN = 256
HALO = 16


def _ffn_kernel(h_ref, halo_ref, x1_ref, wup_ref, cw_ref, cb_ref, wd_ref, gfin_ref, o_ref, act_ref):
    i = pl.program_id(1)
    h = h_ref[...]
    halo = halo_ref[...]
    tm = h.shape[0]
    row = lax.broadcasted_iota(jnp.int32, (tm, FFN_TN), 0)
    live = (i > 0).astype(F32)
    for j in range(D_FF // FFN_TN):
        cols = slice(j * FFN_TN, (j + 1) * FFN_TN)
        wu = wup_ref[:, cols]
        u = _dot(h, wu)
        uh = _dot(halo, wu) * live
        gate = _dot(h, wup_ref[:, D_FF + j * FFN_TN:D_FF + (j + 1) * FFN_TN])
        p1 = jnp.broadcast_to(uh[HALO - 1:HALO, :], (tm, FFN_TN))
        p2 = jnp.broadcast_to(uh[HALO - 2:HALO - 1, :], (tm, FFN_TN))
        u1 = jnp.where(row == 0, p1, pltpu.roll(u, 1, 0))
        u2 = jnp.where(row == 0, p2, jnp.where(row == 1, p1, pltpu.roll(u, 2, 0)))
        uc = cb_ref[:, cols] + cw_ref[0:1, cols] * u2
        uc = uc + cw_ref[1:2, cols] * u1
        uc = uc + cw_ref[2:3, cols] * u
        act_ref[:, j * FFN_TN:(j + 1) * FFN_TN] = (jax.nn.gelu(uc) * gate).astype(BF16)
    y = _dot(act_ref[...], wd_ref[...])
    o_ref[...] = _rms(x1_ref[...] + y, gfin_ref[...])


def _ffn(h2, x1, w_up, conv_w, conv_b, w_down, g_final):
    B, S, D = h2.shape
    tm = FFN_TM
    assert D_FF % FFN_TN == 0
    wup = w_up.astype(BF16)
    cw = conv_w
    cb = conv_b.reshape(1, D_FF)
    wd = w_down.astype(BF16)
    gfin = g_final.reshape(1, D)
    full = lambda a: pl.BlockSpec(a.shape, lambda b, i: (0,) * a.ndim, pipeline_mode=pl.Buffered(1))
    tile = pl.BlockSpec((None, tm, D), lambda b, i: (b, i, 0))
    halo = pl.BlockSpec((None, HALO, D), lambda b, i: (b, jnp.maximum(i * (tm // HALO) - 1, 0), 0))
    return pl.pallas_call(
        _ffn_kernel,
        grid=(B, S // tm),
        in_specs=[tile, halo, tile, full(wup), full(cw), full(cb), full(wd), full(gfin)],
        out_specs=tile,
        out_shape=jax.ShapeDtypeStruct((B, S, D), F32),
        scratch_shapes=[pltpu.VMEM((tm, D_FF), BF16)],
        compiler_params=_params("parallel", "parallel"),
        name="conv_ffn",
    )(h2, h2, x1, wup, cw, cb, wd, gfin)


@jax.jit
def _layer(x, g_mix, w_in, pe_cmp_k, w_cmp_k1, w_cmp_k2, pe_cmp_v, w_cmp_v1, w_cmp_v2,
           w_proj_nsa, w_proj_dil, w_out, g_ffn, w_up, conv_w, conv_b, w_down, g_final):
    B, S, D = x.shape
    depth = g_mix.shape[0]
    for l in range(depth):
        gm = g_mix[l].reshape(1, D)
        kcmp, vcmp, ks, kw, d0, d1, d2, qt, vst, vwt, gates = _in_proj(x, gm, w_in[l])
        kc, _ = _compress(kcmp, pe_cmp_k[l], w_cmp_k1[l], w_cmp_k2[l])
        _, vct = _compress(vcmp, pe_cmp_v[l], w_cmp_v1[l], w_cmp_v2[l])
        o_a = _nsa(qt, kc, vct, ks, vst, kw, vwt, gates)
        outs, lses = _dilated((d0, d1, d2))
        merge_cols = w_in[l].T[w_in.shape[2] - 2 * D:]
        x1, h2 = _merge(x, o_a, outs, lses, gm, merge_cols, w_proj_nsa[l], w_proj_dil[l], w_out[l],
                        g_ffn[l].reshape(1, D))
        x = _ffn(h2, x1, w_up[l], conv_w[l], conv_b[l], w_down[l], g_final)
        assert depth == 1
    return x


def kernel(x, g_mix, w_in, pe_cmp_k, w_cmp_k1, w_cmp_k2, pe_cmp_v, w_cmp_v1, w_cmp_v2, w_proj_nsa, w_proj_dil, w_out, g_ffn, w_up, conv_w, conv_b, w_down, g_final):
    return _layer(x, g_mix, w_in, pe_cmp_k, w_cmp_k1, w_cmp_k2, pe_cmp_v, w_cmp_v1, w_cmp_v2,
                  w_proj_nsa, w_proj_dil, w_out, g_ffn, w_up, conv_w, conv_b, w_down, g_final)
```

```python
import functools
import math

import numpy as np
import jax
import jax.numpy as jnp
from jax import lax
from jax.experimental import pallas as pl
from jax.experimental.pallas import tpu as pltpu

HEAD_DIM = 64
NSA_HEADS = 8
NSA_GROUPS = 2
NSA_REP = NSA_HEADS // NSA_GROUPS
CMP_BLOCK = 32
CMP_STRIDE = 16
CMP_HIDDEN = 128
SLC_BLOCK = 64
SLC_TOP = 16
NSA_WINDOW = 512
FORCE_SCORE = 1.0e4
DIL_CONFIGS = ((128, 1), (512, 4), (2048, 16))
DIL_GROUPS = 3
DIL_HEADS_PER_GROUP = 4
DIL_HEADS = DIL_GROUPS * DIL_HEADS_PER_GROUP
D_FF = 2816
CONV_WIDTH = 3
RMS_EPS = 1e-6
NEG_INF = -1e30

LANES = 128
VMEM_LIMIT_BYTES = 56 * 1024 * 1024

F32 = jnp.float32
BF16 = jnp.bfloat16
NT_DIMS = (((1,), (1,)), ((), ()))


def _alibi_slopes(n):
    return [float(2.0 ** (-8.0 * i / n)) for i in range(1, n + 1)]


def _rms(xf, g):
    ms = jnp.mean(xf * xf, axis=-1, keepdims=True)
    return xf * lax.rsqrt(ms + RMS_EPS) * g


def _dot(a, b):
    return jnp.dot(a, b, preferred_element_type=F32)


def _dot_nt(a, b):
    return lax.dot_general(a, b, NT_DIMS, preferred_element_type=F32)


def _sigmoid(z):
    return 1.0 / (1.0 + jnp.exp(-z))


def _params(*sem):
    return pltpu.CompilerParams(dimension_semantics=sem, vmem_limit_bytes=VMEM_LIMIT_BYTES)


IN_TM = 512
N_KVC = 4 * HEAD_DIM
N_KSEL = NSA_GROUPS * HEAD_DIM
N_DIL = 3 * DIL_HEADS * HEAD_DIM
DIL_WIDTH = DIL_HEADS_PER_GROUP * HEAD_DIM
T_Q = NSA_HEADS * HEAD_DIM
T_V = NSA_GROUPS * HEAD_DIM
GATE_ROWS = 16
K_AUG = 2 * HEAD_DIM
SLC_SHIFT = int(math.log2(SLC_BLOCK))
AUG_NBLK = 8
AUG_HI, AUG_LO = AUG_NBLK, AUG_NBLK + 1
LOG2E = math.log2(math.e)
LN2 = math.log(2.0)
V_PAD = 16
V_ROWS = HEAD_DIM + V_PAD


def _key_position_columns(pos0, rows, step=1):
    pos = pos0 + step * lax.broadcasted_iota(jnp.int32, (rows, HEAD_DIM), 0)
    col = lax.broadcasted_iota(jnp.int32, (rows, HEAD_DIM), 1)
    blk = jnp.bitwise_and(lax.shift_right_logical(pos, SLC_SHIFT), AUG_NBLK - 1)
    hi = lax.shift_left(lax.shift_right_logical(pos, 7), 7).astype(F32)
    lo = jnp.bitwise_and(pos, 127).astype(F32)
    c = jnp.where((col == AUG_HI) | (col == AUG_HI + 2), hi,
                  jnp.where((col == AUG_LO) | (col == AUG_LO + 2), lo, 0.0))
    return jnp.where((col < AUG_NBLK) & (blk == col), 1.0, c)


def _in_proj_kernel(x_ref, g_ref, wn_ref, wt_ref,
                    kcmp_ref, vcmp_ref, ks_ref, kw_ref, d0_ref, d1_ref, d2_ref, qt_ref, vst_ref, vwt_ref, gate_ref,
                    slab_ref):
    tm = x_ref.shape[0]
    h = _rms(x_ref[...], g_ref[...]).astype(BF16)
    c0 = 0
    kvc = _dot_nt(h, wn_ref[c0:c0 + N_KVC, :])
    kcmp_ref[...] = kvc[:, :N_KSEL]
    vcmp_ref[...] = kvc[:, N_KSEL:]
    c0 += N_KVC
    ks = _dot_nt(h, wn_ref[c0:c0 + N_KSEL, :]).astype(BF16)
    c0 += N_KSEL
    kw = _dot_nt(h, wn_ref[c0:c0 + N_KSEL, :]).astype(BF16)
    c0 += N_KSEL
    aug = _key_position_columns(pl.program_id(1) * tm, tm).astype(BF16)
    for g in range(NSA_GROUPS):
        ks_ref[g] = jnp.concatenate([ks[:, g * HEAD_DIM:(g + 1) * HEAD_DIM], aug], axis=1)
        kw_ref[g] = jnp.concatenate([kw[:, g * HEAD_DIM:(g + 1) * HEAD_DIM], aug], axis=1)
    seg = DIL_HEADS * HEAD_DIM
    for which in range(3):
        y = _dot_nt(h, wn_ref[c0 + which * seg:c0 + (which + 1) * seg, :])
        for gi, (d_ref, (_, r)) in enumerate(zip((d0_ref, d1_ref, d2_ref), DIL_CONFIGS)):
            yg = y[:, gi * DIL_WIDTH:(gi + 1) * DIL_WIDTH]
            if r == 1:
                d_ref[which, 0] = yg.astype(BF16)
                continue
            for s in range(DIL_WIDTH // LANES):
                slab_ref[s] = yg[:, s * LANES:(s + 1) * LANES]
            for c in range(r):
                d_ref[which, c] = jnp.concatenate(
                    [slab_ref[s, pl.ds(c, tm // r, stride=r), :] for s in range(DIL_WIDTH // LANES)],
                    axis=1).astype(BF16)
    yt = _dot_nt(wt_ref[...], h)
    qt_ref[...] = yt[0:T_Q].astype(BF16)
    r0 = T_Q
    ones = jnp.where(lax.broadcasted_iota(jnp.int32, (V_PAD, tm), 0) == 0, 1.0, 0.0).astype(BF16)
    for ref in (vst_ref, vwt_ref):
        vt = yt[r0:r0 + T_V].astype(BF16)
        r0 += T_V
        for g in range(NSA_GROUPS):
            ref[g] = jnp.concatenate([vt[g * HEAD_DIM:(g + 1) * HEAD_DIM, :], ones], axis=0)
    gate_ref[...] = _sigmoid(yt[r0:r0 + NSA_GROUPS * GATE_ROWS])


def _in_proj(x, g_mix, w_in):
    B, S, D = x.shape
    scale = HEAD_DIM ** -0.5 * LOG2E
    o_q, o_kv = 0, T_Q
    o_gate = o_kv + 6 * N_KSEL
    o_dil = o_gate + 3 * NSA_HEADS
    o_merge = o_dil + N_DIL
    w_t = w_in.T
    kv = w_t[o_kv:o_gate]

    def kind(k):
        return kv[k * N_KSEL:(k + 1) * N_KSEL]

    dil = w_t[o_dil:o_merge]
    dil = jnp.concatenate([dil[:DIL_HEADS * HEAD_DIM] * scale, dil[DIL_HEADS * HEAD_DIM:]], axis=0)
    wn = jnp.concatenate([kind(0), kind(1), kind(2), kind(4), dil], axis=0).astype(BF16)
    wg = w_t[o_gate:o_dil].reshape(NSA_GROUPS, 3 * NSA_REP, D)
    wg = jnp.pad(wg, ((0, 0), (0, GATE_ROWS - 3 * NSA_REP), (0, 0))).reshape(NSA_GROUPS * GATE_ROWS, D)
    wt = jnp.concatenate([w_t[o_q:o_kv] * scale, kind(3), kind(5), wg], axis=0).astype(BF16)
    tm = IN_TM
    grid = (B, S // tm)
    full = lambda a: pl.BlockSpec(a.shape, lambda b, i: (0,) * a.ndim)
    k_shape = jax.ShapeDtypeStruct((B, NSA_GROUPS, S, K_AUG), BF16)
    v_shape = jax.ShapeDtypeStruct((B, NSA_GROUPS, V_ROWS, S), BF16)
    k_spec = pl.BlockSpec((None, NSA_GROUPS, tm, K_AUG), lambda b, i: (b, 0, i, 0))
    v_spec = pl.BlockSpec((None, NSA_GROUPS, V_ROWS, tm), lambda b, i: (b, 0, 0, i))
    c_shape = jax.ShapeDtypeStruct((B, S, N_KSEL), F32)
    c_spec = pl.BlockSpec((None, tm, N_KSEL), lambda b, i: (b, i, 0))
    d_shapes = tuple(jax.ShapeDtypeStruct((B, 3, r, S // r, DIL_WIDTH), BF16) for _, r in DIL_CONFIGS)
    d_specs = tuple(pl.BlockSpec((None, 3, r, tm // r, DIL_WIDTH), lambda b, i: (b, 0, 0, i, 0))
                    for _, r in DIL_CONFIGS)
    out_shape = (
        c_shape, c_shape, k_shape, k_shape, *d_shapes,
        jax.ShapeDtypeStruct((B, T_Q, S), BF16),
        v_shape,
        v_shape,
        jax.ShapeDtypeStruct((B, NSA_GROUPS * GATE_ROWS, S), F32),
    )
    out_specs = (
        c_spec, c_spec, k_spec, k_spec, *d_specs,
        pl.BlockSpec((None, T_Q, tm), lambda b, i: (b, 0, i)),
        v_spec,
        v_spec,
        pl.BlockSpec((None, NSA_GROUPS * GATE_ROWS, tm), lambda b, i: (b, 0, i)),
    )
    return pl.pallas_call(
        _in_proj_kernel,
        grid=grid,
        in_specs=[pl.BlockSpec((None, tm, D), lambda b, i: (b, i, 0)), full(g_mix), full(wn), full(wt)],
        out_specs=out_specs,
        out_shape=out_shape,
        scratch_shapes=[pltpu.VMEM((DIL_WIDTH // LANES, tm, LANES), F32)],
        compiler_params=_params("parallel", "parallel"),
        name="in_proj",
    )(x, g_mix, wn, wt)


def _compress_kernel(x_ref, pe_ref, w1_ref, w2_ref, w2t_ref, c_ref, ct_ref):
    nch = x_ref.shape[0] // CMP_STRIDE
    ulo = jnp.zeros((nch, NSA_GROUPS * CMP_HIDDEN), F32)
    uhi = jnp.zeros((nch, NSA_GROUPS * CMP_HIDDEN), F32)
    for j in range(CMP_STRIDE):
        xj = x_ref[pl.ds(j, nch, stride=CMP_STRIDE), :]
        ulo = ulo + _dot((xj + pe_ref[j:j + 1, :]).astype(BF16), w1_ref[j])
        uhi = uhi + _dot((xj + pe_ref[CMP_STRIDE + j:CMP_STRIDE + j + 1, :]).astype(BF16), w1_ref[CMP_STRIDE + j])
    pre = ulo + pltpu.roll(uhi, nch - 1, 0)
    hid = jax.nn.gelu(pre).astype(BF16)
    aug = _key_position_columns(CMP_BLOCK - 1, nch, CMP_STRIDE).astype(BF16)
    for g in range(NSA_GROUPS):
        hg = hid[:, g * CMP_HIDDEN:(g + 1) * CMP_HIDDEN]
        c_ref[g] = jnp.concatenate([_dot(hg, w2_ref[...]).astype(BF16), aug], axis=1)
        ct_ref[g] = _dot_nt(w2t_ref[...], hg).astype(BF16)


def _compress(xc, pe, w1, w2):
    B, S, width = xc.shape
    nch = S // CMP_STRIDE
    G, dh, hid = NSA_GROUPS, HEAD_DIM, CMP_HIDDEN
    pe_t = jnp.broadcast_to(pe.reshape(CMP_BLOCK, 1, dh), (CMP_BLOCK, G, dh)).reshape(CMP_BLOCK, width)
    eye = jnp.eye(G, dtype=w1.dtype)
    wexp = jnp.einsum('pdn,ge->pgden', w1.reshape(CMP_BLOCK, dh, hid), eye).reshape(CMP_BLOCK, width, G * hid)
    wexp = wexp.astype(BF16)
    full = lambda a: pl.BlockSpec(a.shape, lambda b: (0,) * a.ndim)
    w2b = w2.astype(BF16)
    w2t = w2.T.astype(BF16)
    return pl.pallas_call(
        _compress_kernel,
        grid=(B,),
        in_specs=[pl.BlockSpec((None, S, width), lambda b: (b, 0, 0)), full(pe_t), full(wexp), full(w2b), full(w2t)],
        out_specs=(pl.BlockSpec((None, G, nch, K_AUG), lambda b: (b, 0, 0, 0)),
                   pl.BlockSpec((None, G, dh, nch), lambda b: (b, 0, 0, 0))),
        out_shape=(jax.ShapeDtypeStruct((B, G, nch, K_AUG), BF16),
                   jax.ShapeDtypeStruct((B, G, dh, nch), BF16)),
        compiler_params=_params("parallel"),
        name="compress",
    )(xc, pe_t, wexp, w2b, w2t)


NSA_TQ = 256
SWEEP_TK = AUG_NBLK * SLC_BLOCK
SEL_SUB = 128
SEL_AHEAD = 4


def _rowmax8(s):
    return jnp.max(s.reshape(s.shape[0] // 8, 8, s.shape[1]), axis=0)


def _online_softmax(chunks, scores, values):
    pending = [scores(ch) for ch in chunks[:SEL_AHEAD]]
    m = acc = None
    for n, ch in enumerate(chunks):
        if n + SEL_AHEAD < len(chunks):
            pending.append(scores(chunks[n + SEL_AHEAD]))
        s = pending.pop(0)
        m_c = jnp.max(_rowmax8(s), axis=0, keepdims=True)
        if m is None:
            m = m_c
            acc = _dot(values(ch), jnp.exp2(s - m).astype(BF16))
        else:
            m_new = jnp.maximum(m, m_c)
            acc = jnp.exp2(m - m_new) * acc + _dot(values(ch), jnp.exp2(s - m_new).astype(BF16))
            m = m_new
    return acc


def _nsa_kernel(qt_ref, kc_ref, vct_ref, ks_ref, vst_ref, kw_ref, vwt_ref, gate_ref, ov_ref,
                o_ref, sbt_ref, osel_ref, idx_ref, *, slopes):
    g = pl.program_id(1)
    i = pl.program_id(2)
    R, dh, tq, tk = NSA_REP, HEAD_DIM, NSA_TQ, SWEEP_TK
    L = R * tq
    t0 = i * tq
    nc = kc_ref.shape[0]
    ns = ov_ref.shape[0]
    n_tiles = ns // AUG_NBLK
    tile4 = lambda a: jnp.concatenate([a] * R, axis=1)

    qt = qt_ref[...]
    qs = jnp.concatenate([qt[r * dh:(r + 1) * dh, :] for r in range(R)], axis=1)
    slope = [jnp.where(g == 0, slopes[r], slopes[R + r]).astype(F32) for r in range(R)]
    slope_row = jnp.concatenate([jnp.full((1, tq), 1.0, F32) * slope[r] for r in range(R)], axis=1)
    t_row = t0 + lax.broadcasted_iota(jnp.int32, (1, tq), 1)

    r8 = lax.broadcasted_iota(jnp.int32, (AUG_NBLK, L), 0)
    s_full = slope_row * LOG2E
    s_hi = s_full.astype(BF16).astype(F32)
    alibi8 = jnp.where(r8 < 2, s_hi, jnp.where(r8 < 4, s_full - s_hi, 0.0))
    q_pad = jnp.zeros((K_AUG - dh - 2 * AUG_NBLK, L), BF16)
    q_plain = jnp.concatenate([qs, jnp.concatenate([jnp.zeros((AUG_NBLK, L), F32), alibi8], axis=0).astype(BF16),
                               q_pad], axis=0)

    cmp_end = lax.broadcasted_iota(jnp.int32, (nc, tq), 0) * CMP_STRIDE + (CMP_BLOCK - 1)
    m_cmp = t_row >= cmp_end
    sc = _dot(kc_ref[...], q_plain) + tile4(jnp.where(m_cmp, 0.0, NEG_INF))

    a0 = jnp.maximum(t0 - NSA_WINDOW, 0)
    kpos_w = lax.broadcasted_iota(jnp.int32, (SEL_SUB, tq), 0)
    t_w = t0 + lax.broadcasted_iota(jnp.int32, (SEL_SUB, tq), 1)

    def win_scores(ch):
        diag, r = ch
        k0 = pl.multiple_of((t0 if diag else a0) + r, SEL_SUB)
        kpos = k0 + kpos_w
        valid = (kpos <= t_w) if diag else ((kpos < t0) & (t_w - kpos < NSA_WINDOW))
        return _dot(kw_ref[pl.ds(k0, SEL_SUB), :], q_plain) + tile4(jnp.where(valid, 0.0, NEG_INF))

    def win_values(ch):
        diag, r = ch
        return vwt_ref[:, pl.ds(pl.multiple_of((t0 if diag else a0) + r, SEL_SUB), SEL_SUB)]

    win_chunks = ([(True, r) for r in range(0, tq, SEL_SUB)]
                  + [(False, r) for r in range(0, NSA_WINDOW, SEL_SUB)])
    acc_w = _online_softmax(win_chunks, win_scores, win_values)
    o_win = acc_w[:dh] * (1.0 / acc_w[dh:dh + 1])

    mx = jnp.max(_rowmax8(sc), axis=0, keepdims=True)
    p = jnp.exp2(sc - mx) * tile4(jnp.where(m_cmp, 1.0, 0.0))
    den = jnp.sum(jnp.sum(p.reshape(nc // 8, 8, L), axis=0), axis=0, keepdims=True)
    pr = p * (1.0 / jnp.maximum(den, 1e-30))
    o_cmp = _dot(vct_ref[...], pr.astype(BF16))
    psum = pr[:, 0:tq]
    for r in range(1, R):
        psum = psum + pr[:, r * tq:(r + 1) * tq]

    p_hi = psum.astype(BF16)
    p_lo = (psum - p_hi.astype(F32)).astype(BF16)
    imp = _dot(ov_ref[...], p_hi) + _dot(ov_ref[...], p_lo)
    blk = lax.broadcasted_iota(jnp.int32, (ns, tq), 0)
    cur = lax.shift_right_logical(t_row, SLC_SHIFT)
    val = jnp.where((blk == cur) | (blk == 0), FORCE_SCORE, imp)
    val = jnp.where(blk <= cur, val, -1.0)
    vals = [val[8 * v:8 * v + 8, :] for v in range(ns // 8)]
    ranks = [jnp.zeros((8, tq), F32) for _ in vals]
    row8 = lax.broadcasted_iota(jnp.int32, (8, tq), 0)
    for j in range(ns):
        vj = jnp.broadcast_to(val[j:j + 1, :], (8, tq))
        for v in range(len(vals)):
            if 8 * v > j:
                ahead = vj >= vals[v]
            elif 8 * v + 7 <= j:
                ahead = vj > vals[v]
            else:
                ahead = (vj > vals[v]) | ((vj == vals[v]) & (row8 > j - 8 * v))
            ranks[v] = ranks[v] + jnp.where(ahead, 1.0, 0.0)
    for T in range(n_tiles):
        selb = jnp.where((ranks[T] < float(SLC_TOP)) & (vals[T] >= 0.0), 0.0, NEG_INF)
        sbt_ref[T] = jnp.concatenate([tile4(selb), alibi8], axis=0).astype(BF16)

    td = lax.div(t0, tk)
    cnt = jnp.int32(0)
    for T in range(n_tiles - 1):
        picked = jnp.where((ranks[T] < float(SLC_TOP)) & (vals[T] >= 0.0), 1.0, 0.0)
        idx_ref[cnt] = jnp.int32(T)
        cnt = cnt + jnp.logical_and(jnp.max(picked) > 0.0, T < td).astype(jnp.int32)
    for k in range(n_tiles):
        @pl.when(cnt == k)
        def _(k=k):
            past = [idx_ref[j] for j in range(k)]
            q_past = [jnp.concatenate([qs, sbt_ref[T], q_pad], axis=0) for T in past]
            q_diag = jnp.concatenate([qs, sbt_ref[td], q_pad], axis=0)
            chunks = ([(j, r) for j in range(k) for r in range(0, tk, SEL_SUB)]
                      + [(None, r) for r in range(0, tk, SEL_SUB)])

            def key_start(ch):
                j, r = ch
                return pl.multiple_of((td if j is None else past[j]) * tk + r, SEL_SUB)

            def chunk_scores(ch):
                k0 = key_start(ch)
                if ch[0] is None:
                    causal = jnp.where(k0 + kpos_w <= t_w, 0.0, NEG_INF)
                    return _dot(ks_ref[pl.ds(k0, SEL_SUB), :], q_diag) + tile4(causal)
                return _dot(ks_ref[pl.ds(k0, SEL_SUB), :], q_past[ch[0]])

            acc = _online_softmax(chunks, chunk_scores, lambda ch: vst_ref[:, pl.ds(key_start(ch), SEL_SUB)])
            osel_ref[...] = acc[:dh] * (1.0 / acc[dh:dh + 1])

    o_sel = osel_ref[...]
    gates = gate_ref[...]
    tiles = []
    for r in range(R):
        sl = slice(r * tq, (r + 1) * tq)
        tiles.append(gates[3 * r:3 * r + 1, :] * o_cmp[:, sl]
                     + gates[3 * r + 1:3 * r + 2, :] * o_sel[:, sl]
                     + gates[3 * r + 2:3 * r + 3, :] * o_win[:, sl])
    o_ref[...] = jnp.concatenate(tiles, axis=0).T.astype(o_ref.dtype)


def _overlap_matrix(nc, ns):
    cs = np.arange(nc)[None, :] * CMP_STRIDE
    ss = np.arange(ns)[:, None] * SLC_BLOCK
    ov = np.clip(np.minimum(cs + CMP_BLOCK, ss + SLC_BLOCK) - np.maximum(cs, ss), 0, None)
    return jnp.asarray(ov.astype(np.float32) / CMP_BLOCK, dtype=BF16)


def _nsa(qt, kc, vct, ks, vst, kw, vwt, gates):
    B, _, S = qt.shape
    G, R, dh, tq = NSA_GROUPS, NSA_REP, HEAD_DIM, NSA_TQ
    nc = kc.shape[2]
    ns = S // SLC_BLOCK
    ov = _overlap_matrix(nc, ns)
    kern = functools.partial(_nsa_kernel, slopes=tuple(_alibi_slopes(NSA_HEADS)))
    assert S % SWEEP_TK == 0 and S >= NSA_WINDOW + tq
    per_bg = lambda shape: pl.BlockSpec((None, None) + shape, lambda b, g, i: (b, g, 0, 0))
    return pl.pallas_call(
        kern,
        grid=(B, G, S // tq),
        in_specs=[
            pl.BlockSpec((None, R * dh, tq), lambda b, g, i: (b, g, i)),
            per_bg((nc, K_AUG)), per_bg((dh, nc)),
            per_bg((S, K_AUG)), per_bg((V_ROWS, S)),
            per_bg((S, K_AUG)), per_bg((V_ROWS, S)),
            pl.BlockSpec((None, GATE_ROWS, tq), lambda b, g, i: (b, g, i)),
            pl.BlockSpec(ov.shape, lambda b, g, i: (0, 0)),
        ],
        out_specs=pl.BlockSpec((None, tq, R * dh), lambda b, g, i: (b, i, g)),
        out_shape=jax.ShapeDtypeStruct((B, S, G * R * dh), BF16),
        scratch_shapes=[pltpu.VMEM((S // SWEEP_TK, 2 * AUG_NBLK, R * tq), BF16),
                        pltpu.VMEM((dh, R * tq), F32),
                        pltpu.SMEM((S // SWEEP_TK,), jnp.int32)],
        compiler_params=_params("parallel", "parallel", "arbitrary"),
        name="nsa_attention",
    )(qt, kc, vct, ks, vst, kw, vwt, gates, ov)


BAND_TQ = 128
BAND_SUB = 8
BAND_PROBLEMS = 8
BAND = 128


def _banded_kernel(q_ref, k_ref, v_ref, o_ref, lse_ref, *, slopes, nsub, ncls):
    tq, dh, nh = BAND_TQ, HEAD_DIM, DIL_HEADS_PER_GROUP
    tk = tq + BAND
    width = nh * dh
    head_of_lane = lambda rows: lax.shift_right_logical(
        lax.broadcasted_iota(jnp.int32, (rows, width), 1), int(math.log2(dh)))
    lane_head, q_head = head_of_lane(tk), head_of_lane(tq)
    keep = [jnp.where(lane_head == h, 1.0, 0.0).astype(BF16) for h in range(nh)]

    def per_head(a):
        return jnp.concatenate([a * keep[h] for h in range(nh)], axis=0)

    ones_h = jnp.concatenate(keep, axis=0)
    def biases(first_key_offset):
        d = first_key_offset + (lax.broadcasted_iota(jnp.int32, (tq, tk), 0)
                                - lax.broadcasted_iota(jnp.int32, (tq, tk), 1))
        mask_bias = jnp.where((d >= 0) & (d <= BAND), 0.0, NEG_INF)
        neg_d = -d.astype(F32)
        return [slopes[h] * neg_d + mask_bias for h in range(nh)]

    subs = []
    for cls in range(ncls):
        for sub in range(nsub):
            i = pl.program_id(2) * nsub + sub
            k0 = pl.multiple_of(jnp.maximum(i - 1, 0) * tq, tq)
            q = q_ref[cls, sub * tq:(sub + 1) * tq, :]
            scores = _dot_nt(q, per_head(k_ref[cls, pl.ds(k0, tk), :]))
            subs.append((cls, sub, i * tq - k0, k0, scores))
    first_bias = biases(subs[0][2])
    inner_bias = biases(tq) if nsub > 1 else None
    probs = []
    for cls, sub, off, k0, s in subs:
        bias = first_bias if sub == 0 else inner_bias
        ps, mxs = [], []
        for h in range(nh):
            sh = s[:, h * tk:(h + 1) * tk] + bias[h]
            mx = jnp.max(sh, axis=-1, keepdims=True)
            ps.append(jnp.exp2(sh - mx).astype(BF16))
            mxs.append(mx)
        probs.append((cls, sub, k0, jnp.concatenate(ps, axis=1), mxs))
    for cls, sub, k0, p, mxs in probs:
        den = _dot(p, ones_h)
        mx_all = mxs[nh - 1]
        for h in range(nh - 2, -1, -1):
            mx_all = jnp.where(q_head == h, mxs[h], mx_all)
        rows = slice(sub * tq, (sub + 1) * tq)
        o_ref[cls, rows, :] = _dot(p, per_head(v_ref[cls, pl.ds(k0, tk), :])) * (1.0 / den)
        lse_ref[cls, rows, :] = mx_all * LN2 + jnp.log(den)


def _banded(d, slopes):
    B, _, r, n, width = d.shape
    nsub = min(BAND_SUB, n // BAND_TQ)
    ncls = min(r, BAND_PROBLEMS // nsub)
    tq = BAND_TQ * nsub
    assert n >= BAND_TQ + BAND and n % tq == 0 and r % ncls == 0
    kern = functools.partial(_banded_kernel, slopes=tuple(slopes), nsub=nsub, ncls=ncls)
    tile = pl.BlockSpec((None, None, ncls, tq, width), lambda b, c, i: (b, 0, c, i, 0))
    kseq = pl.BlockSpec((None, None, ncls, n, width), lambda b, c, i: (b, 1, c, 0, 0))
    vseq = pl.BlockSpec((None, None, ncls, n, width), lambda b, c, i: (b, 2, c, 0, 0))
    out = pl.BlockSpec((None, ncls, tq, width), lambda b, c, i: (b, c, i, 0))
    return pl.pallas_call(
        kern,
        grid=(B, r // ncls, n // tq),
        in_specs=[tile, kseq, vseq],
        out_specs=(out, out),
        out_shape=(jax.ShapeDtypeStruct((B, r, n, width), F32),) * 2,
        compiler_params=_params("parallel", "parallel", "arbitrary"),
        name="banded_attention",
    )(d, d, d)


def _dilated(dils):
    slopes = _alibi_slopes(DIL_HEADS)
    outs, lses = [], []
    for gi, (w, r) in enumerate(DIL_CONFIGS):
        assert w // r == BAND
        sl = [s_ * r * LOG2E for s_ in slopes[gi * DIL_HEADS_PER_GROUP:(gi + 1) * DIL_HEADS_PER_GROUP]]
        o, lse = _banded(dils[gi], sl)
        outs.append(o)
        lses.append(lse)
    return outs, lses


MERGE_TM = 512


def _token_order(ref, slab_ref):
    r, rows, width = ref.shape
    if r == 1:
        return ref[0]
    for c in range(r):
        blk = ref[c]
        for s in range(width // LANES):
            slab_ref[s, pl.ds(c, rows, stride=r), :] = blk[:, s * LANES:(s + 1) * LANES]
    return jnp.concatenate([slab_ref[s] for s in range(width // LANES)], axis=1)


def _merge_kernel(x_ref, oa_ref, o0_ref, o1_ref, o2_ref, l0_ref, l1_ref, l2_ref,
                  gmix_ref, wm_ref, wpn_ref, wpd_ref, wo_ref, gffn_ref, x1_ref, h2_ref, slab_ref):
    x = x_ref[...]
    D = x.shape[1]
    h = _rms(x, gmix_ref[...]).astype(BF16)
    gm = _sigmoid(_dot_nt(h, wm_ref[...]))
    o0, o1, o2 = [_token_order(r_, slab_ref) for r_ in (o0_ref, o1_ref, o2_ref)]
    l0, l1, l2 = [_token_order(r_, slab_ref) for r_ in (l0_ref, l1_ref, l2_ref)]
    mx = jnp.maximum(jnp.maximum(l0, l1), l2)
    e0, e1, e2 = jnp.exp(l0 - mx), jnp.exp(l1 - mx), jnp.exp(l2 - mx)
    inv = 1.0 / (e0 + e1 + e2)
    ob = o0 * (e0 * inv) + o1 * (e1 * inv) + o2 * (e2 * inv)
    a = _dot(oa_ref[...], wpn_ref[...])
    d = _dot(ob.astype(BF16), wpd_ref[...])
    mixed = gm[:, :D] * a + gm[:, D:] * d
    x1 = x + _dot(mixed.astype(BF16), wo_ref[...])
    x1_ref[...] = x1
    h2_ref[...] = _rms(x1, gffn_ref[...]).astype(BF16)


def _merge(x, o_a, outs, lses, g_mix, w_merge, w_proj_nsa, w_proj_dil, w_out, g_ffn):
    B, S, D = x.shape
    tm = MERGE_TM
    row = lambda a: pl.BlockSpec((None, tm, a.shape[2]), lambda b, i: (b, i, 0))
    cls = lambda a: pl.BlockSpec((None, a.shape[1], tm // a.shape[1], a.shape[3]), lambda b, i: (b, 0, i, 0))
    full = lambda a: pl.BlockSpec(a.shape, lambda b, i: (0,) * a.ndim)
    ws = [w_merge.astype(BF16), w_proj_nsa.astype(BF16), w_proj_dil.astype(BF16), w_out.astype(BF16)]
    consts = [g_mix, *ws, g_ffn]
    in_specs = [row(x), row(o_a)] + [cls(a) for a in (*outs, *lses)] + [full(a) for a in consts]
    return pl.pallas_call(
        _merge_kernel,
        grid=(B, S // tm),
        in_specs=in_specs,
        out_specs=(pl.BlockSpec((None, tm, D), lambda b, i: (b, i, 0)),) * 2,
        out_shape=(jax.ShapeDtypeStruct((B, S, D), F32), jax.ShapeDtypeStruct((B, S, D), BF16)),
        scratch_shapes=[pltpu.VMEM((DIL_WIDTH // LANES, tm, LANES), F32)],
        compiler_params=_params("parallel", "parallel"),
        name="merge_proj",
    )(x, o_a, *outs, *lses, *consts)


FFN_TM = 512
FFN_TN = 256
HALO = 16


def _ffn_kernel(h_ref, halo_ref, x1_ref, wup_ref, cw_ref, cb_ref, wd_ref, gfin_ref, o_ref, act_ref):
    i = pl.program_id(1)
    h = h_ref[...]
    halo = halo_ref[...]
    tm = h.shape[0]
    row = lax.broadcasted_iota(jnp.int32, (tm, FFN_TN), 0)
    live = (i > 0).astype(F32)
    for j in range(D_FF // FFN_TN):
        cols = slice(j * FFN_TN, (j + 1) * FFN_TN)
        wu = wup_ref[:, cols]
        u = _dot(h, wu)
        uh = _dot(halo, wu) * live
        gate = _dot(h, wup_ref[:, D_FF + j * FFN_TN:D_FF + (j + 1) * FFN_TN])
        p1 = jnp.broadcast_to(uh[HALO - 1:HALO, :], (tm, FFN_TN))
        p2 = jnp.broadcast_to(uh[HALO - 2:HALO - 1, :], (tm, FFN_TN))
        u1 = jnp.where(row == 0, p1, pltpu.roll(u, 1, 0))
        u2 = jnp.where(row == 0, p2, jnp.where(row == 1, p1, pltpu.roll(u, 2, 0)))
        uc = cb_ref[:, cols] + cw_ref[0:1, cols] * u2
        uc = uc + cw_ref[1:2, cols] * u1
        uc = uc + cw_ref[2:3, cols] * u
        act_ref[:, j * FFN_TN:(j + 1) * FFN_TN] = (jax.nn.gelu(uc) * gate).astype(BF16)
    y = _dot(act_ref[...], wd_ref[...])
    o_ref[...] = _rms(x1_ref[...] + y, gfin_ref[...])


def _ffn(h2, x1, w_up, conv_w, conv_b, w_down, g_final):
    B, S, D = h2.shape
    tm = FFN_TM
    assert D_FF % FFN_TN == 0
    wup = w_up.astype(BF16)
    cw = conv_w
    cb = conv_b.reshape(1, D_FF)
    wd = w_down.astype(BF16)
    gfin = g_final.reshape(1, D)
    full = lambda a: pl.BlockSpec(a.shape, lambda b, i: (0,) * a.ndim, pipeline_mode=pl.Buffered(1))
    tile = pl.BlockSpec((None, tm, D), lambda b, i: (b, i, 0))
    halo = pl.BlockSpec((None, HALO, D), lambda b, i: (b, jnp.maximum(i * (tm // HALO) - 1, 0), 0))
    return pl.pallas_call(
        _ffn_kernel,
        grid=(B, S // tm),
        in_specs=[tile, halo, tile, full(wup), full(cw), full(cb), full(wd), full(gfin)],
        out_specs=tile,
        out_shape=jax.ShapeDtypeStruct((B, S, D), F32),
        scratch_shapes=[pltpu.VMEM((tm, D_FF), BF16)],
        compiler_params=_params("parallel", "parallel"),
        name="conv_ffn",
    )(h2, h2, x1, wup, cw, cb, wd, gfin)


@jax.jit
def _layer(x, g_mix, w_in, pe_cmp_k, w_cmp_k1, w_cmp_k2, pe_cmp_v, w_cmp_v1, w_cmp_v2,
           w_proj_nsa, w_proj_dil, w_out, g_ffn, w_up, conv_w, conv_b, w_down, g_final):
    B, S, D = x.shape
    depth = g_mix.shape[0]
    for l in range(depth):
        gm = g_mix[l].reshape(1, D)
        kcmp, vcmp, ks, kw, d0, d1, d2, qt, vst, vwt, gates = _in_proj(x, gm, w_in[l])
        kc, _ = _compress(kcmp, pe_cmp_k[l], w_cmp_k1[l], w_cmp_k2[l])
        _, vct = _compress(vcmp, pe_cmp_v[l], w_cmp_v1[l], w_cmp_v2[l])
        o_a = _nsa(qt, kc, vct, ks, vst, kw, vwt, gates)
        outs, lses = _dilated((d0, d1, d2))
        merge_cols = w_in[l].T[w_in.shape[2] - 2 * D:]
        x1, h2 = _merge(x, o_a, outs, lses, gm, merge_cols, w_proj_nsa[l], w_proj_dil[l], w_out[l],
                        g_ffn[l].reshape(1, D))
        x = _ffn(h2, x1, w_up[l], conv_w[l], conv_b[l], w_down[l], g_final)
        assert depth == 1
    return x


def kernel(x, g_mix, w_in, pe_cmp_k, w_cmp_k1, w_cmp_k2, pe_cmp_v, w_cmp_v1, w_cmp_v2, w_proj_nsa, w_proj_dil, w_out, g_ffn, w_up, conv_w, conv_b, w_down, g_final):
    return _layer(x, g_mix, w_in, pe_cmp_k, w_cmp_k1, w_cmp_k2, pe_cmp_v, w_cmp_v1, w_cmp_v2,
                  w_proj_nsa, w_proj_dil, w_out, g_ffn, w_up, conv_w, conv_b, w_down, g_final)
```

```python
import functools
import math

import numpy as np
import jax
import jax.numpy as jnp
from jax import lax
from jax.experimental import pallas as pl
from jax.experimental.pallas import tpu as pltpu

HEAD_DIM = 64
NSA_HEADS = 8
NSA_GROUPS = 2
NSA_REP = NSA_HEADS // NSA_GROUPS
CMP_BLOCK = 32
CMP_STRIDE = 16
CMP_HIDDEN = 128
SLC_BLOCK = 64
SLC_TOP = 16
NSA_WINDOW = 512
FORCE_SCORE = 1.0e4
DIL_CONFIGS = ((128, 1), (512, 4), (2048, 16))
DIL_GROUPS = 3
DIL_HEADS_PER_GROUP = 4
DIL_HEADS = DIL_GROUPS * DIL_HEADS_PER_GROUP
D_FF = 2816
CONV_WIDTH = 3
RMS_EPS = 1e-6
NEG_INF = -1e30

LANES = 128
VMEM_LIMIT_BYTES = 56 * 1024 * 1024

F32 = jnp.float32
BF16 = jnp.bfloat16
NT_DIMS = (((1,), (1,)), ((), ()))


def _alibi_slopes(n):
    return [float(2.0 ** (-8.0 * i / n)) for i in range(1, n + 1)]


def _rms(xf, g):
    ms = jnp.mean(xf * xf, axis=-1, keepdims=True)
    return xf * lax.rsqrt(ms + RMS_EPS) * g


def _dot(a, b):
    return jnp.dot(a, b, preferred_element_type=F32)


def _dot_nt(a, b):
    return lax.dot_general(a, b, NT_DIMS, preferred_element_type=F32)


def _sigmoid(z):
    return 1.0 / (1.0 + jnp.exp(-z))


def _params(*sem):
    return pltpu.CompilerParams(dimension_semantics=sem, vmem_limit_bytes=VMEM_LIMIT_BYTES)


IN_TM = 1024
N_KVC = 4 * HEAD_DIM
N_KSEL = NSA_GROUPS * HEAD_DIM
N_DIL = 3 * DIL_HEADS * HEAD_DIM
DIL_WIDTH = DIL_HEADS_PER_GROUP * HEAD_DIM
T_Q = NSA_HEADS * HEAD_DIM
T_V = NSA_GROUPS * HEAD_DIM
GATE_ROWS = 16
K_AUG = 2 * HEAD_DIM
SLC_SHIFT = int(math.log2(SLC_BLOCK))
AUG_NBLK = 8
AUG_HI, AUG_LO = AUG_NBLK, AUG_NBLK + 1
LOG2E = math.log2(math.e)
LN2 = math.log(2.0)
V_PAD = 16
V_ROWS = HEAD_DIM + V_PAD


def _key_position_columns(pos0, rows, step=1):
    pos = pos0 + step * lax.broadcasted_iota(jnp.int32, (rows, HEAD_DIM), 0)
    col = lax.broadcasted_iota(jnp.int32, (rows, HEAD_DIM), 1)
    blk = jnp.bitwise_and(lax.shift_right_logical(pos, SLC_SHIFT), AUG_NBLK - 1)
    hi = lax.shift_left(lax.shift_right_logical(pos, 7), 7).astype(F32)
    lo = jnp.bitwise_and(pos, 127).astype(F32)
    c = jnp.where((col == AUG_HI) | (col == AUG_HI + 2), hi,
                  jnp.where((col == AUG_LO) | (col == AUG_LO + 2), lo, 0.0))
    return jnp.where((col < AUG_NBLK) & (blk == col), 1.0, c)


def _in_proj_kernel(x_ref, g_ref, wn_ref, wt_ref,
                    kcmp_ref, vcmp_ref, ks_ref, kw_ref, d0_ref, d1_ref, d2_ref, qt_ref, vst_ref, vwt_ref, gate_ref,
                    slab_ref):
    tm = x_ref.shape[0]
    h = _rms(x_ref[...], g_ref[...]).astype(BF16)
    c0 = 0
    kvc = _dot_nt(h, wn_ref[c0:c0 + N_KVC, :])
    kcmp_ref[...] = kvc[:, :N_KSEL]
    vcmp_ref[...] = kvc[:, N_KSEL:]
    c0 += N_KVC
    ks = _dot_nt(h, wn_ref[c0:c0 + N_KSEL, :]).astype(BF16)
    c0 += N_KSEL
    kw = _dot_nt(h, wn_ref[c0:c0 + N_KSEL, :]).astype(BF16)
    c0 += N_KSEL
    aug = _key_position_columns(pl.program_id(1) * tm, tm).astype(BF16)
    for g in range(NSA_GROUPS):
        ks_ref[g] = jnp.concatenate([ks[:, g * HEAD_DIM:(g + 1) * HEAD_DIM], aug], axis=1)
        kw_ref[g] = jnp.concatenate([kw[:, g * HEAD_DIM:(g + 1) * HEAD_DIM], aug], axis=1)
    seg = DIL_HEADS * HEAD_DIM
    for which in range(3):
        y = _dot_nt(h, wn_ref[c0 + which * seg:c0 + (which + 1) * seg, :])
        for gi, (d_ref, (_, r)) in enumerate(zip((d0_ref, d1_ref, d2_ref), DIL_CONFIGS)):
            yg = y[:, gi * DIL_WIDTH:(gi + 1) * DIL_WIDTH]
            if r == 1:
                d_ref[which, 0] = yg.astype(BF16)
                continue
            for s in range(DIL_WIDTH // LANES):
                slab_ref[s] = yg[:, s * LANES:(s + 1) * LANES]
            for c in range(r):
                d_ref[which, c] = jnp.concatenate(
                    [slab_ref[s, pl.ds(c, tm // r, stride=r), :] for s in range(DIL_WIDTH // LANES)],
                    axis=1).astype(BF16)
    yt = _dot_nt(wt_ref[...], h)
    qt_ref[...] = yt[0:T_Q].astype(BF16)
    r0 = T_Q
    ones = jnp.where(lax.broadcasted_iota(jnp.int32, (V_PAD, tm), 0) == 0, 1.0, 0.0).astype(BF16)
    for ref in (vst_ref, vwt_ref):
        vt = yt[r0:r0 + T_V].astype(BF16)
        r0 += T_V
        for g in range(NSA_GROUPS):
            ref[g] = jnp.concatenate([vt[g * HEAD_DIM:(g + 1) * HEAD_DIM, :], ones], axis=0)
    gate_ref[...] = _sigmoid(yt[r0:r0 + NSA_GROUPS * GATE_ROWS])


def _in_proj(x, g_mix, w_in):
    B, S, D = x.shape
    scale = HEAD_DIM ** -0.5 * LOG2E
    o_q, o_kv = 0, T_Q
    o_gate = o_kv + 6 * N_KSEL
    o_dil = o_gate + 3 * NSA_HEADS
    o_merge = o_dil + N_DIL
    w_t = w_in.T
    kv = w_t[o_kv:o_gate]

    def kind(k):
        return kv[k * N_KSEL:(k + 1) * N_KSEL]

    dil = w_t[o_dil:o_merge]
    dil = jnp.concatenate([dil[:DIL_HEADS * HEAD_DIM] * scale, dil[DIL_HEADS * HEAD_DIM:]], axis=0)
    wn = jnp.concatenate([kind(0), kind(1), kind(2), kind(4), dil], axis=0).astype(BF16)
    wg = w_t[o_gate:o_dil].reshape(NSA_GROUPS, 3 * NSA_REP, D)
    wg = jnp.pad(wg, ((0, 0), (0, GATE_ROWS - 3 * NSA_REP), (0, 0))).reshape(NSA_GROUPS * GATE_ROWS, D)
    wt = jnp.concatenate([w_t[o_q:o_kv] * scale, kind(3), kind(5), wg], axis=0).astype(BF16)
    tm = IN_TM
    grid = (B, S // tm)
    full = lambda a: pl.BlockSpec(a.shape, lambda b, i: (0,) * a.ndim)
    k_shape = jax.ShapeDtypeStruct((B, NSA_GROUPS, S, K_AUG), BF16)
    v_shape = jax.ShapeDtypeStruct((B, NSA_GROUPS, V_ROWS, S), BF16)
    k_spec = pl.BlockSpec((None, NSA_GROUPS, tm, K_AUG), lambda b, i: (b, 0, i, 0))
    v_spec = pl.BlockSpec((None, NSA_GROUPS, V_ROWS, tm), lambda b, i: (b, 0, 0, i))
    c_shape = jax.ShapeDtypeStruct((B, S, N_KSEL), F32)
    c_spec = pl.BlockSpec((None, tm, N_KSEL), lambda b, i: (b, i, 0))
    d_shapes = tuple(jax.ShapeDtypeStruct((B, 3, r, S // r, DIL_WIDTH), BF16) for _, r in DIL_CONFIGS)
    d_specs = tuple(pl.BlockSpec((None, 3, r, tm // r, DIL_WIDTH), lambda b, i: (b, 0, 0, i, 0))
                    for _, r in DIL_CONFIGS)
    out_shape = (
        c_shape, c_shape, k_shape, k_shape, *d_shapes,
        jax.ShapeDtypeStruct((B, T_Q, S), BF16),
        v_shape,
        v_shape,
        jax.ShapeDtypeStruct((B, NSA_GROUPS * GATE_ROWS, S), F32),
    )
    out_specs = (
        c_spec, c_spec, k_spec, k_spec, *d_specs,
        pl.BlockSpec((None, T_Q, tm), lambda b, i: (b, 0, i)),
        v_spec,
        v_spec,
        pl.BlockSpec((None, NSA_GROUPS * GATE_ROWS, tm), lambda b, i: (b, 0, i)),
    )
    return pl.pallas_call(
        _in_proj_kernel,
        grid=grid,
        in_specs=[pl.BlockSpec((None, tm, D), lambda b, i: (b, i, 0)), full(g_mix), full(wn), full(wt)],
        out_specs=out_specs,
        out_shape=out_shape,
        scratch_shapes=[pltpu.VMEM((DIL_WIDTH // LANES, tm, LANES), F32)],
        compiler_params=_params("parallel", "parallel"),
        name="in_proj",
    )(x, g_mix, wn, wt)


def _compress_kernel(xk_ref, xv_ref, pek_ref, pev_ref, w1k_ref, w1v_ref, w2k_ref, w2vt_ref, kc_ref, vct_ref):
    nch = xk_ref.shape[0] // CMP_STRIDE
    streams = [(xk_ref, pek_ref, w1k_ref), (xv_ref, pev_ref, w1v_ref)]
    acc = [[jnp.zeros((nch, NSA_GROUPS * CMP_HIDDEN), F32) for _ in range(2)] for _ in streams]
    for j in range(CMP_STRIDE):
        for n, (x_ref, pe_ref, w1_ref) in enumerate(streams):
            xj = x_ref[pl.ds(j, nch, stride=CMP_STRIDE), :]
            for half in range(2):
                row = half * CMP_STRIDE + j
                acc[n][half] = acc[n][half] + _dot((xj + pe_ref[row:row + 1, :]).astype(BF16), w1_ref[row])
    hid = [jax.nn.gelu(lo + pltpu.roll(hi, nch - 1, 0)).astype(BF16) for lo, hi in acc]
    aug = _key_position_columns(CMP_BLOCK - 1, nch, CMP_STRIDE).astype(BF16)
    for g in range(NSA_GROUPS):
        cols = slice(g * CMP_HIDDEN, (g + 1) * CMP_HIDDEN)
        kc_ref[g] = jnp.concatenate([_dot(hid[0][:, cols], w2k_ref[...]).astype(BF16), aug], axis=1)
        vct_ref[g] = _dot_nt(w2vt_ref[...], hid[1][:, cols]).astype(BF16)


def _compress(xk, xv, pe_k, w1_k, w2_k, pe_v, w1_v, w2_v):
    B, S, width = xk.shape
    nch = S // CMP_STRIDE
    G, dh, hid = NSA_GROUPS, HEAD_DIM, CMP_HIDDEN
    eye = jnp.eye(G, dtype=w1_k.dtype)

    def expand(pe, w1):
        pe_t = jnp.broadcast_to(pe.reshape(CMP_BLOCK, 1, dh), (CMP_BLOCK, G, dh)).reshape(CMP_BLOCK, width)
        wexp = jnp.einsum('pdn,ge->pgden', w1.reshape(CMP_BLOCK, dh, hid), eye).reshape(CMP_BLOCK, width, G * hid)
        return pe_t, wexp.astype(BF16)

    pek, w1k = expand(pe_k, w1_k)
    pev, w1v = expand(pe_v, w1_v)
    consts = [pek, pev, w1k, w1v, w2_k.astype(BF16), w2_v.T.astype(BF16)]
    full = lambda a: pl.BlockSpec(a.shape, lambda b: (0,) * a.ndim)
    seq = pl.BlockSpec((None, S, width), lambda b: (b, 0, 0))
    return pl.pallas_call(
        _compress_kernel,
        grid=(B,),
        in_specs=[seq, seq] + [full(a) for a in consts],
        out_specs=(pl.BlockSpec((None, G, nch, K_AUG), lambda b: (b, 0, 0, 0)),
                   pl.BlockSpec((None, G, dh, nch), lambda b: (b, 0, 0, 0))),
        out_shape=(jax.ShapeDtypeStruct((B, G, nch, K_AUG), BF16),
                   jax.ShapeDtypeStruct((B, G, dh, nch), BF16)),
        compiler_params=_params("parallel"),
        name="compress",
    )(xk, xv, *consts)


NSA_TQ = 256
SWEEP_TK = AUG_NBLK * SLC_BLOCK
SEL_SUB = 128
SEL_AHEAD = 4


def _rowmax8(s):
    return jnp.max(s.reshape(s.shape[0] // 8, 8, s.shape[1]), axis=0)


def _online_softmax(chunks, scores, values):
    pending = [scores(ch) for ch in chunks[:SEL_AHEAD]]
    m = acc = None
    for n, ch in enumerate(chunks):
        if n + SEL_AHEAD < len(chunks):
            pending.append(scores(chunks[n + SEL_AHEAD]))
        s = pending.pop(0)
        m_c = jnp.max(_rowmax8(s), axis=0, keepdims=True)
        if m is None:
            m = m_c
            acc = _dot(values(ch), jnp.exp2(s - m).astype(BF16))
        else:
            m_new = jnp.maximum(m, m_c)
            acc = jnp.exp2(m - m_new) * acc + _dot(values(ch), jnp.exp2(s - m_new).astype(BF16))
            m = m_new
    return acc


def _nsa_kernel(qt_ref, kc_ref, vct_ref, ks_ref, vst_ref, kw_ref, vwt_ref, gate_ref, ov_ref,
                o_ref, sbt_ref, osel_ref, idx_ref, *, slopes):
    g = pl.program_id(1)
    i = pl.program_id(2)
    R, dh, tq, tk = NSA_REP, HEAD_DIM, NSA_TQ, SWEEP_TK
    L = R * tq
    t0 = i * tq
    nc = kc_ref.shape[0]
    ns = ov_ref.shape[0]
    n_tiles = ns // AUG_NBLK
    tile4 = lambda a: jnp.concatenate([a] * R, axis=1)

    qt = qt_ref[...]
    qs = jnp.concatenate([qt[r * dh:(r + 1) * dh, :] for r in range(R)], axis=1)
    slope = [jnp.where(g == 0, slopes[r], slopes[R + r]).astype(F32) for r in range(R)]
    slope_row = jnp.concatenate([jnp.full((1, tq), 1.0, F32) * slope[r] for r in range(R)], axis=1)
    t_row = t0 + lax.broadcasted_iota(jnp.int32, (1, tq), 1)

    r8 = lax.broadcasted_iota(jnp.int32, (AUG_NBLK, L), 0)
    s_full = slope_row * LOG2E
    s_hi = s_full.astype(BF16).astype(F32)
    alibi8 = jnp.where(r8 < 2, s_hi, jnp.where(r8 < 4, s_full - s_hi, 0.0))
    q_pad = jnp.zeros((K_AUG - dh - 2 * AUG_NBLK, L), BF16)
    q_plain = jnp.concatenate([qs, jnp.concatenate([jnp.zeros((AUG_NBLK, L), F32), alibi8], axis=0).astype(BF16),
                               q_pad], axis=0)

    cmp_end = lax.broadcasted_iota(jnp.int32, (nc, tq), 0) * CMP_STRIDE + (CMP_BLOCK - 1)
    m_cmp = t_row >= cmp_end
    sc = _dot(kc_ref[...], q_plain) + tile4(jnp.where(m_cmp, 0.0, NEG_INF))

    a0 = jnp.maximum(t0 - NSA_WINDOW, 0)
    kpos_w = lax.broadcasted_iota(jnp.int32, (SEL_SUB, tq), 0)
    t_w = t0 + lax.broadcasted_iota(jnp.int32, (SEL_SUB, tq), 1)

    def win_scores(ch):
        diag, r = ch
        k0 = pl.multiple_of((t0 if diag else a0) + r, SEL_SUB)
        kpos = k0 + kpos_w
        valid = (kpos <= t_w) if diag else ((kpos < t0) & (t_w - kpos < NSA_WINDOW))
        return _dot(kw_ref[pl.ds(k0, SEL_SUB), :], q_plain) + tile4(jnp.where(valid, 0.0, NEG_INF))

    def win_values(ch):
        diag, r = ch
        return vwt_ref[:, pl.ds(pl.multiple_of((t0 if diag else a0) + r, SEL_SUB), SEL_SUB)]

    win_chunks = ([(True, r) for r in range(0, tq, SEL_SUB)]
                  + [(False, r) for r in range(0, NSA_WINDOW, SEL_SUB)])
    acc_w = _online_softmax(win_chunks, win_scores, win_values)
    o_win = acc_w[:dh] * (1.0 / acc_w[dh:dh + 1])

    mx = jnp.max(_rowmax8(sc), axis=0, keepdims=True)
    p = jnp.exp2(sc - mx) * tile4(jnp.where(m_cmp, 1.0, 0.0))
    den = jnp.sum(jnp.sum(p.reshape(nc // 8, 8, L), axis=0), axis=0, keepdims=True)
    pr = p * (1.0 / jnp.maximum(den, 1e-30))
    o_cmp = _dot(vct_ref[...], pr.astype(BF16))
    psum = pr[:, 0:tq]
    for r in range(1, R):
        psum = psum + pr[:, r * tq:(r + 1) * tq]

    p_hi = psum.astype(BF16)
    p_lo = (psum - p_hi.astype(F32)).astype(BF16)
    imp = _dot(ov_ref[...], p_hi) + _dot(ov_ref[...], p_lo)
    blk = lax.broadcasted_iota(jnp.int32, (ns, tq), 0)
    cur = lax.shift_right_logical(t_row, SLC_SHIFT)
    val = jnp.where((blk == cur) | (blk == 0), FORCE_SCORE, imp)
    val = jnp.where(blk <= cur, val, -1.0)
    vals = [val[8 * v:8 * v + 8, :] for v in range(ns // 8)]
    ranks = [jnp.zeros((8, tq), F32) for _ in vals]
    row8 = lax.broadcasted_iota(jnp.int32, (8, tq), 0)
    for j in range(ns):
        vj = jnp.broadcast_to(val[j:j + 1, :], (8, tq))
        for v in range(len(vals)):
            if 8 * v > j:
                ahead = vj >= vals[v]
            elif 8 * v + 7 <= j:
                ahead = vj > vals[v]
            else:
                ahead = (vj > vals[v]) | ((vj == vals[v]) & (row8 > j - 8 * v))
            ranks[v] = ranks[v] + jnp.where(ahead, 1.0, 0.0)
    for T in range(n_tiles):
        selb = jnp.where((ranks[T] < float(SLC_TOP)) & (vals[T] >= 0.0), 0.0, NEG_INF)
        sbt_ref[T] = jnp.concatenate([tile4(selb), alibi8], axis=0).astype(BF16)

    td = lax.div(t0, tk)
    cnt = jnp.int32(0)
    for T in range(n_tiles - 1):
        picked = jnp.where((ranks[T] < float(SLC_TOP)) & (vals[T] >= 0.0), 1.0, 0.0)
        idx_ref[cnt] = jnp.int32(T)
        cnt = cnt + jnp.logical_and(jnp.max(picked) > 0.0, T < td).astype(jnp.int32)
    for k in range(n_tiles):
        @pl.when(cnt == k)
        def _(k=k):
            past = [idx_ref[j] for j in range(k)]
            q_past = [jnp.concatenate([qs, sbt_ref[T], q_pad], axis=0) for T in past]
            q_diag = jnp.concatenate([qs, sbt_ref[td], q_pad], axis=0)
            chunks = ([(j, r) for j in range(k) for r in range(0, tk, SEL_SUB)]
                      + [(None, r) for r in range(0, tk, SEL_SUB)])

            def key_start(ch):
                j, r = ch
                return pl.multiple_of((td if j is None else past[j]) * tk + r, SEL_SUB)

            def chunk_scores(ch):
                k0 = key_start(ch)
                if ch[0] is None:
                    causal = jnp.where(k0 + kpos_w <= t_w, 0.0, NEG_INF)
                    return _dot(ks_ref[pl.ds(k0, SEL_SUB), :], q_diag) + tile4(causal)
                return _dot(ks_ref[pl.ds(k0, SEL_SUB), :], q_past[ch[0]])

            acc = _online_softmax(chunks, chunk_scores, lambda ch: vst_ref[:, pl.ds(key_start(ch), SEL_SUB)])
            osel_ref[...] = acc[:dh] * (1.0 / acc[dh:dh + 1])

    o_sel = osel_ref[...]
    gates = gate_ref[...]
    tiles = []
    for r in range(R):
        sl = slice(r * tq, (r + 1) * tq)
        tiles.append(gates[3 * r:3 * r + 1, :] * o_cmp[:, sl]
                     + gates[3 * r + 1:3 * r + 2, :] * o_sel[:, sl]
                     + gates[3 * r + 2:3 * r + 3, :] * o_win[:, sl])
    o_ref[...] = jnp.concatenate(tiles, axis=0).T.astype(o_ref.dtype)


def _overlap_matrix(nc, ns):
    cs = np.arange(nc)[None, :] * CMP_STRIDE
    ss = np.arange(ns)[:, None] * SLC_BLOCK
    ov = np.clip(np.minimum(cs + CMP_BLOCK, ss + SLC_BLOCK) - np.maximum(cs, ss), 0, None)
    return jnp.asarray(ov.astype(np.float32) / CMP_BLOCK, dtype=BF16)


def _nsa(qt, kc, vct, ks, vst, kw, vwt, gates):
    B, _, S = qt.shape
    G, R, dh, tq = NSA_GROUPS, NSA_REP, HEAD_DIM, NSA_TQ
    nc = kc.shape[2]
    ns = S // SLC_BLOCK
    ov = _overlap_matrix(nc, ns)
    kern = functools.partial(_nsa_kernel, slopes=tuple(_alibi_slopes(NSA_HEADS)))
    assert S % SWEEP_TK == 0 and S >= NSA_WINDOW + tq
    per_bg = lambda shape: pl.BlockSpec((None, None) + shape, lambda b, g, i: (b, g, 0, 0))
    return pl.pallas_call(
        kern,
        grid=(B, G, S // tq),
        in_specs=[
            pl.BlockSpec((None, R * dh, tq), lambda b, g, i: (b, g, i)),
            per_bg((nc, K_AUG)), per_bg((dh, nc)),
            per_bg((S, K_AUG)), per_bg((V_ROWS, S)),
            per_bg((S, K_AUG)), per_bg((V_ROWS, S)),
            pl.BlockSpec((None, GATE_ROWS, tq), lambda b, g, i: (b, g, i)),
            pl.BlockSpec(ov.shape, lambda b, g, i: (0, 0)),
        ],
        out_specs=pl.BlockSpec((None, tq, R * dh), lambda b, g, i: (b, i, g)),
        out_shape=jax.ShapeDtypeStruct((B, S, G * R * dh), BF16),
        scratch_shapes=[pltpu.VMEM((S // SWEEP_TK, 2 * AUG_NBLK, R * tq), BF16),
                        pltpu.VMEM((dh, R * tq), F32),
                        pltpu.SMEM((S // SWEEP_TK,), jnp.int32)],
        compiler_params=_params("parallel", "parallel", "arbitrary"),
        name="nsa_attention",
    )(qt, kc, vct, ks, vst, kw, vwt, gates, ov)


BAND_TQ = 128
BAND_SUB = 8
BAND_PROBLEMS = 8
BAND = 128


def _banded_kernel(q_ref, k_ref, v_ref, o_ref, lse_ref, *, slopes, nsub, ncls):
    tq, dh, nh = BAND_TQ, HEAD_DIM, DIL_HEADS_PER_GROUP
    tk = tq + BAND
    width = nh * dh
    head_of_lane = lambda rows: lax.shift_right_logical(
        lax.broadcasted_iota(jnp.int32, (rows, width), 1), int(math.log2(dh)))
    lane_head, q_head = head_of_lane(tk), head_of_lane(tq)
    keep = [jnp.where(lane_head == h, 1.0, 0.0).astype(BF16) for h in range(nh)]

    def per_head(a):
        return jnp.concatenate([a * keep[h] for h in range(nh)], axis=0)

    ones_h = jnp.concatenate(keep, axis=0)
    def biases(first_key_offset):
        d = first_key_offset + (lax.broadcasted_iota(jnp.int32, (tq, tk), 0)
                                - lax.broadcasted_iota(jnp.int32, (tq, tk), 1))
        mask_bias = jnp.where((d >= 0) & (d <= BAND), 0.0, NEG_INF)
        neg_d = -d.astype(F32)
        return [slopes[h] * neg_d + mask_bias for h in range(nh)]

    subs = []
    for cls in range(ncls):
        for sub in range(nsub):
            i = pl.program_id(2) * nsub + sub
            k0 = pl.multiple_of(jnp.maximum(i - 1, 0) * tq, tq)
            q = q_ref[cls, sub * tq:(sub + 1) * tq, :]
            scores = _dot_nt(q, per_head(k_ref[cls, pl.ds(k0, tk), :]))
            subs.append((cls, sub, i * tq - k0, k0, scores))
    first_bias = biases(subs[0][2])
    inner_bias = biases(tq) if nsub > 1 else None
    probs = []
    for cls, sub, off, k0, s in subs:
        bias = first_bias if sub == 0 else inner_bias
        ps, mxs = [], []
        for h in range(nh):
            sh = s[:, h * tk:(h + 1) * tk] + bias[h]
            mx = jnp.max(sh, axis=-1, keepdims=True)
            ps.append(jnp.exp2(sh - mx).astype(BF16))
            mxs.append(mx)
        probs.append((cls, sub, k0, jnp.concatenate(ps, axis=1), mxs))
    for cls, sub, k0, p, mxs in probs:
        den = _dot(p, ones_h)
        mx_all = mxs[nh - 1]
        for h in range(nh - 2, -1, -1):
            mx_all = jnp.where(q_head == h, mxs[h], mx_all)
        rows = slice(sub * tq, (sub + 1) * tq)
        o_ref[cls, rows, :] = _dot(p, per_head(v_ref[cls, pl.ds(k0, tk), :])) * (1.0 / den)
        lse_ref[cls, rows, :] = mx_all * LN2 + jnp.log(den)


def _banded(d, slopes):
    B, _, r, n, width = d.shape
    nsub = min(BAND_SUB, n // BAND_TQ)
    ncls = min(r, BAND_PROBLEMS // nsub)
    tq = BAND_TQ * nsub
    assert n >= BAND_TQ + BAND and n % tq == 0 and r % ncls == 0
    kern = functools.partial(_banded_kernel, slopes=tuple(slopes), nsub=nsub, ncls=ncls)
    tile = pl.BlockSpec((None, None, ncls, tq, width), lambda b, c, i: (b, 0, c, i, 0))
    kseq = pl.BlockSpec((None, None, ncls, n, width), lambda b, c, i: (b, 1, c, 0, 0))
    vseq = pl.BlockSpec((None, None, ncls, n, width), lambda b, c, i: (b, 2, c, 0, 0))
    out = pl.BlockSpec((None, ncls, tq, width), lambda b, c, i: (b, c, i, 0))
    return pl.pallas_call(
        kern,
        grid=(B, r // ncls, n // tq),
        in_specs=[tile, kseq, vseq],
        out_specs=(out, out),
        out_shape=(jax.ShapeDtypeStruct((B, r, n, width), F32),) * 2,
        compiler_params=_params("parallel", "parallel", "arbitrary"),
        name="banded_attention",
    )(d, d, d)


def _dilated(dils):
    slopes = _alibi_slopes(DIL_HEADS)
    outs, lses = [], []
    for gi, (w, r) in enumerate(DIL_CONFIGS):
        assert w // r == BAND
        sl = [s_ * r * LOG2E for s_ in slopes[gi * DIL_HEADS_PER_GROUP:(gi + 1) * DIL_HEADS_PER_GROUP]]
        o, lse = _banded(dils[gi], sl)
        outs.append(o)
        lses.append(lse)
    return outs, lses


MERGE_TM = 512


def _token_order(ref, slab_ref):
    r, rows, width = ref.shape
    if r == 1:
        return ref[0]
    for c in range(r):
        blk = ref[c]
        for s in range(width // LANES):
            slab_ref[s, pl.ds(c, rows, stride=r), :] = blk[:, s * LANES:(s + 1) * LANES]
    return jnp.concatenate([slab_ref[s] for s in range(width // LANES)], axis=1)


def _merge_kernel(x_ref, oa_ref, o0_ref, o1_ref, o2_ref, l0_ref, l1_ref, l2_ref,
                  gmix_ref, wm_ref, wpn_ref, wpd_ref, wo_ref, gffn_ref, x1_ref, h2_ref, slab_ref):
    x = x_ref[...]
    D = x.shape[1]
    h = _rms(x, gmix_ref[...]).astype(BF16)
    gm = _sigmoid(_dot_nt(h, wm_ref[...]))
    o0, o1, o2 = [_token_order(r_, slab_ref) for r_ in (o0_ref, o1_ref, o2_ref)]
    l0, l1, l2 = [_token_order(r_, slab_ref) for r_ in (l0_ref, l1_ref, l2_ref)]
    mx = jnp.maximum(jnp.maximum(l0, l1), l2)
    e0, e1, e2 = jnp.exp(l0 - mx), jnp.exp(l1 - mx), jnp.exp(l2 - mx)
    inv = 1.0 / (e0 + e1 + e2)
    ob = o0 * (e0 * inv) + o1 * (e1 * inv) + o2 * (e2 * inv)
    a = _dot(oa_ref[...], wpn_ref[...])
    d = _dot(ob.astype(BF16), wpd_ref[...])
    mixed = gm[:, :D] * a + gm[:, D:] * d
    x1 = x + _dot(mixed.astype(BF16), wo_ref[...])
    x1_ref[...] = x1
    h2_ref[...] = _rms(x1, gffn_ref[...]).astype(BF16)


def _merge(x, o_a, outs, lses, g_mix, w_merge, w_proj_nsa, w_proj_dil, w_out, g_ffn):
    B, S, D = x.shape
    tm = MERGE_TM
    row = lambda a: pl.BlockSpec((None, tm, a.shape[2]), lambda b, i: (b, i, 0))
    cls = lambda a: pl.BlockSpec((None, a.shape[1], tm // a.shape[1], a.shape[3]), lambda b, i: (b, 0, i, 0))
    full = lambda a: pl.BlockSpec(a.shape, lambda b, i: (0,) * a.ndim)
    ws = [w_merge.astype(BF16), w_proj_nsa.astype(BF16), w_proj_dil.astype(BF16), w_out.astype(BF16)]
    consts = [g_mix, *ws, g_ffn]
    in_specs = [row(x), row(o_a)] + [cls(a) for a in (*outs, *lses)] + [full(a) for a in consts]
    return pl.pallas_call(
        _merge_kernel,
        grid=(B, S // tm),
        in_specs=in_specs,
        out_specs=(pl.BlockSpec((None, tm, D), lambda b, i: (b, i, 0)),) * 2,
        out_shape=(jax.ShapeDtypeStruct((B, S, D), F32), jax.ShapeDtypeStruct((B, S, D), BF16)),
        scratch_shapes=[pltpu.VMEM((DIL_WIDTH // LANES, tm, LANES), F32)],
        compiler_params=_params("parallel", "parallel"),
        name="merge_proj",
    )(x, o_a, *outs, *lses, *consts)


FFN_TM = 512
FFN_TN = 256
HALO = 16


def _ffn_kernel(h_ref, halo_ref, x1_ref, wup_ref, cw_ref, cb_ref, wd_ref, gfin_ref, o_ref, act_ref):
    i = pl.program_id(1)
    h = h_ref[...]
    halo = halo_ref[...]
    tm = h.shape[0]
    row = lax.broadcasted_iota(jnp.int32, (tm, FFN_TN), 0)
    live = (i > 0).astype(F32)
    for j in range(D_FF // FFN_TN):
        cols = slice(j * FFN_TN, (j + 1) * FFN_TN)
        wu = wup_ref[:, cols]
        u = _dot(h, wu)
        uh = _dot(halo, wu) * live
        gate = _dot(h, wup_ref[:, D_FF + j * FFN_TN:D_FF + (j + 1) * FFN_TN])
        p1 = jnp.broadcast_to(uh[HALO - 1:HALO, :], (tm, FFN_TN))
        p2 = jnp.broadcast_to(uh[HALO - 2:HALO - 1, :], (tm, FFN_TN))
        u1 = jnp.where(row == 0, p1, pltpu.roll(u, 1, 0))
        u2 = jnp.where(row == 0, p2, jnp.where(row == 1, p1, pltpu.roll(u, 2, 0)))
        uc = cb_ref[:, cols] + cw_ref[0:1, cols] * u2
        uc = uc + cw_ref[1:2, cols] * u1
        uc = uc + cw_ref[2:3, cols] * u
        act_ref[:, j * FFN_TN:(j + 1) * FFN_TN] = (jax.nn.gelu(uc) * gate).astype(BF16)
    y = _dot(act_ref[...], wd_ref[...])
    o_ref[...] = _rms(x1_ref[...] + y, gfin_ref[...])


def _ffn(h2, x1, w_up, conv_w, conv_b, w_down, g_final):
    B, S, D = h2.shape
    tm = FFN_TM
    assert D_FF % FFN_TN == 0
    wup = w_up.astype(BF16)
    cw = conv_w
    cb = conv_b.reshape(1, D_FF)
    wd = w_down.astype(BF16)
    gfin = g_final.reshape(1, D)
    full = lambda a: pl.BlockSpec(a.shape, lambda b, i: (0,) * a.ndim, pipeline_mode=pl.Buffered(1))
    tile = pl.BlockSpec((None, tm, D), lambda b, i: (b, i, 0))
    halo = pl.BlockSpec((None, HALO, D), lambda b, i: (b, jnp.maximum(i * (tm // HALO) - 1, 0), 0))
    return pl.pallas_call(
        _ffn_kernel,
        grid=(B, S // tm),
        in_specs=[tile, halo, tile, full(wup), full(cw), full(cb), full(wd), full(gfin)],
        out_specs=tile,
        out_shape=jax.ShapeDtypeStruct((B, S, D), F32),
        scratch_shapes=[pltpu.VMEM((tm, D_FF), BF16)],
        compiler_params=_params("parallel", "parallel"),
        name="conv_ffn",
    )(h2, h2, x1, wup, cw, cb, wd, gfin)


@jax.jit
def _layer(x, g_mix, w_in, pe_cmp_k, w_cmp_k1, w_cmp_k2, pe_cmp_v, w_cmp_v1, w_cmp_v2,
           w_proj_nsa, w_proj_dil, w_out, g_ffn, w_up, conv_w, conv_b, w_down, g_final):
    B, S, D = x.shape
    depth = g_mix.shape[0]
    for l in range(depth):
        gm = g_mix[l].reshape(1, D)
        kcmp, vcmp, ks, kw, d0, d1, d2, qt, vst, vwt, gates = _in_proj(x, gm, w_in[l])
        kc, vct = _compress(kcmp, vcmp, pe_cmp_k[l], w_cmp_k1[l], w_cmp_k2[l],
                            pe_cmp_v[l], w_cmp_v1[l], w_cmp_v2[l])
        o_a = _nsa(qt, kc, vct, ks, vst, kw, vwt, gates)
        outs, lses = _dilated((d0, d1, d2))
        merge_cols = w_in[l].T[w_in.shape[2] - 2 * D:]
        x1, h2 = _merge(x, o_a, outs, lses, gm, merge_cols, w_proj_nsa[l], w_proj_dil[l], w_out[l],
                        g_ffn[l].reshape(1, D))
        x = _ffn(h2, x1, w_up[l], conv_w[l], conv_b[l], w_down[l], g_final)
        assert depth == 1
    return x


def kernel(x, g_mix, w_in, pe_cmp_k, w_cmp_k1, w_cmp_k2, pe_cmp_v, w_cmp_v1, w_cmp_v2, w_proj_nsa, w_proj_dil, w_out, g_ffn, w_up, conv_w, conv_b, w_down, g_final):
    return _layer(x, g_mix, w_in, pe_cmp_k, w_cmp_k1, w_cmp_k2, pe_cmp_v, w_cmp_v1, w_cmp_v2,
                  w_proj_nsa, w_proj_dil, w_out, g_ffn, w_up, conv_w, conv_b, w_down, g_final)
```

```python
import functools
import math

import numpy as np
import jax
import jax.numpy as jnp
from jax import lax
from jax.experimental import pallas as pl
from jax.experimental.pallas import tpu as pltpu

HEAD_DIM = 64
NSA_HEADS = 8
NSA_GROUPS = 2
NSA_REP = NSA_HEADS // NSA_GROUPS
CMP_BLOCK = 32
CMP_STRIDE = 16
CMP_HIDDEN = 128
SLC_BLOCK = 64
SLC_TOP = 16
NSA_WINDOW = 512
FORCE_SCORE = 1.0e4
DIL_CONFIGS = ((128, 1), (512, 4), (2048, 16))
DIL_GROUPS = 3
DIL_HEADS_PER_GROUP = 4
DIL_HEADS = DIL_GROUPS * DIL_HEADS_PER_GROUP
D_FF = 2816
CONV_WIDTH = 3
RMS_EPS = 1e-6
NEG_INF = -1e30

LANES = 128
VMEM_LIMIT_BYTES = 56 * 1024 * 1024

F32 = jnp.float32
BF16 = jnp.bfloat16
NT_DIMS = (((1,), (1,)), ((), ()))


def _alibi_slopes(n):
    return [float(2.0 ** (-8.0 * i / n)) for i in range(1, n + 1)]


def _rms(xf, g):
    ms = jnp.mean(xf * xf, axis=-1, keepdims=True)
    return xf * lax.rsqrt(ms + RMS_EPS) * g


def _dot(a, b):
    return jnp.dot(a, b, preferred_element_type=F32)


def _dot_nt(a, b):
    return lax.dot_general(a, b, NT_DIMS, preferred_element_type=F32)


def _sigmoid(z):
    return 1.0 / (1.0 + jnp.exp(-z))


def _params(*sem):
    return pltpu.CompilerParams(dimension_semantics=sem, vmem_limit_bytes=VMEM_LIMIT_BYTES)


IN_TM = 1024
N_KVC = 4 * HEAD_DIM
N_KSEL = NSA_GROUPS * HEAD_DIM
N_DIL = 3 * DIL_HEADS * HEAD_DIM
DIL_WIDTH = DIL_HEADS_PER_GROUP * HEAD_DIM
T_Q = NSA_HEADS * HEAD_DIM
T_V = NSA_GROUPS * HEAD_DIM
GATE_ROWS = 16
K_AUG = 2 * HEAD_DIM
SLC_SHIFT = int(math.log2(SLC_BLOCK))
AUG_NBLK = 8
AUG_HI, AUG_LO = AUG_NBLK, AUG_NBLK + 1
LOG2E = math.log2(math.e)
LN2 = math.log(2.0)
V_PAD = 16
V_ROWS = HEAD_DIM + V_PAD


def _key_position_columns(pos0, rows, step=1):
    pos = pos0 + step * lax.broadcasted_iota(jnp.int32, (rows, HEAD_DIM), 0)
    col = lax.broadcasted_iota(jnp.int32, (rows, HEAD_DIM), 1)
    blk = jnp.bitwise_and(lax.shift_right_logical(pos, SLC_SHIFT), AUG_NBLK - 1)
    hi = lax.shift_left(lax.shift_right_logical(pos, 7), 7).astype(F32)
    lo = jnp.bitwise_and(pos, 127).astype(F32)
    c = jnp.where((col == AUG_HI) | (col == AUG_HI + 2), hi,
                  jnp.where((col == AUG_LO) | (col == AUG_LO + 2), lo, 0.0))
    return jnp.where((col < AUG_NBLK) & (blk == col), 1.0, c)


def _in_proj_kernel(x_ref, g_ref, wn_ref, wt_ref,
                    kcmp_ref, vcmp_ref, ks_ref, kw_ref, d0_ref, d1_ref, d2_ref, qt_ref, vst_ref, vwt_ref, gate_ref,
                    slab_ref):
    tm = x_ref.shape[0]
    h = _rms(x_ref[...], g_ref[...]).astype(BF16)
    c0 = 0
    kvc = _dot_nt(h, wn_ref[c0:c0 + N_KVC, :])
    kcmp_ref[...] = kvc[:, :N_KSEL]
    vcmp_ref[...] = kvc[:, N_KSEL:]
    c0 += N_KVC
    ks = _dot_nt(h, wn_ref[c0:c0 + N_KSEL, :]).astype(BF16)
    c0 += N_KSEL
    kw = _dot_nt(h, wn_ref[c0:c0 + N_KSEL, :]).astype(BF16)
    c0 += N_KSEL
    aug = _key_position_columns(pl.program_id(1) * tm, tm).astype(BF16)
    for g in range(NSA_GROUPS):
        ks_ref[g] = jnp.concatenate([ks[:, g * HEAD_DIM:(g + 1) * HEAD_DIM], aug], axis=1)
        kw_ref[g] = jnp.concatenate([kw[:, g * HEAD_DIM:(g + 1) * HEAD_DIM], aug], axis=1)
    seg = DIL_HEADS * HEAD_DIM
    for which in range(3):
        y = _dot_nt(h, wn_ref[c0 + which * seg:c0 + (which + 1) * seg, :])
        for gi, (d_ref, (_, r)) in enumerate(zip((d0_ref, d1_ref, d2_ref), DIL_CONFIGS)):
            yg = y[:, gi * DIL_WIDTH:(gi + 1) * DIL_WIDTH]
            if r == 1:
                d_ref[which, 0] = yg.astype(BF16)
                continue
            for s in range(DIL_WIDTH // LANES):
                slab_ref[s] = yg[:, s * LANES:(s + 1) * LANES]
            for c in range(r):
                d_ref[which, c] = jnp.concatenate(
                    [slab_ref[s, pl.ds(c, tm // r, stride=r), :] for s in range(DIL_WIDTH // LANES)],
                    axis=1).astype(BF16)
    yt = _dot_nt(wt_ref[...], h)
    qt_ref[...] = yt[0:T_Q].astype(BF16)
    r0 = T_Q
    ones = jnp.where(lax.broadcasted_iota(jnp.int32, (V_PAD, tm), 0) == 0, 1.0, 0.0).astype(BF16)
    for ref in (vst_ref, vwt_ref):
        vt = yt[r0:r0 + T_V].astype(BF16)
        r0 += T_V
        for g in range(NSA_GROUPS):
            ref[g] = jnp.concatenate([vt[g * HEAD_DIM:(g + 1) * HEAD_DIM, :], ones], axis=0)
    gate_ref[...] = _sigmoid(yt[r0:r0 + NSA_GROUPS * GATE_ROWS])


def _in_proj(x, g_mix, w_in):
    B, S, D = x.shape
    scale = HEAD_DIM ** -0.5 * LOG2E
    o_q, o_kv = 0, T_Q
    o_gate = o_kv + 6 * N_KSEL
    o_dil = o_gate + 3 * NSA_HEADS
    o_merge = o_dil + N_DIL
    w_t = w_in.T
    kv = w_t[o_kv:o_gate]

    def kind(k):
        return kv[k * N_KSEL:(k + 1) * N_KSEL]

    dil = w_t[o_dil:o_merge]
    dil = jnp.concatenate([dil[:DIL_HEADS * HEAD_DIM] * scale, dil[DIL_HEADS * HEAD_DIM:]], axis=0)
    wn = jnp.concatenate([kind(0), kind(1), kind(2), kind(4), dil], axis=0).astype(BF16)
    wg = w_t[o_gate:o_dil].reshape(NSA_GROUPS, 3 * NSA_REP, D)
    wg = jnp.pad(wg, ((0, 0), (0, GATE_ROWS - 3 * NSA_REP), (0, 0))).reshape(NSA_GROUPS * GATE_ROWS, D)
    wt = jnp.concatenate([w_t[o_q:o_kv] * scale, kind(3), kind(5), wg], axis=0).astype(BF16)
    tm = IN_TM
    grid = (B, S // tm)
    full = lambda a: pl.BlockSpec(a.shape, lambda b, i: (0,) * a.ndim)
    k_shape = jax.ShapeDtypeStruct((B, NSA_GROUPS, S, K_AUG), BF16)
    v_shape = jax.ShapeDtypeStruct((B, NSA_GROUPS, V_ROWS, S), BF16)
    k_spec = pl.BlockSpec((None, NSA_GROUPS, tm, K_AUG), lambda b, i: (b, 0, i, 0))
    v_spec = pl.BlockSpec((None, NSA_GROUPS, V_ROWS, tm), lambda b, i: (b, 0, 0, i))
    c_shape = jax.ShapeDtypeStruct((B, S, N_KSEL), F32)
    c_spec = pl.BlockSpec((None, tm, N_KSEL), lambda b, i: (b, i, 0))
    d_shapes = tuple(jax.ShapeDtypeStruct((B, 3, r, S // r, DIL_WIDTH), BF16) for _, r in DIL_CONFIGS)
    d_specs = tuple(pl.BlockSpec((None, 3, r, tm // r, DIL_WIDTH), lambda b, i: (b, 0, 0, i, 0))
                    for _, r in DIL_CONFIGS)
    out_shape = (
        c_shape, c_shape, k_shape, k_shape, *d_shapes,
        jax.ShapeDtypeStruct((B, T_Q, S), BF16),
        v_shape,
        v_shape,
        jax.ShapeDtypeStruct((B, NSA_GROUPS * GATE_ROWS, S), F32),
    )
    out_specs = (
        c_spec, c_spec, k_spec, k_spec, *d_specs,
        pl.BlockSpec((None, T_Q, tm), lambda b, i: (b, 0, i)),
        v_spec,
        v_spec,
        pl.BlockSpec((None, NSA_GROUPS * GATE_ROWS, tm), lambda b, i: (b, 0, i)),
    )
    return pl.pallas_call(
        _in_proj_kernel,
        grid=grid,
        in_specs=[pl.BlockSpec((None, tm, D), lambda b, i: (b, i, 0)), full(g_mix), full(wn), full(wt)],
        out_specs=out_specs,
        out_shape=out_shape,
        scratch_shapes=[pltpu.VMEM((DIL_WIDTH // LANES, tm, LANES), F32)],
        compiler_params=_params("parallel", "parallel"),
        name="in_proj",
    )(x, g_mix, wn, wt)


def _compress_kernel(xk_ref, xv_ref, pek_ref, pev_ref, w1k_ref, w1v_ref, w2k_ref, w2vt_ref, kc_ref, vct_ref):
    nch = xk_ref.shape[0] // CMP_STRIDE
    streams = [(xk_ref, pek_ref, w1k_ref), (xv_ref, pev_ref, w1v_ref)]
    acc = [[jnp.zeros((nch, NSA_GROUPS * CMP_HIDDEN), F32) for _ in range(2)] for _ in streams]
    for j in range(CMP_STRIDE):
        for n, (x_ref, pe_ref, w1_ref) in enumerate(streams):
            xj = x_ref[pl.ds(j, nch, stride=CMP_STRIDE), :]
            for half in range(2):
                row = half * CMP_STRIDE + j
                acc[n][half] = acc[n][half] + _dot((xj + pe_ref[row:row + 1, :]).astype(BF16), w1_ref[row])
    hid = [jax.nn.gelu(lo + pltpu.roll(hi, nch - 1, 0)).astype(BF16) for lo, hi in acc]
    aug = _key_position_columns(CMP_BLOCK - 1, nch, CMP_STRIDE).astype(BF16)
    for g in range(NSA_GROUPS):
        cols = slice(g * CMP_HIDDEN, (g + 1) * CMP_HIDDEN)
        kc_ref[g] = jnp.concatenate([_dot(hid[0][:, cols], w2k_ref[...]).astype(BF16), aug], axis=1)
        vct_ref[g] = _dot_nt(w2vt_ref[...], hid[1][:, cols]).astype(BF16)


def _compress(xk, xv, pe_k, w1_k, w2_k, pe_v, w1_v, w2_v):
    B, S, width = xk.shape
    nch = S // CMP_STRIDE
    G, dh, hid = NSA_GROUPS, HEAD_DIM, CMP_HIDDEN
    eye = jnp.eye(G, dtype=w1_k.dtype)

    def expand(pe, w1):
        pe_t = jnp.broadcast_to(pe.reshape(CMP_BLOCK, 1, dh), (CMP_BLOCK, G, dh)).reshape(CMP_BLOCK, width)
        wexp = jnp.einsum('pdn,ge->pgden', w1.reshape(CMP_BLOCK, dh, hid), eye).reshape(CMP_BLOCK, width, G * hid)
        return pe_t, wexp.astype(BF16)

    pek, w1k = expand(pe_k, w1_k)
    pev, w1v = expand(pe_v, w1_v)
    consts = [pek, pev, w1k, w1v, w2_k.astype(BF16), w2_v.T.astype(BF16)]
    full = lambda a: pl.BlockSpec(a.shape, lambda b: (0,) * a.ndim)
    seq = pl.BlockSpec((None, S, width), lambda b: (b, 0, 0))
    return pl.pallas_call(
        _compress_kernel,
        grid=(B,),
        in_specs=[seq, seq] + [full(a) for a in consts],
        out_specs=(pl.BlockSpec((None, G, nch, K_AUG), lambda b: (b, 0, 0, 0)),
                   pl.BlockSpec((None, G, dh, nch), lambda b: (b, 0, 0, 0))),
        out_shape=(jax.ShapeDtypeStruct((B, G, nch, K_AUG), BF16),
                   jax.ShapeDtypeStruct((B, G, dh, nch), BF16)),
        compiler_params=_params("parallel"),
        name="compress",
    )(xk, xv, *consts)


NSA_TQ = 256
SWEEP_TK = AUG_NBLK * SLC_BLOCK
SEL_SUB = 128
SEL_AHEAD = 4


def _rowmax8(s):
    return jnp.max(s.reshape(s.shape[0] // 8, 8, s.shape[1]), axis=0)


def _online_softmax(chunks, scores, values):
    pending = [scores(ch) for ch in chunks[:SEL_AHEAD]]
    m = acc = None
    for n, ch in enumerate(chunks):
        if n + SEL_AHEAD < len(chunks):
            pending.append(scores(chunks[n + SEL_AHEAD]))
        s = pending.pop(0)
        m_c = jnp.max(_rowmax8(s), axis=0, keepdims=True)
        if m is None:
            m = m_c
            acc = _dot(values(ch), jnp.exp2(s - m).astype(BF16))
        else:
            m_new = jnp.maximum(m, m_c)
            acc = jnp.exp2(m - m_new) * acc + _dot(values(ch), jnp.exp2(s - m_new).astype(BF16))
            m = m_new
    return acc


def _nsa_kernel(qt_ref, kc_ref, vct_ref, ks_ref, vst_ref, kw_ref, vwt_ref, gate_ref, ov_ref,
                o_ref, sbt_ref, osel_ref, idx_ref, *, slopes):
    g = pl.program_id(1)
    i = pl.program_id(2)
    R, dh, tq, tk = NSA_REP, HEAD_DIM, NSA_TQ, SWEEP_TK
    L = R * tq
    t0 = i * tq
    nc = kc_ref.shape[0]
    ns = ov_ref.shape[0]
    n_tiles = ns // AUG_NBLK
    tile4 = lambda a: jnp.concatenate([a] * R, axis=1)

    qt = qt_ref[...]
    qs = jnp.concatenate([qt[r * dh:(r + 1) * dh, :] for r in range(R)], axis=1)
    slope = [jnp.where(g == 0, slopes[r], slopes[R + r]).astype(F32) for r in range(R)]
    slope_row = jnp.concatenate([jnp.full((1, tq), 1.0, F32) * slope[r] for r in range(R)], axis=1)
    t_row = t0 + lax.broadcasted_iota(jnp.int32, (1, tq), 1)

    r8 = lax.broadcasted_iota(jnp.int32, (AUG_NBLK, L), 0)
    s_full = slope_row * LOG2E
    s_hi = s_full.astype(BF16).astype(F32)
    alibi8 = jnp.where(r8 < 2, s_hi, jnp.where(r8 < 4, s_full - s_hi, 0.0))
    q_pad = jnp.zeros((K_AUG - dh - 2 * AUG_NBLK, L), BF16)
    q_plain = jnp.concatenate([qs, jnp.concatenate([jnp.zeros((AUG_NBLK, L), F32), alibi8], axis=0).astype(BF16),
                               q_pad], axis=0)

    cmp_end = lax.broadcasted_iota(jnp.int32, (nc, tq), 0) * CMP_STRIDE + (CMP_BLOCK - 1)
    m_cmp = t_row >= cmp_end
    sc = _dot(kc_ref[...], q_plain) + tile4(jnp.where(m_cmp, 0.0, NEG_INF))

    a0 = jnp.maximum(t0 - NSA_WINDOW, 0)
    kpos_w = lax.broadcasted_iota(jnp.int32, (SEL_SUB, tq), 0)
    t_w = t0 + lax.broadcasted_iota(jnp.int32, (SEL_SUB, tq), 1)

    def win_scores(ch):
        diag, r = ch
        k0 = pl.multiple_of((t0 if diag else a0) + r, SEL_SUB)
        kpos = k0 + kpos_w
        valid = (kpos <= t_w) if diag else ((kpos < t0) & (t_w - kpos < NSA_WINDOW))
        return _dot(kw_ref[pl.ds(k0, SEL_SUB), :], q_plain) + tile4(jnp.where(valid, 0.0, NEG_INF))

    def win_values(ch):
        diag, r = ch
        return vwt_ref[:, pl.ds(pl.multiple_of((t0 if diag else a0) + r, SEL_SUB), SEL_SUB)]

    win_chunks = ([(True, r) for r in range(0, tq, SEL_SUB)]
                  + [(False, r) for r in range(0, NSA_WINDOW, SEL_SUB)])
    acc_w = _online_softmax(win_chunks, win_scores, win_values)
    o_win = acc_w[:dh] * (1.0 / acc_w[dh:dh + 1])

    mx = jnp.max(_rowmax8(sc), axis=0, keepdims=True)
    p = jnp.exp2(sc - mx) * tile4(jnp.where(m_cmp, 1.0, 0.0))
    den = jnp.sum(jnp.sum(p.reshape(nc // 8, 8, L), axis=0), axis=0, keepdims=True)
    pr = p * (1.0 / jnp.maximum(den, 1e-30))
    o_cmp = _dot(vct_ref[...], pr.astype(BF16))
    psum = pr[:, 0:tq]
    for r in range(1, R):
        psum = psum + pr[:, r * tq:(r + 1) * tq]

    p_hi = psum.astype(BF16)
    p_lo = (psum - p_hi.astype(F32)).astype(BF16)
    imp = _dot(ov_ref[...], p_hi) + _dot(ov_ref[...], p_lo)
    blk = lax.broadcasted_iota(jnp.int32, (ns, tq), 0)
    cur = lax.shift_right_logical(t_row, SLC_SHIFT)
    val = jnp.where((blk == cur) | (blk == 0), FORCE_SCORE, imp)
    val = jnp.where(blk <= cur, val, -1.0)
    vals = [val[8 * v:8 * v + 8, :] for v in range(ns // 8)]
    ranks = [jnp.zeros((8, tq), F32) for _ in vals]
    row8 = lax.broadcasted_iota(jnp.int32, (8, tq), 0)
    for j in range(ns):
        vj = jnp.broadcast_to(val[j:j + 1, :], (8, tq))
        for v in range(len(vals)):
            if 8 * v > j:
                ahead = vj >= vals[v]
            elif 8 * v + 7 <= j:
                ahead = vj > vals[v]
            else:
                ahead = (vj > vals[v]) | ((vj == vals[v]) & (row8 > j - 8 * v))
            ranks[v] = ranks[v] + jnp.where(ahead, 1.0, 0.0)
    for T in range(n_tiles):
        selb = jnp.where((ranks[T] < float(SLC_TOP)) & (vals[T] >= 0.0), 0.0, NEG_INF)
        sbt_ref[T] = jnp.concatenate([tile4(selb), alibi8], axis=0).astype(BF16)

    td = lax.div(t0, tk)
    cnt = jnp.int32(0)
    for T in range(n_tiles - 1):
        picked = jnp.where((ranks[T] < float(SLC_TOP)) & (vals[T] >= 0.0), 1.0, 0.0)
        idx_ref[cnt] = jnp.int32(T)
        cnt = cnt + jnp.logical_and(jnp.max(picked) > 0.0, T < td).astype(jnp.int32)
    for k in range(n_tiles):
        @pl.when(cnt == k)
        def _(k=k):
            past = [idx_ref[j] for j in range(k)]
            q_past = [jnp.concatenate([qs, sbt_ref[T], q_pad], axis=0) for T in past]
            q_diag = jnp.concatenate([qs, sbt_ref[td], q_pad], axis=0)
            chunks = ([(j, r) for j in range(k) for r in range(0, tk, SEL_SUB)]
                      + [(None, r) for r in range(0, tk, SEL_SUB)])

            def key_start(ch):
                j, r = ch
                return pl.multiple_of((td if j is None else past[j]) * tk + r, SEL_SUB)

            def chunk_scores(ch):
                k0 = key_start(ch)
                if ch[0] is None:
                    causal = jnp.where(k0 + kpos_w <= t_w, 0.0, NEG_INF)
                    return _dot(ks_ref[pl.ds(k0, SEL_SUB), :], q_diag) + tile4(causal)
                return _dot(ks_ref[pl.ds(k0, SEL_SUB), :], q_past[ch[0]])

            acc = _online_softmax(chunks, chunk_scores, lambda ch: vst_ref[:, pl.ds(key_start(ch), SEL_SUB)])
            osel_ref[...] = acc[:dh] * (1.0 / acc[dh:dh + 1])

    o_sel = osel_ref[...]
    gates = gate_ref[...]
    tiles = []
    for r in range(R):
        sl = slice(r * tq, (r + 1) * tq)
        tiles.append(gates[3 * r:3 * r + 1, :] * o_cmp[:, sl]
                     + gates[3 * r + 1:3 * r + 2, :] * o_sel[:, sl]
                     + gates[3 * r + 2:3 * r + 3, :] * o_win[:, sl])
    o_ref[...] = jnp.concatenate(tiles, axis=0).T.astype(o_ref.dtype)


def _overlap_matrix(nc, ns):
    cs = np.arange(nc)[None, :] * CMP_STRIDE
    ss = np.arange(ns)[:, None] * SLC_BLOCK
    ov = np.clip(np.minimum(cs + CMP_BLOCK, ss + SLC_BLOCK) - np.maximum(cs, ss), 0, None)
    return jnp.asarray(ov.astype(np.float32) / CMP_BLOCK, dtype=BF16)


def _nsa(qt, kc, vct, ks, vst, kw, vwt, gates):
    B, _, S = qt.shape
    G, R, dh, tq = NSA_GROUPS, NSA_REP, HEAD_DIM, NSA_TQ
    nc = kc.shape[2]
    ns = S // SLC_BLOCK
    ov = _overlap_matrix(nc, ns)
    kern = functools.partial(_nsa_kernel, slopes=tuple(_alibi_slopes(NSA_HEADS)))
    assert S % SWEEP_TK == 0 and S >= NSA_WINDOW + tq
    per_bg = lambda shape: pl.BlockSpec((None, None) + shape, lambda b, g, i: (b, g, 0, 0))
    return pl.pallas_call(
        kern,
        grid=(B, G, S // tq),
        in_specs=[
            pl.BlockSpec((None, R * dh, tq), lambda b, g, i: (b, g, i)),
            per_bg((nc, K_AUG)), per_bg((dh, nc)),
            per_bg((S, K_AUG)), per_bg((V_ROWS, S)),
            per_bg((S, K_AUG)), per_bg((V_ROWS, S)),
            pl.BlockSpec((None, GATE_ROWS, tq), lambda b, g, i: (b, g, i)),
            pl.BlockSpec(ov.shape, lambda b, g, i: (0, 0)),
        ],
        out_specs=pl.BlockSpec((None, tq, R * dh), lambda b, g, i: (b, i, g)),
        out_shape=jax.ShapeDtypeStruct((B, S, G * R * dh), BF16),
        scratch_shapes=[pltpu.VMEM((S // SWEEP_TK, 2 * AUG_NBLK, R * tq), BF16),
                        pltpu.VMEM((dh, R * tq), F32),
                        pltpu.SMEM((S // SWEEP_TK,), jnp.int32)],
        compiler_params=_params("parallel", "parallel", "arbitrary"),
        name="nsa_attention",
    )(qt, kc, vct, ks, vst, kw, vwt, gates, ov)


BAND_TQ = 128
BAND_SUB = 8
BAND_PROBLEMS = 8
BAND = 128


def _banded_kernel(q_ref, k_ref, v_ref, o_ref, lse_ref, *, slopes, nsub, ncls):
    tq, dh, nh = BAND_TQ, HEAD_DIM, DIL_HEADS_PER_GROUP
    tk = tq + BAND
    width = nh * dh
    head_of_lane = lambda rows: lax.shift_right_logical(
        lax.broadcasted_iota(jnp.int32, (rows, width), 1), int(math.log2(dh)))
    lane_head, q_head = head_of_lane(tk), head_of_lane(tq)
    keep = [jnp.where(lane_head == h, 1.0, 0.0).astype(BF16) for h in range(nh)]

    def per_head(a):
        return jnp.concatenate([a * keep[h] for h in range(nh)], axis=0)

    ones_h = jnp.concatenate(keep, axis=0)
    def biases(first_key_offset):
        d = first_key_offset + (lax.broadcasted_iota(jnp.int32, (tq, tk), 0)
                                - lax.broadcasted_iota(jnp.int32, (tq, tk), 1))
        mask_bias = jnp.where((d >= 0) & (d <= BAND), 0.0, NEG_INF)
        neg_d = -d.astype(F32)
        return [slopes[h] * neg_d + mask_bias for h in range(nh)]

    subs = []
    for cls in range(ncls):
        for sub in range(nsub):
            i = pl.program_id(2) * nsub + sub
            k0 = pl.multiple_of(jnp.maximum(i - 1, 0) * tq, tq)
            q = q_ref[cls, sub * tq:(sub + 1) * tq, :]
            scores = _dot_nt(q, per_head(k_ref[cls, pl.ds(k0, tk), :]))
            subs.append((cls, sub, i * tq - k0, k0, scores))
    first_bias = biases(subs[0][2])
    inner_bias = biases(tq) if nsub > 1 else None
    probs = []
    for cls, sub, off, k0, s in subs:
        bias = first_bias if sub == 0 else inner_bias
        ps, mxs = [], []
        for h in range(nh):
            sh = s[:, h * tk:(h + 1) * tk] + bias[h]
            mx = jnp.max(sh, axis=-1, keepdims=True)
            ps.append(jnp.exp2(sh - mx).astype(BF16))
            mxs.append(mx)
        probs.append((cls, sub, k0, jnp.concatenate(ps, axis=1), mxs))
    for cls, sub, k0, p, mxs in probs:
        den = _dot(p, ones_h)
        mx_all = mxs[nh - 1]
        for h in range(nh - 2, -1, -1):
            mx_all = jnp.where(q_head == h, mxs[h], mx_all)
        rows = slice(sub * tq, (sub + 1) * tq)
        o_ref[cls, rows, :] = _dot(p, per_head(v_ref[cls, pl.ds(k0, tk), :])) * (1.0 / den)
        lse_ref[cls, rows, :] = mx_all * LN2 + jnp.log(den)


def _banded(d, slopes):
    B, _, r, n, width = d.shape
    nsub = min(BAND_SUB, n // BAND_TQ)
    ncls = min(r, BAND_PROBLEMS // nsub)
    tq = BAND_TQ * nsub
    assert n >= BAND_TQ + BAND and n % tq == 0 and r % ncls == 0
    kern = functools.partial(_banded_kernel, slopes=tuple(slopes), nsub=nsub, ncls=ncls)
    tile = pl.BlockSpec((None, None, ncls, tq, width), lambda b, c, i: (b, 0, c, i, 0))
    kseq = pl.BlockSpec((None, None, ncls, n, width), lambda b, c, i: (b, 1, c, 0, 0))
    vseq = pl.BlockSpec((None, None, ncls, n, width), lambda b, c, i: (b, 2, c, 0, 0))
    out = pl.BlockSpec((None, ncls, tq, width), lambda b, c, i: (b, c, i, 0))
    return pl.pallas_call(
        kern,
        grid=(B, r // ncls, n // tq),
        in_specs=[tile, kseq, vseq],
        out_specs=(out, out),
        out_shape=(jax.ShapeDtypeStruct((B, r, n, width), F32),) * 2,
        compiler_params=_params("parallel", "parallel", "arbitrary"),
        name="banded_attention",
    )(d, d, d)


def _dilated(dils):
    slopes = _alibi_slopes(DIL_HEADS)
    outs, lses = [], []
    for gi, (w, r) in enumerate(DIL_CONFIGS):
        assert w // r == BAND
        sl = [s_ * r * LOG2E for s_ in slopes[gi * DIL_HEADS_PER_GROUP:(gi + 1) * DIL_HEADS_PER_GROUP]]
        o, lse = _banded(dils[gi], sl)
        outs.append(o)
        lses.append(lse)
    return outs, lses


MERGE_TM = 1024


def _token_order(ref, slab_ref):
    r, rows, width = ref.shape
    if r == 1:
        return ref[0]
    for c in range(r):
        blk = ref[c]
        for s in range(width // LANES):
            slab_ref[s, pl.ds(c, rows, stride=r), :] = blk[:, s * LANES:(s + 1) * LANES]
    return jnp.concatenate([slab_ref[s] for s in range(width // LANES)], axis=1)


def _merge_kernel(x_ref, oa_ref, o0_ref, o1_ref, o2_ref, l0_ref, l1_ref, l2_ref,
                  gmix_ref, wm_ref, wpn_ref, wpd_ref, wo_ref, gffn_ref, x1_ref, h2_ref, slab_ref):
    x = x_ref[...]
    D = x.shape[1]
    h = _rms(x, gmix_ref[...]).astype(BF16)
    gm = _sigmoid(_dot_nt(h, wm_ref[...]))
    o0, o1, o2 = [_token_order(r_, slab_ref) for r_ in (o0_ref, o1_ref, o2_ref)]
    l0, l1, l2 = [_token_order(r_, slab_ref) for r_ in (l0_ref, l1_ref, l2_ref)]
    mx = jnp.maximum(jnp.maximum(l0, l1), l2)
    e0, e1, e2 = jnp.exp(l0 - mx), jnp.exp(l1 - mx), jnp.exp(l2 - mx)
    inv = 1.0 / (e0 + e1 + e2)
    ob = o0 * (e0 * inv) + o1 * (e1 * inv) + o2 * (e2 * inv)
    a = _dot(oa_ref[...], wpn_ref[...])
    d = _dot(ob.astype(BF16), wpd_ref[...])
    mixed = gm[:, :D] * a + gm[:, D:] * d
    x1 = x + _dot(mixed.astype(BF16), wo_ref[...])
    x1_ref[...] = x1
    h2_ref[...] = _rms(x1, gffn_ref[...]).astype(BF16)


def _merge(x, o_a, outs, lses, g_mix, w_merge, w_proj_nsa, w_proj_dil, w_out, g_ffn):
    B, S, D = x.shape
    tm = MERGE_TM
    row = lambda a: pl.BlockSpec((None, tm, a.shape[2]), lambda b, i: (b, i, 0))
    cls = lambda a: pl.BlockSpec((None, a.shape[1], tm // a.shape[1], a.shape[3]), lambda b, i: (b, 0, i, 0))
    full = lambda a: pl.BlockSpec(a.shape, lambda b, i: (0,) * a.ndim)
    ws = [w_merge.astype(BF16), w_proj_nsa.astype(BF16), w_proj_dil.astype(BF16), w_out.astype(BF16)]
    consts = [g_mix, *ws, g_ffn]
    in_specs = [row(x), row(o_a)] + [cls(a) for a in (*outs, *lses)] + [full(a) for a in consts]
    return pl.pallas_call(
        _merge_kernel,
        grid=(B, S // tm),
        in_specs=in_specs,
        out_specs=(pl.BlockSpec((None, tm, D), lambda b, i: (b, i, 0)),) * 2,
        out_shape=(jax.ShapeDtypeStruct((B, S, D), F32), jax.ShapeDtypeStruct((B, S, D), BF16)),
        scratch_shapes=[pltpu.VMEM((DIL_WIDTH // LANES, tm, LANES), F32)],
        compiler_params=_params("parallel", "parallel"),
        name="merge_proj",
    )(x, o_a, *outs, *lses, *consts)


FFN_TM = 1024
FFN_TN = 256
HALO = 16


def _ffn_kernel(h_ref, halo_ref, x1_ref, wup_ref, cw_ref, cb_ref, wd_ref, gfin_ref, o_ref, act_ref):
    i = pl.program_id(1)
    h = h_ref[...]
    halo = halo_ref[...]
    tm = h.shape[0]
    row = lax.broadcasted_iota(jnp.int32, (tm, FFN_TN), 0)
    live = (i > 0).astype(F32)
    for j in range(D_FF // FFN_TN):
        cols = slice(j * FFN_TN, (j + 1) * FFN_TN)
        wu = wup_ref[:, cols]
        u = _dot(h, wu)
        uh = _dot(halo, wu) * live
        gate = _dot(h, wup_ref[:, D_FF + j * FFN_TN:D_FF + (j + 1) * FFN_TN])
        p1 = jnp.broadcast_to(uh[HALO - 1:HALO, :], (tm, FFN_TN))
        p2 = jnp.broadcast_to(uh[HALO - 2:HALO - 1, :], (tm, FFN_TN))
        u1 = jnp.where(row == 0, p1, pltpu.roll(u, 1, 0))
        u2 = jnp.where(row == 0, p2, jnp.where(row == 1, p1, pltpu.roll(u, 2, 0)))
        uc = cb_ref[:, cols] + cw_ref[0:1, cols] * u2
        uc = uc + cw_ref[1:2, cols] * u1
        uc = uc + cw_ref[2:3, cols] * u
        act_ref[:, j * FFN_TN:(j + 1) * FFN_TN] = (jax.nn.gelu(uc) * gate).astype(BF16)
    y = _dot(act_ref[...], wd_ref[...])
    o_ref[...] = _rms(x1_ref[...] + y, gfin_ref[...])


def _ffn(h2, x1, w_up, conv_w, conv_b, w_down, g_final):
    B, S, D = h2.shape
    tm = FFN_TM
    assert D_FF % FFN_TN == 0
    wup = w_up.astype(BF16)
    cw = conv_w
    cb = conv_b.reshape(1, D_FF)
    wd = w_down.astype(BF16)
    gfin = g_final.reshape(1, D)
    full = lambda a: pl.BlockSpec(a.shape, lambda b, i: (0,) * a.ndim, pipeline_mode=pl.Buffered(1))
    tile = pl.BlockSpec((None, tm, D), lambda b, i: (b, i, 0))
    halo = pl.BlockSpec((None, HALO, D), lambda b, i: (b, jnp.maximum(i * (tm // HALO) - 1, 0), 0))
    return pl.pallas_call(
        _ffn_kernel,
        grid=(B, S // tm),
        in_specs=[tile, halo, tile, full(wup), full(cw), full(cb), full(wd), full(gfin)],
        out_specs=tile,
        out_shape=jax.ShapeDtypeStruct((B, S, D), F32),
        scratch_shapes=[pltpu.VMEM((tm, D_FF), BF16)],
        compiler_params=_params("parallel", "parallel"),
        name="conv_ffn",
    )(h2, h2, x1, wup, cw, cb, wd, gfin)


@jax.jit
def _layer(x, g_mix, w_in, pe_cmp_k, w_cmp_k1, w_cmp_k2, pe_cmp_v, w_cmp_v1, w_cmp_v2,
           w_proj_nsa, w_proj_dil, w_out, g_ffn, w_up, conv_w, conv_b, w_down, g_final):
    B, S, D = x.shape
    depth = g_mix.shape[0]
    for l in range(depth):
        gm = g_mix[l].reshape(1, D)
        kcmp, vcmp, ks, kw, d0, d1, d2, qt, vst, vwt, gates = _in_proj(x, gm, w_in[l])
        kc, vct = _compress(kcmp, vcmp, pe_cmp_k[l], w_cmp_k1[l], w_cmp_k2[l],
                            pe_cmp_v[l], w_cmp_v1[l], w_cmp_v2[l])
        o_a = _nsa(qt, kc, vct, ks, vst, kw, vwt, gates)
        outs, lses = _dilated((d0, d1, d2))
        merge_cols = w_in[l].T[w_in.shape[2] - 2 * D:]
        x1, h2 = _merge(x, o_a, outs, lses, gm, merge_cols, w_proj_nsa[l], w_proj_dil[l], w_out[l],
                        g_ffn[l].reshape(1, D))
        x = _ffn(h2, x1, w_up[l], conv_w[l], conv_b[l], w_down[l], g_final)
        assert depth == 1
    return x


def kernel(x, g_mix, w_in, pe_cmp_k, w_cmp_k1, w_cmp_k2, pe_cmp_v, w_cmp_v1, w_cmp_v2, w_proj_nsa, w_proj_dil, w_out, g_ffn, w_up, conv_w, conv_b, w_down, g_final):
    return _layer(x, g_mix, w_in, pe_cmp_k, w_cmp_k1, w_cmp_k2, pe_cmp_v, w_cmp_v1, w_cmp_v2,
                  w_proj_nsa, w_proj_dil, w_out, g_ffn, w_up, conv_w, conv_b, w_down, g_final)
```

```python
import functools
import math

import numpy as np
import jax
import jax.numpy as jnp
from jax import lax
from jax.experimental import pallas as pl
from jax.experimental.pallas import tpu as pltpu

HEAD_DIM = 64
NSA_HEADS = 8
NSA_GROUPS = 2
NSA_REP = NSA_HEADS // NSA_GROUPS
CMP_BLOCK = 32
CMP_STRIDE = 16
CMP_HIDDEN = 128
SLC_BLOCK = 64
SLC_TOP = 16
NSA_WINDOW = 512
FORCE_SCORE = 1.0e4
DIL_CONFIGS = ((128, 1), (512, 4), (2048, 16))
DIL_GROUPS = 3
DIL_HEADS_PER_GROUP = 4
DIL_HEADS = DIL_GROUPS * DIL_HEADS_PER_GROUP
D_FF = 2816
CONV_WIDTH = 3
RMS_EPS = 1e-6
NEG_INF = -1e30

LANES = 128
VMEM_LIMIT_BYTES = 56 * 1024 * 1024

F32 = jnp.float32
BF16 = jnp.bfloat16
NT_DIMS = (((1,), (1,)), ((), ()))


def _alibi_slopes(n):
    return [float(2.0 ** (-8.0 * i / n)) for i in range(1, n + 1)]


def _rms(xf, g):
    ms = jnp.mean(xf * xf, axis=-1, keepdims=True)
    return xf * lax.rsqrt(ms + RMS_EPS) * g


def _dot(a, b):
    return jnp.dot(a, b, preferred_element_type=F32)


def _dot_nt(a, b):
    return lax.dot_general(a, b, NT_DIMS, preferred_element_type=F32)


def _sigmoid(z):
    return 1.0 / (1.0 + jnp.exp(-z))


def _params(*sem):
    return pltpu.CompilerParams(dimension_semantics=sem, vmem_limit_bytes=VMEM_LIMIT_BYTES)


IN_TM = 1024
N_KVC = 4 * HEAD_DIM
N_KSEL = NSA_GROUPS * HEAD_DIM
N_DIL = 3 * DIL_HEADS * HEAD_DIM
DIL_WIDTH = DIL_HEADS_PER_GROUP * HEAD_DIM
T_Q = NSA_HEADS * HEAD_DIM
T_V = NSA_GROUPS * HEAD_DIM
GATE_ROWS = 16
K_AUG = 2 * HEAD_DIM
SLC_SHIFT = int(math.log2(SLC_BLOCK))
AUG_NBLK = 8
AUG_HI, AUG_LO = AUG_NBLK, AUG_NBLK + 1
LOG2E = math.log2(math.e)
LN2 = math.log(2.0)
V_PAD = 16
V_ROWS = HEAD_DIM + V_PAD


def _key_position_columns(pos0, rows, step=1):
    pos = pos0 + step * lax.broadcasted_iota(jnp.int32, (rows, HEAD_DIM), 0)
    col = lax.broadcasted_iota(jnp.int32, (rows, HEAD_DIM), 1)
    blk = jnp.bitwise_and(lax.shift_right_logical(pos, SLC_SHIFT), AUG_NBLK - 1)
    hi = lax.shift_left(lax.shift_right_logical(pos, 7), 7).astype(F32)
    lo = jnp.bitwise_and(pos, 127).astype(F32)
    c = jnp.where((col == AUG_HI) | (col == AUG_HI + 2), hi,
                  jnp.where((col == AUG_LO) | (col == AUG_LO + 2), lo, 0.0))
    return jnp.where((col < AUG_NBLK) & (blk == col), 1.0, c)


def _in_proj_kernel(x_ref, g_ref, wn_ref, wt_ref,
                    kcmp_ref, vcmp_ref, ks_ref, kw_ref, d0_ref, d1_ref, d2_ref, qt_ref, vst_ref, vwt_ref, gate_ref,
                    slab_ref):
    tm = x_ref.shape[0]
    h = _rms(x_ref[...], g_ref[...]).astype(BF16)
    c0 = 0
    kvc = _dot_nt(h, wn_ref[c0:c0 + N_KVC, :])
    kcmp_ref[...] = kvc[:, :N_KSEL]
    vcmp_ref[...] = kvc[:, N_KSEL:]
    c0 += N_KVC
    ks = _dot_nt(h, wn_ref[c0:c0 + N_KSEL, :]).astype(BF16)
    c0 += N_KSEL
    kw = _dot_nt(h, wn_ref[c0:c0 + N_KSEL, :]).astype(BF16)
    c0 += N_KSEL
    aug = _key_position_columns(pl.program_id(1) * tm, tm).astype(BF16)
    for g in range(NSA_GROUPS):
        ks_ref[g] = jnp.concatenate([ks[:, g * HEAD_DIM:(g + 1) * HEAD_DIM], aug], axis=1)
        kw_ref[g] = jnp.concatenate([kw[:, g * HEAD_DIM:(g + 1) * HEAD_DIM], aug], axis=1)
    seg = DIL_HEADS * HEAD_DIM
    for which in range(3):
        y = _dot_nt(h, wn_ref[c0 + which * seg:c0 + (which + 1) * seg, :])
        for gi, (d_ref, (_, r)) in enumerate(zip((d0_ref, d1_ref, d2_ref), DIL_CONFIGS)):
            yg = y[:, gi * DIL_WIDTH:(gi + 1) * DIL_WIDTH]
            if r == 1:
                d_ref[which, 0] = yg.astype(BF16)
                continue
            for s in range(DIL_WIDTH // LANES):
                slab_ref[s] = yg[:, s * LANES:(s + 1) * LANES]
            for c in range(r):
                d_ref[which, c] = jnp.concatenate(
                    [slab_ref[s, pl.ds(c, tm // r, stride=r), :] for s in range(DIL_WIDTH // LANES)],
                    axis=1).astype(BF16)
    yt = _dot_nt(wt_ref[...], h)
    qt_ref[...] = yt[0:T_Q].astype(BF16)
    r0 = T_Q
    ones = jnp.where(lax.broadcasted_iota(jnp.int32, (V_PAD, tm), 0) == 0, 1.0, 0.0).astype(BF16)
    for ref in (vst_ref, vwt_ref):
        vt = yt[r0:r0 + T_V].astype(BF16)
        r0 += T_V
        for g in range(NSA_GROUPS):
            ref[g] = jnp.concatenate([vt[g * HEAD_DIM:(g + 1) * HEAD_DIM, :], ones], axis=0)
    gate_ref[...] = _sigmoid(yt[r0:r0 + NSA_GROUPS * GATE_ROWS])


def _in_proj(x, g_mix, w_in):
    B, S, D = x.shape
    scale = HEAD_DIM ** -0.5 * LOG2E
    o_q, o_kv = 0, T_Q
    o_gate = o_kv + 6 * N_KSEL
    o_dil = o_gate + 3 * NSA_HEADS
    o_merge = o_dil + N_DIL
    w_t = w_in.T
    kv = w_t[o_kv:o_gate]

    def kind(k):
        return kv[k * N_KSEL:(k + 1) * N_KSEL]

    dil = w_t[o_dil:o_merge]
    dil = jnp.concatenate([dil[:DIL_HEADS * HEAD_DIM] * scale, dil[DIL_HEADS * HEAD_DIM:]], axis=0)
    wn = jnp.concatenate([kind(0), kind(1), kind(2), kind(4), dil], axis=0).astype(BF16)
    wg = w_t[o_gate:o_dil].reshape(NSA_GROUPS, 3 * NSA_REP, D)
    wg = jnp.pad(wg, ((0, 0), (0, GATE_ROWS - 3 * NSA_REP), (0, 0))).reshape(NSA_GROUPS * GATE_ROWS, D)
    wt = jnp.concatenate([w_t[o_q:o_kv] * scale, kind(3), kind(5), wg], axis=0).astype(BF16)
    tm = IN_TM
    grid = (B, S // tm)
    full = lambda a: pl.BlockSpec(a.shape, lambda b, i: (0,) * a.ndim)
    k_shape = jax.ShapeDtypeStruct((B, NSA_GROUPS, S, K_AUG), BF16)
    v_shape = jax.ShapeDtypeStruct((B, NSA_GROUPS, V_ROWS, S), BF16)
    k_spec = pl.BlockSpec((None, NSA_GROUPS, tm, K_AUG), lambda b, i: (b, 0, i, 0))
    v_spec = pl.BlockSpec((None, NSA_GROUPS, V_ROWS, tm), lambda b, i: (b, 0, 0, i))
    c_shape = jax.ShapeDtypeStruct((B, S, N_KSEL), F32)
    c_spec = pl.BlockSpec((None, tm, N_KSEL), lambda b, i: (b, i, 0))
    d_shapes = tuple(jax.ShapeDtypeStruct((B, 3, r, S // r, DIL_WIDTH), BF16) for _, r in DIL_CONFIGS)
    d_specs = tuple(pl.BlockSpec((None, 3, r, tm // r, DIL_WIDTH), lambda b, i: (b, 0, 0, i, 0))
                    for _, r in DIL_CONFIGS)
    out_shape = (
        c_shape, c_shape, k_shape, k_shape, *d_shapes,
        jax.ShapeDtypeStruct((B, T_Q, S), BF16),
        v_shape,
        v_shape,
        jax.ShapeDtypeStruct((B, NSA_GROUPS * GATE_ROWS, S), F32),
    )
    out_specs = (
        c_spec, c_spec, k_spec, k_spec, *d_specs,
        pl.BlockSpec((None, T_Q, tm), lambda b, i: (b, 0, i)),
        v_spec,
        v_spec,
        pl.BlockSpec((None, NSA_GROUPS * GATE_ROWS, tm), lambda b, i: (b, 0, i)),
    )
    return pl.pallas_call(
        _in_proj_kernel,
        grid=grid,
        in_specs=[pl.BlockSpec((None, tm, D), lambda b, i: (b, i, 0)), full(g_mix), full(wn), full(wt)],
        out_specs=out_specs,
        out_shape=out_shape,
        scratch_shapes=[pltpu.VMEM((DIL_WIDTH // LANES, tm, LANES), F32)],
        compiler_params=_params("parallel", "parallel"),
        name="in_proj",
    )(x, g_mix, wn, wt)


def _compress_kernel(xk_ref, xv_ref, pek_ref, pev_ref, w1k_ref, w1v_ref, w2k_ref, w2vt_ref, kc_ref, vct_ref):
    nch = xk_ref.shape[0] // CMP_STRIDE
    streams = [(xk_ref, pek_ref, w1k_ref), (xv_ref, pev_ref, w1v_ref)]
    acc = [[jnp.zeros((nch, NSA_GROUPS * CMP_HIDDEN), F32) for _ in range(2)] for _ in streams]
    for j in range(CMP_STRIDE):
        for n, (x_ref, pe_ref, w1_ref) in enumerate(streams):
            xj = x_ref[pl.ds(j, nch, stride=CMP_STRIDE), :]
            for half in range(2):
                row = half * CMP_STRIDE + j
                acc[n][half] = acc[n][half] + _dot((xj + pe_ref[row:row + 1, :]).astype(BF16), w1_ref[row])
    hid = [jax.nn.gelu(lo + pltpu.roll(hi, nch - 1, 0)).astype(BF16) for lo, hi in acc]
    aug = _key_position_columns(CMP_BLOCK - 1, nch, CMP_STRIDE).astype(BF16)
    for g in range(NSA_GROUPS):
        cols = slice(g * CMP_HIDDEN, (g + 1) * CMP_HIDDEN)
        kc_ref[g] = jnp.concatenate([_dot(hid[0][:, cols], w2k_ref[...]).astype(BF16), aug], axis=1)
        vct_ref[g] = _dot_nt(w2vt_ref[...], hid[1][:, cols]).astype(BF16)


def _compress(xk, xv, pe_k, w1_k, w2_k, pe_v, w1_v, w2_v):
    B, S, width = xk.shape
    nch = S // CMP_STRIDE
    G, dh, hid = NSA_GROUPS, HEAD_DIM, CMP_HIDDEN
    eye = jnp.eye(G, dtype=w1_k.dtype)

    def expand(pe, w1):
        pe_t = jnp.broadcast_to(pe.reshape(CMP_BLOCK, 1, dh), (CMP_BLOCK, G, dh)).reshape(CMP_BLOCK, width)
        wexp = jnp.einsum('pdn,ge->pgden', w1.reshape(CMP_BLOCK, dh, hid), eye).reshape(CMP_BLOCK, width, G * hid)
        return pe_t, wexp.astype(BF16)

    pek, w1k = expand(pe_k, w1_k)
    pev, w1v = expand(pe_v, w1_v)
    consts = [pek, pev, w1k, w1v, w2_k.astype(BF16), w2_v.T.astype(BF16)]
    full = lambda a: pl.BlockSpec(a.shape, lambda b: (0,) * a.ndim)
    seq = pl.BlockSpec((None, S, width), lambda b: (b, 0, 0))
    return pl.pallas_call(
        _compress_kernel,
        grid=(B,),
        in_specs=[seq, seq] + [full(a) for a in consts],
        out_specs=(pl.BlockSpec((None, G, nch, K_AUG), lambda b: (b, 0, 0, 0)),
                   pl.BlockSpec((None, G, dh, nch), lambda b: (b, 0, 0, 0))),
        out_shape=(jax.ShapeDtypeStruct((B, G, nch, K_AUG), BF16),
                   jax.ShapeDtypeStruct((B, G, dh, nch), BF16)),
        compiler_params=_params("parallel"),
        name="compress",
    )(xk, xv, *consts)


NSA_TQ = 256
SWEEP_TK = AUG_NBLK * SLC_BLOCK
SEL_SUB = 128
SEL_AHEAD = 4


def _rowmax8(s):
    return jnp.max(s.reshape(s.shape[0] // 8, 8, s.shape[1]), axis=0)


def _online_softmax(chunks, scores, values):
    pending = [scores(ch) for ch in chunks[:SEL_AHEAD]]
    m = acc = None
    for n, ch in enumerate(chunks):
        if n + SEL_AHEAD < len(chunks):
            pending.append(scores(chunks[n + SEL_AHEAD]))
        s = pending.pop(0)
        m_c = jnp.max(_rowmax8(s), axis=0, keepdims=True)
        if m is None:
            m = m_c
            acc = _dot(values(ch), jnp.exp2(s - m).astype(BF16))
        else:
            m_new = jnp.maximum(m, m_c)
            acc = jnp.exp2(m - m_new) * acc + _dot(values(ch), jnp.exp2(s - m_new).astype(BF16))
            m = m_new
    return acc


def _nsa_kernel(qt_ref, kc_ref, vct_ref, ks_ref, vst_ref, kw_ref, vwt_ref, gate_ref, ov_ref,
                o_ref, sbt_ref, osel_ref, idx_ref, *, slopes):
    g = pl.program_id(1)
    i = pl.program_id(2)
    R, dh, tq, tk = NSA_REP, HEAD_DIM, NSA_TQ, SWEEP_TK
    L = R * tq
    t0 = i * tq
    nc = kc_ref.shape[0]
    ns = ov_ref.shape[0]
    n_tiles = ns // AUG_NBLK
    tile4 = lambda a: jnp.concatenate([a] * R, axis=1)

    qt = qt_ref[...]
    qs = jnp.concatenate([qt[r * dh:(r + 1) * dh, :] for r in range(R)], axis=1)
    slope = [jnp.where(g == 0, slopes[r], slopes[R + r]).astype(F32) for r in range(R)]
    slope_row = jnp.concatenate([jnp.full((1, tq), 1.0, F32) * slope[r] for r in range(R)], axis=1)
    t_row = t0 + lax.broadcasted_iota(jnp.int32, (1, tq), 1)

    r8 = lax.broadcasted_iota(jnp.int32, (AUG_NBLK, L), 0)
    s_full = slope_row * LOG2E
    s_hi = s_full.astype(BF16).astype(F32)
    alibi8 = jnp.where(r8 < 2, s_hi, jnp.where(r8 < 4, s_full - s_hi, 0.0))
    q_pad = jnp.zeros((K_AUG - dh - 2 * AUG_NBLK, L), BF16)
    q_plain = jnp.concatenate([qs, jnp.concatenate([jnp.zeros((AUG_NBLK, L), F32), alibi8], axis=0).astype(BF16),
                               q_pad], axis=0)

    cmp_end = lax.broadcasted_iota(jnp.int32, (nc, tq), 0) * CMP_STRIDE + (CMP_BLOCK - 1)
    m_cmp = t_row >= cmp_end
    sc = _dot(kc_ref[...], q_plain) + tile4(jnp.where(m_cmp, 0.0, NEG_INF))

    a0 = jnp.maximum(t0 - NSA_WINDOW, 0)
    kpos_w = lax.broadcasted_iota(jnp.int32, (SEL_SUB, tq), 0)
    t_w = t0 + lax.broadcasted_iota(jnp.int32, (SEL_SUB, tq), 1)

    def win_scores(ch):
        diag, r = ch
        k0 = pl.multiple_of((t0 if diag else a0) + r, SEL_SUB)
        kpos = k0 + kpos_w
        valid = (kpos <= t_w) if diag else ((kpos < t0) & (t_w - kpos < NSA_WINDOW))
        return _dot(kw_ref[pl.ds(k0, SEL_SUB), :], q_plain) + tile4(jnp.where(valid, 0.0, NEG_INF))

    def win_values(ch):
        diag, r = ch
        return vwt_ref[:, pl.ds(pl.multiple_of((t0 if diag else a0) + r, SEL_SUB), SEL_SUB)]

    win_chunks = ([(True, r) for r in range(0, tq, SEL_SUB)]
                  + [(False, r) for r in range(0, NSA_WINDOW, SEL_SUB)])
    acc_w = _online_softmax(win_chunks, win_scores, win_values)
    o_win = acc_w[:dh] * (1.0 / acc_w[dh:dh + 1])

    mx = jnp.max(_rowmax8(sc), axis=0, keepdims=True)
    p = jnp.exp2(sc - mx) * tile4(jnp.where(m_cmp, 1.0, 0.0))
    den = jnp.sum(jnp.sum(p.reshape(nc // 8, 8, L), axis=0), axis=0, keepdims=True)
    pr = p * (1.0 / jnp.maximum(den, 1e-30))
    o_cmp = _dot(vct_ref[...], pr.astype(BF16))
    psum = pr[:, 0:tq]
    for r in range(1, R):
        psum = psum + pr[:, r * tq:(r + 1) * tq]

    p_hi = psum.astype(BF16)
    p_lo = (psum - p_hi.astype(F32)).astype(BF16)
    imp = _dot(ov_ref[...], p_hi) + _dot(ov_ref[...], p_lo)
    blk = lax.broadcasted_iota(jnp.int32, (ns, tq), 0)
    cur = lax.shift_right_logical(t_row, SLC_SHIFT)
    val = jnp.where((blk == cur) | (blk == 0), FORCE_SCORE, imp)
    val = jnp.where(blk <= cur, val, -1.0)
    vals = [val[8 * v:8 * v + 8, :] for v in range(ns // 8)]
    ranks = [jnp.zeros((8, tq), F32) for _ in vals]
    row8 = lax.broadcasted_iota(jnp.int32, (8, tq), 0)
    for j in range(ns):
        vj = jnp.broadcast_to(val[j:j + 1, :], (8, tq))
        for v in range(len(vals)):
            if 8 * v > j:
                ahead = vj >= vals[v]
            elif 8 * v + 7 <= j:
                ahead = vj > vals[v]
            else:
                ahead = (vj > vals[v]) | ((vj == vals[v]) & (row8 > j - 8 * v))
            ranks[v] = ranks[v] + jnp.where(ahead, 1.0, 0.0)
    for T in range(n_tiles):
        selb = jnp.where((ranks[T] < float(SLC_TOP)) & (vals[T] >= 0.0), 0.0, NEG_INF)
        sbt_ref[T] = jnp.concatenate([tile4(selb), alibi8], axis=0).astype(BF16)

    td = lax.div(t0, tk)
    cnt = jnp.int32(0)
    for T in range(n_tiles - 1):
        picked = jnp.where((ranks[T] < float(SLC_TOP)) & (vals[T] >= 0.0), 1.0, 0.0)
        idx_ref[cnt] = jnp.int32(T)
        cnt = cnt + jnp.logical_and(jnp.max(picked) > 0.0, T < td).astype(jnp.int32)
    for k in range(n_tiles):
        @pl.when(cnt == k)
        def _(k=k):
            past = [idx_ref[j] for j in range(k)]
            q_past = [jnp.concatenate([qs, sbt_ref[T], q_pad], axis=0) for T in past]
            q_diag = jnp.concatenate([qs, sbt_ref[td], q_pad], axis=0)
            chunks = ([(j, r) for j in range(k) for r in range(0, tk, SEL_SUB)]
                      + [(None, r) for r in range(0, tk, SEL_SUB)])

            def key_start(ch):
                j, r = ch
                return pl.multiple_of((td if j is None else past[j]) * tk + r, SEL_SUB)

            def chunk_scores(ch):
                k0 = key_start(ch)
                if ch[0] is None:
                    causal = jnp.where(k0 + kpos_w <= t_w, 0.0, NEG_INF)
                    return _dot(ks_ref[pl.ds(k0, SEL_SUB), :], q_diag) + tile4(causal)
                return _dot(ks_ref[pl.ds(k0, SEL_SUB), :], q_past[ch[0]])

            acc = _online_softmax(chunks, chunk_scores, lambda ch: vst_ref[:, pl.ds(key_start(ch), SEL_SUB)])
            osel_ref[...] = acc[:dh] * (1.0 / acc[dh:dh + 1])

    o_sel = osel_ref[...]
    gates = gate_ref[...]
    tiles = []
    for r in range(R):
        sl = slice(r * tq, (r + 1) * tq)
        tiles.append(gates[3 * r:3 * r + 1, :] * o_cmp[:, sl]
                     + gates[3 * r + 1:3 * r + 2, :] * o_sel[:, sl]
                     + gates[3 * r + 2:3 * r + 3, :] * o_win[:, sl])
    o_ref[...] = jnp.concatenate(tiles, axis=0).T.astype(o_ref.dtype)


def _overlap_matrix(nc, ns):
    cs = np.arange(nc)[None, :] * CMP_STRIDE
    ss = np.arange(ns)[:, None] * SLC_BLOCK
    ov = np.clip(np.minimum(cs + CMP_BLOCK, ss + SLC_BLOCK) - np.maximum(cs, ss), 0, None)
    return jnp.asarray(ov.astype(np.float32) / CMP_BLOCK, dtype=BF16)


def _nsa(qt, kc, vct, ks, vst, kw, vwt, gates):
    B, _, S = qt.shape
    G, R, dh, tq = NSA_GROUPS, NSA_REP, HEAD_DIM, NSA_TQ
    nc = kc.shape[2]
    ns = S // SLC_BLOCK
    ov = _overlap_matrix(nc, ns)
    kern = functools.partial(_nsa_kernel, slopes=tuple(_alibi_slopes(NSA_HEADS)))
    assert S % SWEEP_TK == 0 and S >= NSA_WINDOW + tq
    per_bg = lambda shape: pl.BlockSpec((None, None) + shape, lambda b, g, i: (b, g, 0, 0))
    return pl.pallas_call(
        kern,
        grid=(B, G, S // tq),
        in_specs=[
            pl.BlockSpec((None, R * dh, tq), lambda b, g, i: (b, g, i)),
            per_bg((nc, K_AUG)), per_bg((dh, nc)),
            per_bg((S, K_AUG)), per_bg((V_ROWS, S)),
            per_bg((S, K_AUG)), per_bg((V_ROWS, S)),
            pl.BlockSpec((None, GATE_ROWS, tq), lambda b, g, i: (b, g, i)),
            pl.BlockSpec(ov.shape, lambda b, g, i: (0, 0)),
        ],
        out_specs=pl.BlockSpec((None, tq, R * dh), lambda b, g, i: (b, i, g)),
        out_shape=jax.ShapeDtypeStruct((B, S, G * R * dh), BF16),
        scratch_shapes=[pltpu.VMEM((S // SWEEP_TK, 2 * AUG_NBLK, R * tq), BF16),
                        pltpu.VMEM((dh, R * tq), F32),
                        pltpu.SMEM((S // SWEEP_TK,), jnp.int32)],
        compiler_params=_params("parallel", "parallel", "arbitrary"),
        name="nsa_attention",
    )(qt, kc, vct, ks, vst, kw, vwt, gates, ov)


BAND_TQ = 128
BAND_SUB = 8
BAND_PROBLEMS = 8
BAND = 128


def _banded_kernel(q_ref, k_ref, v_ref, o_ref, lse_ref, *, slopes, nsub, ncls):
    tq, dh, nh = BAND_TQ, HEAD_DIM, DIL_HEADS_PER_GROUP
    tk = tq + BAND
    width = nh * dh
    head_of_lane = lambda rows: lax.shift_right_logical(
        lax.broadcasted_iota(jnp.int32, (rows, width), 1), int(math.log2(dh)))
    lane_head, q_head = head_of_lane(tk), head_of_lane(tq)
    keep = [jnp.where(lane_head == h, 1.0, 0.0).astype(BF16) for h in range(nh)]

    def per_head(a):
        return jnp.concatenate([a * keep[h] for h in range(nh)], axis=0)

    def biases(first_key_offset):
        d = first_key_offset + (lax.broadcasted_iota(jnp.int32, (tq, tk), 0)
                                - lax.broadcasted_iota(jnp.int32, (tq, tk), 1))
        mask_bias = jnp.where((d >= 0) & (d <= BAND), 0.0, NEG_INF)
        neg_d = -d.astype(F32)
        return [slopes[h] * neg_d + mask_bias for h in range(nh)]

    subs = []
    for cls in range(ncls):
        for sub in range(nsub):
            i = pl.program_id(2) * nsub + sub
            k0 = pl.multiple_of(jnp.maximum(i - 1, 0) * tq, tq)
            q = q_ref[cls, sub * tq:(sub + 1) * tq, :]
            scores = _dot_nt(q, per_head(k_ref[cls, pl.ds(k0, tk), :]))
            subs.append((cls, sub, i * tq - k0, k0, scores))
    first_bias = biases(subs[0][2])
    inner_bias = biases(tq) if nsub > 1 else None
    probs = []
    for cls, sub, off, k0, s in subs:
        bias = first_bias if sub == 0 else inner_bias
        ps, mxs, dens = [], [], []
        for h in range(nh):
            sh = s[:, h * tk:(h + 1) * tk] + bias[h]
            mx = jnp.max(sh, axis=-1, keepdims=True)
            e = jnp.exp2(sh - mx)
            dens.append(jnp.sum(e, axis=-1, keepdims=True))
            ps.append(e.astype(BF16))
            mxs.append(mx)
        probs.append((cls, sub, k0, jnp.concatenate(ps, axis=1), mxs, dens))
    for cls, sub, k0, p, mxs, dens in probs:
        mx_all, den = mxs[nh - 1], dens[nh - 1]
        for h in range(nh - 2, -1, -1):
            mx_all = jnp.where(q_head == h, mxs[h], mx_all)
            den = jnp.where(q_head == h, dens[h], den)
        rows = slice(sub * tq, (sub + 1) * tq)
        o_ref[cls, rows, :] = _dot(p, per_head(v_ref[cls, pl.ds(k0, tk), :])) * (1.0 / den)
        lse_ref[cls, rows, :] = mx_all * LN2 + jnp.log(den)


def _banded(d, slopes):
    B, _, r, n, width = d.shape
    nsub = min(BAND_SUB, n // BAND_TQ)
    ncls = min(r, BAND_PROBLEMS // nsub)
    tq = BAND_TQ * nsub
    assert n >= BAND_TQ + BAND and n % tq == 0 and r % ncls == 0
    kern = functools.partial(_banded_kernel, slopes=tuple(slopes), nsub=nsub, ncls=ncls)
    tile = pl.BlockSpec((None, None, ncls, tq, width), lambda b, c, i: (b, 0, c, i, 0))
    kseq = pl.BlockSpec((None, None, ncls, n, width), lambda b, c, i: (b, 1, c, 0, 0))
    vseq = pl.BlockSpec((None, None, ncls, n, width), lambda b, c, i: (b, 2, c, 0, 0))
    out = pl.BlockSpec((None, ncls, tq, width), lambda b, c, i: (b, c, i, 0))
    return pl.pallas_call(
        kern,
        grid=(B, r // ncls, n // tq),
        in_specs=[tile, kseq, vseq],
        out_specs=(out, out),
        out_shape=(jax.ShapeDtypeStruct((B, r, n, width), F32),) * 2,
        compiler_params=_params("parallel", "parallel", "arbitrary"),
        name="banded_attention",
    )(d, d, d)


def _dilated(dils):
    slopes = _alibi_slopes(DIL_HEADS)
    outs, lses = [], []
    for gi, (w, r) in enumerate(DIL_CONFIGS):
        assert w // r == BAND
        sl = [s_ * r * LOG2E for s_ in slopes[gi * DIL_HEADS_PER_GROUP:(gi + 1) * DIL_HEADS_PER_GROUP]]
        o, lse = _banded(dils[gi], sl)
        outs.append(o)
        lses.append(lse)
    return outs, lses


MERGE_TM = 1024


def _token_order(ref, slab_ref):
    r, rows, width = ref.shape
    if r == 1:
        return ref[0]
    for c in range(r):
        blk = ref[c]
        for s in range(width // LANES):
            slab_ref[s, pl.ds(c, rows, stride=r), :] = blk[:, s * LANES:(s + 1) * LANES]
    return jnp.concatenate([slab_ref[s] for s in range(width // LANES)], axis=1)


def _merge_kernel(x_ref, oa_ref, o0_ref, o1_ref, o2_ref, l0_ref, l1_ref, l2_ref,
                  gmix_ref, wm_ref, wpn_ref, wpd_ref, wo_ref, gffn_ref, x1_ref, h2_ref, slab_ref):
    x = x_ref[...]
    D = x.shape[1]
    h = _rms(x, gmix_ref[...]).astype(BF16)
    gm = _sigmoid(_dot_nt(h, wm_ref[...]))
    o0, o1, o2 = [_token_order(r_, slab_ref) for r_ in (o0_ref, o1_ref, o2_ref)]
    l0, l1, l2 = [_token_order(r_, slab_ref) for r_ in (l0_ref, l1_ref, l2_ref)]
    mx = jnp.maximum(jnp.maximum(l0, l1), l2)
    e0, e1, e2 = jnp.exp(l0 - mx), jnp.exp(l1 - mx), jnp.exp(l2 - mx)
    inv = 1.0 / (e0 + e1 + e2)
    ob = o0 * (e0 * inv) + o1 * (e1 * inv) + o2 * (e2 * inv)
    a = _dot(oa_ref[...], wpn_ref[...])
    d = _dot(ob.astype(BF16), wpd_ref[...])
    mixed = gm[:, :D] * a + gm[:, D:] * d
    x1 = x + _dot(mixed.astype(BF16), wo_ref[...])
    x1_ref[...] = x1
    h2_ref[...] = _rms(x1, gffn_ref[...]).astype(BF16)


def _merge(x, o_a, outs, lses, g_mix, w_merge, w_proj_nsa, w_proj_dil, w_out, g_ffn):
    B, S, D = x.shape
    tm = MERGE_TM
    row = lambda a: pl.BlockSpec((None, tm, a.shape[2]), lambda b, i: (b, i, 0))
    cls = lambda a: pl.BlockSpec((None, a.shape[1], tm // a.shape[1], a.shape[3]), lambda b, i: (b, 0, i, 0))
    full = lambda a: pl.BlockSpec(a.shape, lambda b, i: (0,) * a.ndim)
    ws = [w_merge.astype(BF16), w_proj_nsa.astype(BF16), w_proj_dil.astype(BF16), w_out.astype(BF16)]
    consts = [g_mix, *ws, g_ffn]
    in_specs = [row(x), row(o_a)] + [cls(a) for a in (*outs, *lses)] + [full(a) for a in consts]
    return pl.pallas_call(
        _merge_kernel,
        grid=(B, S // tm),
        in_specs=in_specs,
        out_specs=(pl.BlockSpec((None, tm, D), lambda b, i: (b, i, 0)),) * 2,
        out_shape=(jax.ShapeDtypeStruct((B, S, D), F32), jax.ShapeDtypeStruct((B, S, D), BF16)),
        scratch_shapes=[pltpu.VMEM((DIL_WIDTH // LANES, tm, LANES), F32)],
        compiler_params=_params("parallel", "parallel"),
        name="merge_proj",
    )(x, o_a, *outs, *lses, *consts)


FFN_TM = 1024
FFN_TN = 256
HALO = 16


def _ffn_kernel(h_ref, halo_ref, x1_ref, wup_ref, cw_ref, cb_ref, wd_ref, gfin_ref, o_ref, act_ref):
    i = pl.program_id(1)
    h = h_ref[...]
    halo = halo_ref[...]
    tm = h.shape[0]
    row = lax.broadcasted_iota(jnp.int32, (tm, FFN_TN), 0)
    live = (i > 0).astype(F32)
    for j in range(D_FF // FFN_TN):
        cols = slice(j * FFN_TN, (j + 1) * FFN_TN)
        wu = wup_ref[:, cols]
        u = _dot(h, wu)
        uh = _dot(halo, wu) * live
        gate = _dot(h, wup_ref[:, D_FF + j * FFN_TN:D_FF + (j + 1) * FFN_TN])
        p1 = jnp.broadcast_to(uh[HALO - 1:HALO, :], (tm, FFN_TN))
        p2 = jnp.broadcast_to(uh[HALO - 2:HALO - 1, :], (tm, FFN_TN))
        u1 = jnp.where(row == 0, p1, pltpu.roll(u, 1, 0))
        u2 = jnp.where(row == 0, p2, jnp.where(row == 1, p1, pltpu.roll(u, 2, 0)))
        uc = cb_ref[:, cols] + cw_ref[0:1, cols] * u2
        uc = uc + cw_ref[1:2, cols] * u1
        uc = uc + cw_ref[2:3, cols] * u
        act_ref[:, j * FFN_TN:(j + 1) * FFN_TN] = (jax.nn.gelu(uc) * gate).astype(BF16)
    y = _dot(act_ref[...], wd_ref[...])
    o_ref[...] = _rms(x1_ref[...] + y, gfin_ref[...])


def _ffn(h2, x1, w_up, conv_w, conv_b, w_down, g_final):
    B, S, D = h2.shape
    tm = FFN_TM
    assert D_FF % FFN_TN == 0
    wup = w_up.astype(BF16)
    cw = conv_w
    cb = conv_b.reshape(1, D_FF)
    wd = w_down.astype(BF16)
    gfin = g_final.reshape(1, D)
    full = lambda a: pl.BlockSpec(a.shape, lambda b, i: (0,) * a.ndim, pipeline_mode=pl.Buffered(1))
    tile = pl.BlockSpec((None, tm, D), lambda b, i: (b, i, 0))
    halo = pl.BlockSpec((None, HALO, D), lambda b, i: (b, jnp.maximum(i * (tm // HALO) - 1, 0), 0))
    return pl.pallas_call(
        _ffn_kernel,
        grid=(B, S // tm),
        in_specs=[tile, halo, tile, full(wup), full(cw), full(cb), full(wd), full(gfin)],
        out_specs=tile,
        out_shape=jax.ShapeDtypeStruct((B, S, D), F32),
        scratch_shapes=[pltpu.VMEM((tm, D_FF), BF16)],
        compiler_params=_params("parallel", "parallel"),
        name="conv_ffn",
    )(h2, h2, x1, wup, cw, cb, wd, gfin)


@jax.jit
def _layer(x, g_mix, w_in, pe_cmp_k, w_cmp_k1, w_cmp_k2, pe_cmp_v, w_cmp_v1, w_cmp_v2,
           w_proj_nsa, w_proj_dil, w_out, g_ffn, w_up, conv_w, conv_b, w_down, g_final):
    B, S, D = x.shape
    depth = g_mix.shape[0]
    for l in range(depth):
        gm = g_mix[l].reshape(1, D)
        kcmp, vcmp, ks, kw, d0, d1, d2, qt, vst, vwt, gates = _in_proj(x, gm, w_in[l])
        kc, vct = _compress(kcmp, vcmp, pe_cmp_k[l], w_cmp_k1[l], w_cmp_k2[l],
                            pe_cmp_v[l], w_cmp_v1[l], w_cmp_v2[l])
        o_a = _nsa(qt, kc, vct, ks, vst, kw, vwt, gates)
        outs, lses = _dilated((d0, d1, d2))
        merge_cols = w_in[l].T[w_in.shape[2] - 2 * D:]
        x1, h2 = _merge(x, o_a, outs, lses, gm, merge_cols, w_proj_nsa[l], w_proj_dil[l], w_out[l],
                        g_ffn[l].reshape(1, D))
        x = _ffn(h2, x1, w_up[l], conv_w[l], conv_b[l], w_down[l], g_final)
        assert depth == 1
    return x


def kernel(x, g_mix, w_in, pe_cmp_k, w_cmp_k1, w_cmp_k2, pe_cmp_v, w_cmp_v1, w_cmp_v2, w_proj_nsa, w_proj_dil, w_out, g_ffn, w_up, conv_w, conv_b, w_down, g_final):
    return _layer(x, g_mix, w_in, pe_cmp_k, w_cmp_k1, w_cmp_k2, pe_cmp_v, w_cmp_v1, w_cmp_v2,
                  w_proj_nsa, w_proj_dil, w_out, g_ffn, w_up, conv_w, conv_b, w_down, g_final)
```

```python
import functools
import math

import numpy as np
import jax
import jax.numpy as jnp
from jax import lax
from jax.experimental import pallas as pl
from jax.experimental.pallas import tpu as pltpu

HEAD_DIM = 64
NSA_HEADS = 8
NSA_GROUPS = 2
NSA_REP = NSA_HEADS // NSA_GROUPS
CMP_BLOCK = 32
CMP_STRIDE = 16
CMP_HIDDEN = 128
SLC_BLOCK = 64
SLC_TOP = 16
NSA_WINDOW = 512
FORCE_SCORE = 1.0e4
DIL_CONFIGS = ((128, 1), (512, 4), (2048, 16))
DIL_GROUPS = 3
DIL_HEADS_PER_GROUP = 4
DIL_HEADS = DIL_GROUPS * DIL_HEADS_PER_GROUP
D_FF = 2816
CONV_WIDTH = 3
RMS_EPS = 1e-6
NEG_INF = -1e30

LANES = 128
VMEM_LIMIT_BYTES = 56 * 1024 * 1024

F32 = jnp.float32
BF16 = jnp.bfloat16
NT_DIMS = (((1,), (1,)), ((), ()))


def _alibi_slopes(n):
    return [float(2.0 ** (-8.0 * i / n)) for i in range(1, n + 1)]


def _rms(xf, g):
    ms = jnp.mean(xf * xf, axis=-1, keepdims=True)
    return xf * lax.rsqrt(ms + RMS_EPS) * g


def _dot(a, b):
    return jnp.dot(a, b, preferred_element_type=F32)


def _dot_nt(a, b):
    return lax.dot_general(a, b, NT_DIMS, preferred_element_type=F32)


def _sigmoid(z):
    return 1.0 / (1.0 + jnp.exp(-z))


def _params(*sem):
    return pltpu.CompilerParams(dimension_semantics=sem, vmem_limit_bytes=VMEM_LIMIT_BYTES)


IN_TM = 1024
N_KVC = 4 * HEAD_DIM
N_KSEL = NSA_GROUPS * HEAD_DIM
N_DIL = 3 * DIL_HEADS * HEAD_DIM
DIL_WIDTH = DIL_HEADS_PER_GROUP * HEAD_DIM
T_Q = NSA_HEADS * HEAD_DIM
T_V = NSA_GROUPS * HEAD_DIM
GATE_ROWS = 16
K_AUG = 2 * HEAD_DIM
SLC_SHIFT = int(math.log2(SLC_BLOCK))
AUG_NBLK = 8
AUG_HI, AUG_LO = AUG_NBLK, AUG_NBLK + 1
LOG2E = math.log2(math.e)
LN2 = math.log(2.0)
V_PAD = 16
V_ROWS = HEAD_DIM + V_PAD


def _key_position_columns(pos0, rows, step=1):
    pos = pos0 + step * lax.broadcasted_iota(jnp.int32, (rows, HEAD_DIM), 0)
    col = lax.broadcasted_iota(jnp.int32, (rows, HEAD_DIM), 1)
    blk = jnp.bitwise_and(lax.shift_right_logical(pos, SLC_SHIFT), AUG_NBLK - 1)
    hi = lax.shift_left(lax.shift_right_logical(pos, 7), 7).astype(F32)
    lo = jnp.bitwise_and(pos, 127).astype(F32)
    c = jnp.where((col == AUG_HI) | (col == AUG_HI + 2), hi,
                  jnp.where((col == AUG_LO) | (col == AUG_LO + 2), lo, 0.0))
    return jnp.where((col < AUG_NBLK) & (blk == col), 1.0, c)


def _in_proj_kernel(x_ref, g_ref, wn_ref, wt_ref,
                    kcmp_ref, vcmp_ref, ks_ref, kw_ref, d0_ref, d1_ref, d2_ref, qt_ref, vst_ref, vwt_ref, gate_ref,
                    slab_ref):
    tm = x_ref.shape[0]
    h = _rms(x_ref[...], g_ref[...]).astype(BF16)
    c0 = 0
    kvc = _dot_nt(h, wn_ref[c0:c0 + N_KVC, :])
    kcmp_ref[...] = kvc[:, :N_KSEL]
    vcmp_ref[...] = kvc[:, N_KSEL:]
    c0 += N_KVC
    ks = _dot_nt(h, wn_ref[c0:c0 + N_KSEL, :]).astype(BF16)
    c0 += N_KSEL
    kw = _dot_nt(h, wn_ref[c0:c0 + N_KSEL, :]).astype(BF16)
    c0 += N_KSEL
    aug = _key_position_columns(pl.program_id(1) * tm, tm).astype(BF16)
    for g in range(NSA_GROUPS):
        ks_ref[g] = jnp.concatenate([ks[:, g * HEAD_DIM:(g + 1) * HEAD_DIM], aug], axis=1)
        kw_ref[g] = jnp.concatenate([kw[:, g * HEAD_DIM:(g + 1) * HEAD_DIM], aug], axis=1)
    seg = DIL_HEADS * HEAD_DIM
    for which in range(3):
        y = _dot_nt(h, wn_ref[c0 + which * seg:c0 + (which + 1) * seg, :])
        for gi, (d_ref, (_, r)) in enumerate(zip((d0_ref, d1_ref, d2_ref), DIL_CONFIGS)):
            yg = y[:, gi * DIL_WIDTH:(gi + 1) * DIL_WIDTH]
            if r == 1:
                d_ref[which, 0] = yg.astype(BF16)
                continue
            for s in range(DIL_WIDTH // LANES):
                slab_ref[s] = yg[:, s * LANES:(s + 1) * LANES]
            for c in range(r):
                d_ref[which, c] = jnp.concatenate(
                    [slab_ref[s, pl.ds(c, tm // r, stride=r), :] for s in range(DIL_WIDTH // LANES)],
                    axis=1).astype(BF16)
    yt = _dot_nt(wt_ref[...], h)
    qt_ref[...] = yt[0:T_Q].astype(BF16)
    r0 = T_Q
    ones = jnp.where(lax.broadcasted_iota(jnp.int32, (V_PAD, tm), 0) == 0, 1.0, 0.0).astype(BF16)
    for ref in (vst_ref, vwt_ref):
        vt = yt[r0:r0 + T_V].astype(BF16)
        r0 += T_V
        for g in range(NSA_GROUPS):
            ref[g] = jnp.concatenate([vt[g * HEAD_DIM:(g + 1) * HEAD_DIM, :], ones], axis=0)
    gate_ref[...] = _sigmoid(yt[r0:r0 + NSA_GROUPS * GATE_ROWS])


def _in_proj(x, g_mix, w_in):
    B, S, D = x.shape
    scale = HEAD_DIM ** -0.5 * LOG2E
    o_q, o_kv = 0, T_Q
    o_gate = o_kv + 6 * N_KSEL
    o_dil = o_gate + 3 * NSA_HEADS
    o_merge = o_dil + N_DIL
    w_t = w_in.T
    kv = w_t[o_kv:o_gate]

    def kind(k):
        return kv[k * N_KSEL:(k + 1) * N_KSEL]

    dil = w_t[o_dil:o_merge]
    dil = jnp.concatenate([dil[:DIL_HEADS * HEAD_DIM] * scale, dil[DIL_HEADS * HEAD_DIM:]], axis=0)
    wn = jnp.concatenate([kind(0), kind(1), kind(2), kind(4), dil], axis=0).astype(BF16)
    wg = w_t[o_gate:o_dil].reshape(NSA_GROUPS, 3 * NSA_REP, D)
    wg = jnp.pad(wg, ((0, 0), (0, GATE_ROWS - 3 * NSA_REP), (0, 0))).reshape(NSA_GROUPS * GATE_ROWS, D)
    wt = jnp.concatenate([w_t[o_q:o_kv] * scale, kind(3), kind(5), wg], axis=0).astype(BF16)
    tm = IN_TM
    grid = (B, S // tm)
    full = lambda a: pl.BlockSpec(a.shape, lambda b, i: (0,) * a.ndim)
    k_shape = jax.ShapeDtypeStruct((B, NSA_GROUPS, S, K_AUG), BF16)
    v_shape = jax.ShapeDtypeStruct((B, NSA_GROUPS, V_ROWS, S), BF16)
    k_spec = pl.BlockSpec((None, NSA_GROUPS, tm, K_AUG), lambda b, i: (b, 0, i, 0))
    v_spec = pl.BlockSpec((None, NSA_GROUPS, V_ROWS, tm), lambda b, i: (b, 0, 0, i))
    c_shape = jax.ShapeDtypeStruct((B, S, N_KSEL), F32)
    c_spec = pl.BlockSpec((None, tm, N_KSEL), lambda b, i: (b, i, 0))
    d_shapes = tuple(jax.ShapeDtypeStruct((B, 3, r, S // r, DIL_WIDTH), BF16) for _, r in DIL_CONFIGS)
    d_specs = tuple(pl.BlockSpec((None, 3, r, tm // r, DIL_WIDTH), lambda b, i: (b, 0, 0, i, 0))
                    for _, r in DIL_CONFIGS)
    out_shape = (
        c_shape, c_shape, k_shape, k_shape, *d_shapes,
        jax.ShapeDtypeStruct((B, T_Q, S), BF16),
        v_shape,
        v_shape,
        jax.ShapeDtypeStruct((B, NSA_GROUPS * GATE_ROWS, S), F32),
    )
    out_specs = (
        c_spec, c_spec, k_spec, k_spec, *d_specs,
        pl.BlockSpec((None, T_Q, tm), lambda b, i: (b, 0, i)),
        v_spec,
        v_spec,
        pl.BlockSpec((None, NSA_GROUPS * GATE_ROWS, tm), lambda b, i: (b, 0, i)),
    )
    return pl.pallas_call(
        _in_proj_kernel,
        grid=grid,
        in_specs=[pl.BlockSpec((None, tm, D), lambda b, i: (b, i, 0)), full(g_mix), full(wn), full(wt)],
        out_specs=out_specs,
        out_shape=out_shape,
        scratch_shapes=[pltpu.VMEM((DIL_WIDTH // LANES, tm, LANES), F32)],
        compiler_params=_params("parallel", "parallel"),
        name="in_proj",
    )(x, g_mix, wn, wt)


def _compress_kernel(xk_ref, xv_ref, pek_ref, pev_ref, w1k_ref, w1v_ref, w2k_ref, w2vt_ref, kc_ref, vct_ref):
    nch = xk_ref.shape[0] // CMP_STRIDE
    streams = [(xk_ref, pek_ref, w1k_ref), (xv_ref, pev_ref, w1v_ref)]
    acc = [[jnp.zeros((nch, NSA_GROUPS * CMP_HIDDEN), F32) for _ in range(2)] for _ in streams]
    for j in range(CMP_STRIDE):
        for n, (x_ref, pe_ref, w1_ref) in enumerate(streams):
            xj = x_ref[pl.ds(j, nch, stride=CMP_STRIDE), :]
            for half in range(2):
                row = half * CMP_STRIDE + j
                acc[n][half] = acc[n][half] + _dot((xj + pe_ref[row:row + 1, :]).astype(BF16), w1_ref[row])
    hid = [jax.nn.gelu(lo + pltpu.roll(hi, nch - 1, 0)).astype(BF16) for lo, hi in acc]
    aug = _key_position_columns(CMP_BLOCK - 1, nch, CMP_STRIDE).astype(BF16)
    for g in range(NSA_GROUPS):
        cols = slice(g * CMP_HIDDEN, (g + 1) * CMP_HIDDEN)
        kc_ref[g] = jnp.concatenate([_dot(hid[0][:, cols], w2k_ref[...]).astype(BF16), aug], axis=1)
        vct_ref[g] = _dot_nt(w2vt_ref[...], hid[1][:, cols]).astype(BF16)


def _compress(xk, xv, pe_k, w1_k, w2_k, pe_v, w1_v, w2_v):
    B, S, width = xk.shape
    nch = S // CMP_STRIDE
    G, dh, hid = NSA_GROUPS, HEAD_DIM, CMP_HIDDEN
    eye = jnp.eye(G, dtype=w1_k.dtype)

    def expand(pe, w1):
        pe_t = jnp.broadcast_to(pe.reshape(CMP_BLOCK, 1, dh), (CMP_BLOCK, G, dh)).reshape(CMP_BLOCK, width)
        wexp = jnp.einsum('pdn,ge->pgden', w1.reshape(CMP_BLOCK, dh, hid), eye).reshape(CMP_BLOCK, width, G * hid)
        return pe_t, wexp.astype(BF16)

    pek, w1k = expand(pe_k, w1_k)
    pev, w1v = expand(pe_v, w1_v)
    consts = [pek, pev, w1k, w1v, w2_k.astype(BF16), w2_v.T.astype(BF16)]
    full = lambda a: pl.BlockSpec(a.shape, lambda b: (0,) * a.ndim)
    seq = pl.BlockSpec((None, S, width), lambda b: (b, 0, 0))
    return pl.pallas_call(
        _compress_kernel,
        grid=(B,),
        in_specs=[seq, seq] + [full(a) for a in consts],
        out_specs=(pl.BlockSpec((None, G, nch, K_AUG), lambda b: (b, 0, 0, 0)),
                   pl.BlockSpec((None, G, dh, nch), lambda b: (b, 0, 0, 0))),
        out_shape=(jax.ShapeDtypeStruct((B, G, nch, K_AUG), BF16),
                   jax.ShapeDtypeStruct((B, G, dh, nch), BF16)),
        compiler_params=_params("parallel"),
        name="compress",
    )(xk, xv, *consts)


NSA_TQ = 256
SWEEP_TK = AUG_NBLK * SLC_BLOCK
SEL_SUB = 128
SEL_AHEAD = 4


def _rowmax8(s):
    return jnp.max(s.reshape(s.shape[0] // 8, 8, s.shape[1]), axis=0)


def _online_softmax(chunks, scores, values):
    pending = [scores(ch) for ch in chunks[:SEL_AHEAD]]
    m = acc = None
    for n, ch in enumerate(chunks):
        if n + SEL_AHEAD < len(chunks):
            pending.append(scores(chunks[n + SEL_AHEAD]))
        s = pending.pop(0)
        m_c = jnp.max(_rowmax8(s), axis=0, keepdims=True)
        if m is None:
            m = m_c
            acc = _dot(values(ch), jnp.exp2(s - m).astype(BF16))
        else:
            m_new = jnp.maximum(m, m_c)
            acc = jnp.exp2(m - m_new) * acc + _dot(values(ch), jnp.exp2(s - m_new).astype(BF16))
            m = m_new
    return acc


def _nsa_kernel(qt_ref, kc_ref, vct_ref, ks_ref, vst_ref, kw_ref, vwt_ref, gate_ref, ov_ref,
                o_ref, sbt_ref, osel_ref, idx_ref, *, slopes):
    g = pl.program_id(1)
    i = pl.program_id(2)
    R, dh, tq, tk = NSA_REP, HEAD_DIM, NSA_TQ, SWEEP_TK
    L = R * tq
    t0 = i * tq
    nc = kc_ref.shape[0]
    ns = ov_ref.shape[0]
    n_tiles = ns // AUG_NBLK
    tile4 = lambda a: jnp.concatenate([a] * R, axis=1)

    qt = qt_ref[...]
    qs = jnp.concatenate([qt[r * dh:(r + 1) * dh, :] for r in range(R)], axis=1)
    slope = [jnp.where(g == 0, slopes[r], slopes[R + r]).astype(F32) for r in range(R)]
    slope_row = jnp.concatenate([jnp.full((1, tq), 1.0, F32) * slope[r] for r in range(R)], axis=1)
    t_row = t0 + lax.broadcasted_iota(jnp.int32, (1, tq), 1)

    r8 = lax.broadcasted_iota(jnp.int32, (AUG_NBLK, L), 0)
    s_full = slope_row * LOG2E
    s_hi = s_full.astype(BF16).astype(F32)
    alibi8 = jnp.where(r8 < 2, s_hi, jnp.where(r8 < 4, s_full - s_hi, 0.0))
    q_pad = jnp.zeros((K_AUG - dh - 2 * AUG_NBLK, L), BF16)
    q_plain = jnp.concatenate([qs, jnp.concatenate([jnp.zeros((AUG_NBLK, L), F32), alibi8], axis=0).astype(BF16),
                               q_pad], axis=0)

    cmp_end = lax.broadcasted_iota(jnp.int32, (nc, tq), 0) * CMP_STRIDE + (CMP_BLOCK - 1)
    m_cmp = t_row >= cmp_end
    sc = _dot(kc_ref[...], q_plain) + tile4(jnp.where(m_cmp, 0.0, NEG_INF))

    a0 = jnp.maximum(t0 - NSA_WINDOW, 0)
    row_w = lax.broadcasted_iota(jnp.int32, (SEL_SUB, tq), 0)
    row_minus_lane = row_w - lax.broadcasted_iota(jnp.int32, (SEL_SUB, tq), 1)

    def causal_bias(k0):
        return jnp.where(row_minus_lane <= t0 - k0, 0.0, NEG_INF)

    def win_scores(ch):
        diag, r = ch
        k0 = pl.multiple_of((t0 if diag else a0) + r, SEL_SUB)
        if diag:
            bias = causal_bias(k0)
        else:
            bias = jnp.where((row_w < t0 - k0) & (row_minus_lane > t0 - k0 - NSA_WINDOW), 0.0, NEG_INF)
        return _dot(kw_ref[pl.ds(k0, SEL_SUB), :], q_plain) + tile4(bias)

    def win_values(ch):
        diag, r = ch
        return vwt_ref[:, pl.ds(pl.multiple_of((t0 if diag else a0) + r, SEL_SUB), SEL_SUB)]

    win_chunks = ([(True, r) for r in range(0, tq, SEL_SUB)]
                  + [(False, r) for r in range(0, NSA_WINDOW, SEL_SUB)])
    acc_w = _online_softmax(win_chunks, win_scores, win_values)
    o_win = acc_w[:dh] * (1.0 / acc_w[dh:dh + 1])

    mx = jnp.max(_rowmax8(sc), axis=0, keepdims=True)
    p = jnp.exp2(sc - mx) * tile4(jnp.where(m_cmp, 1.0, 0.0))
    den = jnp.sum(jnp.sum(p.reshape(nc // 8, 8, L), axis=0), axis=0, keepdims=True)
    pr = p * (1.0 / jnp.maximum(den, 1e-30))
    o_cmp = _dot(vct_ref[...], pr.astype(BF16))
    psum = pr[:, 0:tq]
    for r in range(1, R):
        psum = psum + pr[:, r * tq:(r + 1) * tq]

    p_hi = psum.astype(BF16)
    p_lo = (psum - p_hi.astype(F32)).astype(BF16)
    imp = _dot(ov_ref[...], p_hi) + _dot(ov_ref[...], p_lo)
    blk = lax.broadcasted_iota(jnp.int32, (ns, tq), 0)
    cur = lax.shift_right_logical(t_row, SLC_SHIFT)
    val = jnp.where((blk == cur) | (blk == 0), FORCE_SCORE, imp)
    val = jnp.where(blk <= cur, val, -1.0)
    vals = [val[8 * v:8 * v + 8, :] for v in range(ns // 8)]
    ranks = [jnp.zeros((8, tq), F32) for _ in vals]
    row8 = lax.broadcasted_iota(jnp.int32, (8, tq), 0)
    for j in range(ns):
        vj = jnp.broadcast_to(val[j:j + 1, :], (8, tq))
        for v in range(len(vals)):
            if 8 * v > j:
                ahead = vj >= vals[v]
            elif 8 * v + 7 <= j:
                ahead = vj > vals[v]
            else:
                ahead = (vj > vals[v]) | ((vj == vals[v]) & (row8 > j - 8 * v))
            ranks[v] = ranks[v] + jnp.where(ahead, 1.0, 0.0)
    for T in range(n_tiles):
        selb = jnp.where((ranks[T] < float(SLC_TOP)) & (vals[T] >= 0.0), 0.0, NEG_INF)
        sbt_ref[T] = jnp.concatenate([tile4(selb), alibi8], axis=0).astype(BF16)

    td = lax.div(t0, tk)
    cnt = jnp.int32(0)
    for T in range(n_tiles - 1):
        picked = jnp.where((ranks[T] < float(SLC_TOP)) & (vals[T] >= 0.0), 1.0, 0.0)
        idx_ref[cnt] = jnp.int32(T)
        cnt = cnt + jnp.logical_and(jnp.max(picked) > 0.0, T < td).astype(jnp.int32)
    for k in range(n_tiles):
        @pl.when(cnt == k)
        def _(k=k):
            past = [idx_ref[j] for j in range(k)]
            q_past = [jnp.concatenate([qs, sbt_ref[T], q_pad], axis=0) for T in past]
            q_diag = jnp.concatenate([qs, sbt_ref[td], q_pad], axis=0)
            chunks = ([(j, r) for j in range(k) for r in range(0, tk, SEL_SUB)]
                      + [(None, r) for r in range(0, tk, SEL_SUB)])

            def key_start(ch):
                j, r = ch
                return pl.multiple_of((td if j is None else past[j]) * tk + r, SEL_SUB)

            def chunk_scores(ch):
                k0 = key_start(ch)
                if ch[0] is None:
                    return _dot(ks_ref[pl.ds(k0, SEL_SUB), :], q_diag) + tile4(causal_bias(k0))
                return _dot(ks_ref[pl.ds(k0, SEL_SUB), :], q_past[ch[0]])

            acc = _online_softmax(chunks, chunk_scores, lambda ch: vst_ref[:, pl.ds(key_start(ch), SEL_SUB)])
            osel_ref[...] = acc[:dh] * (1.0 / acc[dh:dh + 1])

    o_sel = osel_ref[...]
    gates = gate_ref[...]
    tiles = []
    for r in range(R):
        sl = slice(r * tq, (r + 1) * tq)
        tiles.append(gates[3 * r:3 * r + 1, :] * o_cmp[:, sl]
                     + gates[3 * r + 1:3 * r + 2, :] * o_sel[:, sl]
                     + gates[3 * r + 2:3 * r + 3, :] * o_win[:, sl])
    o_ref[...] = jnp.concatenate(tiles, axis=0).T.astype(o_ref.dtype)


def _overlap_matrix(nc, ns):
    cs = np.arange(nc)[None, :] * CMP_STRIDE
    ss = np.arange(ns)[:, None] * SLC_BLOCK
    ov = np.clip(np.minimum(cs + CMP_BLOCK, ss + SLC_BLOCK) - np.maximum(cs, ss), 0, None)
    return jnp.asarray(ov.astype(np.float32) / CMP_BLOCK, dtype=BF16)


def _nsa(qt, kc, vct, ks, vst, kw, vwt, gates):
    B, _, S = qt.shape
    G, R, dh, tq = NSA_GROUPS, NSA_REP, HEAD_DIM, NSA_TQ
    nc = kc.shape[2]
    ns = S // SLC_BLOCK
    ov = _overlap_matrix(nc, ns)
    kern = functools.partial(_nsa_kernel, slopes=tuple(_alibi_slopes(NSA_HEADS)))
    assert S % SWEEP_TK == 0 and S >= NSA_WINDOW + tq
    per_bg = lambda shape: pl.BlockSpec((None, None) + shape, lambda b, g, i: (b, g, 0, 0))
    return pl.pallas_call(
        kern,
        grid=(B, G, S // tq),
        in_specs=[
            pl.BlockSpec((None, R * dh, tq), lambda b, g, i: (b, g, i)),
            per_bg((nc, K_AUG)), per_bg((dh, nc)),
            per_bg((S, K_AUG)), per_bg((V_ROWS, S)),
            per_bg((S, K_AUG)), per_bg((V_ROWS, S)),
            pl.BlockSpec((None, GATE_ROWS, tq), lambda b, g, i: (b, g, i)),
            pl.BlockSpec(ov.shape, lambda b, g, i: (0, 0)),
        ],
        out_specs=pl.BlockSpec((None, tq, R * dh), lambda b, g, i: (b, i, g)),
        out_shape=jax.ShapeDtypeStruct((B, S, G * R * dh), BF16),
        scratch_shapes=[pltpu.VMEM((S // SWEEP_TK, 2 * AUG_NBLK, R * tq), BF16),
                        pltpu.VMEM((dh, R * tq), F32),
                        pltpu.SMEM((S // SWEEP_TK,), jnp.int32)],
        compiler_params=_params("parallel", "parallel", "arbitrary"),
        name="nsa_attention",
    )(qt, kc, vct, ks, vst, kw, vwt, gates, ov)


BAND_TQ = 128
BAND_SUB = 8
BAND_PROBLEMS = 8
BAND = 128


def _banded_kernel(q_ref, k_ref, v_ref, o_ref, lse_ref, *, slopes, nsub, ncls):
    tq, dh, nh = BAND_TQ, HEAD_DIM, DIL_HEADS_PER_GROUP
    tk = tq + BAND
    width = nh * dh
    head_of_lane = lambda rows: lax.shift_right_logical(
        lax.broadcasted_iota(jnp.int32, (rows, width), 1), int(math.log2(dh)))
    lane_head, q_head = head_of_lane(tk), head_of_lane(tq)
    keep = [jnp.where(lane_head == h, 1.0, 0.0).astype(BF16) for h in range(nh)]

    def per_head(a):
        return jnp.concatenate([a * keep[h] for h in range(nh)], axis=0)

    def biases(first_key_offset):
        d = first_key_offset + (lax.broadcasted_iota(jnp.int32, (tq, tk), 0)
                                - lax.broadcasted_iota(jnp.int32, (tq, tk), 1))
        mask_bias = jnp.where((d >= 0) & (d <= BAND), 0.0, NEG_INF)
        neg_d = -d.astype(F32)
        return [slopes[h] * neg_d + mask_bias for h in range(nh)]

    subs = []
    for cls in range(ncls):
        for sub in range(nsub):
            i = pl.program_id(2) * nsub + sub
            k0 = pl.multiple_of(jnp.maximum(i - 1, 0) * tq, tq)
            q = q_ref[cls, sub * tq:(sub + 1) * tq, :]
            scores = _dot_nt(q, per_head(k_ref[cls, pl.ds(k0, tk), :]))
            subs.append((cls, sub, i * tq - k0, k0, scores))
    first_bias = biases(subs[0][2])
    inner_bias = biases(tq) if nsub > 1 else None
    probs = []
    for cls, sub, off, k0, s in subs:
        bias = first_bias if sub == 0 else inner_bias
        ps, mxs, dens = [], [], []
        for h in range(nh):
            sh = s[:, h * tk:(h + 1) * tk] + bias[h]
            mx = jnp.max(sh, axis=-1, keepdims=True)
            e = jnp.exp2(sh - mx)
            dens.append(jnp.sum(e, axis=-1, keepdims=True))
            ps.append(e.astype(BF16))
            mxs.append(mx)
        probs.append((cls, sub, k0, jnp.concatenate(ps, axis=1), mxs, dens))
    for cls, sub, k0, p, mxs, dens in probs:
        mx_all, den = mxs[nh - 1], dens[nh - 1]
        for h in range(nh - 2, -1, -1):
            mx_all = jnp.where(q_head == h, mxs[h], mx_all)
            den = jnp.where(q_head == h, dens[h], den)
        rows = slice(sub * tq, (sub + 1) * tq)
        o_ref[cls, rows, :] = _dot(p, per_head(v_ref[cls, pl.ds(k0, tk), :])) * (1.0 / den)
        lse_ref[cls, rows, :] = mx_all * LN2 + jnp.log(den)


def _banded(d, slopes):
    B, _, r, n, width = d.shape
    nsub = min(BAND_SUB, n // BAND_TQ)
    ncls = min(r, BAND_PROBLEMS // nsub)
    tq = BAND_TQ * nsub
    assert n >= BAND_TQ + BAND and n % tq == 0 and r % ncls == 0
    kern = functools.partial(_banded_kernel, slopes=tuple(slopes), nsub=nsub, ncls=ncls)
    tile = pl.BlockSpec((None, None, ncls, tq, width), lambda b, c, i: (b, 0, c, i, 0))
    kseq = pl.BlockSpec((None, None, ncls, n, width), lambda b, c, i: (b, 1, c, 0, 0))
    vseq = pl.BlockSpec((None, None, ncls, n, width), lambda b, c, i: (b, 2, c, 0, 0))
    out = pl.BlockSpec((None, ncls, tq, width), lambda b, c, i: (b, c, i, 0))
    return pl.pallas_call(
        kern,
        grid=(B, r // ncls, n // tq),
        in_specs=[tile, kseq, vseq],
        out_specs=(out, out),
        out_shape=(jax.ShapeDtypeStruct((B, r, n, width), F32),) * 2,
        compiler_params=_params("parallel", "parallel", "arbitrary"),
        name="banded_attention",
    )(d, d, d)


def _dilated(dils):
    slopes = _alibi_slopes(DIL_HEADS)
    outs, lses = [], []
    for gi, (w, r) in enumerate(DIL_CONFIGS):
        assert w // r == BAND
        sl = [s_ * r * LOG2E for s_ in slopes[gi * DIL_HEADS_PER_GROUP:(gi + 1) * DIL_HEADS_PER_GROUP]]
        o, lse = _banded(dils[gi], sl)
        outs.append(o)
        lses.append(lse)
    return outs, lses


MERGE_TM = 1024


def _token_order(ref, slab_ref):
    r, rows, width = ref.shape
    if r == 1:
        return ref[0]
    for c in range(r):
        blk = ref[c]
        for s in range(width // LANES):
            slab_ref[s, pl.ds(c, rows, stride=r), :] = blk[:, s * LANES:(s + 1) * LANES]
    return jnp.concatenate([slab_ref[s] for s in range(width // LANES)], axis=1)


def _merge_kernel(x_ref, oa_ref, o0_ref, o1_ref, o2_ref, l0_ref, l1_ref, l2_ref,
                  gmix_ref, wm_ref, wpn_ref, wpd_ref, wo_ref, gffn_ref, x1_ref, h2_ref, slab_ref):
    x = x_ref[...]
    D = x.shape[1]
    h = _rms(x, gmix_ref[...]).astype(BF16)
    gm = _sigmoid(_dot_nt(h, wm_ref[...]))
    o0, o1, o2 = [_token_order(r_, slab_ref) for r_ in (o0_ref, o1_ref, o2_ref)]
    l0, l1, l2 = [_token_order(r_, slab_ref) for r_ in (l0_ref, l1_ref, l2_ref)]
    mx = jnp.maximum(jnp.maximum(l0, l1), l2)
    e0, e1, e2 = jnp.exp(l0 - mx), jnp.exp(l1 - mx), jnp.exp(l2 - mx)
    inv = 1.0 / (e0 + e1 + e2)
    ob = o0 * (e0 * inv) + o1 * (e1 * inv) + o2 * (e2 * inv)
    a = _dot(oa_ref[...], wpn_ref[...])
    d = _dot(ob.astype(BF16), wpd_ref[...])
    mixed = gm[:, :D] * a + gm[:, D:] * d
    x1 = x + _dot(mixed.astype(BF16), wo_ref[...])
    x1_ref[...] = x1
    h2_ref[...] = _rms(x1, gffn_ref[...]).astype(BF16)


def _merge(x, o_a, outs, lses, g_mix, w_merge, w_proj_nsa, w_proj_dil, w_out, g_ffn):
    B, S, D = x.shape
    tm = MERGE_TM
    row = lambda a: pl.BlockSpec((None, tm, a.shape[2]), lambda b, i: (b, i, 0))
    cls = lambda a: pl.BlockSpec((None, a.shape[1], tm // a.shape[1], a.shape[3]), lambda b, i: (b, 0, i, 0))
    full = lambda a: pl.BlockSpec(a.shape, lambda b, i: (0,) * a.ndim)
    ws = [w_merge.astype(BF16), w_proj_nsa.astype(BF16), w_proj_dil.astype(BF16), w_out.astype(BF16)]
    consts = [g_mix, *ws, g_ffn]
    in_specs = [row(x), row(o_a)] + [cls(a) for a in (*outs, *lses)] + [full(a) for a in consts]
    return pl.pallas_call(
        _merge_kernel,
        grid=(B, S // tm),
        in_specs=in_specs,
        out_specs=(pl.BlockSpec((None, tm, D), lambda b, i: (b, i, 0)),) * 2,
        out_shape=(jax.ShapeDtypeStruct((B, S, D), F32), jax.ShapeDtypeStruct((B, S, D), BF16)),
        scratch_shapes=[pltpu.VMEM((DIL_WIDTH // LANES, tm, LANES), F32)],
        compiler_params=_params("parallel", "parallel"),
        name="merge_proj",
    )(x, o_a, *outs, *lses, *consts)


FFN_TM = 1024
FFN_TN = 256
HALO = 16


def _ffn_kernel(h_ref, halo_ref, x1_ref, wup_ref, cw_ref, cb_ref, wd_ref, gfin_ref, o_ref, act_ref):
    i = pl.program_id(1)
    h = h_ref[...]
    halo = halo_ref[...]
    tm = h.shape[0]
    row = lax.broadcasted_iota(jnp.int32, (tm, FFN_TN), 0)
    live = (i > 0).astype(F32)
    for j in range(D_FF // FFN_TN):
        cols = slice(j * FFN_TN, (j + 1) * FFN_TN)
        wu = wup_ref[:, cols]
        u = _dot(h, wu)
        uh = _dot(halo, wu) * live
        gate = _dot(h, wup_ref[:, D_FF + j * FFN_TN:D_FF + (j + 1) * FFN_TN])
        p1 = jnp.broadcast_to(uh[HALO - 1:HALO, :], (tm, FFN_TN))
        p2 = jnp.broadcast_to(uh[HALO - 2:HALO - 1, :], (tm, FFN_TN))
        u1 = jnp.where(row == 0, p1, pltpu.roll(u, 1, 0))
        u2 = jnp.where(row == 0, p2, jnp.where(row == 1, p1, pltpu.roll(u, 2, 0)))
        uc = cb_ref[:, cols] + cw_ref[0:1, cols] * u2
        uc = uc + cw_ref[1:2, cols] * u1
        uc = uc + cw_ref[2:3, cols] * u
        act_ref[:, j * FFN_TN:(j + 1) * FFN_TN] = (jax.nn.gelu(uc) * gate).astype(BF16)
    y = _dot(act_ref[...], wd_ref[...])
    o_ref[...] = _rms(x1_ref[...] + y, gfin_ref[...])


def _ffn(h2, x1, w_up, conv_w, conv_b, w_down, g_final):
    B, S, D = h2.shape
    tm = FFN_TM
    assert D_FF % FFN_TN == 0
    wup = w_up.astype(BF16)
    cw = conv_w
    cb = conv_b.reshape(1, D_FF)
    wd = w_down.astype(BF16)
    gfin = g_final.reshape(1, D)
    full = lambda a: pl.BlockSpec(a.shape, lambda b, i: (0,) * a.ndim, pipeline_mode=pl.Buffered(1))
    tile = pl.BlockSpec((None, tm, D), lambda b, i: (b, i, 0))
    halo = pl.BlockSpec((None, HALO, D), lambda b, i: (b, jnp.maximum(i * (tm // HALO) - 1, 0), 0))
    return pl.pallas_call(
        _ffn_kernel,
        grid=(B, S // tm),
        in_specs=[tile, halo, tile, full(wup), full(cw), full(cb), full(wd), full(gfin)],
        out_specs=tile,
        out_shape=jax.ShapeDtypeStruct((B, S, D), F32),
        scratch_shapes=[pltpu.VMEM((tm, D_FF), BF16)],
        compiler_params=_params("parallel", "parallel"),
        name="conv_ffn",
    )(h2, h2, x1, wup, cw, cb, wd, gfin)


@jax.jit
def _layer(x, g_mix, w_in, pe_cmp_k, w_cmp_k1, w_cmp_k2, pe_cmp_v, w_cmp_v1, w_cmp_v2,
           w_proj_nsa, w_proj_dil, w_out, g_ffn, w_up, conv_w, conv_b, w_down, g_final):
    B, S, D = x.shape
    depth = g_mix.shape[0]
    for l in range(depth):
        gm = g_mix[l].reshape(1, D)
        kcmp, vcmp, ks, kw, d0, d1, d2, qt, vst, vwt, gates = _in_proj(x, gm, w_in[l])
        kc, vct = _compress(kcmp, vcmp, pe_cmp_k[l], w_cmp_k1[l], w_cmp_k2[l],
                            pe_cmp_v[l], w_cmp_v1[l], w_cmp_v2[l])
        o_a = _nsa(qt, kc, vct, ks, vst, kw, vwt, gates)
        outs, lses = _dilated((d0, d1, d2))
        merge_cols = w_in[l].T[w_in.shape[2] - 2 * D:]
        x1, h2 = _merge(x, o_a, outs, lses, gm, merge_cols, w_proj_nsa[l], w_proj_dil[l], w_out[l],
                        g_ffn[l].reshape(1, D))
        x = _ffn(h2, x1, w_up[l], conv_w[l], conv_b[l], w_down[l], g_final)
        assert depth == 1
    return x


def kernel(x, g_mix, w_in, pe_cmp_k, w_cmp_k1, w_cmp_k2, pe_cmp_v, w_cmp_v1, w_cmp_v2, w_proj_nsa, w_proj_dil, w_out, g_ffn, w_up, conv_w, conv_b, w_down, g_final):
    return _layer(x, g_mix, w_in, pe_cmp_k, w_cmp_k1, w_cmp_k2, pe_cmp_v, w_cmp_v1, w_cmp_v2,
                  w_proj_nsa, w_proj_dil, w_out, g_ffn, w_up, conv_w, conv_b, w_down, g_final)
```

```python
import functools
import math

import numpy as np
import jax
import jax.numpy as jnp
from jax import lax
from jax.experimental import pallas as pl
from jax.experimental.pallas import tpu as pltpu

HEAD_DIM = 64
NSA_HEADS = 8
NSA_GROUPS = 2
NSA_REP = NSA_HEADS // NSA_GROUPS
CMP_BLOCK = 32
CMP_STRIDE = 16
CMP_HIDDEN = 128
SLC_BLOCK = 64
SLC_TOP = 16
NSA_WINDOW = 512
FORCE_SCORE = 1.0e4
DIL_CONFIGS = ((128, 1), (512, 4), (2048, 16))
DIL_GROUPS = 3
DIL_HEADS_PER_GROUP = 4
DIL_HEADS = DIL_GROUPS * DIL_HEADS_PER_GROUP
D_FF = 2816
CONV_WIDTH = 3
RMS_EPS = 1e-6
NEG_INF = -1e30

LANES = 128
VMEM_LIMIT_BYTES = 56 * 1024 * 1024

F32 = jnp.float32
BF16 = jnp.bfloat16
NT_DIMS = (((1,), (1,)), ((), ()))


def _alibi_slopes(n):
    return [float(2.0 ** (-8.0 * i / n)) for i in range(1, n + 1)]


def _rms(xf, g):
    ms = jnp.mean(xf * xf, axis=-1, keepdims=True)
    return xf * lax.rsqrt(ms + RMS_EPS) * g


def _dot(a, b):
    return jnp.dot(a, b, preferred_element_type=F32)


def _dot_nt(a, b):
    return lax.dot_general(a, b, NT_DIMS, preferred_element_type=F32)


def _sigmoid(z):
    return 1.0 / (1.0 + jnp.exp(-z))


def _params(*sem):
    return pltpu.CompilerParams(dimension_semantics=sem, vmem_limit_bytes=VMEM_LIMIT_BYTES)


IN_TM = 1024
N_KVC = 4 * HEAD_DIM
N_KSEL = NSA_GROUPS * HEAD_DIM
N_DIL = 3 * DIL_HEADS * HEAD_DIM
DIL_WIDTH = DIL_HEADS_PER_GROUP * HEAD_DIM
T_Q = NSA_HEADS * HEAD_DIM
T_V = NSA_GROUPS * HEAD_DIM
GATE_ROWS = 16
K_AUG = 2 * HEAD_DIM
SLC_SHIFT = int(math.log2(SLC_BLOCK))
AUG_NBLK = 8
AUG_HI, AUG_LO = AUG_NBLK, AUG_NBLK + 1
LOG2E = math.log2(math.e)
LN2 = math.log(2.0)
V_PAD = 16
V_ROWS = HEAD_DIM + V_PAD


def _key_position_columns(pos0, rows, step=1):
    pos = pos0 + step * lax.broadcasted_iota(jnp.int32, (rows, HEAD_DIM), 0)
    col = lax.broadcasted_iota(jnp.int32, (rows, HEAD_DIM), 1)
    blk = jnp.bitwise_and(lax.shift_right_logical(pos, SLC_SHIFT), AUG_NBLK - 1)
    hi = lax.shift_left(lax.shift_right_logical(pos, 7), 7).astype(F32)
    lo = jnp.bitwise_and(pos, 127).astype(F32)
    c = jnp.where((col == AUG_HI) | (col == AUG_HI + 2), hi,
                  jnp.where((col == AUG_LO) | (col == AUG_LO + 2), lo, 0.0))
    return jnp.where((col < AUG_NBLK) & (blk == col), 1.0, c)


def _in_proj_kernel(x_ref, g_ref, wn_ref, wt_ref,
                    kcmp_ref, vcmp_ref, ks_ref, kw_ref, d0_ref, d1_ref, d2_ref, qt_ref, vst_ref, vwt_ref, gate_ref,
                    slab_ref):
    tm = x_ref.shape[0]
    h = _rms(x_ref[...], g_ref[...]).astype(BF16)
    c0 = 0
    kvc = _dot_nt(h, wn_ref[c0:c0 + N_KVC, :])
    kcmp_ref[...] = kvc[:, :N_KSEL]
    vcmp_ref[...] = kvc[:, N_KSEL:]
    c0 += N_KVC
    ks = _dot_nt(h, wn_ref[c0:c0 + N_KSEL, :]).astype(BF16)
    c0 += N_KSEL
    kw = _dot_nt(h, wn_ref[c0:c0 + N_KSEL, :]).astype(BF16)
    c0 += N_KSEL
    aug = _key_position_columns(pl.program_id(1) * tm, tm).astype(BF16)
    for g in range(NSA_GROUPS):
        ks_ref[g] = jnp.concatenate([ks[:, g * HEAD_DIM:(g + 1) * HEAD_DIM], aug], axis=1)
        kw_ref[g] = jnp.concatenate([kw[:, g * HEAD_DIM:(g + 1) * HEAD_DIM], aug], axis=1)
    seg = DIL_HEADS * HEAD_DIM
    for which in range(3):
        y = _dot_nt(h, wn_ref[c0 + which * seg:c0 + (which + 1) * seg, :])
        for gi, (d_ref, (_, r)) in enumerate(zip((d0_ref, d1_ref, d2_ref), DIL_CONFIGS)):
            yg = y[:, gi * DIL_WIDTH:(gi + 1) * DIL_WIDTH]
            if r == 1:
                d_ref[which, 0] = yg.astype(BF16)
                continue
            for s in range(DIL_WIDTH // LANES):
                slab_ref[s] = yg[:, s * LANES:(s + 1) * LANES]
            for c in range(r):
                d_ref[which, c] = jnp.concatenate(
                    [slab_ref[s, pl.ds(c, tm // r, stride=r), :] for s in range(DIL_WIDTH // LANES)],
                    axis=1).astype(BF16)
    yt = _dot_nt(wt_ref[...], h)
    qt_ref[...] = yt[0:T_Q].astype(BF16)
    r0 = T_Q
    ones = jnp.where(lax.broadcasted_iota(jnp.int32, (V_PAD, tm), 0) == 0, 1.0, 0.0).astype(BF16)
    for ref in (vst_ref, vwt_ref):
        vt = yt[r0:r0 + T_V].astype(BF16)
        r0 += T_V
        for g in range(NSA_GROUPS):
            ref[g] = jnp.concatenate([vt[g * HEAD_DIM:(g + 1) * HEAD_DIM, :], ones], axis=0)
    gate_ref[...] = _sigmoid(yt[r0:r0 + NSA_GROUPS * GATE_ROWS])


def _in_proj(x, g_mix, w_in):
    B, S, D = x.shape
    scale = HEAD_DIM ** -0.5 * LOG2E
    o_q, o_kv = 0, T_Q
    o_gate = o_kv + 6 * N_KSEL
    o_dil = o_gate + 3 * NSA_HEADS
    o_merge = o_dil + N_DIL
    w_t = w_in.T
    kv = w_t[o_kv:o_gate]

    def kind(k):
        return kv[k * N_KSEL:(k + 1) * N_KSEL]

    dil = w_t[o_dil:o_merge]
    dil = jnp.concatenate([dil[:DIL_HEADS * HEAD_DIM] * scale, dil[DIL_HEADS * HEAD_DIM:]], axis=0)
    wn = jnp.concatenate([kind(0), kind(1), kind(2), kind(4), dil], axis=0).astype(BF16)
    wg = w_t[o_gate:o_dil].reshape(NSA_GROUPS, 3 * NSA_REP, D)
    wg = jnp.pad(wg, ((0, 0), (0, GATE_ROWS - 3 * NSA_REP), (0, 0))).reshape(NSA_GROUPS * GATE_ROWS, D)
    wt = jnp.concatenate([w_t[o_q:o_kv] * scale, kind(3), kind(5), wg], axis=0).astype(BF16)
    tm = IN_TM
    grid = (B, S // tm)
    full = lambda a: pl.BlockSpec(a.shape, lambda b, i: (0,) * a.ndim)
    k_shape = jax.ShapeDtypeStruct((B, NSA_GROUPS, S, K_AUG), BF16)
    v_shape = jax.ShapeDtypeStruct((B, NSA_GROUPS, V_ROWS, S), BF16)
    k_spec = pl.BlockSpec((None, NSA_GROUPS, tm, K_AUG), lambda b, i: (b, 0, i, 0))
    v_spec = pl.BlockSpec((None, NSA_GROUPS, V_ROWS, tm), lambda b, i: (b, 0, 0, i))
    c_shape = jax.ShapeDtypeStruct((B, S, N_KSEL), F32)
    c_spec = pl.BlockSpec((None, tm, N_KSEL), lambda b, i: (b, i, 0))
    d_shapes = tuple(jax.ShapeDtypeStruct((B, 3, r, S // r, DIL_WIDTH), BF16) for _, r in DIL_CONFIGS)
    d_specs = tuple(pl.BlockSpec((None, 3, r, tm // r, DIL_WIDTH), lambda b, i: (b, 0, 0, i, 0))
                    for _, r in DIL_CONFIGS)
    out_shape = (
        c_shape, c_shape, k_shape, k_shape, *d_shapes,
        jax.ShapeDtypeStruct((B, T_Q, S), BF16),
        v_shape,
        v_shape,
        jax.ShapeDtypeStruct((B, NSA_GROUPS * GATE_ROWS, S), F32),
    )
    out_specs = (
        c_spec, c_spec, k_spec, k_spec, *d_specs,
        pl.BlockSpec((None, T_Q, tm), lambda b, i: (b, 0, i)),
        v_spec,
        v_spec,
        pl.BlockSpec((None, NSA_GROUPS * GATE_ROWS, tm), lambda b, i: (b, 0, i)),
    )
    return pl.pallas_call(
        _in_proj_kernel,
        grid=grid,
        in_specs=[pl.BlockSpec((None, tm, D), lambda b, i: (b, i, 0)), full(g_mix), full(wn), full(wt)],
        out_specs=out_specs,
        out_shape=out_shape,
        scratch_shapes=[pltpu.VMEM((DIL_WIDTH // LANES, tm, LANES), F32)],
        compiler_params=_params("parallel", "parallel"),
        name="in_proj",
    )(x, g_mix, wn, wt)


def _compress_kernel(xk_ref, xv_ref, pek_ref, pev_ref, w1k_ref, w1v_ref, w2k_ref, w2vt_ref, kc_ref, vct_ref):
    nch = xk_ref.shape[0] // CMP_STRIDE
    streams = [(xk_ref, pek_ref, w1k_ref), (xv_ref, pev_ref, w1v_ref)]
    acc = [[jnp.zeros((nch, NSA_GROUPS * CMP_HIDDEN), F32) for _ in range(2)] for _ in streams]
    for j in range(CMP_STRIDE):
        for n, (x_ref, pe_ref, w1_ref) in enumerate(streams):
            xj = x_ref[pl.ds(j, nch, stride=CMP_STRIDE), :]
            for half in range(2):
                row = half * CMP_STRIDE + j
                acc[n][half] = acc[n][half] + _dot((xj + pe_ref[row:row + 1, :]).astype(BF16), w1_ref[row])
    hid = [jax.nn.gelu(lo + pltpu.roll(hi, nch - 1, 0)).astype(BF16) for lo, hi in acc]
    aug = _key_position_columns(CMP_BLOCK - 1, nch, CMP_STRIDE).astype(BF16)
    for g in range(NSA_GROUPS):
        cols = slice(g * CMP_HIDDEN, (g + 1) * CMP_HIDDEN)
        kc_ref[g] = jnp.concatenate([_dot(hid[0][:, cols], w2k_ref[...]).astype(BF16), aug], axis=1)
        vct_ref[g] = _dot_nt(w2vt_ref[...], hid[1][:, cols]).astype(BF16)


def _compress(xk, xv, pe_k, w1_k, w2_k, pe_v, w1_v, w2_v):
    B, S, width = xk.shape
    nch = S // CMP_STRIDE
    G, dh, hid = NSA_GROUPS, HEAD_DIM, CMP_HIDDEN
    eye = jnp.eye(G, dtype=w1_k.dtype)

    def expand(pe, w1):
        pe_t = jnp.broadcast_to(pe.reshape(CMP_BLOCK, 1, dh), (CMP_BLOCK, G, dh)).reshape(CMP_BLOCK, width)
        wexp = jnp.einsum('pdn,ge->pgden', w1.reshape(CMP_BLOCK, dh, hid), eye).reshape(CMP_BLOCK, width, G * hid)
        return pe_t, wexp.astype(BF16)

    pek, w1k = expand(pe_k, w1_k)
    pev, w1v = expand(pe_v, w1_v)
    consts = [pek, pev, w1k, w1v, w2_k.astype(BF16), w2_v.T.astype(BF16)]
    full = lambda a: pl.BlockSpec(a.shape, lambda b: (0,) * a.ndim)
    seq = pl.BlockSpec((None, S, width), lambda b: (b, 0, 0))
    return pl.pallas_call(
        _compress_kernel,
        grid=(B,),
        in_specs=[seq, seq] + [full(a) for a in consts],
        out_specs=(pl.BlockSpec((None, G, nch, K_AUG), lambda b: (b, 0, 0, 0)),
                   pl.BlockSpec((None, G, dh, nch), lambda b: (b, 0, 0, 0))),
        out_shape=(jax.ShapeDtypeStruct((B, G, nch, K_AUG), BF16),
                   jax.ShapeDtypeStruct((B, G, dh, nch), BF16)),
        compiler_params=_params("parallel"),
        name="compress",
    )(xk, xv, *consts)


NSA_TQ = 256
SWEEP_TK = AUG_NBLK * SLC_BLOCK
SEL_SUB = 128
SEL_AHEAD = 6


def _rowmax8(s):
    return jnp.max(s.reshape(s.shape[0] // 8, 8, s.shape[1]), axis=0)


def _online_softmax(chunks, scores, values):
    pending = [scores(ch) for ch in chunks[:SEL_AHEAD]]
    m = acc = None
    for n, ch in enumerate(chunks):
        if n + SEL_AHEAD < len(chunks):
            pending.append(scores(chunks[n + SEL_AHEAD]))
        s = pending.pop(0)
        m_c = jnp.max(_rowmax8(s), axis=0, keepdims=True)
        if m is None:
            m = m_c
            acc = _dot(values(ch), jnp.exp2(s - m).astype(BF16))
        else:
            m_new = jnp.maximum(m, m_c)
            acc = jnp.exp2(m - m_new) * acc + _dot(values(ch), jnp.exp2(s - m_new).astype(BF16))
            m = m_new
    return acc


def _nsa_kernel(qt_ref, kc_ref, vct_ref, ks_ref, vst_ref, kw_ref, vwt_ref, gate_ref, ov_ref,
                o_ref, sbt_ref, osel_ref, idx_ref, *, slopes):
    g = pl.program_id(1)
    i = pl.program_id(2)
    R, dh, tq, tk = NSA_REP, HEAD_DIM, NSA_TQ, SWEEP_TK
    L = R * tq
    t0 = i * tq
    nc = kc_ref.shape[0]
    ns = ov_ref.shape[0]
    n_tiles = ns // AUG_NBLK
    tile4 = lambda a: jnp.concatenate([a] * R, axis=1)

    qt = qt_ref[...]
    qs = jnp.concatenate([qt[r * dh:(r + 1) * dh, :] for r in range(R)], axis=1)
    slope = [jnp.where(g == 0, slopes[r], slopes[R + r]).astype(F32) for r in range(R)]
    slope_row = jnp.concatenate([jnp.full((1, tq), 1.0, F32) * slope[r] for r in range(R)], axis=1)
    t_row = t0 + lax.broadcasted_iota(jnp.int32, (1, tq), 1)

    r8 = lax.broadcasted_iota(jnp.int32, (AUG_NBLK, L), 0)
    s_full = slope_row * LOG2E
    s_hi = s_full.astype(BF16).astype(F32)
    alibi8 = jnp.where(r8 < 2, s_hi, jnp.where(r8 < 4, s_full - s_hi, 0.0))
    q_pad = jnp.zeros((K_AUG - dh - 2 * AUG_NBLK, L), BF16)
    q_plain = jnp.concatenate([qs, jnp.concatenate([jnp.zeros((AUG_NBLK, L), F32), alibi8], axis=0).astype(BF16),
                               q_pad], axis=0)

    cmp_end = lax.broadcasted_iota(jnp.int32, (nc, tq), 0) * CMP_STRIDE + (CMP_BLOCK - 1)
    m_cmp = t_row >= cmp_end
    sc = _dot(kc_ref[...], q_plain) + tile4(jnp.where(m_cmp, 0.0, NEG_INF))

    a0 = jnp.maximum(t0 - NSA_WINDOW, 0)
    row_w = lax.broadcasted_iota(jnp.int32, (SEL_SUB, tq), 0)
    row_minus_lane = row_w - lax.broadcasted_iota(jnp.int32, (SEL_SUB, tq), 1)

    def causal_bias(k0):
        return jnp.where(row_minus_lane <= t0 - k0, 0.0, NEG_INF)

    def win_scores(ch):
        diag, r = ch
        k0 = pl.multiple_of((t0 if diag else a0) + r, SEL_SUB)
        if diag:
            bias = causal_bias(k0)
        else:
            bias = jnp.where((row_w < t0 - k0) & (row_minus_lane > t0 - k0 - NSA_WINDOW), 0.0, NEG_INF)
        return _dot(kw_ref[pl.ds(k0, SEL_SUB), :], q_plain) + tile4(bias)

    def win_values(ch):
        diag, r = ch
        return vwt_ref[:, pl.ds(pl.multiple_of((t0 if diag else a0) + r, SEL_SUB), SEL_SUB)]

    win_chunks = ([(True, r) for r in range(0, tq, SEL_SUB)]
                  + [(False, r) for r in range(0, NSA_WINDOW, SEL_SUB)])
    acc_w = _online_softmax(win_chunks, win_scores, win_values)
    o_win = acc_w[:dh] * (1.0 / acc_w[dh:dh + 1])

    mx = jnp.max(_rowmax8(sc), axis=0, keepdims=True)
    p = jnp.exp2(sc - mx) * tile4(jnp.where(m_cmp, 1.0, 0.0))
    den = jnp.sum(jnp.sum(p.reshape(nc // 8, 8, L), axis=0), axis=0, keepdims=True)
    pr = p * (1.0 / jnp.maximum(den, 1e-30))
    o_cmp = _dot(vct_ref[...], pr.astype(BF16))
    psum = pr[:, 0:tq]
    for r in range(1, R):
        psum = psum + pr[:, r * tq:(r + 1) * tq]

    p_hi = psum.astype(BF16)
    p_lo = (psum - p_hi.astype(F32)).astype(BF16)
    imp = _dot(ov_ref[...], p_hi) + _dot(ov_ref[...], p_lo)
    blk = lax.broadcasted_iota(jnp.int32, (ns, tq), 0)
    cur = lax.shift_right_logical(t_row, SLC_SHIFT)
    val = jnp.where((blk == cur) | (blk == 0), FORCE_SCORE, imp)
    val = jnp.where(blk <= cur, val, -1.0)
    vals = [val[8 * v:8 * v + 8, :] for v in range(ns // 8)]
    ranks = [jnp.zeros((8, tq), F32) for _ in vals]
    row8 = lax.broadcasted_iota(jnp.int32, (8, tq), 0)
    for j in range(ns):
        vj = jnp.broadcast_to(val[j:j + 1, :], (8, tq))
        for v in range(len(vals)):
            if 8 * v > j:
                ahead = vj >= vals[v]
            elif 8 * v + 7 <= j:
                ahead = vj > vals[v]
            else:
                ahead = (vj > vals[v]) | ((vj == vals[v]) & (row8 > j - 8 * v))
            ranks[v] = ranks[v] + jnp.where(ahead, 1.0, 0.0)
    for T in range(n_tiles):
        selb = jnp.where((ranks[T] < float(SLC_TOP)) & (vals[T] >= 0.0), 0.0, NEG_INF)
        sbt_ref[T] = jnp.concatenate([tile4(selb), alibi8], axis=0).astype(BF16)

    td = lax.div(t0, tk)
    cnt = jnp.int32(0)
    for T in range(n_tiles - 1):
        picked = jnp.where((ranks[T] < float(SLC_TOP)) & (vals[T] >= 0.0), 1.0, 0.0)
        idx_ref[cnt] = jnp.int32(T)
        cnt = cnt + jnp.logical_and(jnp.max(picked) > 0.0, T < td).astype(jnp.int32)
    for k in range(n_tiles):
        @pl.when(cnt == k)
        def _(k=k):
            past = [idx_ref[j] for j in range(k)]
            q_past = [jnp.concatenate([qs, sbt_ref[T], q_pad], axis=0) for T in past]
            q_diag = jnp.concatenate([qs, sbt_ref[td], q_pad], axis=0)
            chunks = ([(j, r) for j in range(k) for r in range(0, tk, SEL_SUB)]
                      + [(None, r) for r in range(0, tk, SEL_SUB)])

            def key_start(ch):
                j, r = ch
                return pl.multiple_of((td if j is None else past[j]) * tk + r, SEL_SUB)

            def chunk_scores(ch):
                k0 = key_start(ch)
                if ch[0] is None:
                    return _dot(ks_ref[pl.ds(k0, SEL_SUB), :], q_diag) + tile4(causal_bias(k0))
                return _dot(ks_ref[pl.ds(k0, SEL_SUB), :], q_past[ch[0]])

            acc = _online_softmax(chunks, chunk_scores, lambda ch: vst_ref[:, pl.ds(key_start(ch), SEL_SUB)])
            osel_ref[...] = acc[:dh] * (1.0 / acc[dh:dh + 1])

    o_sel = osel_ref[...]
    gates = gate_ref[...]
    tiles = []
    for r in range(R):
        sl = slice(r * tq, (r + 1) * tq)
        tiles.append(gates[3 * r:3 * r + 1, :] * o_cmp[:, sl]
                     + gates[3 * r + 1:3 * r + 2, :] * o_sel[:, sl]
                     + gates[3 * r + 2:3 * r + 3, :] * o_win[:, sl])
    o_ref[...] = jnp.concatenate(tiles, axis=0).T.astype(o_ref.dtype)


def _overlap_matrix(nc, ns):
    cs = np.arange(nc)[None, :] * CMP_STRIDE
    ss = np.arange(ns)[:, None] * SLC_BLOCK
    ov = np.clip(np.minimum(cs + CMP_BLOCK, ss + SLC_BLOCK) - np.maximum(cs, ss), 0, None)
    return jnp.asarray(ov.astype(np.float32) / CMP_BLOCK, dtype=BF16)


def _nsa(qt, kc, vct, ks, vst, kw, vwt, gates):
    B, _, S = qt.shape
    G, R, dh, tq = NSA_GROUPS, NSA_REP, HEAD_DIM, NSA_TQ
    nc = kc.shape[2]
    ns = S // SLC_BLOCK
    ov = _overlap_matrix(nc, ns)
    kern = functools.partial(_nsa_kernel, slopes=tuple(_alibi_slopes(NSA_HEADS)))
    assert S % SWEEP_TK == 0 and S >= NSA_WINDOW + tq
    per_bg = lambda shape: pl.BlockSpec((None, None) + shape, lambda b, g, i: (b, g, 0, 0))
    return pl.pallas_call(
        kern,
        grid=(B, G, S // tq),
        in_specs=[
            pl.BlockSpec((None, R * dh, tq), lambda b, g, i: (b, g, i)),
            per_bg((nc, K_AUG)), per_bg((dh, nc)),
            per_bg((S, K_AUG)), per_bg((V_ROWS, S)),
            per_bg((S, K_AUG)), per_bg((V_ROWS, S)),
            pl.BlockSpec((None, GATE_ROWS, tq), lambda b, g, i: (b, g, i)),
            pl.BlockSpec(ov.shape, lambda b, g, i: (0, 0)),
        ],
        out_specs=pl.BlockSpec((None, tq, R * dh), lambda b, g, i: (b, i, g)),
        out_shape=jax.ShapeDtypeStruct((B, S, G * R * dh), BF16),
        scratch_shapes=[pltpu.VMEM((S // SWEEP_TK, 2 * AUG_NBLK, R * tq), BF16),
                        pltpu.VMEM((dh, R * tq), F32),
                        pltpu.SMEM((S // SWEEP_TK,), jnp.int32)],
        compiler_params=_params("parallel", "parallel", "arbitrary"),
        name="nsa_attention",
    )(qt, kc, vct, ks, vst, kw, vwt, gates, ov)


BAND_TQ = 128
BAND_SUB = 16
BAND_PROBLEMS = 16
BAND = 128


def _banded_kernel(q_ref, k_ref, v_ref, o_ref, lse_ref, *, slopes, nsub, ncls):
    tq, dh, nh = BAND_TQ, HEAD_DIM, DIL_HEADS_PER_GROUP
    tk = tq + BAND
    width = nh * dh
    head_of_lane = lambda rows: lax.shift_right_logical(
        lax.broadcasted_iota(jnp.int32, (rows, width), 1), int(math.log2(dh)))
    lane_head, q_head = head_of_lane(tk), head_of_lane(tq)
    keep = [jnp.where(lane_head == h, 1.0, 0.0).astype(BF16) for h in range(nh)]

    def per_head(a):
        return jnp.concatenate([a * keep[h] for h in range(nh)], axis=0)

    def biases(first_key_offset):
        d = first_key_offset + (lax.broadcasted_iota(jnp.int32, (tq, tk), 0)
                                - lax.broadcasted_iota(jnp.int32, (tq, tk), 1))
        mask_bias = jnp.where((d >= 0) & (d <= BAND), 0.0, NEG_INF)
        neg_d = -d.astype(F32)
        return [slopes[h] * neg_d + mask_bias for h in range(nh)]

    subs = []
    for cls in range(ncls):
        for sub in range(nsub):
            i = pl.program_id(2) * nsub + sub
            k0 = pl.multiple_of(jnp.maximum(i - 1, 0) * tq, tq)
            q = q_ref[cls, sub * tq:(sub + 1) * tq, :]
            scores = _dot_nt(q, per_head(k_ref[cls, pl.ds(k0, tk), :]))
            subs.append((cls, sub, i * tq - k0, k0, scores))
    first_bias = biases(subs[0][2])
    inner_bias = biases(tq) if nsub > 1 else None
    probs = []
    for cls, sub, off, k0, s in subs:
        bias = first_bias if sub == 0 else inner_bias
        ps, mxs, dens = [], [], []
        for h in range(nh):
            sh = s[:, h * tk:(h + 1) * tk] + bias[h]
            mx = jnp.max(sh, axis=-1, keepdims=True)
            e = jnp.exp2(sh - mx)
            dens.append(jnp.sum(e, axis=-1, keepdims=True))
            ps.append(e.astype(BF16))
            mxs.append(mx)
        probs.append((cls, sub, k0, jnp.concatenate(ps, axis=1), mxs, dens))
    for cls, sub, k0, p, mxs, dens in probs:
        mx_all, den = mxs[nh - 1], dens[nh - 1]
        for h in range(nh - 2, -1, -1):
            mx_all = jnp.where(q_head == h, mxs[h], mx_all)
            den = jnp.where(q_head == h, dens[h], den)
        rows = slice(sub * tq, (sub + 1) * tq)
        o_ref[cls, rows, :] = _dot(p, per_head(v_ref[cls, pl.ds(k0, tk), :])) * (1.0 / den)
        lse_ref[cls, rows, :] = mx_all * LN2 + jnp.log(den)


def _banded(d, slopes):
    B, _, r, n, width = d.shape
    nsub = min(BAND_SUB, n // BAND_TQ)
    ncls = min(r, BAND_PROBLEMS // nsub)
    tq = BAND_TQ * nsub
    assert n >= BAND_TQ + BAND and n % tq == 0 and r % ncls == 0
    kern = functools.partial(_banded_kernel, slopes=tuple(slopes), nsub=nsub, ncls=ncls)
    tile = pl.BlockSpec((None, None, ncls, tq, width), lambda b, c, i: (b, 0, c, i, 0))
    kseq = pl.BlockSpec((None, None, ncls, n, width), lambda b, c, i: (b, 1, c, 0, 0))
    vseq = pl.BlockSpec((None, None, ncls, n, width), lambda b, c, i: (b, 2, c, 0, 0))
    out = pl.BlockSpec((None, ncls, tq, width), lambda b, c, i: (b, c, i, 0))
    return pl.pallas_call(
        kern,
        grid=(B, r // ncls, n // tq),
        in_specs=[tile, kseq, vseq],
        out_specs=(out, out),
        out_shape=(jax.ShapeDtypeStruct((B, r, n, width), F32),) * 2,
        compiler_params=_params("parallel", "parallel", "arbitrary"),
        name="banded_attention",
    )(d, d, d)


def _dilated(dils):
    slopes = _alibi_slopes(DIL_HEADS)
    outs, lses = [], []
    for gi, (w, r) in enumerate(DIL_CONFIGS):
        assert w // r == BAND
        sl = [s_ * r * LOG2E for s_ in slopes[gi * DIL_HEADS_PER_GROUP:(gi + 1) * DIL_HEADS_PER_GROUP]]
        o, lse = _banded(dils[gi], sl)
        outs.append(o)
        lses.append(lse)
    return outs, lses


MERGE_TM = 1024


def _token_order(ref, slab_ref):
    r, rows, width = ref.shape
    if r == 1:
        return ref[0]
    for c in range(r):
        blk = ref[c]
        for s in range(width // LANES):
            slab_ref[s, pl.ds(c, rows, stride=r), :] = blk[:, s * LANES:(s + 1) * LANES]
    return jnp.concatenate([slab_ref[s] for s in range(width // LANES)], axis=1)


def _merge_kernel(x_ref, oa_ref, o0_ref, o1_ref, o2_ref, l0_ref, l1_ref, l2_ref,
                  gmix_ref, wm_ref, wpn_ref, wpd_ref, wo_ref, gffn_ref, x1_ref, h2_ref, slab_ref):
    x = x_ref[...]
    D = x.shape[1]
    h = _rms(x, gmix_ref[...]).astype(BF16)
    gm = _sigmoid(_dot_nt(h, wm_ref[...]))
    o0, o1, o2 = [_token_order(r_, slab_ref) for r_ in (o0_ref, o1_ref, o2_ref)]
    l0, l1, l2 = [_token_order(r_, slab_ref) for r_ in (l0_ref, l1_ref, l2_ref)]
    mx = jnp.maximum(jnp.maximum(l0, l1), l2)
    e0, e1, e2 = jnp.exp(l0 - mx), jnp.exp(l1 - mx), jnp.exp(l2 - mx)
    inv = 1.0 / (e0 + e1 + e2)
    ob = o0 * (e0 * inv) + o1 * (e1 * inv) + o2 * (e2 * inv)
    a = _dot(oa_ref[...], wpn_ref[...])
    d = _dot(ob.astype(BF16), wpd_ref[...])
    mixed = gm[:, :D] * a + gm[:, D:] * d
    x1 = x + _dot(mixed.astype(BF16), wo_ref[...])
    x1_ref[...] = x1
    h2_ref[...] = _rms(x1, gffn_ref[...]).astype(BF16)


def _merge(x, o_a, outs, lses, g_mix, w_merge, w_proj_nsa, w_proj_dil, w_out, g_ffn):
    B, S, D = x.shape
    tm = MERGE_TM
    row = lambda a: pl.BlockSpec((None, tm, a.shape[2]), lambda b, i: (b, i, 0))
    cls = lambda a: pl.BlockSpec((None, a.shape[1], tm // a.shape[1], a.shape[3]), lambda b, i: (b, 0, i, 0))
    full = lambda a: pl.BlockSpec(a.shape, lambda b, i: (0,) * a.ndim)
    ws = [w_merge.astype(BF16), w_proj_nsa.astype(BF16), w_proj_dil.astype(BF16), w_out.astype(BF16)]
    consts = [g_mix, *ws, g_ffn]
    in_specs = [row(x), row(o_a)] + [cls(a) for a in (*outs, *lses)] + [full(a) for a in consts]
    return pl.pallas_call(
        _merge_kernel,
        grid=(B, S // tm),
        in_specs=in_specs,
        out_specs=(pl.BlockSpec((None, tm, D), lambda b, i: (b, i, 0)),) * 2,
        out_shape=(jax.ShapeDtypeStruct((B, S, D), F32), jax.ShapeDtypeStruct((B, S, D), BF16)),
        scratch_shapes=[pltpu.VMEM((DIL_WIDTH // LANES, tm, LANES), F32)],
        compiler_params=_params("parallel", "parallel"),
        name="merge_proj",
    )(x, o_a, *outs, *lses, *consts)


FFN_TM = 1024
FFN_TN = 256
HALO = 16


def _ffn_kernel(h_ref, halo_ref, x1_ref, wup_ref, cw_ref, cb_ref, wd_ref, gfin_ref, o_ref, act_ref):
    i = pl.program_id(1)
    h = h_ref[...]
    halo = halo_ref[...]
    tm = h.shape[0]
    row = lax.broadcasted_iota(jnp.int32, (tm, FFN_TN), 0)
    live = (i > 0).astype(F32)
    for j in range(D_FF // FFN_TN):
        cols = slice(j * FFN_TN, (j + 1) * FFN_TN)
        wu = wup_ref[:, cols]
        u = _dot(h, wu)
        uh = _dot(halo, wu) * live
        gate = _dot(h, wup_ref[:, D_FF + j * FFN_TN:D_FF + (j + 1) * FFN_TN])
        p1 = jnp.broadcast_to(uh[HALO - 1:HALO, :], (tm, FFN_TN))
        p2 = jnp.broadcast_to(uh[HALO - 2:HALO - 1, :], (tm, FFN_TN))
        u1 = jnp.where(row == 0, p1, pltpu.roll(u, 1, 0))
        u2 = jnp.where(row == 0, p2, jnp.where(row == 1, p1, pltpu.roll(u, 2, 0)))
        uc = cb_ref[:, cols] + cw_ref[0:1, cols] * u2
        uc = uc + cw_ref[1:2, cols] * u1
        uc = uc + cw_ref[2:3, cols] * u
        act_ref[:, j * FFN_TN:(j + 1) * FFN_TN] = (jax.nn.gelu(uc) * gate).astype(BF16)
    y = _dot(act_ref[...], wd_ref[...])
    o_ref[...] = _rms(x1_ref[...] + y, gfin_ref[...])


def _ffn(h2, x1, w_up, conv_w, conv_b, w_down, g_final):
    B, S, D = h2.shape
    tm = FFN_TM
    assert D_FF % FFN_TN == 0
    wup = w_up.astype(BF16)
    cw = conv_w
    cb = conv_b.reshape(1, D_FF)
    wd = w_down.astype(BF16)
    gfin = g_final.reshape(1, D)
    full = lambda a: pl.BlockSpec(a.shape, lambda b, i: (0,) * a.ndim, pipeline_mode=pl.Buffered(1))
    tile = pl.BlockSpec((None, tm, D), lambda b, i: (b, i, 0))
    halo = pl.BlockSpec((None, HALO, D), lambda b, i: (b, jnp.maximum(i * (tm // HALO) - 1, 0), 0))
    return pl.pallas_call(
        _ffn_kernel,
        grid=(B, S // tm),
        in_specs=[tile, halo, tile, full(wup), full(cw), full(cb), full(wd), full(gfin)],
        out_specs=tile,
        out_shape=jax.ShapeDtypeStruct((B, S, D), F32),
        scratch_shapes=[pltpu.VMEM((tm, D_FF), BF16)],
        compiler_params=_params("parallel", "parallel"),
        name="conv_ffn",
    )(h2, h2, x1, wup, cw, cb, wd, gfin)


@jax.jit
def _layer(x, g_mix, w_in, pe_cmp_k, w_cmp_k1, w_cmp_k2, pe_cmp_v, w_cmp_v1, w_cmp_v2,
           w_proj_nsa, w_proj_dil, w_out, g_ffn, w_up, conv_w, conv_b, w_down, g_final):
    B, S, D = x.shape
    depth = g_mix.shape[0]
    for l in range(depth):
        gm = g_mix[l].reshape(1, D)
        kcmp, vcmp, ks, kw, d0, d1, d2, qt, vst, vwt, gates = _in_proj(x, gm, w_in[l])
        kc, vct = _compress(kcmp, vcmp, pe_cmp_k[l], w_cmp_k1[l], w_cmp_k2[l],
                            pe_cmp_v[l], w_cmp_v1[l], w_cmp_v2[l])
        o_a = _nsa(qt, kc, vct, ks, vst, kw, vwt, gates)
        outs, lses = _dilated((d0, d1, d2))
        merge_cols = w_in[l].T[w_in.shape[2] - 2 * D:]
        x1, h2 = _merge(x, o_a, outs, lses, gm, merge_cols, w_proj_nsa[l], w_proj_dil[l], w_out[l],
                        g_ffn[l].reshape(1, D))
        x = _ffn(h2, x1, w_up[l], conv_w[l], conv_b[l], w_down[l], g_final)
        assert depth == 1
    return x


def kernel(x, g_mix, w_in, pe_cmp_k, w_cmp_k1, w_cmp_k2, pe_cmp_v, w_cmp_v1, w_cmp_v2, w_proj_nsa, w_proj_dil, w_out, g_ffn, w_up, conv_w, conv_b, w_down, g_final):
    return _layer(x, g_mix, w_in, pe_cmp_k, w_cmp_k1, w_cmp_k2, pe_cmp_v, w_cmp_v1, w_cmp_v2,
                  w_proj_nsa, w_proj_dil, w_out, g_ffn, w_up, conv_w, conv_b, w_down, g_final)
```

```python
import functools
import math

import numpy as np
import jax
import jax.numpy as jnp
from jax import lax
from jax.experimental import pallas as pl
from jax.experimental.pallas import tpu as pltpu

HEAD_DIM = 64
NSA_HEADS = 8
NSA_GROUPS = 2
NSA_REP = NSA_HEADS // NSA_GROUPS
CMP_BLOCK = 32
CMP_STRIDE = 16
CMP_HIDDEN = 128
SLC_BLOCK = 64
SLC_TOP = 16
NSA_WINDOW = 512
FORCE_SCORE = 1.0e4
DIL_CONFIGS = ((128, 1), (512, 4), (2048, 16))
DIL_GROUPS = 3
DIL_HEADS_PER_GROUP = 4
DIL_HEADS = DIL_GROUPS * DIL_HEADS_PER_GROUP
D_FF = 2816
CONV_WIDTH = 3
RMS_EPS = 1e-6
NEG_INF = -1e30

LANES = 128
VMEM_LIMIT_BYTES = 56 * 1024 * 1024

F32 = jnp.float32
BF16 = jnp.bfloat16
NT_DIMS = (((1,), (1,)), ((), ()))


def _alibi_slopes(n):
    return [float(2.0 ** (-8.0 * i / n)) for i in range(1, n + 1)]


def _rms(xf, g):
    ms = jnp.mean(xf * xf, axis=-1, keepdims=True)
    return xf * lax.rsqrt(ms + RMS_EPS) * g


def _dot(a, b):
    return jnp.dot(a, b, preferred_element_type=F32)


def _dot_nt(a, b):
    return lax.dot_general(a, b, NT_DIMS, preferred_element_type=F32)


def _sigmoid(z):
    return 1.0 / (1.0 + jnp.exp(-z))


def _params(*sem):
    return pltpu.CompilerParams(dimension_semantics=sem, vmem_limit_bytes=VMEM_LIMIT_BYTES)


IN_TM = 1024
N_KVC = 4 * HEAD_DIM
N_KSEL = NSA_GROUPS * HEAD_DIM
N_DIL = 3 * DIL_HEADS * HEAD_DIM
DIL_WIDTH = DIL_HEADS_PER_GROUP * HEAD_DIM
T_Q = NSA_HEADS * HEAD_DIM
T_V = NSA_GROUPS * HEAD_DIM
GATE_ROWS = 16
K_AUG = 2 * HEAD_DIM
SLC_SHIFT = int(math.log2(SLC_BLOCK))
AUG_NBLK = 8
AUG_HI, AUG_LO = AUG_NBLK, AUG_NBLK + 1
LOG2E = math.log2(math.e)
LN2 = math.log(2.0)
V_PAD = 16
V_ROWS = HEAD_DIM + V_PAD


def _key_position_columns(pos0, rows, step=1):
    pos = pos0 + step * lax.broadcasted_iota(jnp.int32, (rows, HEAD_DIM), 0)
    col = lax.broadcasted_iota(jnp.int32, (rows, HEAD_DIM), 1)
    blk = jnp.bitwise_and(lax.shift_right_logical(pos, SLC_SHIFT), AUG_NBLK - 1)
    hi = lax.shift_left(lax.shift_right_logical(pos, 7), 7).astype(F32)
    lo = jnp.bitwise_and(pos, 127).astype(F32)
    c = jnp.where((col == AUG_HI) | (col == AUG_HI + 2), hi,
                  jnp.where((col == AUG_LO) | (col == AUG_LO + 2), lo, 0.0))
    return jnp.where((col < AUG_NBLK) & (blk == col), 1.0, c)


def _in_proj_kernel(x_ref, g_ref, wn_ref, wt_ref,
                    kcmp_ref, vcmp_ref, ks_ref, kw_ref, d0_ref, d1_ref, d2_ref, qt_ref, vst_ref, vwt_ref, gate_ref,
                    slab_ref):
    tm = x_ref.shape[0]
    h = _rms(x_ref[...], g_ref[...]).astype(BF16)
    c0 = 0
    kvc = _dot_nt(h, wn_ref[c0:c0 + N_KVC, :])
    kcmp_ref[...] = kvc[:, :N_KSEL]
    vcmp_ref[...] = kvc[:, N_KSEL:]
    c0 += N_KVC
    ks = _dot_nt(h, wn_ref[c0:c0 + N_KSEL, :]).astype(BF16)
    c0 += N_KSEL
    kw = _dot_nt(h, wn_ref[c0:c0 + N_KSEL, :]).astype(BF16)
    c0 += N_KSEL
    aug = _key_position_columns(pl.program_id(1) * tm, tm).astype(BF16)
    for g in range(NSA_GROUPS):
        ks_ref[g] = jnp.concatenate([ks[:, g * HEAD_DIM:(g + 1) * HEAD_DIM], aug], axis=1)
        kw_ref[g] = jnp.concatenate([kw[:, g * HEAD_DIM:(g + 1) * HEAD_DIM], aug], axis=1)
    seg = DIL_HEADS * HEAD_DIM
    for which in range(3):
        y = _dot_nt(h, wn_ref[c0 + which * seg:c0 + (which + 1) * seg, :])
        for gi, (d_ref, (_, r)) in enumerate(zip((d0_ref, d1_ref, d2_ref), DIL_CONFIGS)):
            yg = y[:, gi * DIL_WIDTH:(gi + 1) * DIL_WIDTH]
            if r == 1:
                d_ref[which, 0] = yg.astype(BF16)
                continue
            for s in range(DIL_WIDTH // LANES):
                slab_ref[s] = yg[:, s * LANES:(s + 1) * LANES]
            for c in range(r):
                d_ref[which, c] = jnp.concatenate(
                    [slab_ref[s, pl.ds(c, tm // r, stride=r), :] for s in range(DIL_WIDTH // LANES)],
                    axis=1).astype(BF16)
    yt = _dot_nt(wt_ref[...], h)
    qt_ref[...] = yt[0:T_Q].astype(BF16)
    r0 = T_Q
    ones = jnp.where(lax.broadcasted_iota(jnp.int32, (V_PAD, tm), 0) == 0, 1.0, 0.0).astype(BF16)
    for ref in (vst_ref, vwt_ref):
        vt = yt[r0:r0 + T_V].astype(BF16)
        r0 += T_V
        for g in range(NSA_GROUPS):
            ref[g] = jnp.concatenate([vt[g * HEAD_DIM:(g + 1) * HEAD_DIM, :], ones], axis=0)
    gate_ref[...] = _sigmoid(yt[r0:r0 + NSA_GROUPS * GATE_ROWS])


def _in_proj(x, g_mix, w_in):
    B, S, D = x.shape
    scale = HEAD_DIM ** -0.5 * LOG2E
    o_q, o_kv = 0, T_Q
    o_gate = o_kv + 6 * N_KSEL
    o_dil = o_gate + 3 * NSA_HEADS
    o_merge = o_dil + N_DIL
    w_t = w_in.T
    kv = w_t[o_kv:o_gate]

    def kind(k):
        return kv[k * N_KSEL:(k + 1) * N_KSEL]

    dil = w_t[o_dil:o_merge]
    dil = jnp.concatenate([dil[:DIL_HEADS * HEAD_DIM] * scale, dil[DIL_HEADS * HEAD_DIM:]], axis=0)
    wn = jnp.concatenate([kind(0), kind(1), kind(2), kind(4), dil], axis=0).astype(BF16)
    wg = w_t[o_gate:o_dil].reshape(NSA_GROUPS, 3 * NSA_REP, D)
    wg = jnp.pad(wg, ((0, 0), (0, GATE_ROWS - 3 * NSA_REP), (0, 0))).reshape(NSA_GROUPS * GATE_ROWS, D)
    wt = jnp.concatenate([w_t[o_q:o_kv] * scale, kind(3), kind(5), wg], axis=0).astype(BF16)
    tm = IN_TM
    grid = (B, S // tm)
    full = lambda a: pl.BlockSpec(a.shape, lambda b, i: (0,) * a.ndim)
    k_shape = jax.ShapeDtypeStruct((B, NSA_GROUPS, S, K_AUG), BF16)
    v_shape = jax.ShapeDtypeStruct((B, NSA_GROUPS, V_ROWS, S), BF16)
    k_spec = pl.BlockSpec((None, NSA_GROUPS, tm, K_AUG), lambda b, i: (b, 0, i, 0))
    v_spec = pl.BlockSpec((None, NSA_GROUPS, V_ROWS, tm), lambda b, i: (b, 0, 0, i))
    c_shape = jax.ShapeDtypeStruct((B, S, N_KSEL), F32)
    c_spec = pl.BlockSpec((None, tm, N_KSEL), lambda b, i: (b, i, 0))
    d_shapes = tuple(jax.ShapeDtypeStruct((B, 3, r, S // r, DIL_WIDTH), BF16) for _, r in DIL_CONFIGS)
    d_specs = tuple(pl.BlockSpec((None, 3, r, tm // r, DIL_WIDTH), lambda b, i: (b, 0, 0, i, 0))
                    for _, r in DIL_CONFIGS)
    out_shape = (
        c_shape, c_shape, k_shape, k_shape, *d_shapes,
        jax.ShapeDtypeStruct((B, T_Q, S), BF16),
        v_shape,
        v_shape,
        jax.ShapeDtypeStruct((B, NSA_GROUPS * GATE_ROWS, S), F32),
    )
    out_specs = (
        c_spec, c_spec, k_spec, k_spec, *d_specs,
        pl.BlockSpec((None, T_Q, tm), lambda b, i: (b, 0, i)),
        v_spec,
        v_spec,
        pl.BlockSpec((None, NSA_GROUPS * GATE_ROWS, tm), lambda b, i: (b, 0, i)),
    )
    return pl.pallas_call(
        _in_proj_kernel,
        grid=grid,
        in_specs=[pl.BlockSpec((None, tm, D), lambda b, i: (b, i, 0)), full(g_mix), full(wn), full(wt)],
        out_specs=out_specs,
        out_shape=out_shape,
        scratch_shapes=[pltpu.VMEM((DIL_WIDTH // LANES, tm, LANES), F32)],
        compiler_params=_params("parallel", "parallel"),
        name="in_proj",
    )(x, g_mix, wn, wt)


def _compress_kernel(xk_ref, xv_ref, pek_ref, pev_ref, w1k_ref, w1v_ref, w2k_ref, w2vt_ref, kc_ref, vct_ref):
    nch = xk_ref.shape[0] // CMP_STRIDE
    streams = [(xk_ref, pek_ref, w1k_ref), (xv_ref, pev_ref, w1v_ref)]
    acc = [[jnp.zeros((nch, NSA_GROUPS * CMP_HIDDEN), F32) for _ in range(2)] for _ in streams]
    for j in range(CMP_STRIDE):
        for n, (x_ref, pe_ref, w1_ref) in enumerate(streams):
            xj = x_ref[pl.ds(j, nch, stride=CMP_STRIDE), :]
            for half in range(2):
                row = half * CMP_STRIDE + j
                acc[n][half] = acc[n][half] + _dot((xj + pe_ref[row:row + 1, :]).astype(BF16), w1_ref[row])
    hid = [jax.nn.gelu(lo + pltpu.roll(hi, nch - 1, 0)).astype(BF16) for lo, hi in acc]
    aug = _key_position_columns(CMP_BLOCK - 1, nch, CMP_STRIDE).astype(BF16)
    for g in range(NSA_GROUPS):
        cols = slice(g * CMP_HIDDEN, (g + 1) * CMP_HIDDEN)
        kc_ref[g] = jnp.concatenate([_dot(hid[0][:, cols], w2k_ref[...]).astype(BF16), aug], axis=1)
        vct_ref[g] = _dot_nt(w2vt_ref[...], hid[1][:, cols]).astype(BF16)


def _compress(xk, xv, pe_k, w1_k, w2_k, pe_v, w1_v, w2_v):
    B, S, width = xk.shape
    nch = S // CMP_STRIDE
    G, dh, hid = NSA_GROUPS, HEAD_DIM, CMP_HIDDEN
    eye = jnp.eye(G, dtype=w1_k.dtype)

    def expand(pe, w1):
        pe_t = jnp.broadcast_to(pe.reshape(CMP_BLOCK, 1, dh), (CMP_BLOCK, G, dh)).reshape(CMP_BLOCK, width)
        wexp = jnp.einsum('pdn,ge->pgden', w1.reshape(CMP_BLOCK, dh, hid), eye).reshape(CMP_BLOCK, width, G * hid)
        return pe_t, wexp.astype(BF16)

    pek, w1k = expand(pe_k, w1_k)
    pev, w1v = expand(pe_v, w1_v)
    consts = [pek, pev, w1k, w1v, w2_k.astype(BF16), w2_v.T.astype(BF16)]
    full = lambda a: pl.BlockSpec(a.shape, lambda b: (0,) * a.ndim)
    seq = pl.BlockSpec((None, S, width), lambda b: (b, 0, 0))
    return pl.pallas_call(
        _compress_kernel,
        grid=(B,),
        in_specs=[seq, seq] + [full(a) for a in consts],
        out_specs=(pl.BlockSpec((None, G, nch, K_AUG), lambda b: (b, 0, 0, 0)),
                   pl.BlockSpec((None, G, dh, nch), lambda b: (b, 0, 0, 0))),
        out_shape=(jax.ShapeDtypeStruct((B, G, nch, K_AUG), BF16),
                   jax.ShapeDtypeStruct((B, G, dh, nch), BF16)),
        compiler_params=_params("parallel"),
        name="compress",
    )(xk, xv, *consts)


NSA_TQ = 256
SWEEP_TK = AUG_NBLK * SLC_BLOCK
SEL_SUB = 128
SEL_AHEAD = 6


def _rowmax8(s):
    return jnp.max(s.reshape(s.shape[0] // 8, 8, s.shape[1]), axis=0)


def _online_softmax(chunks, scores, values):
    pending = [scores(ch) for ch in chunks[:SEL_AHEAD]]
    m = acc = None
    for n, ch in enumerate(chunks):
        if n + SEL_AHEAD < len(chunks):
            pending.append(scores(chunks[n + SEL_AHEAD]))
        s = pending.pop(0)
        m_c = jnp.max(_rowmax8(s), axis=0, keepdims=True)
        if m is None:
            m = m_c
            acc = _dot(values(ch), jnp.exp2(s - m).astype(BF16))
        else:
            m_new = jnp.maximum(m, m_c)
            acc = jnp.exp2(m - m_new) * acc + _dot(values(ch), jnp.exp2(s - m_new).astype(BF16))
            m = m_new
    return acc


def _nsa_kernel(qt_ref, kc_ref, vct_ref, ks_ref, vst_ref, kw_ref, vwt_ref, gate_ref, ov_ref,
                o_ref, sbt_ref, osel_ref, idx_ref, *, slopes):
    g = pl.program_id(1)
    i = pl.program_id(2)
    R, dh, tq, tk = NSA_REP, HEAD_DIM, NSA_TQ, SWEEP_TK
    L = R * tq
    t0 = i * tq
    nc = kc_ref.shape[0]
    ns = ov_ref.shape[0]
    n_tiles = ns // AUG_NBLK
    tile4 = lambda a: jnp.concatenate([a] * R, axis=1)

    qt = qt_ref[...]
    qs = jnp.concatenate([qt[r * dh:(r + 1) * dh, :] for r in range(R)], axis=1)
    slope = [jnp.where(g == 0, slopes[r], slopes[R + r]).astype(F32) for r in range(R)]
    slope_row = jnp.concatenate([jnp.full((1, tq), 1.0, F32) * slope[r] for r in range(R)], axis=1)
    t_row = t0 + lax.broadcasted_iota(jnp.int32, (1, tq), 1)

    r8 = lax.broadcasted_iota(jnp.int32, (AUG_NBLK, L), 0)
    s_full = slope_row * LOG2E
    s_hi = s_full.astype(BF16).astype(F32)
    alibi8 = jnp.where(r8 < 2, s_hi, jnp.where(r8 < 4, s_full - s_hi, 0.0))
    q_pad = jnp.zeros((K_AUG - dh - 2 * AUG_NBLK, L), BF16)
    q_plain = jnp.concatenate([qs, jnp.concatenate([jnp.zeros((AUG_NBLK, L), F32), alibi8], axis=0).astype(BF16),
                               q_pad], axis=0)

    cmp_end = lax.broadcasted_iota(jnp.int32, (nc, tq), 0) * CMP_STRIDE + (CMP_BLOCK - 1)
    m_cmp = t_row >= cmp_end
    sc = _dot(kc_ref[...], q_plain) + tile4(jnp.where(m_cmp, 0.0, NEG_INF))

    a0 = jnp.maximum(t0 - NSA_WINDOW, 0)
    row_w = lax.broadcasted_iota(jnp.int32, (SEL_SUB, tq), 0)
    row_minus_lane = row_w - lax.broadcasted_iota(jnp.int32, (SEL_SUB, tq), 1)

    def causal_bias(k0):
        return jnp.where(row_minus_lane <= t0 - k0, 0.0, NEG_INF)

    def win_scores(ch):
        diag, r = ch
        k0 = pl.multiple_of((t0 if diag else a0) + r, SEL_SUB)
        if diag:
            bias = causal_bias(k0)
        else:
            bias = jnp.where((row_w < t0 - k0) & (row_minus_lane > t0 - k0 - NSA_WINDOW), 0.0, NEG_INF)
        return _dot(kw_ref[pl.ds(k0, SEL_SUB), :], q_plain) + tile4(bias)

    def win_values(ch):
        diag, r = ch
        return vwt_ref[:, pl.ds(pl.multiple_of((t0 if diag else a0) + r, SEL_SUB), SEL_SUB)]

    win_chunks = ([(True, r) for r in range(0, tq, SEL_SUB)]
                  + [(False, r) for r in range(0, NSA_WINDOW, SEL_SUB)])
    acc_w = _online_softmax(win_chunks, win_scores, win_values)
    o_win = acc_w[:dh] * (1.0 / acc_w[dh:dh + 1])

    mx = jnp.max(_rowmax8(sc), axis=0, keepdims=True)
    p = jnp.exp2(sc - mx) * tile4(jnp.where(m_cmp, 1.0, 0.0))
    den = jnp.sum(jnp.sum(p.reshape(nc // 8, 8, L), axis=0), axis=0, keepdims=True)
    pr = p * (1.0 / jnp.maximum(den, 1e-30))
    o_cmp = _dot(vct_ref[...], pr.astype(BF16))
    psum = pr[:, 0:tq]
    for r in range(1, R):
        psum = psum + pr[:, r * tq:(r + 1) * tq]

    p_hi = psum.astype(BF16)
    p_lo = (psum - p_hi.astype(F32)).astype(BF16)
    imp = _dot(ov_ref[...], p_hi) + _dot(ov_ref[...], p_lo)
    blk = lax.broadcasted_iota(jnp.int32, (ns, tq), 0)
    cur = lax.shift_right_logical(t_row, SLC_SHIFT)
    val = jnp.where((blk == cur) | (blk == 0), FORCE_SCORE, imp)
    val = jnp.where(blk <= cur, val, -1.0)
    vals = [val[8 * v:8 * v + 8, :] for v in range(ns // 8)]
    ranks = [jnp.zeros((8, tq), F32) for _ in vals]
    row8 = lax.broadcasted_iota(jnp.int32, (8, tq), 0)
    for j in range(ns):
        vj = jnp.broadcast_to(val[j:j + 1, :], (8, tq))
        for v in range(len(vals)):
            if 8 * v > j:
                ahead = vj >= vals[v]
            elif 8 * v + 7 <= j:
                ahead = vj > vals[v]
            else:
                ahead = (vj > vals[v]) | ((vj == vals[v]) & (row8 > j - 8 * v))
            ranks[v] = ranks[v] + jnp.where(ahead, 1.0, 0.0)
    for T in range(n_tiles):
        selb = jnp.where((ranks[T] < float(SLC_TOP)) & (vals[T] >= 0.0), 0.0, NEG_INF)
        sbt_ref[T] = jnp.concatenate([tile4(selb), alibi8], axis=0).astype(BF16)

    td = lax.div(t0, tk)
    cnt = jnp.int32(0)
    for T in range(n_tiles - 1):
        picked = jnp.where((ranks[T] < float(SLC_TOP)) & (vals[T] >= 0.0), 1.0, 0.0)
        idx_ref[cnt] = jnp.int32(T)
        cnt = cnt + jnp.logical_and(jnp.max(picked) > 0.0, T < td).astype(jnp.int32)
    for k in range(n_tiles):
        @pl.when(cnt == k)
        def _(k=k):
            past = [idx_ref[j] for j in range(k)]
            q_past = [jnp.concatenate([qs, sbt_ref[T], q_pad], axis=0) for T in past]
            q_diag = jnp.concatenate([qs, sbt_ref[td], q_pad], axis=0)
            chunks = ([(j, r) for j in range(k) for r in range(0, tk, SEL_SUB)]
                      + [(None, r) for r in range(0, tk, SEL_SUB)])

            def key_start(ch):
                j, r = ch
                return pl.multiple_of((td if j is None else past[j]) * tk + r, SEL_SUB)

            def chunk_scores(ch):
                k0 = key_start(ch)
                if ch[0] is None:
                    return _dot(ks_ref[pl.ds(k0, SEL_SUB), :], q_diag) + tile4(causal_bias(k0))
                return _dot(ks_ref[pl.ds(k0, SEL_SUB), :], q_past[ch[0]])

            acc = _online_softmax(chunks, chunk_scores, lambda ch: vst_ref[:, pl.ds(key_start(ch), SEL_SUB)])
            osel_ref[...] = acc[:dh] * (1.0 / acc[dh:dh + 1])

    o_sel = osel_ref[...]
    gates = gate_ref[...]
    tiles = []
    for r in range(R):
        sl = slice(r * tq, (r + 1) * tq)
        tiles.append(gates[3 * r:3 * r + 1, :] * o_cmp[:, sl]
                     + gates[3 * r + 1:3 * r + 2, :] * o_sel[:, sl]
                     + gates[3 * r + 2:3 * r + 3, :] * o_win[:, sl])
    o_ref[...] = jnp.concatenate(tiles, axis=0).T.astype(o_ref.dtype)


def _overlap_matrix(nc, ns):
    cs = np.arange(nc)[None, :] * CMP_STRIDE
    ss = np.arange(ns)[:, None] * SLC_BLOCK
    ov = np.clip(np.minimum(cs + CMP_BLOCK, ss + SLC_BLOCK) - np.maximum(cs, ss), 0, None)
    return jnp.asarray(ov.astype(np.float32) / CMP_BLOCK, dtype=BF16)


def _nsa(qt, kc, vct, ks, vst, kw, vwt, gates):
    B, _, S = qt.shape
    G, R, dh, tq = NSA_GROUPS, NSA_REP, HEAD_DIM, NSA_TQ
    nc = kc.shape[2]
    ns = S // SLC_BLOCK
    ov = _overlap_matrix(nc, ns)
    kern = functools.partial(_nsa_kernel, slopes=tuple(_alibi_slopes(NSA_HEADS)))
    assert S % SWEEP_TK == 0 and S >= NSA_WINDOW + tq
    per_bg = lambda shape: pl.BlockSpec((None, None) + shape, lambda b, g, i: (b, g, 0, 0))
    return pl.pallas_call(
        kern,
        grid=(B, G, S // tq),
        in_specs=[
            pl.BlockSpec((None, R * dh, tq), lambda b, g, i: (b, g, i)),
            per_bg((nc, K_AUG)), per_bg((dh, nc)),
            per_bg((S, K_AUG)), per_bg((V_ROWS, S)),
            per_bg((S, K_AUG)), per_bg((V_ROWS, S)),
            pl.BlockSpec((None, GATE_ROWS, tq), lambda b, g, i: (b, g, i)),
            pl.BlockSpec(ov.shape, lambda b, g, i: (0, 0)),
        ],
        out_specs=pl.BlockSpec((None, tq, R * dh), lambda b, g, i: (b, i, g)),
        out_shape=jax.ShapeDtypeStruct((B, S, G * R * dh), BF16),
        scratch_shapes=[pltpu.VMEM((S // SWEEP_TK, 2 * AUG_NBLK, R * tq), BF16),
                        pltpu.VMEM((dh, R * tq), F32),
                        pltpu.SMEM((S // SWEEP_TK,), jnp.int32)],
        compiler_params=_params("parallel", "parallel", "arbitrary"),
        name="nsa_attention",
    )(qt, kc, vct, ks, vst, kw, vwt, gates, ov)


BAND_TQ = 128
BAND_SUB = 32
BAND_PROBLEMS = 32
BAND = 128


def _banded_kernel(q_ref, k_ref, v_ref, o_ref, lse_ref, *, slopes, nsub, ncls):
    tq, dh, nh = BAND_TQ, HEAD_DIM, DIL_HEADS_PER_GROUP
    tk = tq + BAND
    width = nh * dh
    head_of_lane = lambda rows: lax.shift_right_logical(
        lax.broadcasted_iota(jnp.int32, (rows, width), 1), int(math.log2(dh)))
    lane_head, q_head = head_of_lane(tk), head_of_lane(tq)
    keep = [jnp.where(lane_head == h, 1.0, 0.0).astype(BF16) for h in range(nh)]

    def per_head(a):
        return jnp.concatenate([a * keep[h] for h in range(nh)], axis=0)

    def biases(first_key_offset):
        d = first_key_offset + (lax.broadcasted_iota(jnp.int32, (tq, tk), 0)
                                - lax.broadcasted_iota(jnp.int32, (tq, tk), 1))
        mask_bias = jnp.where((d >= 0) & (d <= BAND), 0.0, NEG_INF)
        neg_d = -d.astype(F32)
        return [slopes[h] * neg_d + mask_bias for h in range(nh)]

    subs = []
    for cls in range(ncls):
        for sub in range(nsub):
            i = pl.program_id(2) * nsub + sub
            k0 = pl.multiple_of(jnp.maximum(i - 1, 0) * tq, tq)
            q = q_ref[cls, sub * tq:(sub + 1) * tq, :]
            scores = _dot_nt(q, per_head(k_ref[cls, pl.ds(k0, tk), :]))
            subs.append((cls, sub, i * tq - k0, k0, scores))
    first_bias = biases(subs[0][2])
    inner_bias = biases(tq) if nsub > 1 else None
    probs = []
    for cls, sub, off, k0, s in subs:
        bias = first_bias if sub == 0 else inner_bias
        ps, mxs, dens = [], [], []
        for h in range(nh):
            sh = s[:, h * tk:(h + 1) * tk] + bias[h]
            mx = jnp.max(sh, axis=-1, keepdims=True)
            e = jnp.exp2(sh - mx)
            dens.append(jnp.sum(e, axis=-1, keepdims=True))
            ps.append(e.astype(BF16))
            mxs.append(mx)
        probs.append((cls, sub, k0, jnp.concatenate(ps, axis=1), mxs, dens))
    for cls, sub, k0, p, mxs, dens in probs:
        mx_all, den = mxs[nh - 1], dens[nh - 1]
        for h in range(nh - 2, -1, -1):
            mx_all = jnp.where(q_head == h, mxs[h], mx_all)
            den = jnp.where(q_head == h, dens[h], den)
        rows = slice(sub * tq, (sub + 1) * tq)
        o_ref[cls, rows, :] = _dot(p, per_head(v_ref[cls, pl.ds(k0, tk), :])) * (1.0 / den)
        lse_ref[cls, rows, :] = mx_all * LN2 + jnp.log(den)


def _banded(d, slopes):
    B, _, r, n, width = d.shape
    nsub = min(BAND_SUB, n // BAND_TQ)
    ncls = min(r, BAND_PROBLEMS // nsub)
    tq = BAND_TQ * nsub
    assert n >= BAND_TQ + BAND and n % tq == 0 and r % ncls == 0
    kern = functools.partial(_banded_kernel, slopes=tuple(slopes), nsub=nsub, ncls=ncls)
    tile = pl.BlockSpec((None, None, ncls, tq, width), lambda b, c, i: (b, 0, c, i, 0))
    kseq = pl.BlockSpec((None, None, ncls, n, width), lambda b, c, i: (b, 1, c, 0, 0))
    vseq = pl.BlockSpec((None, None, ncls, n, width), lambda b, c, i: (b, 2, c, 0, 0))
    out = pl.BlockSpec((None, ncls, tq, width), lambda b, c, i: (b, c, i, 0))
    return pl.pallas_call(
        kern,
        grid=(B, r // ncls, n // tq),
        in_specs=[tile, kseq, vseq],
        out_specs=(out, out),
        out_shape=(jax.ShapeDtypeStruct((B, r, n, width), F32),) * 2,
        compiler_params=_params("parallel", "parallel", "arbitrary"),
        name="banded_attention",
    )(d, d, d)


def _dilated(dils):
    slopes = _alibi_slopes(DIL_HEADS)
    outs, lses = [], []
    for gi, (w, r) in enumerate(DIL_CONFIGS):
        assert w // r == BAND
        sl = [s_ * r * LOG2E for s_ in slopes[gi * DIL_HEADS_PER_GROUP:(gi + 1) * DIL_HEADS_PER_GROUP]]
        o, lse = _banded(dils[gi], sl)
        outs.append(o)
        lses.append(lse)
    return outs, lses


MERGE_TM = 1024


def _token_order(ref, slab_ref):
    r, rows, width = ref.shape
    if r == 1:
        return ref[0]
    for c in range(r):
        blk = ref[c]
        for s in range(width // LANES):
            slab_ref[s, pl.ds(c, rows, stride=r), :] = blk[:, s * LANES:(s + 1) * LANES]
    return jnp.concatenate([slab_ref[s] for s in range(width // LANES)], axis=1)


def _merge_kernel(x_ref, oa_ref, o0_ref, o1_ref, o2_ref, l0_ref, l1_ref, l2_ref,
                  gmix_ref, wm_ref, wpn_ref, wpd_ref, wo_ref, gffn_ref, x1_ref, h2_ref, slab_ref):
    x = x_ref[...]
    D = x.shape[1]
    h = _rms(x, gmix_ref[...]).astype(BF16)
    gm = _sigmoid(_dot_nt(h, wm_ref[...]))
    o0, o1, o2 = [_token_order(r_, slab_ref) for r_ in (o0_ref, o1_ref, o2_ref)]
    l0, l1, l2 = [_token_order(r_, slab_ref) for r_ in (l0_ref, l1_ref, l2_ref)]
    mx = jnp.maximum(jnp.maximum(l0, l1), l2)
    e0, e1, e2 = jnp.exp(l0 - mx), jnp.exp(l1 - mx), jnp.exp(l2 - mx)
    inv = 1.0 / (e0 + e1 + e2)
    ob = o0 * (e0 * inv) + o1 * (e1 * inv) + o2 * (e2 * inv)
    a = _dot(oa_ref[...], wpn_ref[...])
    d = _dot(ob.astype(BF16), wpd_ref[...])
    mixed = gm[:, :D] * a + gm[:, D:] * d
    x1 = x + _dot(mixed.astype(BF16), wo_ref[...])
    x1_ref[...] = x1
    h2_ref[...] = _rms(x1, gffn_ref[...]).astype(BF16)


def _merge(x, o_a, outs, lses, g_mix, w_merge, w_proj_nsa, w_proj_dil, w_out, g_ffn):
    B, S, D = x.shape
    tm = MERGE_TM
    row = lambda a: pl.BlockSpec((None, tm, a.shape[2]), lambda b, i: (b, i, 0))
    cls = lambda a: pl.BlockSpec((None, a.shape[1], tm // a.shape[1], a.shape[3]), lambda b, i: (b, 0, i, 0))
    full = lambda a: pl.BlockSpec(a.shape, lambda b, i: (0,) * a.ndim)
    ws = [w_merge.astype(BF16), w_proj_nsa.astype(BF16), w_proj_dil.astype(BF16), w_out.astype(BF16)]
    consts = [g_mix, *ws, g_ffn]
    in_specs = [row(x), row(o_a)] + [cls(a) for a in (*outs, *lses)] + [full(a) for a in consts]
    return pl.pallas_call(
        _merge_kernel,
        grid=(B, S // tm),
        in_specs=in_specs,
        out_specs=(pl.BlockSpec((None, tm, D), lambda b, i: (b, i, 0)),) * 2,
        out_shape=(jax.ShapeDtypeStruct((B, S, D), F32), jax.ShapeDtypeStruct((B, S, D), BF16)),
        scratch_shapes=[pltpu.VMEM((DIL_WIDTH // LANES, tm, LANES), F32)],
        compiler_params=_params("parallel", "parallel"),
        name="merge_proj",
    )(x, o_a, *outs, *lses, *consts)


FFN_TM = 1024
FFN_TN = 256
HALO = 16


def _ffn_kernel(h_ref, halo_ref, x1_ref, wup_ref, cw_ref, cb_ref, wd_ref, gfin_ref, o_ref, act_ref):
    i = pl.program_id(1)
    h = h_ref[...]
    halo = halo_ref[...]
    tm = h.shape[0]
    row = lax.broadcasted_iota(jnp.int32, (tm, FFN_TN), 0)
    live = (i > 0).astype(F32)
    for j in range(D_FF // FFN_TN):
        cols = slice(j * FFN_TN, (j + 1) * FFN_TN)
        wu = wup_ref[:, cols]
        u = _dot(h, wu)
        uh = _dot(halo, wu) * live
        gate = _dot(h, wup_ref[:, D_FF + j * FFN_TN:D_FF + (j + 1) * FFN_TN])
        p1 = jnp.broadcast_to(uh[HALO - 1:HALO, :], (tm, FFN_TN))
        p2 = jnp.broadcast_to(uh[HALO - 2:HALO - 1, :], (tm, FFN_TN))
        u1 = jnp.where(row == 0, p1, pltpu.roll(u, 1, 0))
        u2 = jnp.where(row == 0, p2, jnp.where(row == 1, p1, pltpu.roll(u, 2, 0)))
        uc = cb_ref[:, cols] + cw_ref[0:1, cols] * u2
        uc = uc + cw_ref[1:2, cols] * u1
        uc = uc + cw_ref[2:3, cols] * u
        act_ref[:, j * FFN_TN:(j + 1) * FFN_TN] = (jax.nn.gelu(uc) * gate).astype(BF16)
    y = _dot(act_ref[...], wd_ref[...])
    o_ref[...] = _rms(x1_ref[...] + y, gfin_ref[...])


def _ffn(h2, x1, w_up, conv_w, conv_b, w_down, g_final):
    B, S, D = h2.shape
    tm = FFN_TM
    assert D_FF % FFN_TN == 0
    wup = w_up.astype(BF16)
    cw = conv_w
    cb = conv_b.reshape(1, D_FF)
    wd = w_down.astype(BF16)
    gfin = g_final.reshape(1, D)
    full = lambda a: pl.BlockSpec(a.shape, lambda b, i: (0,) * a.ndim, pipeline_mode=pl.Buffered(1))
    tile = pl.BlockSpec((None, tm, D), lambda b, i: (b, i, 0))
    halo = pl.BlockSpec((None, HALO, D), lambda b, i: (b, jnp.maximum(i * (tm // HALO) - 1, 0), 0))
    return pl.pallas_call(
        _ffn_kernel,
        grid=(B, S // tm),
        in_specs=[tile, halo, tile, full(wup), full(cw), full(cb), full(wd), full(gfin)],
        out_specs=tile,
        out_shape=jax.ShapeDtypeStruct((B, S, D), F32),
        scratch_shapes=[pltpu.VMEM((tm, D_FF), BF16)],
        compiler_params=_params("parallel", "parallel"),
        name="conv_ffn",
    )(h2, h2, x1, wup, cw, cb, wd, gfin)


@jax.jit
def _layer(x, g_mix, w_in, pe_cmp_k, w_cmp_k1, w_cmp_k2, pe_cmp_v, w_cmp_v1, w_cmp_v2,
           w_proj_nsa, w_proj_dil, w_out, g_ffn, w_up, conv_w, conv_b, w_down, g_final):
    B, S, D = x.shape
    depth = g_mix.shape[0]
    for l in range(depth):
        gm = g_mix[l].reshape(1, D)
        kcmp, vcmp, ks, kw, d0, d1, d2, qt, vst, vwt, gates = _in_proj(x, gm, w_in[l])
        kc, vct = _compress(kcmp, vcmp, pe_cmp_k[l], w_cmp_k1[l], w_cmp_k2[l],
                            pe_cmp_v[l], w_cmp_v1[l], w_cmp_v2[l])
        o_a = _nsa(qt, kc, vct, ks, vst, kw, vwt, gates)
        outs, lses = _dilated((d0, d1, d2))
        merge_cols = w_in[l].T[w_in.shape[2] - 2 * D:]
        x1, h2 = _merge(x, o_a, outs, lses, gm, merge_cols, w_proj_nsa[l], w_proj_dil[l], w_out[l],
                        g_ffn[l].reshape(1, D))
        x = _ffn(h2, x1, w_up[l], conv_w[l], conv_b[l], w_down[l], g_final)
        assert depth == 1
    return x


def kernel(x, g_mix, w_in, pe_cmp_k, w_cmp_k1, w_cmp_k2, pe_cmp_v, w_cmp_v1, w_cmp_v2, w_proj_nsa, w_proj_dil, w_out, g_ffn, w_up, conv_w, conv_b, w_down, g_final):
    return _layer(x, g_mix, w_in, pe_cmp_k, w_cmp_k1, w_cmp_k2, pe_cmp_v, w_cmp_v1, w_cmp_v2,
                  w_proj_nsa, w_proj_dil, w_out, g_ffn, w_up, conv_w, conv_b, w_down, g_final)
```

```python
import functools
import math

import numpy as np
import jax
import jax.numpy as jnp
from jax import lax
from jax.experimental import pallas as pl
from jax.experimental.pallas import tpu as pltpu

HEAD_DIM = 64
NSA_HEADS = 8
NSA_GROUPS = 2
NSA_REP = NSA_HEADS // NSA_GROUPS
CMP_BLOCK = 32
CMP_STRIDE = 16
CMP_HIDDEN = 128
SLC_BLOCK = 64
SLC_TOP = 16
NSA_WINDOW = 512
FORCE_SCORE = 1.0e4
DIL_CONFIGS = ((128, 1), (512, 4), (2048, 16))
DIL_GROUPS = 3
DIL_HEADS_PER_GROUP = 4
DIL_HEADS = DIL_GROUPS * DIL_HEADS_PER_GROUP
D_FF = 2816
CONV_WIDTH = 3
RMS_EPS = 1e-6
NEG_INF = -1e30

LANES = 128
VMEM_LIMIT_BYTES = 56 * 1024 * 1024

F32 = jnp.float32
BF16 = jnp.bfloat16
NT_DIMS = (((1,), (1,)), ((), ()))


def _alibi_slopes(n):
    return [float(2.0 ** (-8.0 * i / n)) for i in range(1, n + 1)]


def _rms(xf, g):
    ms = jnp.mean(xf * xf, axis=-1, keepdims=True)
    return xf * lax.rsqrt(ms + RMS_EPS) * g


def _dot(a, b):
    return jnp.dot(a, b, preferred_element_type=F32)


def _dot_nt(a, b):
    return lax.dot_general(a, b, NT_DIMS, preferred_element_type=F32)


def _sigmoid(z):
    return 1.0 / (1.0 + jnp.exp(-z))


def _params(*sem):
    return pltpu.CompilerParams(dimension_semantics=sem, vmem_limit_bytes=VMEM_LIMIT_BYTES)


IN_TM = 1024
N_KVC = 4 * HEAD_DIM
N_KSEL = NSA_GROUPS * HEAD_DIM
N_DIL = 3 * DIL_HEADS * HEAD_DIM
DIL_WIDTH = DIL_HEADS_PER_GROUP * HEAD_DIM
T_Q = NSA_HEADS * HEAD_DIM
T_V = NSA_GROUPS * HEAD_DIM
GATE_ROWS = 16
K_AUG = 2 * HEAD_DIM
SLC_SHIFT = int(math.log2(SLC_BLOCK))
AUG_NBLK = 8
AUG_HI, AUG_LO = AUG_NBLK, AUG_NBLK + 1
LOG2E = math.log2(math.e)
LN2 = math.log(2.0)
V_PAD = 16
V_ROWS = HEAD_DIM + V_PAD


def _key_position_columns(pos0, rows, step=1):
    pos = pos0 + step * lax.broadcasted_iota(jnp.int32, (rows, HEAD_DIM), 0)
    col = lax.broadcasted_iota(jnp.int32, (rows, HEAD_DIM), 1)
    blk = jnp.bitwise_and(lax.shift_right_logical(pos, SLC_SHIFT), AUG_NBLK - 1)
    hi = lax.shift_left(lax.shift_right_logical(pos, 7), 7).astype(F32)
    lo = jnp.bitwise_and(pos, 127).astype(F32)
    c = jnp.where((col == AUG_HI) | (col == AUG_HI + 2), hi,
                  jnp.where((col == AUG_LO) | (col == AUG_LO + 2), lo, 0.0))
    return jnp.where((col < AUG_NBLK) & (blk == col), 1.0, c)


def _in_proj_kernel(x_ref, g_ref, wn_ref, wt_ref,
                    kcmp_ref, vcmp_ref, ks_ref, kw_ref, d0_ref, d1_ref, d2_ref, qt_ref, vst_ref, vwt_ref, gate_ref,
                    slab_ref):
    tm = x_ref.shape[0]
    h = _rms(x_ref[...], g_ref[...]).astype(BF16)
    c0 = 0
    kvc = _dot_nt(h, wn_ref[c0:c0 + N_KVC, :])
    kcmp_ref[...] = kvc[:, :N_KSEL]
    vcmp_ref[...] = kvc[:, N_KSEL:]
    c0 += N_KVC
    ks = _dot_nt(h, wn_ref[c0:c0 + N_KSEL, :]).astype(BF16)
    c0 += N_KSEL
    kw = _dot_nt(h, wn_ref[c0:c0 + N_KSEL, :]).astype(BF16)
    c0 += N_KSEL
    aug = _key_position_columns(pl.program_id(1) * tm, tm).astype(BF16)
    for g in range(NSA_GROUPS):
        ks_ref[g] = jnp.concatenate([ks[:, g * HEAD_DIM:(g + 1) * HEAD_DIM], aug], axis=1)
        kw_ref[g] = jnp.concatenate([kw[:, g * HEAD_DIM:(g + 1) * HEAD_DIM], aug], axis=1)
    seg = DIL_HEADS * HEAD_DIM
    for which in range(3):
        y = _dot_nt(h, wn_ref[c0 + which * seg:c0 + (which + 1) * seg, :])
        for gi, (d_ref, (_, r)) in enumerate(zip((d0_ref, d1_ref, d2_ref), DIL_CONFIGS)):
            yg = y[:, gi * DIL_WIDTH:(gi + 1) * DIL_WIDTH]
            if r == 1:
                d_ref[which, 0] = yg.astype(BF16)
                continue
            for s in range(DIL_WIDTH // LANES):
                slab_ref[s] = yg[:, s * LANES:(s + 1) * LANES]
            for c in range(r):
                d_ref[which, c] = jnp.concatenate(
                    [slab_ref[s, pl.ds(c, tm // r, stride=r), :] for s in range(DIL_WIDTH // LANES)],
                    axis=1).astype(BF16)
    yt = _dot_nt(wt_ref[...], h)
    qt_ref[...] = yt[0:T_Q].astype(BF16)
    r0 = T_Q
    ones = jnp.where(lax.broadcasted_iota(jnp.int32, (V_PAD, tm), 0) == 0, 1.0, 0.0).astype(BF16)
    for ref in (vst_ref, vwt_ref):
        vt = yt[r0:r0 + T_V].astype(BF16)
        r0 += T_V
        for g in range(NSA_GROUPS):
            ref[g] = jnp.concatenate([vt[g * HEAD_DIM:(g + 1) * HEAD_DIM, :], ones], axis=0)
    gate_ref[...] = _sigmoid(yt[r0:r0 + NSA_GROUPS * GATE_ROWS])


def _in_proj(x, g_mix, w_in):
    B, S, D = x.shape
    scale = HEAD_DIM ** -0.5 * LOG2E
    o_q, o_kv = 0, T_Q
    o_gate = o_kv + 6 * N_KSEL
    o_dil = o_gate + 3 * NSA_HEADS
    o_merge = o_dil + N_DIL
    w_t = w_in.T
    kv = w_t[o_kv:o_gate]

    def kind(k):
        return kv[k * N_KSEL:(k + 1) * N_KSEL]

    dil = w_t[o_dil:o_merge]
    dil = jnp.concatenate([dil[:DIL_HEADS * HEAD_DIM] * scale, dil[DIL_HEADS * HEAD_DIM:]], axis=0)
    wn = jnp.concatenate([kind(0), kind(1), kind(2), kind(4), dil], axis=0).astype(BF16)
    wg = w_t[o_gate:o_dil].reshape(NSA_GROUPS, 3 * NSA_REP, D)
    wg = jnp.pad(wg, ((0, 0), (0, GATE_ROWS - 3 * NSA_REP), (0, 0))).reshape(NSA_GROUPS * GATE_ROWS, D)
    wt = jnp.concatenate([w_t[o_q:o_kv] * scale, kind(3), kind(5), wg], axis=0).astype(BF16)
    tm = IN_TM
    grid = (B, S // tm)
    full = lambda a: pl.BlockSpec(a.shape, lambda b, i: (0,) * a.ndim)
    k_shape = jax.ShapeDtypeStruct((B, NSA_GROUPS, S, K_AUG), BF16)
    v_shape = jax.ShapeDtypeStruct((B, NSA_GROUPS, V_ROWS, S), BF16)
    k_spec = pl.BlockSpec((None, NSA_GROUPS, tm, K_AUG), lambda b, i: (b, 0, i, 0))
    v_spec = pl.BlockSpec((None, NSA_GROUPS, V_ROWS, tm), lambda b, i: (b, 0, 0, i))
    c_shape = jax.ShapeDtypeStruct((B, S, N_KSEL), F32)
    c_spec = pl.BlockSpec((None, tm, N_KSEL), lambda b, i: (b, i, 0))
    d_shapes = tuple(jax.ShapeDtypeStruct((B, 3, r, S // r, DIL_WIDTH), BF16) for _, r in DIL_CONFIGS)
    d_specs = tuple(pl.BlockSpec((None, 3, r, tm // r, DIL_WIDTH), lambda b, i: (b, 0, 0, i, 0))
                    for _, r in DIL_CONFIGS)
    out_shape = (
        c_shape, c_shape, k_shape, k_shape, *d_shapes,
        jax.ShapeDtypeStruct((B, T_Q, S), BF16),
        v_shape,
        v_shape,
        jax.ShapeDtypeStruct((B, NSA_GROUPS * GATE_ROWS, S), F32),
    )
    out_specs = (
        c_spec, c_spec, k_spec, k_spec, *d_specs,
        pl.BlockSpec((None, T_Q, tm), lambda b, i: (b, 0, i)),
        v_spec,
        v_spec,
        pl.BlockSpec((None, NSA_GROUPS * GATE_ROWS, tm), lambda b, i: (b, 0, i)),
    )
    return pl.pallas_call(
        _in_proj_kernel,
        grid=grid,
        in_specs=[pl.BlockSpec((None, tm, D), lambda b, i: (b, i, 0)), full(g_mix), full(wn), full(wt)],
        out_specs=out_specs,
        out_shape=out_shape,
        scratch_shapes=[pltpu.VMEM((DIL_WIDTH // LANES, tm, LANES), F32)],
        compiler_params=_params("parallel", "parallel"),
        name="in_proj",
    )(x, g_mix, wn, wt)


def _compress_kernel(xk_ref, xv_ref, pek_ref, pev_ref, w1k_ref, w1v_ref, w2k_ref, w2vt_ref, kc_ref, vct_ref):
    nch = xk_ref.shape[0] // CMP_STRIDE
    streams = [(xk_ref, pek_ref, w1k_ref), (xv_ref, pev_ref, w1v_ref)]
    acc = [[jnp.zeros((nch, NSA_GROUPS * CMP_HIDDEN), F32) for _ in range(2)] for _ in streams]
    for j in range(CMP_STRIDE):
        for n, (x_ref, pe_ref, w1_ref) in enumerate(streams):
            xj = x_ref[pl.ds(j, nch, stride=CMP_STRIDE), :]
            for half in range(2):
                row = half * CMP_STRIDE + j
                acc[n][half] = acc[n][half] + _dot((xj + pe_ref[row:row + 1, :]).astype(BF16), w1_ref[row])
    hid = [jax.nn.gelu(lo + pltpu.roll(hi, nch - 1, 0)).astype(BF16) for lo, hi in acc]
    aug = _key_position_columns(CMP_BLOCK - 1, nch, CMP_STRIDE).astype(BF16)
    for g in range(NSA_GROUPS):
        cols = slice(g * CMP_HIDDEN, (g + 1) * CMP_HIDDEN)
        kc_ref[g] = jnp.concatenate([_dot(hid[0][:, cols], w2k_ref[...]).astype(BF16), aug], axis=1)
        vct_ref[g] = _dot_nt(w2vt_ref[...], hid[1][:, cols]).astype(BF16)


def _compress(xk, xv, pe_k, w1_k, w2_k, pe_v, w1_v, w2_v):
    B, S, width = xk.shape
    nch = S // CMP_STRIDE
    G, dh, hid = NSA_GROUPS, HEAD_DIM, CMP_HIDDEN
    eye = jnp.eye(G, dtype=w1_k.dtype)

    def expand(pe, w1):
        pe_t = jnp.broadcast_to(pe.reshape(CMP_BLOCK, 1, dh), (CMP_BLOCK, G, dh)).reshape(CMP_BLOCK, width)
        wexp = jnp.einsum('pdn,ge->pgden', w1.reshape(CMP_BLOCK, dh, hid), eye).reshape(CMP_BLOCK, width, G * hid)
        return pe_t, wexp.astype(BF16)

    pek, w1k = expand(pe_k, w1_k)
    pev, w1v = expand(pe_v, w1_v)
    consts = [pek, pev, w1k, w1v, w2_k.astype(BF16), w2_v.T.astype(BF16)]
    full = lambda a: pl.BlockSpec(a.shape, lambda b: (0,) * a.ndim)
    seq = pl.BlockSpec((None, S, width), lambda b: (b, 0, 0))
    return pl.pallas_call(
        _compress_kernel,
        grid=(B,),
        in_specs=[seq, seq] + [full(a) for a in consts],
        out_specs=(pl.BlockSpec((None, G, nch, K_AUG), lambda b: (b, 0, 0, 0)),
                   pl.BlockSpec((None, G, dh, nch), lambda b: (b, 0, 0, 0))),
        out_shape=(jax.ShapeDtypeStruct((B, G, nch, K_AUG), BF16),
                   jax.ShapeDtypeStruct((B, G, dh, nch), BF16)),
        compiler_params=_params("parallel"),
        name="compress",
    )(xk, xv, *consts)


NSA_TQ = 256
SWEEP_TK = AUG_NBLK * SLC_BLOCK
SEL_SUB = 128
SEL_AHEAD = 6


def _rowmax8(s):
    return jnp.max(s.reshape(s.shape[0] // 8, 8, s.shape[1]), axis=0)


def _online_softmax(chunks, scores, values):
    pending = [scores(ch) for ch in chunks[:SEL_AHEAD]]
    m = acc = None
    for n, ch in enumerate(chunks):
        if n + SEL_AHEAD < len(chunks):
            pending.append(scores(chunks[n + SEL_AHEAD]))
        s = pending.pop(0)
        m_c = jnp.max(_rowmax8(s), axis=0, keepdims=True)
        if m is None:
            m = m_c
            acc = _dot(values(ch), jnp.exp2(s - m).astype(BF16))
        else:
            m_new = jnp.maximum(m, m_c)
            acc = jnp.exp2(m - m_new) * acc + _dot(values(ch), jnp.exp2(s - m_new).astype(BF16))
            m = m_new
    return acc


def _nsa_kernel(qt_ref, kc_ref, vct_ref, ks_ref, vst_ref, kw_ref, vwt_ref, gate_ref, ov_ref,
                o_ref, sbt_ref, osel_ref, idx_ref, *, slopes):
    g = pl.program_id(1)
    i = pl.program_id(2)
    R, dh, tq, tk = NSA_REP, HEAD_DIM, NSA_TQ, SWEEP_TK
    L = R * tq
    t0 = i * tq
    nc = kc_ref.shape[0]
    ns = ov_ref.shape[0]
    n_tiles = ns // AUG_NBLK
    tile4 = lambda a: jnp.concatenate([a] * R, axis=1)

    qt = qt_ref[...]
    qs = jnp.concatenate([qt[r * dh:(r + 1) * dh, :] for r in range(R)], axis=1)
    slope = [jnp.where(g == 0, slopes[r], slopes[R + r]).astype(F32) for r in range(R)]
    slope_row = jnp.concatenate([jnp.full((1, tq), 1.0, F32) * slope[r] for r in range(R)], axis=1)
    t_row = t0 + lax.broadcasted_iota(jnp.int32, (1, tq), 1)

    r8 = lax.broadcasted_iota(jnp.int32, (AUG_NBLK, L), 0)
    s_full = slope_row * LOG2E
    s_hi = s_full.astype(BF16).astype(F32)
    alibi8 = jnp.where(r8 < 2, s_hi, jnp.where(r8 < 4, s_full - s_hi, 0.0))
    q_pad = jnp.zeros((K_AUG - dh - 2 * AUG_NBLK, L), BF16)
    q_plain = jnp.concatenate([qs, jnp.concatenate([jnp.zeros((AUG_NBLK, L), F32), alibi8], axis=0).astype(BF16),
                               q_pad], axis=0)

    cmp_end = lax.broadcasted_iota(jnp.int32, (nc, tq), 0) * CMP_STRIDE + (CMP_BLOCK - 1)
    m_cmp = t_row >= cmp_end
    sc = _dot(kc_ref[...], q_plain) + tile4(jnp.where(m_cmp, 0.0, NEG_INF))

    a0 = jnp.maximum(t0 - NSA_WINDOW, 0)
    row_w = lax.broadcasted_iota(jnp.int32, (SEL_SUB, tq), 0)
    row_minus_lane = row_w - lax.broadcasted_iota(jnp.int32, (SEL_SUB, tq), 1)

    def causal_bias(k0):
        return jnp.where(row_minus_lane <= t0 - k0, 0.0, NEG_INF)

    def win_scores(ch):
        diag, r = ch
        k0 = pl.multiple_of((t0 if diag else a0) + r, SEL_SUB)
        if diag:
            bias = causal_bias(k0)
        else:
            bias = jnp.where((row_w < t0 - k0) & (row_minus_lane > t0 - k0 - NSA_WINDOW), 0.0, NEG_INF)
        return _dot(kw_ref[pl.ds(k0, SEL_SUB), :], q_plain) + tile4(bias)

    def win_values(ch):
        diag, r = ch
        return vwt_ref[:, pl.ds(pl.multiple_of((t0 if diag else a0) + r, SEL_SUB), SEL_SUB)]

    win_chunks = ([(True, r) for r in range(0, tq, SEL_SUB)]
                  + [(False, r) for r in range(0, NSA_WINDOW, SEL_SUB)])
    acc_w = _online_softmax(win_chunks, win_scores, win_values)
    o_win = acc_w[:dh] * (1.0 / acc_w[dh:dh + 1])

    mx = jnp.max(_rowmax8(sc), axis=0, keepdims=True)
    p = jnp.exp2(sc - mx) * tile4(jnp.where(m_cmp, 1.0, 0.0))
    den = jnp.sum(jnp.sum(p.reshape(nc // 8, 8, L), axis=0), axis=0, keepdims=True)
    pr = p * (1.0 / jnp.maximum(den, 1e-30))
    o_cmp = _dot(vct_ref[...], pr.astype(BF16))
    psum = pr[:, 0:tq]
    for r in range(1, R):
        psum = psum + pr[:, r * tq:(r + 1) * tq]

    p_hi = psum.astype(BF16)
    p_lo = (psum - p_hi.astype(F32)).astype(BF16)
    imp = _dot(ov_ref[...], p_hi) + _dot(ov_ref[...], p_lo)
    blk = lax.broadcasted_iota(jnp.int32, (ns, tq), 0)
    cur = lax.shift_right_logical(t_row, SLC_SHIFT)
    val = jnp.where((blk == cur) | (blk == 0), FORCE_SCORE, imp)
    val = jnp.where(blk <= cur, val, -1.0)
    vals = [val[8 * v:8 * v + 8, :] for v in range(ns // 8)]
    ranks = [jnp.zeros((8, tq), F32) for _ in vals]
    row8 = lax.broadcasted_iota(jnp.int32, (8, tq), 0)
    for j in range(ns):
        vj = jnp.broadcast_to(val[j:j + 1, :], (8, tq))
        for v in range(len(vals)):
            if 8 * v > j:
                ahead = vj >= vals[v]
            elif 8 * v + 7 <= j:
                ahead = vj > vals[v]
            else:
                ahead = (vj > vals[v]) | ((vj == vals[v]) & (row8 > j - 8 * v))
            ranks[v] = ranks[v] + jnp.where(ahead, 1.0, 0.0)
    for T in range(n_tiles):
        selb = jnp.where((ranks[T] < float(SLC_TOP)) & (vals[T] >= 0.0), 0.0, NEG_INF)
        sbt_ref[T] = jnp.concatenate([tile4(selb), alibi8], axis=0).astype(BF16)

    td = lax.div(t0, tk)
    cnt = jnp.int32(0)
    for T in range(n_tiles - 1):
        picked = jnp.where((ranks[T] < float(SLC_TOP)) & (vals[T] >= 0.0), 1.0, 0.0)
        idx_ref[cnt] = jnp.int32(T)
        cnt = cnt + jnp.logical_and(jnp.max(picked) > 0.0, T < td).astype(jnp.int32)
    for k in range(n_tiles):
        @pl.when(cnt == k)
        def _(k=k):
            past = [idx_ref[j] for j in range(k)]
            q_past = [jnp.concatenate([qs, sbt_ref[T], q_pad], axis=0) for T in past]
            q_diag = jnp.concatenate([qs, sbt_ref[td], q_pad], axis=0)
            chunks = ([(j, r) for j in range(k) for r in range(0, tk, SEL_SUB)]
                      + [(None, r) for r in range(0, tk, SEL_SUB)])

            def key_start(ch):
                j, r = ch
                return pl.multiple_of((td if j is None else past[j]) * tk + r, SEL_SUB)

            def chunk_scores(ch):
                k0 = key_start(ch)
                if ch[0] is None:
                    return _dot(ks_ref[pl.ds(k0, SEL_SUB), :], q_diag) + tile4(causal_bias(k0))
                return _dot(ks_ref[pl.ds(k0, SEL_SUB), :], q_past[ch[0]])

            acc = _online_softmax(chunks, chunk_scores, lambda ch: vst_ref[:, pl.ds(key_start(ch), SEL_SUB)])
            osel_ref[...] = acc[:dh] * (1.0 / acc[dh:dh + 1])

    o_sel = osel_ref[...]
    gates = gate_ref[...]
    tiles = []
    for r in range(R):
        sl = slice(r * tq, (r + 1) * tq)
        tiles.append(gates[3 * r:3 * r + 1, :] * o_cmp[:, sl]
                     + gates[3 * r + 1:3 * r + 2, :] * o_sel[:, sl]
                     + gates[3 * r + 2:3 * r + 3, :] * o_win[:, sl])
    o_ref[...] = jnp.concatenate(tiles, axis=0).T.astype(o_ref.dtype)


def _overlap_matrix(nc, ns):
    cs = np.arange(nc)[None, :] * CMP_STRIDE
    ss = np.arange(ns)[:, None] * SLC_BLOCK
    ov = np.clip(np.minimum(cs + CMP_BLOCK, ss + SLC_BLOCK) - np.maximum(cs, ss), 0, None)
    return jnp.asarray(ov.astype(np.float32) / CMP_BLOCK, dtype=BF16)


def _nsa(qt, kc, vct, ks, vst, kw, vwt, gates):
    B, _, S = qt.shape
    G, R, dh, tq = NSA_GROUPS, NSA_REP, HEAD_DIM, NSA_TQ
    nc = kc.shape[2]
    ns = S // SLC_BLOCK
    ov = _overlap_matrix(nc, ns)
    kern = functools.partial(_nsa_kernel, slopes=tuple(_alibi_slopes(NSA_HEADS)))
    assert S % SWEEP_TK == 0 and S >= NSA_WINDOW + tq
    per_bg = lambda shape: pl.BlockSpec((None, None) + shape, lambda b, g, i: (b, g, 0, 0))
    return pl.pallas_call(
        kern,
        grid=(B, G, S // tq),
        in_specs=[
            pl.BlockSpec((None, R * dh, tq), lambda b, g, i: (b, g, i)),
            per_bg((nc, K_AUG)), per_bg((dh, nc)),
            per_bg((S, K_AUG)), per_bg((V_ROWS, S)),
            per_bg((S, K_AUG)), per_bg((V_ROWS, S)),
            pl.BlockSpec((None, GATE_ROWS, tq), lambda b, g, i: (b, g, i)),
            pl.BlockSpec(ov.shape, lambda b, g, i: (0, 0)),
        ],
        out_specs=pl.BlockSpec((None, tq, R * dh), lambda b, g, i: (b, i, g)),
        out_shape=jax.ShapeDtypeStruct((B, S, G * R * dh), BF16),
        scratch_shapes=[pltpu.VMEM((S // SWEEP_TK, 2 * AUG_NBLK, R * tq), BF16),
                        pltpu.VMEM((dh, R * tq), F32),
                        pltpu.SMEM((S // SWEEP_TK,), jnp.int32)],
        compiler_params=_params("parallel", "parallel", "arbitrary"),
        name="nsa_attention",
    )(qt, kc, vct, ks, vst, kw, vwt, gates, ov)


BAND_TQ = 128
BAND_SUB = 16
BAND_PROBLEMS = 16
BAND = 128


def _banded_kernel(q_ref, k_ref, v_ref, o_ref, lse_ref, *, slopes, nsub, ncls):
    tq, dh, nh = BAND_TQ, HEAD_DIM, DIL_HEADS_PER_GROUP
    tk = tq + BAND
    width = nh * dh
    head_of_lane = lambda rows: lax.shift_right_logical(
        lax.broadcasted_iota(jnp.int32, (rows, width), 1), int(math.log2(dh)))
    lane_head, q_head = head_of_lane(tk), head_of_lane(tq)
    keep = [jnp.where(lane_head == h, 1.0, 0.0).astype(BF16) for h in range(nh)]

    def per_head(a):
        return jnp.concatenate([a * keep[h] for h in range(nh)], axis=0)

    def biases(first_key_offset):
        d = first_key_offset + (lax.broadcasted_iota(jnp.int32, (tq, tk), 0)
                                - lax.broadcasted_iota(jnp.int32, (tq, tk), 1))
        mask_bias = jnp.where((d >= 0) & (d <= BAND), 0.0, NEG_INF)
        neg_d = -d.astype(F32)
        return [slopes[h] * neg_d + mask_bias for h in range(nh)]

    subs = []
    for cls in range(ncls):
        for sub in range(nsub):
            i = pl.program_id(2) * nsub + sub
            k0 = pl.multiple_of(jnp.maximum(i - 1, 0) * tq, tq)
            q = q_ref[cls, sub * tq:(sub + 1) * tq, :]
            scores = _dot_nt(q, per_head(k_ref[cls, pl.ds(k0, tk), :]))
            subs.append((cls, sub, i * tq - k0, k0, scores))
    first_bias = biases(subs[0][2])
    inner_bias = biases(tq) if nsub > 1 else None
    probs = []
    for cls, sub, off, k0, s in subs:
        bias = first_bias if sub == 0 else inner_bias
        ps, mxs, dens = [], [], []
        for h in range(nh):
            sh = s[:, h * tk:(h + 1) * tk] + bias[h]
            mx = jnp.max(sh, axis=-1, keepdims=True)
            e = jnp.exp2(sh - mx)
            dens.append(jnp.sum(e, axis=-1, keepdims=True))
            ps.append(e.astype(BF16))
            mxs.append(mx)
        probs.append((cls, sub, k0, jnp.concatenate(ps, axis=1), mxs, dens))
    for cls, sub, k0, p, mxs, dens in probs:
        mx_all, den = mxs[nh - 1], dens[nh - 1]
        for h in range(nh - 2, -1, -1):
            mx_all = jnp.where(q_head == h, mxs[h], mx_all)
            den = jnp.where(q_head == h, dens[h], den)
        rows = slice(sub * tq, (sub + 1) * tq)
        o_ref[cls, rows, :] = _dot(p, per_head(v_ref[cls, pl.ds(k0, tk), :])) * (1.0 / den)
        lse_ref[cls, rows, :] = mx_all * LN2 + jnp.log(den)


def _banded(d, slopes):
    B, _, r, n, width = d.shape
    nsub = min(BAND_SUB, n // BAND_TQ)
    ncls = min(r, BAND_PROBLEMS // nsub)
    tq = BAND_TQ * nsub
    assert n >= BAND_TQ + BAND and n % tq == 0 and r % ncls == 0
    kern = functools.partial(_banded_kernel, slopes=tuple(slopes), nsub=nsub, ncls=ncls)
    tile = pl.BlockSpec((None, None, ncls, tq, width), lambda b, c, i: (b, 0, c, i, 0))
    kseq = pl.BlockSpec((None, None, ncls, n, width), lambda b, c, i: (b, 1, c, 0, 0))
    vseq = pl.BlockSpec((None, None, ncls, n, width), lambda b, c, i: (b, 2, c, 0, 0))
    out = pl.BlockSpec((None, ncls, tq, width), lambda b, c, i: (b, c, i, 0))
    return pl.pallas_call(
        kern,
        grid=(B, r // ncls, n // tq),
        in_specs=[tile, kseq, vseq],
        out_specs=(out, out),
        out_shape=(jax.ShapeDtypeStruct((B, r, n, width), F32),) * 2,
        compiler_params=_params("parallel", "parallel", "arbitrary"),
        name="banded_attention",
    )(d, d, d)


def _dilated(dils):
    slopes = _alibi_slopes(DIL_HEADS)
    outs, lses = [], []
    for gi, (w, r) in enumerate(DIL_CONFIGS):
        assert w // r == BAND
        sl = [s_ * r * LOG2E for s_ in slopes[gi * DIL_HEADS_PER_GROUP:(gi + 1) * DIL_HEADS_PER_GROUP]]
        o, lse = _banded(dils[gi], sl)
        outs.append(o)
        lses.append(lse)
    return outs, lses


MERGE_TM = 1024
MERGE_TN = 256


def _token_order(ref, slab_ref):
    r, rows, width = ref.shape
    if r == 1:
        return ref[0]
    for c in range(r):
        blk = ref[c]
        for s in range(width // LANES):
            slab_ref[s, pl.ds(c, rows, stride=r), :] = blk[:, s * LANES:(s + 1) * LANES]
    return jnp.concatenate([slab_ref[s] for s in range(width // LANES)], axis=1)


def _merge_kernel(x_ref, oa_ref, o0_ref, o1_ref, o2_ref, l0_ref, l1_ref, l2_ref,
                  gmix_ref, wm_ref, wpn_ref, wpd_ref, wo_ref, gffn_ref, x1_ref, h2_ref, slab_ref, mix_ref):
    x = x_ref[...]
    D = x.shape[1]
    h = _rms(x, gmix_ref[...]).astype(BF16)
    o0, o1, o2 = [_token_order(r_, slab_ref) for r_ in (o0_ref, o1_ref, o2_ref)]
    l0, l1, l2 = [_token_order(r_, slab_ref) for r_ in (l0_ref, l1_ref, l2_ref)]
    mx = jnp.maximum(jnp.maximum(l0, l1), l2)
    e0, e1, e2 = jnp.exp(l0 - mx), jnp.exp(l1 - mx), jnp.exp(l2 - mx)
    inv = 1.0 / (e0 + e1 + e2)
    ob = (o0 * (e0 * inv) + o1 * (e1 * inv) + o2 * (e2 * inv)).astype(BF16)
    oa = oa_ref[...]
    for c in range(D // MERGE_TN):
        cols = slice(c * MERGE_TN, (c + 1) * MERGE_TN)
        g_a = _sigmoid(_dot_nt(h, wm_ref[c * MERGE_TN:(c + 1) * MERGE_TN, :]))
        g_d = _sigmoid(_dot_nt(h, wm_ref[D + c * MERGE_TN:D + (c + 1) * MERGE_TN, :]))
        mix_ref[:, cols] = (g_a * _dot(oa, wpn_ref[:, cols]) + g_d * _dot(ob, wpd_ref[:, cols])).astype(BF16)
    x1 = x + _dot(mix_ref[...], wo_ref[...])
    x1_ref[...] = x1
    h2_ref[...] = _rms(x1, gffn_ref[...]).astype(BF16)


def _merge(x, o_a, outs, lses, g_mix, w_merge, w_proj_nsa, w_proj_dil, w_out, g_ffn):
    B, S, D = x.shape
    tm = MERGE_TM
    row = lambda a: pl.BlockSpec((None, tm, a.shape[2]), lambda b, i: (b, i, 0))
    cls = lambda a: pl.BlockSpec((None, a.shape[1], tm // a.shape[1], a.shape[3]), lambda b, i: (b, 0, i, 0))
    full = lambda a: pl.BlockSpec(a.shape, lambda b, i: (0,) * a.ndim)
    ws = [w_merge.astype(BF16), w_proj_nsa.astype(BF16), w_proj_dil.astype(BF16), w_out.astype(BF16)]
    consts = [g_mix, *ws, g_ffn]
    in_specs = [row(x), row(o_a)] + [cls(a) for a in (*outs, *lses)] + [full(a) for a in consts]
    return pl.pallas_call(
        _merge_kernel,
        grid=(B, S // tm),
        in_specs=in_specs,
        out_specs=(pl.BlockSpec((None, tm, D), lambda b, i: (b, i, 0)),) * 2,
        out_shape=(jax.ShapeDtypeStruct((B, S, D), F32), jax.ShapeDtypeStruct((B, S, D), BF16)),
        scratch_shapes=[pltpu.VMEM((DIL_WIDTH // LANES, tm, LANES), F32), pltpu.VMEM((tm, D), BF16)],
        compiler_params=_params("parallel", "parallel"),
        name="merge_proj",
    )(x, o_a, *outs, *lses, *consts)


FFN_TM = 1024
FFN_TN = 256
HALO = 16


def _ffn_kernel(h_ref, halo_ref, x1_ref, wup_ref, cw_ref, cb_ref, wd_ref, gfin_ref, o_ref, act_ref):
    i = pl.program_id(1)
    h = h_ref[...]
    halo = halo_ref[...]
    tm = h.shape[0]
    row = lax.broadcasted_iota(jnp.int32, (tm, FFN_TN), 0)
    live = (i > 0).astype(F32)
    for j in range(D_FF // FFN_TN):
        cols = slice(j * FFN_TN, (j + 1) * FFN_TN)
        wu = wup_ref[:, cols]
        u = _dot(h, wu)
        uh = _dot(halo, wu) * live
        gate = _dot(h, wup_ref[:, D_FF + j * FFN_TN:D_FF + (j + 1) * FFN_TN])
        p1 = jnp.broadcast_to(uh[HALO - 1:HALO, :], (tm, FFN_TN))
        p2 = jnp.broadcast_to(uh[HALO - 2:HALO - 1, :], (tm, FFN_TN))
        u1 = jnp.where(row == 0, p1, pltpu.roll(u, 1, 0))
        u2 = jnp.where(row == 0, p2, jnp.where(row == 1, p1, pltpu.roll(u, 2, 0)))
        uc = cb_ref[:, cols] + cw_ref[0:1, cols] * u2
        uc = uc + cw_ref[1:2, cols] * u1
        uc = uc + cw_ref[2:3, cols] * u
        act_ref[:, j * FFN_TN:(j + 1) * FFN_TN] = (jax.nn.gelu(uc) * gate).astype(BF16)
    y = _dot(act_ref[...], wd_ref[...])
    o_ref[...] = _rms(x1_ref[...] + y, gfin_ref[...])


def _ffn(h2, x1, w_up, conv_w, conv_b, w_down, g_final):
    B, S, D = h2.shape
    tm = FFN_TM
    assert D_FF % FFN_TN == 0
    wup = w_up.astype(BF16)
    cw = conv_w
    cb = conv_b.reshape(1, D_FF)
    wd = w_down.astype(BF16)
    gfin = g_final.reshape(1, D)
    full = lambda a: pl.BlockSpec(a.shape, lambda b, i: (0,) * a.ndim, pipeline_mode=pl.Buffered(1))
    tile = pl.BlockSpec((None, tm, D), lambda b, i: (b, i, 0))
    halo = pl.BlockSpec((None, HALO, D), lambda b, i: (b, jnp.maximum(i * (tm // HALO) - 1, 0), 0))
    return pl.pallas_call(
        _ffn_kernel,
        grid=(B, S // tm),
        in_specs=[tile, halo, tile, full(wup), full(cw), full(cb), full(wd), full(gfin)],
        out_specs=tile,
        out_shape=jax.ShapeDtypeStruct((B, S, D), F32),
        scratch_shapes=[pltpu.VMEM((tm, D_FF), BF16)],
        compiler_params=_params("parallel", "parallel"),
        name="conv_ffn",
    )(h2, h2, x1, wup, cw, cb, wd, gfin)


@jax.jit
def _layer(x, g_mix, w_in, pe_cmp_k, w_cmp_k1, w_cmp_k2, pe_cmp_v, w_cmp_v1, w_cmp_v2,
           w_proj_nsa, w_proj_dil, w_out, g_ffn, w_up, conv_w, conv_b, w_down, g_final):
    B, S, D = x.shape
    depth = g_mix.shape[0]
    for l in range(depth):
        gm = g_mix[l].reshape(1, D)
        kcmp, vcmp, ks, kw, d0, d1, d2, qt, vst, vwt, gates = _in_proj(x, gm, w_in[l])
        kc, vct = _compress(kcmp, vcmp, pe_cmp_k[l], w_cmp_k1[l], w_cmp_k2[l],
                            pe_cmp_v[l], w_cmp_v1[l], w_cmp_v2[l])
        o_a = _nsa(qt, kc, vct, ks, vst, kw, vwt, gates)
        outs, lses = _dilated((d0, d1, d2))
        merge_cols = w_in[l].T[w_in.shape[2] - 2 * D:]
        x1, h2 = _merge(x, o_a, outs, lses, gm, merge_cols, w_proj_nsa[l], w_proj_dil[l], w_out[l],
                        g_ffn[l].reshape(1, D))
        x = _ffn(h2, x1, w_up[l], conv_w[l], conv_b[l], w_down[l], g_final)
        assert depth == 1
    return x


def kernel(x, g_mix, w_in, pe_cmp_k, w_cmp_k1, w_cmp_k2, pe_cmp_v, w_cmp_v1, w_cmp_v2, w_proj_nsa, w_proj_dil, w_out, g_ffn, w_up, conv_w, conv_b, w_down, g_final):
    return _layer(x, g_mix, w_in, pe_cmp_k, w_cmp_k1, w_cmp_k2, pe_cmp_v, w_cmp_v1, w_cmp_v2,
                  w_proj_nsa, w_proj_dil, w_out, g_ffn, w_up, conv_w, conv_b, w_down, g_final)
```

```python
import functools
import math

import numpy as np
import jax
import jax.numpy as jnp
from jax import lax
from jax.experimental import pallas as pl
from jax.experimental.pallas import tpu as pltpu

HEAD_DIM = 64
NSA_HEADS = 8
NSA_GROUPS = 2
NSA_REP = NSA_HEADS // NSA_GROUPS
CMP_BLOCK = 32
CMP_STRIDE = 16
CMP_HIDDEN = 128
SLC_BLOCK = 64
SLC_TOP = 16
NSA_WINDOW = 512
FORCE_SCORE = 1.0e4
DIL_CONFIGS = ((128, 1), (512, 4), (2048, 16))
DIL_GROUPS = 3
DIL_HEADS_PER_GROUP = 4
DIL_HEADS = DIL_GROUPS * DIL_HEADS_PER_GROUP
D_FF = 2816
CONV_WIDTH = 3
RMS_EPS = 1e-6
NEG_INF = -1e30

LANES = 128
VMEM_LIMIT_BYTES = 56 * 1024 * 1024

F32 = jnp.float32
BF16 = jnp.bfloat16
NT_DIMS = (((1,), (1,)), ((), ()))


def _alibi_slopes(n):
    return [float(2.0 ** (-8.0 * i / n)) for i in range(1, n + 1)]


def _rms(xf, g):
    ms = jnp.mean(xf * xf, axis=-1, keepdims=True)
    return xf * lax.rsqrt(ms + RMS_EPS) * g


def _dot(a, b):
    return jnp.dot(a, b, preferred_element_type=F32)


def _dot_nt(a, b):
    return lax.dot_general(a, b, NT_DIMS, preferred_element_type=F32)


def _sigmoid(z):
    return 1.0 / (1.0 + jnp.exp(-z))


def _params(*sem):
    return pltpu.CompilerParams(dimension_semantics=sem, vmem_limit_bytes=VMEM_LIMIT_BYTES)


IN_TM = 1024
N_KVC = 4 * HEAD_DIM
N_KSEL = NSA_GROUPS * HEAD_DIM
N_DIL = 3 * DIL_HEADS * HEAD_DIM
DIL_WIDTH = DIL_HEADS_PER_GROUP * HEAD_DIM
T_Q = NSA_HEADS * HEAD_DIM
T_V = NSA_GROUPS * HEAD_DIM
GATE_ROWS = 16
K_AUG = 2 * HEAD_DIM
SLC_SHIFT = int(math.log2(SLC_BLOCK))
AUG_NBLK = 8
AUG_HI, AUG_LO = AUG_NBLK, AUG_NBLK + 1
LOG2E = math.log2(math.e)
LN2 = math.log(2.0)
V_PAD = 16
V_ROWS = HEAD_DIM + V_PAD


def _key_position_columns(pos0, rows, step=1):
    pos = pos0 + step * lax.broadcasted_iota(jnp.int32, (rows, HEAD_DIM), 0)
    col = lax.broadcasted_iota(jnp.int32, (rows, HEAD_DIM), 1)
    blk = jnp.bitwise_and(lax.shift_right_logical(pos, SLC_SHIFT), AUG_NBLK - 1)
    hi = lax.shift_left(lax.shift_right_logical(pos, 7), 7).astype(F32)
    lo = jnp.bitwise_and(pos, 127).astype(F32)
    c = jnp.where((col == AUG_HI) | (col == AUG_HI + 2), hi,
                  jnp.where((col == AUG_LO) | (col == AUG_LO + 2), lo, 0.0))
    return jnp.where((col < AUG_NBLK) & (blk == col), 1.0, c)


def _in_proj_kernel(x_ref, g_ref, wn_ref, wt_ref,
                    kcmp_ref, vcmp_ref, ks_ref, kw_ref, d0_ref, d1_ref, d2_ref, qt_ref, vst_ref, vwt_ref, gate_ref,
                    slab_ref):
    tm = x_ref.shape[0]
    h = _rms(x_ref[...], g_ref[...]).astype(BF16)
    c0 = 0
    kvc = _dot_nt(h, wn_ref[c0:c0 + N_KVC, :])
    kcmp_ref[...] = kvc[:, :N_KSEL]
    vcmp_ref[...] = kvc[:, N_KSEL:]
    c0 += N_KVC
    ks = _dot_nt(h, wn_ref[c0:c0 + N_KSEL, :]).astype(BF16)
    c0 += N_KSEL
    kw = _dot_nt(h, wn_ref[c0:c0 + N_KSEL, :]).astype(BF16)
    c0 += N_KSEL
    aug = _key_position_columns(pl.program_id(1) * tm, tm).astype(BF16)
    for g in range(NSA_GROUPS):
        ks_ref[g] = jnp.concatenate([ks[:, g * HEAD_DIM:(g + 1) * HEAD_DIM], aug], axis=1)
        kw_ref[g] = jnp.concatenate([kw[:, g * HEAD_DIM:(g + 1) * HEAD_DIM], aug], axis=1)
    seg = DIL_HEADS * HEAD_DIM
    for which in range(3):
        y = _dot_nt(h, wn_ref[c0 + which * seg:c0 + (which + 1) * seg, :])
        for gi, (d_ref, (_, r)) in enumerate(zip((d0_ref, d1_ref, d2_ref), DIL_CONFIGS)):
            yg = y[:, gi * DIL_WIDTH:(gi + 1) * DIL_WIDTH]
            if r == 1:
                d_ref[which, 0] = yg.astype(BF16)
                continue
            for s in range(DIL_WIDTH // LANES):
                slab_ref[s] = yg[:, s * LANES:(s + 1) * LANES]
            for c in range(r):
                d_ref[which, c] = jnp.concatenate(
                    [slab_ref[s, pl.ds(c, tm // r, stride=r), :] for s in range(DIL_WIDTH // LANES)],
                    axis=1).astype(BF16)
    yt = _dot_nt(wt_ref[...], h)
    qt_ref[...] = yt[0:T_Q].astype(BF16)
    r0 = T_Q
    ones = jnp.where(lax.broadcasted_iota(jnp.int32, (V_PAD, tm), 0) == 0, 1.0, 0.0).astype(BF16)
    for ref in (vst_ref, vwt_ref):
        vt = yt[r0:r0 + T_V].astype(BF16)
        r0 += T_V
        for g in range(NSA_GROUPS):
            ref[g] = jnp.concatenate([vt[g * HEAD_DIM:(g + 1) * HEAD_DIM, :], ones], axis=0)
    gate_ref[...] = _sigmoid(yt[r0:r0 + NSA_GROUPS * GATE_ROWS])


def _in_proj(x, g_mix, w_in):
    B, S, D = x.shape
    scale = HEAD_DIM ** -0.5 * LOG2E
    o_q, o_kv = 0, T_Q
    o_gate = o_kv + 6 * N_KSEL
    o_dil = o_gate + 3 * NSA_HEADS
    o_merge = o_dil + N_DIL
    w_t = w_in.T
    kv = w_t[o_kv:o_gate]

    def kind(k):
        return kv[k * N_KSEL:(k + 1) * N_KSEL]

    dil = w_t[o_dil:o_merge]
    dil = jnp.concatenate([dil[:DIL_HEADS * HEAD_DIM] * scale, dil[DIL_HEADS * HEAD_DIM:]], axis=0)
    wn = jnp.concatenate([kind(0), kind(1), kind(2), kind(4), dil], axis=0).astype(BF16)
    wg = w_t[o_gate:o_dil].reshape(NSA_GROUPS, 3 * NSA_REP, D)
    wg = jnp.pad(wg, ((0, 0), (0, GATE_ROWS - 3 * NSA_REP), (0, 0))).reshape(NSA_GROUPS * GATE_ROWS, D)
    wt = jnp.concatenate([w_t[o_q:o_kv] * scale, kind(3), kind(5), wg], axis=0).astype(BF16)
    tm = IN_TM
    grid = (B, S // tm)
    full = lambda a: pl.BlockSpec(a.shape, lambda b, i: (0,) * a.ndim)
    k_shape = jax.ShapeDtypeStruct((B, NSA_GROUPS, S, K_AUG), BF16)
    v_shape = jax.ShapeDtypeStruct((B, NSA_GROUPS, V_ROWS, S), BF16)
    k_spec = pl.BlockSpec((None, NSA_GROUPS, tm, K_AUG), lambda b, i: (b, 0, i, 0))
    v_spec = pl.BlockSpec((None, NSA_GROUPS, V_ROWS, tm), lambda b, i: (b, 0, 0, i))
    c_shape = jax.ShapeDtypeStruct((B, S, N_KSEL), F32)
    c_spec = pl.BlockSpec((None, tm, N_KSEL), lambda b, i: (b, i, 0))
    d_shapes = tuple(jax.ShapeDtypeStruct((B, 3, r, S // r, DIL_WIDTH), BF16) for _, r in DIL_CONFIGS)
    d_specs = tuple(pl.BlockSpec((None, 3, r, tm // r, DIL_WIDTH), lambda b, i: (b, 0, 0, i, 0))
                    for _, r in DIL_CONFIGS)
    out_shape = (
        c_shape, c_shape, k_shape, k_shape, *d_shapes,
        jax.ShapeDtypeStruct((B, T_Q, S), BF16),
        v_shape,
        v_shape,
        jax.ShapeDtypeStruct((B, NSA_GROUPS * GATE_ROWS, S), F32),
    )
    out_specs = (
        c_spec, c_spec, k_spec, k_spec, *d_specs,
        pl.BlockSpec((None, T_Q, tm), lambda b, i: (b, 0, i)),
        v_spec,
        v_spec,
        pl.BlockSpec((None, NSA_GROUPS * GATE_ROWS, tm), lambda b, i: (b, 0, i)),
    )
    return pl.pallas_call(
        _in_proj_kernel,
        grid=grid,
        in_specs=[pl.BlockSpec((None, tm, D), lambda b, i: (b, i, 0)), full(g_mix), full(wn), full(wt)],
        out_specs=out_specs,
        out_shape=out_shape,
        scratch_shapes=[pltpu.VMEM((DIL_WIDTH // LANES, tm, LANES), F32)],
        compiler_params=_params("parallel", "parallel"),
        name="in_proj",
    )(x, g_mix, wn, wt)


def _compress_kernel(xk_ref, xv_ref, pek_ref, pev_ref, w1k_ref, w1v_ref, w2k_ref, w2vt_ref, kc_ref, vct_ref):
    nch = xk_ref.shape[0] // CMP_STRIDE
    streams = [(xk_ref, pek_ref, w1k_ref), (xv_ref, pev_ref, w1v_ref)]
    acc = [[jnp.zeros((nch, NSA_GROUPS * CMP_HIDDEN), F32) for _ in range(2)] for _ in streams]
    for j in range(CMP_STRIDE):
        for n, (x_ref, pe_ref, w1_ref) in enumerate(streams):
            xj = x_ref[pl.ds(j, nch, stride=CMP_STRIDE), :]
            for half in range(2):
                row = half * CMP_STRIDE + j
                acc[n][half] = acc[n][half] + _dot((xj + pe_ref[row:row + 1, :]).astype(BF16), w1_ref[row])
    hid = [jax.nn.gelu(lo + pltpu.roll(hi, nch - 1, 0)).astype(BF16) for lo, hi in acc]
    aug = _key_position_columns(CMP_BLOCK - 1, nch, CMP_STRIDE).astype(BF16)
    for g in range(NSA_GROUPS):
        cols = slice(g * CMP_HIDDEN, (g + 1) * CMP_HIDDEN)
        kc_ref[g] = jnp.concatenate([_dot(hid[0][:, cols], w2k_ref[...]).astype(BF16), aug], axis=1)
        vct_ref[g] = _dot_nt(w2vt_ref[...], hid[1][:, cols]).astype(BF16)


def _compress(xk, xv, pe_k, w1_k, w2_k, pe_v, w1_v, w2_v):
    B, S, width = xk.shape
    nch = S // CMP_STRIDE
    G, dh, hid = NSA_GROUPS, HEAD_DIM, CMP_HIDDEN
    eye = jnp.eye(G, dtype=w1_k.dtype)

    def expand(pe, w1):
        pe_t = jnp.broadcast_to(pe.reshape(CMP_BLOCK, 1, dh), (CMP_BLOCK, G, dh)).reshape(CMP_BLOCK, width)
        wexp = jnp.einsum('pdn,ge->pgden', w1.reshape(CMP_BLOCK, dh, hid), eye).reshape(CMP_BLOCK, width, G * hid)
        return pe_t, wexp.astype(BF16)

    pek, w1k = expand(pe_k, w1_k)
    pev, w1v = expand(pe_v, w1_v)
    consts = [pek, pev, w1k, w1v, w2_k.astype(BF16), w2_v.T.astype(BF16)]
    full = lambda a: pl.BlockSpec(a.shape, lambda b: (0,) * a.ndim)
    seq = pl.BlockSpec((None, S, width), lambda b: (b, 0, 0))
    return pl.pallas_call(
        _compress_kernel,
        grid=(B,),
        in_specs=[seq, seq] + [full(a) for a in consts],
        out_specs=(pl.BlockSpec((None, G, nch, K_AUG), lambda b: (b, 0, 0, 0)),
                   pl.BlockSpec((None, G, dh, nch), lambda b: (b, 0, 0, 0))),
        out_shape=(jax.ShapeDtypeStruct((B, G, nch, K_AUG), BF16),
                   jax.ShapeDtypeStruct((B, G, dh, nch), BF16)),
        compiler_params=_params("parallel"),
        name="compress",
    )(xk, xv, *consts)


NSA_TQ = 256
SWEEP_TK = AUG_NBLK * SLC_BLOCK
SEL_SUB = 128
SEL_AHEAD = 6


def _rowmax8(s):
    return jnp.max(s.reshape(s.shape[0] // 8, 8, s.shape[1]), axis=0)


def _online_softmax(chunks, scores, values):
    pending = [scores(ch) for ch in chunks[:SEL_AHEAD]]
    m = acc = None
    for n, ch in enumerate(chunks):
        if n + SEL_AHEAD < len(chunks):
            pending.append(scores(chunks[n + SEL_AHEAD]))
        s = pending.pop(0)
        m_c = jnp.max(_rowmax8(s), axis=0, keepdims=True)
        if m is None:
            m = m_c
            acc = _dot(values(ch), jnp.exp2(s - m).astype(BF16))
        else:
            m_new = jnp.maximum(m, m_c)
            acc = jnp.exp2(m - m_new) * acc + _dot(values(ch), jnp.exp2(s - m_new).astype(BF16))
            m = m_new
    return acc


def _nsa_kernel(qt_ref, kc_ref, vct_ref, ks_ref, vst_ref, kw_ref, vwt_ref, gate_ref, ov_ref,
                o_ref, sbt_ref, osel_ref, idx_ref, *, slopes):
    g = pl.program_id(1)
    i = pl.program_id(2)
    R, dh, tq, tk = NSA_REP, HEAD_DIM, NSA_TQ, SWEEP_TK
    L = R * tq
    t0 = i * tq
    nc = kc_ref.shape[0]
    ns = ov_ref.shape[0]
    n_tiles = ns // AUG_NBLK
    tile4 = lambda a: jnp.concatenate([a] * R, axis=1)

    qt = qt_ref[...]
    qs = jnp.concatenate([qt[r * dh:(r + 1) * dh, :] for r in range(R)], axis=1)
    slope = [jnp.where(g == 0, slopes[r], slopes[R + r]).astype(F32) for r in range(R)]
    slope_row = jnp.concatenate([jnp.full((1, tq), 1.0, F32) * slope[r] for r in range(R)], axis=1)
    t_row = t0 + lax.broadcasted_iota(jnp.int32, (1, tq), 1)

    r8 = lax.broadcasted_iota(jnp.int32, (AUG_NBLK, L), 0)
    s_full = slope_row * LOG2E
    s_hi = s_full.astype(BF16).astype(F32)
    alibi8 = jnp.where(r8 < 2, s_hi, jnp.where(r8 < 4, s_full - s_hi, 0.0))
    q_pad = jnp.zeros((K_AUG - dh - 2 * AUG_NBLK, L), BF16)
    q_plain = jnp.concatenate([qs, jnp.concatenate([jnp.zeros((AUG_NBLK, L), F32), alibi8], axis=0).astype(BF16),
                               q_pad], axis=0)

    cmp_end = lax.broadcasted_iota(jnp.int32, (nc, tq), 0) * CMP_STRIDE + (CMP_BLOCK - 1)
    m_cmp = t_row >= cmp_end
    sc = _dot(kc_ref[...], q_plain) + tile4(jnp.where(m_cmp, 0.0, NEG_INF))

    a0 = jnp.maximum(t0 - NSA_WINDOW, 0)
    row_w = lax.broadcasted_iota(jnp.int32, (SEL_SUB, tq), 0)
    row_minus_lane = row_w - lax.broadcasted_iota(jnp.int32, (SEL_SUB, tq), 1)

    def causal_bias(k0):
        return jnp.where(row_minus_lane <= t0 - k0, 0.0, NEG_INF)

    def win_scores(ch):
        diag, r = ch
        k0 = pl.multiple_of((t0 if diag else a0) + r, SEL_SUB)
        if diag:
            bias = causal_bias(k0)
        else:
            bias = jnp.where((row_w < t0 - k0) & (row_minus_lane > t0 - k0 - NSA_WINDOW), 0.0, NEG_INF)
        return _dot(kw_ref[pl.ds(k0, SEL_SUB), :], q_plain) + tile4(bias)

    def win_values(ch):
        diag, r = ch
        return vwt_ref[:, pl.ds(pl.multiple_of((t0 if diag else a0) + r, SEL_SUB), SEL_SUB)]

    win_chunks = ([(True, r) for r in range(0, tq, SEL_SUB)]
                  + [(False, r) for r in range(0, NSA_WINDOW, SEL_SUB)])
    acc_w = _online_softmax(win_chunks, win_scores, win_values)
    o_win = acc_w[:dh] * (1.0 / acc_w[dh:dh + 1])

    mx = jnp.max(_rowmax8(sc), axis=0, keepdims=True)
    p = jnp.exp2(sc - mx) * tile4(jnp.where(m_cmp, 1.0, 0.0))
    den = jnp.sum(jnp.sum(p.reshape(nc // 8, 8, L), axis=0), axis=0, keepdims=True)
    pr = p * (1.0 / jnp.maximum(den, 1e-30))
    o_cmp = _dot(vct_ref[...], pr.astype(BF16))
    psum = pr[:, 0:tq]
    for r in range(1, R):
        psum = psum + pr[:, r * tq:(r + 1) * tq]

    p_hi = psum.astype(BF16)
    p_lo = (psum - p_hi.astype(F32)).astype(BF16)
    imp = _dot(ov_ref[...], p_hi) + _dot(ov_ref[...], p_lo)
    blk = lax.broadcasted_iota(jnp.int32, (ns, tq), 0)
    cur = lax.shift_right_logical(t_row, SLC_SHIFT)
    val = jnp.where((blk == cur) | (blk == 0), FORCE_SCORE, imp)
    val = jnp.where(blk <= cur, val, -1.0)
    vals = [val[8 * v:8 * v + 8, :] for v in range(ns // 8)]
    ranks = [jnp.zeros((8, tq), F32) for _ in vals]
    row8 = lax.broadcasted_iota(jnp.int32, (8, tq), 0)
    for j in range(ns):
        vj = jnp.broadcast_to(val[j:j + 1, :], (8, tq))
        for v in range(len(vals)):
            if 8 * v > j:
                ahead = vj >= vals[v]
            elif 8 * v + 7 <= j:
                ahead = vj > vals[v]
            else:
                ahead = (vj > vals[v]) | ((vj == vals[v]) & (row8 > j - 8 * v))
            ranks[v] = ranks[v] + jnp.where(ahead, 1.0, 0.0)
    for T in range(n_tiles):
        selb = jnp.where((ranks[T] < float(SLC_TOP)) & (vals[T] >= 0.0), 0.0, NEG_INF)
        sbt_ref[T] = jnp.concatenate([tile4(selb), alibi8], axis=0).astype(BF16)

    td = lax.div(t0, tk)
    cnt = jnp.int32(0)
    for T in range(n_tiles - 1):
        picked = jnp.where((ranks[T] < float(SLC_TOP)) & (vals[T] >= 0.0), 1.0, 0.0)
        idx_ref[cnt] = jnp.int32(T)
        cnt = cnt + jnp.logical_and(jnp.max(picked) > 0.0, T < td).astype(jnp.int32)
    for k in range(n_tiles):
        @pl.when(cnt == k)
        def _(k=k):
            past = [idx_ref[j] for j in range(k)]
            q_past = [jnp.concatenate([qs, sbt_ref[T], q_pad], axis=0) for T in past]
            q_diag = jnp.concatenate([qs, sbt_ref[td], q_pad], axis=0)
            chunks = ([(j, r) for j in range(k) for r in range(0, tk, SEL_SUB)]
                      + [(None, r) for r in range(0, tk, SEL_SUB)])

            def key_start(ch):
                j, r = ch
                return pl.multiple_of((td if j is None else past[j]) * tk + r, SEL_SUB)

            def chunk_scores(ch):
                k0 = key_start(ch)
                if ch[0] is None:
                    return _dot(ks_ref[pl.ds(k0, SEL_SUB), :], q_diag) + tile4(causal_bias(k0))
                return _dot(ks_ref[pl.ds(k0, SEL_SUB), :], q_past[ch[0]])

            acc = _online_softmax(chunks, chunk_scores, lambda ch: vst_ref[:, pl.ds(key_start(ch), SEL_SUB)])
            osel_ref[...] = acc[:dh] * (1.0 / acc[dh:dh + 1])

    o_sel = osel_ref[...]
    gates = gate_ref[...]
    tiles = []
    for r in range(R):
        sl = slice(r * tq, (r + 1) * tq)
        tiles.append(gates[3 * r:3 * r + 1, :] * o_cmp[:, sl]
                     + gates[3 * r + 1:3 * r + 2, :] * o_sel[:, sl]
                     + gates[3 * r + 2:3 * r + 3, :] * o_win[:, sl])
    o_ref[...] = jnp.concatenate(tiles, axis=0).T.astype(o_ref.dtype)


def _overlap_matrix(nc, ns):
    cs = np.arange(nc)[None, :] * CMP_STRIDE
    ss = np.arange(ns)[:, None] * SLC_BLOCK
    ov = np.clip(np.minimum(cs + CMP_BLOCK, ss + SLC_BLOCK) - np.maximum(cs, ss), 0, None)
    return jnp.asarray(ov.astype(np.float32) / CMP_BLOCK, dtype=BF16)


def _nsa(qt, kc, vct, ks, vst, kw, vwt, gates):
    B, _, S = qt.shape
    G, R, dh, tq = NSA_GROUPS, NSA_REP, HEAD_DIM, NSA_TQ
    nc = kc.shape[2]
    ns = S // SLC_BLOCK
    ov = _overlap_matrix(nc, ns)
    kern = functools.partial(_nsa_kernel, slopes=tuple(_alibi_slopes(NSA_HEADS)))
    assert S % SWEEP_TK == 0 and S >= NSA_WINDOW + tq
    per_bg = lambda shape: pl.BlockSpec((None, None) + shape, lambda b, g, i: (b, g, 0, 0))
    return pl.pallas_call(
        kern,
        grid=(B, G, S // tq),
        in_specs=[
            pl.BlockSpec((None, R * dh, tq), lambda b, g, i: (b, g, i)),
            per_bg((nc, K_AUG)), per_bg((dh, nc)),
            per_bg((S, K_AUG)), per_bg((V_ROWS, S)),
            per_bg((S, K_AUG)), per_bg((V_ROWS, S)),
            pl.BlockSpec((None, GATE_ROWS, tq), lambda b, g, i: (b, g, i)),
            pl.BlockSpec(ov.shape, lambda b, g, i: (0, 0)),
        ],
        out_specs=pl.BlockSpec((None, tq, R * dh), lambda b, g, i: (b, i, g)),
        out_shape=jax.ShapeDtypeStruct((B, S, G * R * dh), BF16),
        scratch_shapes=[pltpu.VMEM((S // SWEEP_TK, 2 * AUG_NBLK, R * tq), BF16),
                        pltpu.VMEM((dh, R * tq), F32),
                        pltpu.SMEM((S // SWEEP_TK,), jnp.int32)],
        compiler_params=_params("parallel", "parallel", "arbitrary"),
        name="nsa_attention",
    )(qt, kc, vct, ks, vst, kw, vwt, gates, ov)


BAND_TQ = 128
BAND_SUB = 16
BAND_PROBLEMS = 16
BAND = 128


def _banded_kernel(q_ref, k_ref, v_ref, o_ref, lse_ref, *, slopes, nsub, ncls):
    tq, dh, nh = BAND_TQ, HEAD_DIM, DIL_HEADS_PER_GROUP
    tk = tq + BAND
    width = nh * dh
    head_of_lane = lambda rows: lax.shift_right_logical(
        lax.broadcasted_iota(jnp.int32, (rows, width), 1), int(math.log2(dh)))
    lane_head, q_head = head_of_lane(tk), head_of_lane(tq)
    keep = [jnp.where(lane_head == h, 1.0, 0.0).astype(BF16) for h in range(nh)]

    def per_head(a):
        return jnp.concatenate([a * keep[h] for h in range(nh)], axis=0)

    def biases(first_key_offset):
        d = first_key_offset + (lax.broadcasted_iota(jnp.int32, (tq, tk), 0)
                                - lax.broadcasted_iota(jnp.int32, (tq, tk), 1))
        mask_bias = jnp.where((d >= 0) & (d <= BAND), 0.0, NEG_INF)
        neg_d = -d.astype(F32)
        return [slopes[h] * neg_d + mask_bias for h in range(nh)]

    subs = []
    for cls in range(ncls):
        for sub in range(nsub):
            i = pl.program_id(2) * nsub + sub
            k0 = pl.multiple_of(jnp.maximum(i - 1, 0) * tq, tq)
            q = q_ref[cls, sub * tq:(sub + 1) * tq, :]
            scores = _dot_nt(q, per_head(k_ref[cls, pl.ds(k0, tk), :]))
            subs.append((cls, sub, i * tq - k0, k0, scores))
    first_bias = biases(subs[0][2])
    inner_bias = biases(tq) if nsub > 1 else None
    probs = []
    for cls, sub, off, k0, s in subs:
        bias = first_bias if sub == 0 else inner_bias
        ps, mxs, dens = [], [], []
        for h in range(nh):
            sh = s[:, h * tk:(h + 1) * tk] + bias[h]
            mx = jnp.max(sh, axis=-1, keepdims=True)
            e = jnp.exp2(sh - mx)
            dens.append(jnp.sum(e, axis=-1, keepdims=True))
            ps.append(e.astype(BF16))
            mxs.append(mx)
        probs.append((cls, sub, k0, jnp.concatenate(ps, axis=1), mxs, dens))
    for cls, sub, k0, p, mxs, dens in probs:
        mx_all, den = mxs[nh - 1], dens[nh - 1]
        for h in range(nh - 2, -1, -1):
            mx_all = jnp.where(q_head == h, mxs[h], mx_all)
            den = jnp.where(q_head == h, dens[h], den)
        rows = slice(sub * tq, (sub + 1) * tq)
        o_ref[cls, rows, :] = _dot(p, per_head(v_ref[cls, pl.ds(k0, tk), :])) * (1.0 / den)
        lse_ref[cls, rows, :] = mx_all * LN2 + jnp.log(den)


def _banded(d, slopes):
    B, _, r, n, width = d.shape
    nsub = min(BAND_SUB, n // BAND_TQ)
    ncls = min(r, BAND_PROBLEMS // nsub)
    tq = BAND_TQ * nsub
    assert n >= BAND_TQ + BAND and n % tq == 0 and r % ncls == 0
    kern = functools.partial(_banded_kernel, slopes=tuple(slopes), nsub=nsub, ncls=ncls)
    tile = pl.BlockSpec((None, None, ncls, tq, width), lambda b, c, i: (b, 0, c, i, 0))
    kseq = pl.BlockSpec((None, None, ncls, n, width), lambda b, c, i: (b, 1, c, 0, 0))
    vseq = pl.BlockSpec((None, None, ncls, n, width), lambda b, c, i: (b, 2, c, 0, 0))
    out = pl.BlockSpec((None, ncls, tq, width), lambda b, c, i: (b, c, i, 0))
    return pl.pallas_call(
        kern,
        grid=(B, r // ncls, n // tq),
        in_specs=[tile, kseq, vseq],
        out_specs=(out, out),
        out_shape=(jax.ShapeDtypeStruct((B, r, n, width), F32),) * 2,
        compiler_params=_params("parallel", "parallel", "arbitrary"),
        name="banded_attention",
    )(d, d, d)


def _dilated(dils):
    slopes = _alibi_slopes(DIL_HEADS)
    outs, lses = [], []
    for gi, (w, r) in enumerate(DIL_CONFIGS):
        assert w // r == BAND
        sl = [s_ * r * LOG2E for s_ in slopes[gi * DIL_HEADS_PER_GROUP:(gi + 1) * DIL_HEADS_PER_GROUP]]
        o, lse = _banded(dils[gi], sl)
        outs.append(o)
        lses.append(lse)
    return outs, lses


MERGE_TM = 1024
MERGE_TN = 256


def _token_order(ref, slab_ref, s):
    r, rows, _ = ref.shape
    lanes = slice(s * LANES, (s + 1) * LANES)
    if r == 1:
        return ref[0, :, lanes]
    for c in range(r):
        slab_ref[s, pl.ds(c, rows, stride=r), :] = ref[c, :, lanes]
    return slab_ref[s]


def _merge_kernel(x_ref, oa_ref, o0_ref, o1_ref, o2_ref, l0_ref, l1_ref, l2_ref,
                  gmix_ref, wm_ref, wpn_ref, wpd_ref, wo_ref, gffn_ref, x1_ref, h2_ref, slab_ref, mix_ref):
    x = x_ref[...]
    D = x.shape[1]
    h = _rms(x, gmix_ref[...]).astype(BF16)
    halves = []
    for s in range(DIL_WIDTH // LANES):
        l0, l1, l2 = [_token_order(r_, slab_ref, s) for r_ in (l0_ref, l1_ref, l2_ref)]
        mx = jnp.maximum(jnp.maximum(l0, l1), l2)
        e0, e1, e2 = jnp.exp(l0 - mx), jnp.exp(l1 - mx), jnp.exp(l2 - mx)
        inv = 1.0 / (e0 + e1 + e2)
        w0, w1, w2 = e0 * inv, e1 * inv, e2 * inv
        o0, o1, o2 = [_token_order(r_, slab_ref, s) for r_ in (o0_ref, o1_ref, o2_ref)]
        halves.append((o0 * w0 + o1 * w1 + o2 * w2).astype(BF16))
    ob = jnp.concatenate(halves, axis=1)
    oa = oa_ref[...]
    for c in range(D // MERGE_TN):
        cols = slice(c * MERGE_TN, (c + 1) * MERGE_TN)
        g_a = _sigmoid(_dot_nt(h, wm_ref[c * MERGE_TN:(c + 1) * MERGE_TN, :]))
        g_d = _sigmoid(_dot_nt(h, wm_ref[D + c * MERGE_TN:D + (c + 1) * MERGE_TN, :]))
        mix_ref[:, cols] = (g_a * _dot(oa, wpn_ref[:, cols]) + g_d * _dot(ob, wpd_ref[:, cols])).astype(BF16)
    x1 = x + _dot(mix_ref[...], wo_ref[...])
    x1_ref[...] = x1
    h2_ref[...] = _rms(x1, gffn_ref[...]).astype(BF16)


def _merge(x, o_a, outs, lses, g_mix, w_merge, w_proj_nsa, w_proj_dil, w_out, g_ffn):
    B, S, D = x.shape
    tm = MERGE_TM
    row = lambda a: pl.BlockSpec((None, tm, a.shape[2]), lambda b, i: (b, i, 0))
    cls = lambda a: pl.BlockSpec((None, a.shape[1], tm // a.shape[1], a.shape[3]), lambda b, i: (b, 0, i, 0))
    full = lambda a: pl.BlockSpec(a.shape, lambda b, i: (0,) * a.ndim)
    ws = [w_merge.astype(BF16), w_proj_nsa.astype(BF16), w_proj_dil.astype(BF16), w_out.astype(BF16)]
    consts = [g_mix, *ws, g_ffn]
    in_specs = [row(x), row(o_a)] + [cls(a) for a in (*outs, *lses)] + [full(a) for a in consts]
    return pl.pallas_call(
        _merge_kernel,
        grid=(B, S // tm),
        in_specs=in_specs,
        out_specs=(pl.BlockSpec((None, tm, D), lambda b, i: (b, i, 0)),) * 2,
        out_shape=(jax.ShapeDtypeStruct((B, S, D), F32), jax.ShapeDtypeStruct((B, S, D), BF16)),
        scratch_shapes=[pltpu.VMEM((DIL_WIDTH // LANES, tm, LANES), F32), pltpu.VMEM((tm, D), BF16)],
        compiler_params=_params("parallel", "parallel"),
        name="merge_proj",
    )(x, o_a, *outs, *lses, *consts)


FFN_TM = 1024
FFN_TN = 256
HALO = 16


def _ffn_kernel(h_ref, halo_ref, x1_ref, wup_ref, cw_ref, cb_ref, wd_ref, gfin_ref, o_ref, act_ref):
    i = pl.program_id(1)
    h = h_ref[...]
    halo = halo_ref[...]
    tm = h.shape[0]
    row = lax.broadcasted_iota(jnp.int32, (tm, FFN_TN), 0)
    live = (i > 0).astype(F32)
    for j in range(D_FF // FFN_TN):
        cols = slice(j * FFN_TN, (j + 1) * FFN_TN)
        wu = wup_ref[:, cols]
        u = _dot(h, wu)
        uh = _dot(halo, wu) * live
        gate = _dot(h, wup_ref[:, D_FF + j * FFN_TN:D_FF + (j + 1) * FFN_TN])
        p1 = jnp.broadcast_to(uh[HALO - 1:HALO, :], (tm, FFN_TN))
        p2 = jnp.broadcast_to(uh[HALO - 2:HALO - 1, :], (tm, FFN_TN))
        u1 = jnp.where(row == 0, p1, pltpu.roll(u, 1, 0))
        u2 = jnp.where(row == 0, p2, jnp.where(row == 1, p1, pltpu.roll(u, 2, 0)))
        uc = cb_ref[:, cols] + cw_ref[0:1, cols] * u2
        uc = uc + cw_ref[1:2, cols] * u1
        uc = uc + cw_ref[2:3, cols] * u
        act_ref[:, j * FFN_TN:(j + 1) * FFN_TN] = (jax.nn.gelu(uc) * gate).astype(BF16)
    y = _dot(act_ref[...], wd_ref[...])
    o_ref[...] = _rms(x1_ref[...] + y, gfin_ref[...])


def _ffn(h2, x1, w_up, conv_w, conv_b, w_down, g_final):
    B, S, D = h2.shape
    tm = FFN_TM
    assert D_FF % FFN_TN == 0
    wup = w_up.astype(BF16)
    cw = conv_w
    cb = conv_b.reshape(1, D_FF)
    wd = w_down.astype(BF16)
    gfin = g_final.reshape(1, D)
    full = lambda a: pl.BlockSpec(a.shape, lambda b, i: (0,) * a.ndim, pipeline_mode=pl.Buffered(1))
    tile = pl.BlockSpec((None, tm, D), lambda b, i: (b, i, 0))
    halo = pl.BlockSpec((None, HALO, D), lambda b, i: (b, jnp.maximum(i * (tm // HALO) - 1, 0), 0))
    return pl.pallas_call(
        _ffn_kernel,
        grid=(B, S // tm),
        in_specs=[tile, halo, tile, full(wup), full(cw), full(cb), full(wd), full(gfin)],
        out_specs=tile,
        out_shape=jax.ShapeDtypeStruct((B, S, D), F32),
        scratch_shapes=[pltpu.VMEM((tm, D_FF), BF16)],
        compiler_params=_params("parallel", "parallel"),
        name="conv_ffn",
    )(h2, h2, x1, wup, cw, cb, wd, gfin)


@jax.jit
def _layer(x, g_mix, w_in, pe_cmp_k, w_cmp_k1, w_cmp_k2, pe_cmp_v, w_cmp_v1, w_cmp_v2,
           w_proj_nsa, w_proj_dil, w_out, g_ffn, w_up, conv_w, conv_b, w_down, g_final):
    B, S, D = x.shape
    depth = g_mix.shape[0]
    for l in range(depth):
        gm = g_mix[l].reshape(1, D)
        kcmp, vcmp, ks, kw, d0, d1, d2, qt, vst, vwt, gates = _in_proj(x, gm, w_in[l])
        kc, vct = _compress(kcmp, vcmp, pe_cmp_k[l], w_cmp_k1[l], w_cmp_k2[l],
                            pe_cmp_v[l], w_cmp_v1[l], w_cmp_v2[l])
        o_a = _nsa(qt, kc, vct, ks, vst, kw, vwt, gates)
        outs, lses = _dilated((d0, d1, d2))
        merge_cols = w_in[l].T[w_in.shape[2] - 2 * D:]
        x1, h2 = _merge(x, o_a, outs, lses, gm, merge_cols, w_proj_nsa[l], w_proj_dil[l], w_out[l],
                        g_ffn[l].reshape(1, D))
        x = _ffn(h2, x1, w_up[l], conv_w[l], conv_b[l], w_down[l], g_final)
        assert depth == 1
    return x


def kernel(x, g_mix, w_in, pe_cmp_k, w_cmp_k1, w_cmp_k2, pe_cmp_v, w_cmp_v1, w_cmp_v2, w_proj_nsa, w_proj_dil, w_out, g_ffn, w_up, conv_w, conv_b, w_down, g_final):
    return _layer(x, g_mix, w_in, pe_cmp_k, w_cmp_k1, w_cmp_k2, pe_cmp_v, w_cmp_v1, w_cmp_v2,
                  w_proj_nsa, w_proj_dil, w_out, g_ffn, w_up, conv_w, conv_b, w_down, g_final)
```

```python
import functools
import math

import numpy as np
import jax
import jax.numpy as jnp
from jax import lax
from jax.experimental import pallas as pl
from jax.experimental.pallas import tpu as pltpu

HEAD_DIM = 64
NSA_HEADS = 8
NSA_GROUPS = 2
NSA_REP = NSA_HEADS // NSA_GROUPS
CMP_BLOCK = 32
CMP_STRIDE = 16
CMP_HIDDEN = 128
SLC_BLOCK = 64
SLC_TOP = 16
NSA_WINDOW = 512
FORCE_SCORE = 1.0e4
DIL_CONFIGS = ((128, 1), (512, 4), (2048, 16))
DIL_GROUPS = 3
DIL_HEADS_PER_GROUP = 4
DIL_HEADS = DIL_GROUPS * DIL_HEADS_PER_GROUP
D_FF = 2816
CONV_WIDTH = 3
RMS_EPS = 1e-6
NEG_INF = -1e30

LANES = 128
VMEM_LIMIT_BYTES = 56 * 1024 * 1024

F32 = jnp.float32
BF16 = jnp.bfloat16
NT_DIMS = (((1,), (1,)), ((), ()))


def _alibi_slopes(n):
    return [float(2.0 ** (-8.0 * i / n)) for i in range(1, n + 1)]


def _rms(xf, g):
    ms = jnp.mean(xf * xf, axis=-1, keepdims=True)
    return xf * lax.rsqrt(ms + RMS_EPS) * g


def _dot(a, b):
    return jnp.dot(a, b, preferred_element_type=F32)


def _dot_nt(a, b):
    return lax.dot_general(a, b, NT_DIMS, preferred_element_type=F32)


def _sigmoid(z):
    return 1.0 / (1.0 + jnp.exp(-z))


def _params(*sem):
    return pltpu.CompilerParams(dimension_semantics=sem, vmem_limit_bytes=VMEM_LIMIT_BYTES)


IN_TM = 1024
N_KVC = 4 * HEAD_DIM
N_KSEL = NSA_GROUPS * HEAD_DIM
N_DIL = 3 * DIL_HEADS * HEAD_DIM
DIL_WIDTH = DIL_HEADS_PER_GROUP * HEAD_DIM
T_Q = NSA_HEADS * HEAD_DIM
T_V = NSA_GROUPS * HEAD_DIM
GATE_ROWS = 16
K_AUG = 2 * HEAD_DIM
SLC_SHIFT = int(math.log2(SLC_BLOCK))
AUG_NBLK = 8
AUG_HI, AUG_LO = AUG_NBLK, AUG_NBLK + 1
LOG2E = math.log2(math.e)
LN2 = math.log(2.0)
V_PAD = 16
V_ROWS = HEAD_DIM + V_PAD


def _key_position_columns(pos0, rows, step=1):
    pos = pos0 + step * lax.broadcasted_iota(jnp.int32, (rows, HEAD_DIM), 0)
    col = lax.broadcasted_iota(jnp.int32, (rows, HEAD_DIM), 1)
    blk = jnp.bitwise_and(lax.shift_right_logical(pos, SLC_SHIFT), AUG_NBLK - 1)
    hi = lax.shift_left(lax.shift_right_logical(pos, 7), 7).astype(F32)
    lo = jnp.bitwise_and(pos, 127).astype(F32)
    c = jnp.where((col == AUG_HI) | (col == AUG_HI + 2), hi,
                  jnp.where((col == AUG_LO) | (col == AUG_LO + 2), lo, 0.0))
    return jnp.where((col < AUG_NBLK) & (blk == col), 1.0, c)


def _in_proj_kernel(x_ref, g_ref, wn_ref, wt_ref,
                    kcmp_ref, vcmp_ref, ks_ref, kw_ref, d0_ref, d1_ref, d2_ref, qt_ref, vst_ref, vwt_ref, gate_ref,
                    slab_ref):
    tm = x_ref.shape[0]
    h = _rms(x_ref[...], g_ref[...]).astype(BF16)
    c0 = 0
    kvc = _dot_nt(h, wn_ref[c0:c0 + N_KVC, :])
    kcmp_ref[...] = kvc[:, :N_KSEL]
    vcmp_ref[...] = kvc[:, N_KSEL:]
    c0 += N_KVC
    ks = _dot_nt(h, wn_ref[c0:c0 + N_KSEL, :]).astype(BF16)
    c0 += N_KSEL
    kw = _dot_nt(h, wn_ref[c0:c0 + N_KSEL, :]).astype(BF16)
    c0 += N_KSEL
    aug = _key_position_columns(pl.program_id(1) * tm, tm).astype(BF16)
    for g in range(NSA_GROUPS):
        ks_ref[g] = jnp.concatenate([ks[:, g * HEAD_DIM:(g + 1) * HEAD_DIM], aug], axis=1)
        kw_ref[g] = jnp.concatenate([kw[:, g * HEAD_DIM:(g + 1) * HEAD_DIM], aug], axis=1)
    seg = DIL_HEADS * HEAD_DIM
    for which in range(3):
        y = _dot_nt(h, wn_ref[c0 + which * seg:c0 + (which + 1) * seg, :])
        for gi, (d_ref, (_, r)) in enumerate(zip((d0_ref, d1_ref, d2_ref), DIL_CONFIGS)):
            yg = y[:, gi * DIL_WIDTH:(gi + 1) * DIL_WIDTH]
            if r == 1:
                d_ref[which, 0] = yg.astype(BF16)
                continue
            for s in range(DIL_WIDTH // LANES):
                slab_ref[s] = yg[:, s * LANES:(s + 1) * LANES]
            for c in range(r):
                d_ref[which, c] = jnp.concatenate(
                    [slab_ref[s, pl.ds(c, tm // r, stride=r), :] for s in range(DIL_WIDTH // LANES)],
                    axis=1).astype(BF16)
    yt = _dot_nt(wt_ref[...], h)
    qt_ref[...] = yt[0:T_Q].astype(BF16)
    r0 = T_Q
    ones = jnp.where(lax.broadcasted_iota(jnp.int32, (V_PAD, tm), 0) == 0, 1.0, 0.0).astype(BF16)
    for ref in (vst_ref, vwt_ref):
        vt = yt[r0:r0 + T_V].astype(BF16)
        r0 += T_V
        for g in range(NSA_GROUPS):
            ref[g] = jnp.concatenate([vt[g * HEAD_DIM:(g + 1) * HEAD_DIM, :], ones], axis=0)
    gate_ref[...] = _sigmoid(yt[r0:r0 + NSA_GROUPS * GATE_ROWS])


def _in_proj(x, g_mix, w_in):
    B, S, D = x.shape
    scale = HEAD_DIM ** -0.5 * LOG2E
    o_q, o_kv = 0, T_Q
    o_gate = o_kv + 6 * N_KSEL
    o_dil = o_gate + 3 * NSA_HEADS
    o_merge = o_dil + N_DIL
    w_t = w_in.T
    kv = w_t[o_kv:o_gate]

    def kind(k):
        return kv[k * N_KSEL:(k + 1) * N_KSEL]

    dil = w_t[o_dil:o_merge]
    dil = jnp.concatenate([dil[:DIL_HEADS * HEAD_DIM] * scale, dil[DIL_HEADS * HEAD_DIM:]], axis=0)
    wn = jnp.concatenate([kind(0), kind(1), kind(2), kind(4), dil], axis=0).astype(BF16)
    wg = w_t[o_gate:o_dil].reshape(NSA_GROUPS, 3 * NSA_REP, D)
    wg = jnp.pad(wg, ((0, 0), (0, GATE_ROWS - 3 * NSA_REP), (0, 0))).reshape(NSA_GROUPS * GATE_ROWS, D)
    wt = jnp.concatenate([w_t[o_q:o_kv] * scale, kind(3), kind(5), wg], axis=0).astype(BF16)
    tm = IN_TM
    grid = (B, S // tm)
    full = lambda a: pl.BlockSpec(a.shape, lambda b, i: (0,) * a.ndim)
    k_shape = jax.ShapeDtypeStruct((B, NSA_GROUPS, S, K_AUG), BF16)
    v_shape = jax.ShapeDtypeStruct((B, NSA_GROUPS, V_ROWS, S), BF16)
    k_spec = pl.BlockSpec((None, NSA_GROUPS, tm, K_AUG), lambda b, i: (b, 0, i, 0))
    v_spec = pl.BlockSpec((None, NSA_GROUPS, V_ROWS, tm), lambda b, i: (b, 0, 0, i))
    c_shape = jax.ShapeDtypeStruct((B, S, N_KSEL), F32)
    c_spec = pl.BlockSpec((None, tm, N_KSEL), lambda b, i: (b, i, 0))
    d_shapes = tuple(jax.ShapeDtypeStruct((B, 3, r, S // r, DIL_WIDTH), BF16) for _, r in DIL_CONFIGS)
    d_specs = tuple(pl.BlockSpec((None, 3, r, tm // r, DIL_WIDTH), lambda b, i: (b, 0, 0, i, 0))
                    for _, r in DIL_CONFIGS)
    out_shape = (
        c_shape, c_shape, k_shape, k_shape, *d_shapes,
        jax.ShapeDtypeStruct((B, T_Q, S), BF16),
        v_shape,
        v_shape,
        jax.ShapeDtypeStruct((B, NSA_GROUPS * GATE_ROWS, S), F32),
    )
    out_specs = (
        c_spec, c_spec, k_spec, k_spec, *d_specs,
        pl.BlockSpec((None, T_Q, tm), lambda b, i: (b, 0, i)),
        v_spec,
        v_spec,
        pl.BlockSpec((None, NSA_GROUPS * GATE_ROWS, tm), lambda b, i: (b, 0, i)),
    )
    return pl.pallas_call(
        _in_proj_kernel,
        grid=grid,
        in_specs=[pl.BlockSpec((None, tm, D), lambda b, i: (b, i, 0)), full(g_mix), full(wn), full(wt)],
        out_specs=out_specs,
        out_shape=out_shape,
        scratch_shapes=[pltpu.VMEM((DIL_WIDTH // LANES, tm, LANES), F32)],
        compiler_params=_params("parallel", "parallel"),
        name="in_proj",
    )(x, g_mix, wn, wt)


def _compress_kernel(xk_ref, xv_ref, pek_ref, pev_ref, w1k_ref, w1v_ref, w2k_ref, w2vt_ref, kc_ref, vct_ref):
    nch = xk_ref.shape[0] // CMP_STRIDE
    streams = [(xk_ref, pek_ref, w1k_ref), (xv_ref, pev_ref, w1v_ref)]
    acc = [[jnp.zeros((nch, NSA_GROUPS * CMP_HIDDEN), F32) for _ in range(2)] for _ in streams]
    for j in range(CMP_STRIDE):
        for n, (x_ref, pe_ref, w1_ref) in enumerate(streams):
            xj = x_ref[pl.ds(j, nch, stride=CMP_STRIDE), :]
            for half in range(2):
                row = half * CMP_STRIDE + j
                acc[n][half] = acc[n][half] + _dot((xj + pe_ref[row:row + 1, :]).astype(BF16), w1_ref[row])
    hid = [jax.nn.gelu(lo + pltpu.roll(hi, nch - 1, 0)).astype(BF16) for lo, hi in acc]
    aug = _key_position_columns(CMP_BLOCK - 1, nch, CMP_STRIDE).astype(BF16)
    for g in range(NSA_GROUPS):
        cols = slice(g * CMP_HIDDEN, (g + 1) * CMP_HIDDEN)
        kc_ref[g] = jnp.concatenate([_dot(hid[0][:, cols], w2k_ref[...]).astype(BF16), aug], axis=1)
        vct_ref[g] = _dot_nt(w2vt_ref[...], hid[1][:, cols]).astype(BF16)


def _compress(xk, xv, pe_k, w1_k, w2_k, pe_v, w1_v, w2_v):
    B, S, width = xk.shape
    nch = S // CMP_STRIDE
    G, dh, hid = NSA_GROUPS, HEAD_DIM, CMP_HIDDEN
    eye = jnp.eye(G, dtype=w1_k.dtype)

    def expand(pe, w1):
        pe_t = jnp.broadcast_to(pe.reshape(CMP_BLOCK, 1, dh), (CMP_BLOCK, G, dh)).reshape(CMP_BLOCK, width)
        wexp = jnp.einsum('pdn,ge->pgden', w1.reshape(CMP_BLOCK, dh, hid), eye).reshape(CMP_BLOCK, width, G * hid)
        return pe_t, wexp.astype(BF16)

    pek, w1k = expand(pe_k, w1_k)
    pev, w1v = expand(pe_v, w1_v)
    consts = [pek, pev, w1k, w1v, w2_k.astype(BF16), w2_v.T.astype(BF16)]
    full = lambda a: pl.BlockSpec(a.shape, lambda b: (0,) * a.ndim)
    seq = pl.BlockSpec((None, S, width), lambda b: (b, 0, 0))
    return pl.pallas_call(
        _compress_kernel,
        grid=(B,),
        in_specs=[seq, seq] + [full(a) for a in consts],
        out_specs=(pl.BlockSpec((None, G, nch, K_AUG), lambda b: (b, 0, 0, 0)),
                   pl.BlockSpec((None, G, dh, nch), lambda b: (b, 0, 0, 0))),
        out_shape=(jax.ShapeDtypeStruct((B, G, nch, K_AUG), BF16),
                   jax.ShapeDtypeStruct((B, G, dh, nch), BF16)),
        compiler_params=_params("parallel"),
        name="compress",
    )(xk, xv, *consts)


NSA_TQ = 256
SWEEP_TK = AUG_NBLK * SLC_BLOCK
SEL_SUB = 128
SEL_AHEAD = 6


def _rowmax8(s):
    return jnp.max(s.reshape(s.shape[0] // 8, 8, s.shape[1]), axis=0)


def _online_softmax(chunks, scores, values):
    pending = [scores(ch) for ch in chunks[:SEL_AHEAD]]
    m = acc = None
    for n, ch in enumerate(chunks):
        if n + SEL_AHEAD < len(chunks):
            pending.append(scores(chunks[n + SEL_AHEAD]))
        s = pending.pop(0)
        m_c = jnp.max(_rowmax8(s), axis=0, keepdims=True)
        if m is None:
            m = m_c
            acc = _dot(values(ch), jnp.exp2(s - m).astype(BF16))
        else:
            m_new = jnp.maximum(m, m_c)
            acc = jnp.exp2(m - m_new) * acc + _dot(values(ch), jnp.exp2(s - m_new).astype(BF16))
            m = m_new
    return acc


def _nsa_kernel(qt_ref, kc_ref, vct_ref, ks_ref, vst_ref, kw_ref, vwt_ref, gate_ref, ov_ref,
                o_ref, sbt_ref, osel_ref, idx_ref, *, slopes):
    g = pl.program_id(1)
    i = pl.program_id(2)
    R, dh, tq, tk = NSA_REP, HEAD_DIM, NSA_TQ, SWEEP_TK
    L = R * tq
    t0 = i * tq
    nc = kc_ref.shape[0]
    ns = ov_ref.shape[0]
    n_tiles = ns // AUG_NBLK
    tile4 = lambda a: jnp.concatenate([a] * R, axis=1)

    qt = qt_ref[...]
    qs = jnp.concatenate([qt[r * dh:(r + 1) * dh, :] for r in range(R)], axis=1)
    slope = [jnp.where(g == 0, slopes[r], slopes[R + r]).astype(F32) for r in range(R)]
    slope_row = jnp.concatenate([jnp.full((1, tq), 1.0, F32) * slope[r] for r in range(R)], axis=1)
    t_row = t0 + lax.broadcasted_iota(jnp.int32, (1, tq), 1)

    r8 = lax.broadcasted_iota(jnp.int32, (AUG_NBLK, L), 0)
    s_full = slope_row * LOG2E
    s_hi = s_full.astype(BF16).astype(F32)
    alibi8 = jnp.where(r8 < 2, s_hi, jnp.where(r8 < 4, s_full - s_hi, 0.0))
    q_pad = jnp.zeros((K_AUG - dh - 2 * AUG_NBLK, L), BF16)
    q_plain = jnp.concatenate([qs, jnp.concatenate([jnp.zeros((AUG_NBLK, L), F32), alibi8], axis=0).astype(BF16),
                               q_pad], axis=0)

    cmp_end = lax.broadcasted_iota(jnp.int32, (nc, tq), 0) * CMP_STRIDE + (CMP_BLOCK - 1)
    m_cmp = t_row >= cmp_end
    sc = _dot(kc_ref[...], q_plain) + tile4(jnp.where(m_cmp, 0.0, NEG_INF))

    a0 = jnp.maximum(t0 - NSA_WINDOW, 0)
    row_w = lax.broadcasted_iota(jnp.int32, (SEL_SUB, tq), 0)
    row_minus_lane = row_w - lax.broadcasted_iota(jnp.int32, (SEL_SUB, tq), 1)

    def causal_bias(k0):
        return jnp.where(row_minus_lane <= t0 - k0, 0.0, NEG_INF)

    def win_scores(ch):
        diag, r = ch
        k0 = pl.multiple_of((t0 if diag else a0) + r, SEL_SUB)
        if diag:
            bias = causal_bias(k0)
        else:
            bias = jnp.where((row_w < t0 - k0) & (row_minus_lane > t0 - k0 - NSA_WINDOW), 0.0, NEG_INF)
        return _dot(kw_ref[pl.ds(k0, SEL_SUB), :], q_plain) + tile4(bias)

    def win_values(ch):
        diag, r = ch
        return vwt_ref[:, pl.ds(pl.multiple_of((t0 if diag else a0) + r, SEL_SUB), SEL_SUB)]

    win_chunks = ([(True, r) for r in range(0, tq, SEL_SUB)]
                  + [(False, r) for r in range(0, NSA_WINDOW, SEL_SUB)])
    acc_w = _online_softmax(win_chunks, win_scores, win_values)
    o_win = acc_w[:dh] * (1.0 / acc_w[dh:dh + 1])

    mx = jnp.max(_rowmax8(sc), axis=0, keepdims=True)
    p = jnp.exp2(sc - mx) * tile4(jnp.where(m_cmp, 1.0, 0.0))
    den = jnp.sum(jnp.sum(p.reshape(nc // 8, 8, L), axis=0), axis=0, keepdims=True)
    pr = p * (1.0 / jnp.maximum(den, 1e-30))
    o_cmp = _dot(vct_ref[...], pr.astype(BF16))
    psum = pr[:, 0:tq]
    for r in range(1, R):
        psum = psum + pr[:, r * tq:(r + 1) * tq]

    p_hi = psum.astype(BF16)
    p_lo = (psum - p_hi.astype(F32)).astype(BF16)
    imp = _dot(ov_ref[...], p_hi) + _dot(ov_ref[...], p_lo)
    blk = lax.broadcasted_iota(jnp.int32, (ns, tq), 0)
    cur = lax.shift_right_logical(t_row, SLC_SHIFT)
    val = jnp.where((blk == cur) | (blk == 0), FORCE_SCORE, imp)
    val = jnp.where(blk <= cur, val, -1.0)
    vals = [val[8 * v:8 * v + 8, :] for v in range(ns // 8)]
    ranks = [jnp.zeros((8, tq), F32) for _ in vals]
    row8 = lax.broadcasted_iota(jnp.int32, (8, tq), 0)
    for j in range(ns):
        vj = jnp.broadcast_to(val[j:j + 1, :], (8, tq))
        for v in range(len(vals)):
            if 8 * v > j:
                ahead = vj >= vals[v]
            elif 8 * v + 7 <= j:
                ahead = vj > vals[v]
            else:
                ahead = (vj > vals[v]) | ((vj == vals[v]) & (row8 > j - 8 * v))
            ranks[v] = ranks[v] + jnp.where(ahead, 1.0, 0.0)
    for T in range(n_tiles):
        selb = jnp.where((ranks[T] < float(SLC_TOP)) & (vals[T] >= 0.0), 0.0, NEG_INF)
        sbt_ref[T] = jnp.concatenate([tile4(selb), alibi8], axis=0).astype(BF16)

    td = lax.div(t0, tk)
    cnt = jnp.int32(0)
    for T in range(n_tiles - 1):
        picked = jnp.where((ranks[T] < float(SLC_TOP)) & (vals[T] >= 0.0), 1.0, 0.0)
        idx_ref[cnt] = jnp.int32(T)
        cnt = cnt + jnp.logical_and(jnp.max(picked) > 0.0, T < td).astype(jnp.int32)
    for k in range(n_tiles):
        @pl.when(cnt == k)
        def _(k=k):
            past = [idx_ref[j] for j in range(k)]
            q_rows = {}

            def q_aug(j):
                if j not in q_rows:
                    q_rows[j] = jnp.concatenate([qs, sbt_ref[td if j is None else past[j]], q_pad], axis=0)
                return q_rows[j]
            chunks = ([(j, r) for j in range(k) for r in range(0, tk, SEL_SUB)]
                      + [(None, r) for r in range(0, tk, SEL_SUB)])

            def key_start(ch):
                j, r = ch
                return pl.multiple_of((td if j is None else past[j]) * tk + r, SEL_SUB)

            def chunk_scores(ch):
                k0 = key_start(ch)
                s = _dot(ks_ref[pl.ds(k0, SEL_SUB), :], q_aug(ch[0]))
                return s + tile4(causal_bias(k0)) if ch[0] is None else s

            acc = _online_softmax(chunks, chunk_scores, lambda ch: vst_ref[:, pl.ds(key_start(ch), SEL_SUB)])
            osel_ref[...] = acc[:dh] * (1.0 / acc[dh:dh + 1])

    o_sel = osel_ref[...]
    gates = gate_ref[...]
    tiles = []
    for r in range(R):
        sl = slice(r * tq, (r + 1) * tq)
        tiles.append(gates[3 * r:3 * r + 1, :] * o_cmp[:, sl]
                     + gates[3 * r + 1:3 * r + 2, :] * o_sel[:, sl]
                     + gates[3 * r + 2:3 * r + 3, :] * o_win[:, sl])
    o_ref[...] = jnp.concatenate(tiles, axis=0).T.astype(o_ref.dtype)


def _overlap_matrix(nc, ns):
    cs = np.arange(nc)[None, :] * CMP_STRIDE
    ss = np.arange(ns)[:, None] * SLC_BLOCK
    ov = np.clip(np.minimum(cs + CMP_BLOCK, ss + SLC_BLOCK) - np.maximum(cs, ss), 0, None)
    return jnp.asarray(ov.astype(np.float32) / CMP_BLOCK, dtype=BF16)


def _nsa(qt, kc, vct, ks, vst, kw, vwt, gates):
    B, _, S = qt.shape
    G, R, dh, tq = NSA_GROUPS, NSA_REP, HEAD_DIM, NSA_TQ
    nc = kc.shape[2]
    ns = S // SLC_BLOCK
    ov = _overlap_matrix(nc, ns)
    kern = functools.partial(_nsa_kernel, slopes=tuple(_alibi_slopes(NSA_HEADS)))
    assert S % SWEEP_TK == 0 and S >= NSA_WINDOW + tq
    per_bg = lambda shape: pl.BlockSpec((None, None) + shape, lambda b, g, i: (b, g, 0, 0))
    return pl.pallas_call(
        kern,
        grid=(B, G, S // tq),
        in_specs=[
            pl.BlockSpec((None, R * dh, tq), lambda b, g, i: (b, g, i)),
            per_bg((nc, K_AUG)), per_bg((dh, nc)),
            per_bg((S, K_AUG)), per_bg((V_ROWS, S)),
            per_bg((S, K_AUG)), per_bg((V_ROWS, S)),
            pl.BlockSpec((None, GATE_ROWS, tq), lambda b, g, i: (b, g, i)),
            pl.BlockSpec(ov.shape, lambda b, g, i: (0, 0)),
        ],
        out_specs=pl.BlockSpec((None, tq, R * dh), lambda b, g, i: (b, i, g)),
        out_shape=jax.ShapeDtypeStruct((B, S, G * R * dh), BF16),
        scratch_shapes=[pltpu.VMEM((S // SWEEP_TK, 2 * AUG_NBLK, R * tq), BF16),
                        pltpu.VMEM((dh, R * tq), F32),
                        pltpu.SMEM((S // SWEEP_TK,), jnp.int32)],
        compiler_params=_params("parallel", "parallel", "arbitrary"),
        name="nsa_attention",
    )(qt, kc, vct, ks, vst, kw, vwt, gates, ov)


BAND_TQ = 128
BAND_SUB = 16
BAND_PROBLEMS = 16
BAND = 128


def _banded_kernel(q_ref, k_ref, v_ref, o_ref, lse_ref, *, slopes, nsub, ncls):
    tq, dh, nh = BAND_TQ, HEAD_DIM, DIL_HEADS_PER_GROUP
    tk = tq + BAND
    width = nh * dh
    head_of_lane = lambda rows: lax.shift_right_logical(
        lax.broadcasted_iota(jnp.int32, (rows, width), 1), int(math.log2(dh)))
    lane_head, q_head = head_of_lane(tk), head_of_lane(tq)
    keep = [jnp.where(lane_head == h, 1.0, 0.0).astype(BF16) for h in range(nh)]

    def per_head(a):
        return jnp.concatenate([a * keep[h] for h in range(nh)], axis=0)

    def biases(first_key_offset):
        d = first_key_offset + (lax.broadcasted_iota(jnp.int32, (tq, tk), 0)
                                - lax.broadcasted_iota(jnp.int32, (tq, tk), 1))
        mask_bias = jnp.where((d >= 0) & (d <= BAND), 0.0, NEG_INF)
        neg_d = -d.astype(F32)
        return [slopes[h] * neg_d + mask_bias for h in range(nh)]

    subs = []
    for cls in range(ncls):
        for sub in range(nsub):
            i = pl.program_id(2) * nsub + sub
            k0 = pl.multiple_of(jnp.maximum(i - 1, 0) * tq, tq)
            q = q_ref[cls, sub * tq:(sub + 1) * tq, :]
            scores = _dot_nt(q, per_head(k_ref[cls, pl.ds(k0, tk), :]))
            subs.append((cls, sub, i * tq - k0, k0, scores))
    first_bias = biases(subs[0][2])
    inner_bias = biases(tq) if nsub > 1 else None
    probs = []
    for cls, sub, off, k0, s in subs:
        bias = first_bias if sub == 0 else inner_bias
        ps, mxs, dens = [], [], []
        for h in range(nh):
            sh = s[:, h * tk:(h + 1) * tk] + bias[h]
            mx = jnp.max(sh, axis=-1, keepdims=True)
            e = jnp.exp2(sh - mx)
            dens.append(jnp.sum(e, axis=-1, keepdims=True))
            ps.append(e.astype(BF16))
            mxs.append(mx)
        probs.append((cls, sub, k0, jnp.concatenate(ps, axis=1), mxs, dens))
    for cls, sub, k0, p, mxs, dens in probs:
        mx_all, den = mxs[nh - 1], dens[nh - 1]
        for h in range(nh - 2, -1, -1):
            mx_all = jnp.where(q_head == h, mxs[h], mx_all)
            den = jnp.where(q_head == h, dens[h], den)
        rows = slice(sub * tq, (sub + 1) * tq)
        o_ref[cls, rows, :] = _dot(p, per_head(v_ref[cls, pl.ds(k0, tk), :])) * (1.0 / den)
        lse_ref[cls, rows, :] = mx_all * LN2 + jnp.log(den)


def _banded(d, slopes):
    B, _, r, n, width = d.shape
    nsub = min(BAND_SUB, n // BAND_TQ)
    ncls = min(r, BAND_PROBLEMS // nsub)
    tq = BAND_TQ * nsub
    assert n >= BAND_TQ + BAND and n % tq == 0 and r % ncls == 0
    kern = functools.partial(_banded_kernel, slopes=tuple(slopes), nsub=nsub, ncls=ncls)
    tile = pl.BlockSpec((None, None, ncls, tq, width), lambda b, c, i: (b, 0, c, i, 0))
    kseq = pl.BlockSpec((None, None, ncls, n, width), lambda b, c, i: (b, 1, c, 0, 0))
    vseq = pl.BlockSpec((None, None, ncls, n, width), lambda b, c, i: (b, 2, c, 0, 0))
    out = pl.BlockSpec((None, ncls, tq, width), lambda b, c, i: (b, c, i, 0))
    return pl.pallas_call(
        kern,
        grid=(B, r // ncls, n // tq),
        in_specs=[tile, kseq, vseq],
        out_specs=(out, out),
        out_shape=(jax.ShapeDtypeStruct((B, r, n, width), F32),) * 2,
        compiler_params=_params("parallel", "parallel", "arbitrary"),
        name="banded_attention",
    )(d, d, d)


def _dilated(dils):
    slopes = _alibi_slopes(DIL_HEADS)
    outs, lses = [], []
    for gi, (w, r) in enumerate(DIL_CONFIGS):
        assert w // r == BAND
        sl = [s_ * r * LOG2E for s_ in slopes[gi * DIL_HEADS_PER_GROUP:(gi + 1) * DIL_HEADS_PER_GROUP]]
        o, lse = _banded(dils[gi], sl)
        outs.append(o)
        lses.append(lse)
    return outs, lses


MERGE_TM = 1024
MERGE_TN = 256


def _token_order(ref, slab_ref, s):
    r, rows, _ = ref.shape
    lanes = slice(s * LANES, (s + 1) * LANES)
    if r == 1:
        return ref[0, :, lanes]
    for c in range(r):
        slab_ref[s, pl.ds(c, rows, stride=r), :] = ref[c, :, lanes]
    return slab_ref[s]


def _merge_kernel(x_ref, oa_ref, o0_ref, o1_ref, o2_ref, l0_ref, l1_ref, l2_ref,
                  gmix_ref, wm_ref, wpn_ref, wpd_ref, wo_ref, gffn_ref, x1_ref, h2_ref, slab_ref, mix_ref):
    x = x_ref[...]
    D = x.shape[1]
    h = _rms(x, gmix_ref[...]).astype(BF16)
    halves = []
    for s in range(DIL_WIDTH // LANES):
        l0, l1, l2 = [_token_order(r_, slab_ref, s) for r_ in (l0_ref, l1_ref, l2_ref)]
        mx = jnp.maximum(jnp.maximum(l0, l1), l2)
        e0, e1, e2 = jnp.exp(l0 - mx), jnp.exp(l1 - mx), jnp.exp(l2 - mx)
        inv = 1.0 / (e0 + e1 + e2)
        w0, w1, w2 = e0 * inv, e1 * inv, e2 * inv
        o0, o1, o2 = [_token_order(r_, slab_ref, s) for r_ in (o0_ref, o1_ref, o2_ref)]
        halves.append((o0 * w0 + o1 * w1 + o2 * w2).astype(BF16))
    ob = jnp.concatenate(halves, axis=1)
    oa = oa_ref[...]
    for c in range(D // MERGE_TN):
        cols = slice(c * MERGE_TN, (c + 1) * MERGE_TN)
        g_a = _sigmoid(_dot_nt(h, wm_ref[c * MERGE_TN:(c + 1) * MERGE_TN, :]))
        g_d = _sigmoid(_dot_nt(h, wm_ref[D + c * MERGE_TN:D + (c + 1) * MERGE_TN, :]))
        mix_ref[:, cols] = (g_a * _dot(oa, wpn_ref[:, cols]) + g_d * _dot(ob, wpd_ref[:, cols])).astype(BF16)
    x1 = x + _dot(mix_ref[...], wo_ref[...])
    x1_ref[...] = x1
    h2_ref[...] = _rms(x1, gffn_ref[...]).astype(BF16)


def _merge(x, o_a, outs, lses, g_mix, w_merge, w_proj_nsa, w_proj_dil, w_out, g_ffn):
    B, S, D = x.shape
    tm = MERGE_TM
    row = lambda a: pl.BlockSpec((None, tm, a.shape[2]), lambda b, i: (b, i, 0))
    cls = lambda a: pl.BlockSpec((None, a.shape[1], tm // a.shape[1], a.shape[3]), lambda b, i: (b, 0, i, 0))
    full = lambda a: pl.BlockSpec(a.shape, lambda b, i: (0,) * a.ndim)
    ws = [w_merge.astype(BF16), w_proj_nsa.astype(BF16), w_proj_dil.astype(BF16), w_out.astype(BF16)]
    consts = [g_mix, *ws, g_ffn]
    in_specs = [row(x), row(o_a)] + [cls(a) for a in (*outs, *lses)] + [full(a) for a in consts]
    return pl.pallas_call(
        _merge_kernel,
        grid=(B, S // tm),
        in_specs=in_specs,
        out_specs=(pl.BlockSpec((None, tm, D), lambda b, i: (b, i, 0)),) * 2,
        out_shape=(jax.ShapeDtypeStruct((B, S, D), F32), jax.ShapeDtypeStruct((B, S, D), BF16)),
        scratch_shapes=[pltpu.VMEM((DIL_WIDTH // LANES, tm, LANES), F32), pltpu.VMEM((tm, D), BF16)],
        compiler_params=_params("parallel", "parallel"),
        name="merge_proj",
    )(x, o_a, *outs, *lses, *consts)


FFN_TM = 1024
FFN_TN = 256
HALO = 16


def _ffn_kernel(h_ref, halo_ref, x1_ref, wup_ref, cw_ref, cb_ref, wd_ref, gfin_ref, o_ref, act_ref):
    i = pl.program_id(1)
    h = h_ref[...]
    halo = halo_ref[...]
    tm = h.shape[0]
    row = lax.broadcasted_iota(jnp.int32, (tm, FFN_TN), 0)
    live = (i > 0).astype(F32)
    for j in range(D_FF // FFN_TN):
        cols = slice(j * FFN_TN, (j + 1) * FFN_TN)
        wu = wup_ref[:, cols]
        u = _dot(h, wu)
        uh = _dot(halo, wu) * live
        gate = _dot(h, wup_ref[:, D_FF + j * FFN_TN:D_FF + (j + 1) * FFN_TN])
        p1 = jnp.broadcast_to(uh[HALO - 1:HALO, :], (tm, FFN_TN))
        p2 = jnp.broadcast_to(uh[HALO - 2:HALO - 1, :], (tm, FFN_TN))
        u1 = jnp.where(row == 0, p1, pltpu.roll(u, 1, 0))
        u2 = jnp.where(row == 0, p2, jnp.where(row == 1, p1, pltpu.roll(u, 2, 0)))
        uc = cb_ref[:, cols] + cw_ref[0:1, cols] * u2
        uc = uc + cw_ref[1:2, cols] * u1
        uc = uc + cw_ref[2:3, cols] * u
        act_ref[:, j * FFN_TN:(j + 1) * FFN_TN] = (jax.nn.gelu(uc) * gate).astype(BF16)
    y = _dot(act_ref[...], wd_ref[...])
    o_ref[...] = _rms(x1_ref[...] + y, gfin_ref[...])


def _ffn(h2, x1, w_up, conv_w, conv_b, w_down, g_final):
    B, S, D = h2.shape
    tm = FFN_TM
    assert D_FF % FFN_TN == 0
    wup = w_up.astype(BF16)
    cw = conv_w
    cb = conv_b.reshape(1, D_FF)
    wd = w_down.astype(BF16)
    gfin = g_final.reshape(1, D)
    full = lambda a: pl.BlockSpec(a.shape, lambda b, i: (0,) * a.ndim, pipeline_mode=pl.Buffered(1))
    tile = pl.BlockSpec((None, tm, D), lambda b, i: (b, i, 0))
    halo = pl.BlockSpec((None, HALO, D), lambda b, i: (b, jnp.maximum(i * (tm // HALO) - 1, 0), 0))
    return pl.pallas_call(
        _ffn_kernel,
        grid=(B, S // tm),
        in_specs=[tile, halo, tile, full(wup), full(cw), full(cb), full(wd), full(gfin)],
        out_specs=tile,
        out_shape=jax.ShapeDtypeStruct((B, S, D), F32),
        scratch_shapes=[pltpu.VMEM((tm, D_FF), BF16)],
        compiler_params=_params("parallel", "parallel"),
        name="conv_ffn",
    )(h2, h2, x1, wup, cw, cb, wd, gfin)


@jax.jit
def _layer(x, g_mix, w_in, pe_cmp_k, w_cmp_k1, w_cmp_k2, pe_cmp_v, w_cmp_v1, w_cmp_v2,
           w_proj_nsa, w_proj_dil, w_out, g_ffn, w_up, conv_w, conv_b, w_down, g_final):
    B, S, D = x.shape
    depth = g_mix.shape[0]
    for l in range(depth):
        gm = g_mix[l].reshape(1, D)
        kcmp, vcmp, ks, kw, d0, d1, d2, qt, vst, vwt, gates = _in_proj(x, gm, w_in[l])
        kc, vct = _compress(kcmp, vcmp, pe_cmp_k[l], w_cmp_k1[l], w_cmp_k2[l],
                            pe_cmp_v[l], w_cmp_v1[l], w_cmp_v2[l])
        o_a = _nsa(qt, kc, vct, ks, vst, kw, vwt, gates)
        outs, lses = _dilated((d0, d1, d2))
        merge_cols = w_in[l].T[w_in.shape[2] - 2 * D:]
        x1, h2 = _merge(x, o_a, outs, lses, gm, merge_cols, w_proj_nsa[l], w_proj_dil[l], w_out[l],
                        g_ffn[l].reshape(1, D))
        x = _ffn(h2, x1, w_up[l], conv_w[l], conv_b[l], w_down[l], g_final)
        assert depth == 1
    return x


def kernel(x, g_mix, w_in, pe_cmp_k, w_cmp_k1, w_cmp_k2, pe_cmp_v, w_cmp_v1, w_cmp_v2, w_proj_nsa, w_proj_dil, w_out, g_ffn, w_up, conv_w, conv_b, w_down, g_final):
    return _layer(x, g_mix, w_in, pe_cmp_k, w_cmp_k1, w_cmp_k2, pe_cmp_v, w_cmp_v1, w_cmp_v2,
                  w_proj_nsa, w_proj_dil, w_out, g_ffn, w_up, conv_w, conv_b, w_down, g_final)
```
